```python
import jax, jax.numpy as jnp
from jax import lax
import numpy as np

D_MODEL = 1024
BATCH = 8
SEQ = 2048
DEPTH = 1
DEC_BATCH = 32
DEC_SEQ = 16
PAST_LEN = 2048

CHUNK = 64
D_MIX = D_MODEL
GDN_HEADS = 4
GDN_HEAD_DIM = 128
GDN_WIDTH = GDN_HEADS * GDN_HEAD_DIM
QKV_CONV_WIDTH = 4
CONF_WIDTH = D_MIX - GDN_WIDTH
DW_CONV_WIDTH = 31
N_EXPERTS = 32
TOP_K = 4
D_EXPERT = 1024
SWIGLU_LIMIT = 7.0
SWIGLU_ALPHA = 1.702
MOE_BLOCK = 128
RMS_EPS = 1e-6
LN_EPS = 1e-5
L2_EPS = 1e-6
QKV_COLS = 3 * GDN_WIDTH
Z_COLS = GDN_WIDTH
BETA_COLS = GDN_HEADS
DECAY_COLS = GDN_HEADS
GLU_COLS = 2 * CONF_WIDTH
D_IN_PROJ = QKV_COLS + Z_COLS + BETA_COLS + DECAY_COLS + GLU_COLS
SPLITS = (QKV_COLS, QKV_COLS + Z_COLS, QKV_COLS + Z_COLS + BETA_COLS,
          QKV_COLS + Z_COLS + BETA_COLS + DECAY_COLS)

kernel_name = "hymba_gdn_conformer_moe_stream_step"


def rms_norm(x, g):
    xf = x.astype(jnp.float32)
    y = xf * lax.rsqrt(jnp.mean(xf * xf, axis=-1, keepdims=True) + RMS_EPS)
    return (y * g.astype(jnp.float32)).astype(x.dtype)


def layer_norm(x, g, b):
    xf = x.astype(jnp.float32)
    xc = xf - jnp.mean(xf, axis=-1, keepdims=True)
    y = xc * lax.rsqrt(jnp.mean(xc * xc, axis=-1, keepdims=True) + LN_EPS)
    return (y * g.astype(jnp.float32) + b.astype(jnp.float32)).astype(x.dtype)


def l2_normalize(x):
    xf = x.astype(jnp.float32)
    return (xf * lax.rsqrt(jnp.sum(xf * xf, axis=-1, keepdims=True) + L2_EPS)).astype(x.dtype)


def causal_dwconv(x_ext, w):
    c = w.shape[1]
    return lax.conv_general_dilated(
        x_ext, w[:, None, :].astype(x_ext.dtype), window_strides=(1,), padding='VALID',
        dimension_numbers=('NWC', 'WIO', 'NWC'), feature_group_count=c)


def gated_delta_chunked(q, k, v, g, beta, s0, chunk):
    bsz, t, nh, dk = q.shape
    dv = v.shape[-1]
    n = t // chunk
    f32 = jnp.float32

    def to_chunks(a):
        a = a.astype(f32).reshape((bsz, n, chunk, nh) + a.shape[3:])
        return jnp.moveaxis(a, (1, 3), (0, 2))

    qc, kc, vc, bc = to_chunks(q), to_chunks(k), to_chunks(v), to_chunks(beta)
    gc = jnp.cumsum(to_chunks(g), axis=-1)
    pos = jnp.arange(chunk)
    causal = pos[:, None] >= pos[None, :]
    decay = jnp.exp(jnp.where(causal, gc[..., :, None] - gc[..., None, :], -jnp.inf))
    kb = kc * bc[..., None]
    m = jnp.einsum('nbhik,nbhjk->nbhij', kb, kc) * jnp.where(pos[:, None] > pos[None, :], decay, 0.0)
    lhs = jnp.eye(chunk, dtype=f32) + m
    rhs = jnp.concatenate([vc * bc[..., None], kb * jnp.exp(gc)[..., None]], axis=-1)
    sol = lax.linalg.triangular_solve(lhs, rhs, left_side=True, lower=True, unit_diagonal=True)
    u, w = sol[..., :dv], sol[..., dv:]

    def step(s, inp):
        q_i, k_i, u_i, w_i, g_i, d_i = inp
        v_new = u_i - jnp.einsum('bhck,bhkv->bhcv', w_i, s)
        o_i = (jnp.einsum('bhck,bhkv->bhcv', q_i * jnp.exp(g_i)[..., None], s)
               + jnp.einsum('bhij,bhjv->bhiv', jnp.einsum('bhik,bhjk->bhij', q_i, k_i) * d_i, v_new))
        g_last = g_i[..., -1:]
        s = (s * jnp.exp(g_last)[..., None]
             + jnp.einsum('bhck,bhcv->bhkv', k_i * jnp.exp(g_last - g_i)[..., None], v_new))
        return s, o_i

    s_fin, o = lax.scan(step, s0.astype(f32), (qc, kc, u, w, gc, decay))
    o = jnp.moveaxis(o, (0, 2), (1, 3)).reshape(bsz, t, nh, dv)
    return o.astype(q.dtype), s_fin.astype(s0.dtype)


def token_mixers(h, qkv_hist, s0, glu_hist, w_in, w_conv_qkv, a_log, dt_bias, w_onorm,
                 b_glu, w_dw, b_dw, ln_g, ln_b, w_out, chunk):
    bsz, t, _ = h.shape
    proj = jnp.einsum('btd,de->bte', h, w_in)
    qkv, z, b_beta, a_dec, glu = jnp.split(proj, SPLITS, axis=-1)

    qkv_ext = jnp.concatenate([qkv_hist.astype(qkv.dtype), qkv], axis=1)
    new_qkv_hist = qkv_ext[:, -(QKV_CONV_WIDTH - 1):]
    qkv = jax.nn.silu(causal_dwconv(qkv_ext, w_conv_qkv))
    q, k, v = jnp.split(qkv, 3, axis=-1)
    q = l2_normalize(q.reshape(bsz, t, GDN_HEADS, GDN_HEAD_DIM)) * (GDN_HEAD_DIM ** -0.5)
    k = l2_normalize(k.reshape(bsz, t, GDN_HEADS, GDN_HEAD_DIM))
    v = v.reshape(bsz, t, GDN_HEADS, GDN_HEAD_DIM)
    beta = jax.nn.sigmoid(b_beta.astype(jnp.float32))
    g = -jnp.exp(a_log.astype(jnp.float32)) * jax.nn.softplus(
        a_dec.astype(jnp.float32) + dt_bias.astype(jnp.float32))
    o, s_new = gated_delta_chunked(q, k, v, g, beta, s0, chunk)
    o = rms_norm(o, w_onorm) * jax.nn.silu(z.reshape(bsz, t, GDN_HEADS, GDN_HEAD_DIM))
    o = o.reshape(bsz, t, GDN_WIDTH)

    glu = glu + b_glu
    ga, gb = jnp.split(glu, 2, axis=-1)
    uu = ga * jax.nn.sigmoid(gb)
    u_ext = jnp.concatenate([glu_hist.astype(uu.dtype), uu], axis=1)
    new_glu_hist = u_ext[:, -(DW_CONV_WIDTH - 1):]
    c = causal_dwconv(u_ext, w_dw) + b_dw
    c = jax.nn.silu(layer_norm(c, ln_g, ln_b))

    out = jnp.einsum('bte,ed->btd', jnp.concatenate([o, c], axis=-1), w_out)
    return out, s_new, new_qkv_hist, new_glu_hist


def moe_ffn(h, w_router, b_router, w_gate_up, b_gate_up, w_down, b_down):
    n_tok, d = h.shape
    logits = jnp.dot(h.astype(jnp.float32), w_router.astype(jnp.float32)) + b_router.astype(jnp.float32)
    top_val, top_idx = lax.top_k(logits, TOP_K)
    gates = jax.nn.softmax(top_val, axis=-1).astype(h.dtype)
    n_assign = n_tok * TOP_K
    flat_e = top_idx.reshape(-1).astype(jnp.int32)
    flat_tok = jnp.arange(n_assign, dtype=jnp.int32) // TOP_K
    order = jnp.argsort(flat_e)
    sorted_e = flat_e[order]
    counts = jnp.bincount(flat_e, length=N_EXPERTS).astype(jnp.int32)
    padded = (counts + MOE_BLOCK - 1) // MOE_BLOCK * MOE_BLOCK
    padded_end = jnp.cumsum(padded)
    padded_start = padded_end - padded
    start = jnp.cumsum(counts) - counts
    dest = padded_start[sorted_e] + jnp.arange(n_assign, dtype=jnp.int32) - start[sorted_e]
    n_blocks = -(-n_assign // MOE_BLOCK) + N_EXPERTS
    n_rows = n_blocks * MOE_BLOCK
    row_tok = jnp.full((n_rows,), n_tok, jnp.int32).at[dest].set(flat_tok[order])
    row_gate = jnp.zeros((n_rows,), h.dtype).at[dest].set(gates.reshape(-1)[order])
    block_expert = jnp.minimum(
        jnp.searchsorted(padded_end, jnp.arange(n_blocks, dtype=jnp.int32) * MOE_BLOCK, side='right'),
        N_EXPERTS - 1)
    h_pad = jnp.concatenate([h, jnp.zeros((1, d), h.dtype)], axis=0)
    xb = h_pad[row_tok].reshape(n_blocks, MOE_BLOCK, d)

    def expert_block(args):
        xblk, e = args
        gu = xblk @ w_gate_up[e] + b_gate_up[e]
        x_glu, x_lin = jnp.split(gu, 2, axis=-1)
        x_glu = jnp.minimum(x_glu, SWIGLU_LIMIT)
        x_lin = jnp.clip(x_lin, -SWIGLU_LIMIT, SWIGLU_LIMIT)
        act = x_glu * jax.nn.sigmoid(SWIGLU_ALPHA * x_glu) * (x_lin + 1.0)
        return act @ w_down[e] + b_down[e]

    yb = lax.map(expert_block, (xb, block_expert))
    y = jnp.zeros((n_tok + 1, d), h.dtype).at[row_tok].add(yb.reshape(n_rows, d) * row_gate[:, None])
    return y[:n_tok]


def encoder_layer(x, qkv_hist, s0, glu_hist, norm_mix_g, w_in, w_conv_qkv, a_log, dt_bias, w_onorm,
                  b_glu, w_dw, b_dw, ln_g, ln_b, w_out, norm_ffn_g, w_router, b_router,
                  w_gate_up, b_gate_up, w_down, b_down, chunk):
    h = rms_norm(x, norm_mix_g)
    mix, s_new, qkv_new, glu_new = token_mixers(h, qkv_hist, s0, glu_hist, w_in, w_conv_qkv, a_log,
                                                dt_bias, w_onorm, b_glu, w_dw, b_dw, ln_g, ln_b,
                                                w_out, chunk)
    x = x + mix
    h = rms_norm(x, norm_ffn_g)
    ff = moe_ffn(h.reshape(-1, h.shape[-1]), w_router, b_router, w_gate_up, b_gate_up, w_down, b_down)
    x = x + ff.reshape(x.shape)
    return x, s_new, qkv_new, glu_new


def setup_inputs(seed: int = 0) -> dict:
    key = jax.random.key(seed)
    ks = jax.random.split(key, 32)
    f32 = jnp.float32
    nrm = lambda k, shape, s: jax.random.normal(k, shape, f32) * s
    dt = jnp.exp(jax.random.uniform(ks[10], (DEPTH, GDN_HEADS), f32, np.log(1e-3), np.log(1e-1)))
    return {
        "x_prompt": nrm(ks[0], (BATCH, SEQ, D_MODEL), 1.0),
        "x_sample": nrm(ks[1], (DEC_BATCH, DEC_SEQ, D_MODEL), 1.0),
        "state_gdn": nrm(ks[2], (DEPTH, DEC_BATCH, GDN_HEADS, GDN_HEAD_DIM, GDN_HEAD_DIM), 0.1),
        "state_qkv_conv": nrm(ks[3], (DEPTH, DEC_BATCH, QKV_CONV_WIDTH - 1, QKV_COLS), 1.0),
        "state_dwconv": nrm(ks[4], (DEPTH, DEC_BATCH, DW_CONV_WIDTH - 1, CONF_WIDTH), 0.5),
        "norm_mix_g": 1.0 + nrm(ks[5], (DEPTH, D_MODEL), 0.02),
        "w_in": nrm(ks[6], (DEPTH, D_MODEL, D_IN_PROJ), D_MODEL ** -0.5),
        "w_conv_qkv": nrm(ks[7], (DEPTH, QKV_CONV_WIDTH, QKV_COLS), QKV_CONV_WIDTH ** -0.5),
        "a_log": jnp.log(jax.random.uniform(ks[8], (DEPTH, GDN_HEADS), f32, 1.0, 16.0)),
        "dt_bias": dt + jnp.log(-jnp.expm1(-dt)),
        "w_onorm": 1.0 + nrm(ks[9], (DEPTH, GDN_HEAD_DIM), 0.02),
        "b_glu": nrm(ks[11], (DEPTH, GLU_COLS), 0.02),
        "w_dw": nrm(ks[12], (DEPTH, DW_CONV_WIDTH, CONF_WIDTH), DW_CONV_WIDTH ** -0.5),
        "b_dw": nrm(ks[13], (DEPTH, CONF_WIDTH), 0.02),
        "ln_g": 1.0 + nrm(ks[14], (DEPTH, CONF_WIDTH), 0.02),
        "ln_b": nrm(ks[15], (DEPTH, CONF_WIDTH), 0.02),
        "w_out": nrm(ks[16], (DEPTH, D_MIX, D_MODEL), D_MIX ** -0.5),
        "norm_ffn_g": 1.0 + nrm(ks[17], (DEPTH, D_MODEL), 0.02),
        "w_router": nrm(ks[18], (DEPTH, D_MODEL, N_EXPERTS), D_MODEL ** -0.5),
        "b_router": nrm(ks[19], (DEPTH, N_EXPERTS), 0.01),
        "w_gate_up": nrm(ks[20], (DEPTH, N_EXPERTS, D_MODEL, 2 * D_EXPERT), D_MODEL ** -0.5),
        "b_gate_up": nrm(ks[21], (DEPTH, N_EXPERTS, 2 * D_EXPERT), 0.01),
        "w_down": nrm(ks[22], (DEPTH, N_EXPERTS, D_EXPERT, D_MODEL), D_EXPERT ** -0.5),
        "b_down": nrm(ks[23], (DEPTH, N_EXPERTS, D_MODEL), 0.01),
        "norm_final_g": 1.0 + nrm(ks[24], (D_MODEL,), 0.02),
    }


def reference(x_prompt, x_sample, state_gdn, state_qkv_conv, state_dwconv, norm_mix_g, w_in,
              w_conv_qkv, a_log, dt_bias, w_onorm, b_glu, w_dw, b_dw, ln_g, ln_b, w_out,
              norm_ffn_g, w_router, b_router, w_gate_up, b_gate_up, w_down, b_down, norm_final_g):
    bp = x_prompt.shape[0]
    xp, xs = x_prompt, x_sample
    sp_l, qp_l, gp_l, ss_l, qs_l, gs_l = [], [], [], [], [], []
    for l in range(DEPTH):
        lw = (norm_mix_g[l], w_in[l], w_conv_qkv[l], a_log[l], dt_bias[l], w_onorm[l], b_glu[l],
              w_dw[l], b_dw[l], ln_g[l], ln_b[l], w_out[l], norm_ffn_g[l], w_router[l], b_router[l],
              w_gate_up[l], b_gate_up[l], w_down[l], b_down[l])
        zero_qkv = jnp.zeros((bp, QKV_CONV_WIDTH - 1, QKV_COLS), xp.dtype)
        zero_s = jnp.zeros((bp, GDN_HEADS, GDN_HEAD_DIM, GDN_HEAD_DIM), state_gdn.dtype)
        zero_glu = jnp.zeros((bp, DW_CONV_WIDTH - 1, CONF_WIDTH), xp.dtype)
        xp, sp, qp, gp = encoder_layer(xp, zero_qkv, zero_s, zero_glu, *lw, chunk=CHUNK)
        xs, ss, qs, gs = encoder_layer(xs, state_qkv_conv[l], state_gdn[l], state_dwconv[l], *lw,
                                       chunk=xs.shape[1])
        sp_l.append(sp); qp_l.append(qp); gp_l.append(gp)
        ss_l.append(ss); qs_l.append(qs); gs_l.append(gs)
    y_prompt = rms_norm(xp, norm_final_g)
    y_sample = rms_norm(xs, norm_final_g)
    return (y_prompt, y_sample, jnp.stack(sp_l), jnp.stack(qp_l), jnp.stack(gp_l),
            jnp.stack(ss_l), jnp.stack(qs_l), jnp.stack(gs_l))
```

```python
import functools
import math

import jax
import jax.numpy as jnp
from jax import lax
from jax.experimental import pallas as pl
from jax.experimental.pallas import tpu as pltpu

F32 = jnp.float32
BF16 = jnp.bfloat16
I32 = jnp.int32
HIGHEST = lax.Precision.HIGHEST

D_MODEL = 1024
GDN_HEADS = 4
HEAD_DIM = 128
GDN_WIDTH = GDN_HEADS * HEAD_DIM
QKV_COLS = 3 * GDN_WIDTH
CONF_WIDTH = 512
QKV_TAPS = 4
DW_TAPS = 31
N_EXPERTS = 32
TOP_K = 4
D_EXPERT = 1024
SWIGLU_LIMIT = 7.0
SWIGLU_ALPHA = 1.702
RMS_EPS = 1e-6
LN_EPS = 1e-5
L2_EPS = 1e-6
PROMPT_CHUNK = 64

LANES = 128
SUBLANES = 8
TOKEN_TILE = 512
GDN_TILE = 256
CONV_TILE = 256
CONV_ROWS = 32
ROUTE_TILE = 256
EXPERT_ROWS = 256
VMEM_LIMIT = 48 * 1024 * 1024


def _dotb(a, b):
    return jnp.dot(a.astype(BF16), b.astype(BF16), preferred_element_type=F32)


def _dotf(a, b):
    return jnp.dot(a, b, precision=HIGHEST, preferred_element_type=F32)


def _sigmoid(x):
    return jax.nn.sigmoid(x)


def _params(sem):
    return pltpu.CompilerParams(dimension_semantics=sem, vmem_limit_bytes=VMEM_LIMIT)


def _inproj_kernel(chunk_shift, x_ref, g_ref, wqkv_ref, wz_ref, wbd_ref, wglu_ref, bglu_ref,
                   pa_ref, pb_ref, qkv_ref, z_ref, bd_ref, bdt_ref, uu_ref):
    x = x_ref[...]
    tm = x.shape[0]
    ms = jnp.mean(x * x, axis=-1, keepdims=True)
    h = (x * lax.rsqrt(ms + RMS_EPS) * g_ref[...]).astype(BF16)
    qkv_ref[...] = jnp.dot(h, wqkv_ref[...], preferred_element_type=F32)
    z_ref[...] = jnp.dot(h, wz_ref[...], preferred_element_type=F32)
    glu = jnp.dot(h, wglu_ref[...], preferred_element_type=F32) + bglu_ref[...]
    uu_ref[...] = glu[:, :CONF_WIDTH] * _sigmoid(glu[:, CONF_WIDTH:])
    bd = jnp.dot(h, wbd_ref[...], preferred_element_type=F32)
    beta = _sigmoid(bd)
    v = bd + pb_ref[...]
    softplus = jnp.maximum(v, 0.0) + jnp.log1p(jnp.exp(-jnp.abs(v)))
    g = pa_ref[...] * softplus
    r = lax.broadcasted_iota(I32, (tm, tm), 0)
    c = lax.broadcasted_iota(I32, (tm, tm), 1)
    tri = ((r >= c) & ((r >> chunk_shift) == (c >> chunk_shift))).astype(F32)
    gc = _dotf(tri, g)
    lane = lax.broadcasted_iota(I32, (tm, LANES), 1)
    res = jnp.where(lane < GDN_HEADS, beta, gc)
    bd_ref[...] = res[:, :2 * GDN_HEADS]
    bdt_ref[...] = res.T[:2 * GDN_HEADS, :]


def _inproj(x, g, wqkv, wz, wbd, wglu, bglu, pa, pb, chunk):
    t = x.shape[0]
    tm = TOKEN_TILE
    full = lambda shape: pl.BlockSpec(shape, lambda i: (0, 0))
    return pl.pallas_call(
        functools.partial(_inproj_kernel, int(math.log2(chunk))),
        grid=(t // tm,),
        in_specs=[
            pl.BlockSpec((tm, D_MODEL), lambda i: (i, 0)),
            full((1, D_MODEL)),
            full((D_MODEL, QKV_COLS)),
            full((D_MODEL, GDN_WIDTH)),
            full((D_MODEL, LANES)),
            full((D_MODEL, 2 * CONF_WIDTH)),
            full((1, 2 * CONF_WIDTH)),
            full((1, LANES)),
            full((1, LANES)),
        ],
        out_specs=[
            pl.BlockSpec((tm, QKV_COLS), lambda i: (i, 0)),
            pl.BlockSpec((tm, GDN_WIDTH), lambda i: (i, 0)),
            pl.BlockSpec((tm, 2 * GDN_HEADS), lambda i: (i, 0)),
            pl.BlockSpec((2 * GDN_HEADS, tm), lambda i: (0, i)),
            pl.BlockSpec((tm, CONF_WIDTH), lambda i: (i, 0)),
        ],
        out_shape=[
            jax.ShapeDtypeStruct((t, QKV_COLS), F32),
            jax.ShapeDtypeStruct((t, GDN_WIDTH), F32),
            jax.ShapeDtypeStruct((t, 2 * GDN_HEADS), F32),
            jax.ShapeDtypeStruct((2 * GDN_HEADS, t), F32),
            jax.ShapeDtypeStruct((t, CONF_WIDTH), F32),
        ],
        compiler_params=_params(("arbitrary",)),
        name="inproj",
    )(x, g, wqkv, wz, wbd, wglu, bglu, pa, pb)


def _unit_lower_inverse(m, eye, n):
    p = eye - m
    mp = m
    for _ in range(int(math.log2(n)) - 1):
        mp = _dotf(mp, mp)
        p = p + _dotf(p, mp)
    return p


def _gdn_kernel(chunk, x_ref, z_ref, bd_ref, bdt_ref, hist_ref, s0_ref, wc_ref, won_ref,
                o_ref, s_ref, xbuf):
    i = pl.program_id(1)
    tt = x_ref.shape[0]
    pad = SUBLANES
    hist_rows = QKV_TAPS - 1

    @pl.when(i == 0)
    def _():
        xbuf[pad - hist_rows:pad, :] = hist_ref[...]
        s_ref[...] = s0_ref[...]

    xbuf[pad:pad + tt, :] = x_ref[...]

    ii = lax.broadcasted_iota(I32, (chunk, chunk), 0)
    jj = lax.broadcasted_iota(I32, (chunk, chunk), 1)
    lower_incl = ii >= jj
    lower_strict = ii > jj
    eye = (ii == jj).astype(F32)
    bd = bd_ref[...]
    bdt = bdt_ref[...]
    won = won_ref[...]

    def conv_silu(col0):
        cols = slice(col0, col0 + HEAD_DIM)
        acc = wc_ref[hist_rows:hist_rows + 1, cols] * xbuf[pad:pad + tt, cols]
        for j in range(hist_rows):
            r0 = pad - hist_rows + j
            acc = acc + wc_ref[j:j + 1, cols] * xbuf[r0:r0 + tt, cols]
        return acc * _sigmoid(acc)

    for h in range(GDN_HEADS):
        q = conv_silu(h * HEAD_DIM)
        k = conv_silu(GDN_WIDTH + h * HEAD_DIM)
        v = conv_silu(2 * GDN_WIDTH + h * HEAD_DIM)
        q = q * lax.rsqrt(jnp.sum(q * q, axis=-1, keepdims=True) + L2_EPS) * (HEAD_DIM ** -0.5)
        k = k * lax.rsqrt(jnp.sum(k * k, axis=-1, keepdims=True) + L2_EPS)
        kt = k.T
        beta = bd[:, h:h + 1]
        gcol = bd[:, GDN_HEADS + h:GDN_HEADS + h + 1]
        grow = bdt[GDN_HEADS + h:GDN_HEADS + h + 1, :]
        s = s_ref[h]
        for c in range(tt // chunk):
            rows = slice(c * chunk, (c + 1) * chunk)
            qc, kc, vc, ktc = q[rows], k[rows], v[rows], kt[:, rows]
            bc, gc, gr = beta[rows], gcol[rows], grow[:, rows]
            dec = jnp.exp(jnp.where(lower_incl, gc - gr, -jnp.inf))
            kb = kc * bc
            eg = jnp.exp(gc)
            m = _dotb(kb, ktc) * jnp.where(lower_strict, dec, 0.0)
            tinv = _unit_lower_inverse(m, eye, chunk)
            sol = _dotf(tinv, jnp.concatenate([vc * bc, kb * eg], axis=1))
            u, w = sol[:, :HEAD_DIM], sol[:, HEAD_DIM:]
            qk = _dotb(qc, ktc) * dec
            v_new = u - _dotb(w, s)
            o_c = _dotb(qc * eg, s) + _dotb(qk, v_new)
            g_last_col = gc[chunk - 1:chunk, :]
            g_last_row = gr[:, chunk - 1:chunk]
            kdt = ktc * jnp.exp(g_last_row - gr)
            s = s * jnp.exp(g_last_col) + _dotb(kdt, v_new)
            cols = slice(h * HEAD_DIM, (h + 1) * HEAD_DIM)
            on = o_c * lax.rsqrt(jnp.mean(o_c * o_c, axis=-1, keepdims=True) + RMS_EPS) * won
            zc = z_ref[rows, cols]
            o_ref[rows, cols] = on * (zc * _sigmoid(zc))
        s_ref[h] = s

    xbuf[pad - hist_rows:pad, :] = x_ref[tt - hist_rows:tt, :]


def _gdn(qkv, z, bd, bdt, hist, s0, wconv, wonorm, chunk, tt):
    b, t, _ = qkv.shape
    nt = t // tt
    return pl.pallas_call(
        functools.partial(_gdn_kernel, chunk),
        grid=(b, nt),
        in_specs=[
            pl.BlockSpec((None, tt, QKV_COLS), lambda bi, i: (bi, i, 0)),
            pl.BlockSpec((None, tt, GDN_WIDTH), lambda bi, i: (bi, i, 0)),
            pl.BlockSpec((None, tt, 2 * GDN_HEADS), lambda bi, i: (bi, i, 0)),
            pl.BlockSpec((None, 2 * GDN_HEADS, tt), lambda bi, i: (bi, 0, i)),
            pl.BlockSpec((None, QKV_TAPS - 1, QKV_COLS), lambda bi, i: (bi, 0, 0)),
            pl.BlockSpec((None, GDN_HEADS, HEAD_DIM, HEAD_DIM), lambda bi, i: (bi, 0, 0, 0)),
            pl.BlockSpec((QKV_TAPS, QKV_COLS), lambda bi, i: (0, 0)),
            pl.BlockSpec((1, HEAD_DIM), lambda bi, i: (0, 0)),
        ],
        out_specs=[
            pl.BlockSpec((None, tt, GDN_WIDTH), lambda bi, i: (bi, i, 0)),
            pl.BlockSpec((None, GDN_HEADS, HEAD_DIM, HEAD_DIM), lambda bi, i: (bi, 0, 0, 0)),
        ],
        out_shape=[
            jax.ShapeDtypeStruct((b, t, GDN_WIDTH), F32),
            jax.ShapeDtypeStruct((b, GDN_HEADS, HEAD_DIM, HEAD_DIM), F32),
        ],
        scratch_shapes=[pltpu.VMEM((tt + SUBLANES, QKV_COLS), F32)],
        compiler_params=_params(("arbitrary", "arbitrary")),
        name="gdn",
    )(qkv, z, bd, bdt, hist, s0, wconv, wonorm)


def _cconv_kernel(carry, u_ref, hist_ref, w_ref, b_ref, lg_ref, lb_ref, c_ref, ubuf):
    i = pl.program_id(1)
    tt = u_ref.shape[0]
    hist_rows = DW_TAPS - 1
    pad = 4 * SUBLANES
    base = pad - hist_rows

    @pl.when(i == 0)
    def _():
        ubuf[base:pad, :] = hist_ref[...]

    ubuf[pad:pad + tt, :] = u_ref[...]
    rows = min(CONV_ROWS, tt)
    for r0 in range(0, tt, rows):
        acc = w_ref[0:1, :] * ubuf[base + r0:base + r0 + rows, :]
        for j in range(1, DW_TAPS):
            acc = acc + w_ref[j:j + 1, :] * ubuf[base + r0 + j:base + r0 + j + rows, :]
        cv = acc + b_ref[...]
        mu = jnp.mean(cv, axis=-1, keepdims=True)
        xc = cv - mu
        var = jnp.mean(xc * xc, axis=-1, keepdims=True)
        y = xc * lax.rsqrt(var + LN_EPS) * lg_ref[...] + lb_ref[...]
        c_ref[r0:r0 + rows, :] = y * _sigmoid(y)
    if carry:
        ubuf[base:pad, :] = u_ref[tt - hist_rows:tt, :]


def _cconv(uu, hist, w, b, lg, lb, tt):
    bsz, t, _ = uu.shape
    nt = t // tt
    vec = pl.BlockSpec((1, CONF_WIDTH), lambda bi, i: (0, 0))
    return pl.pallas_call(
        functools.partial(_cconv_kernel, nt > 1),
        grid=(bsz, nt),
        in_specs=[
            pl.BlockSpec((None, tt, CONF_WIDTH), lambda bi, i: (bi, i, 0)),
            pl.BlockSpec((None, DW_TAPS - 1, CONF_WIDTH), lambda bi, i: (bi, 0, 0)),
            pl.BlockSpec((DW_TAPS, CONF_WIDTH), lambda bi, i: (0, 0)),
            vec, vec, vec,
        ],
        out_specs=pl.BlockSpec((None, tt, CONF_WIDTH), lambda bi, i: (bi, i, 0)),
        out_shape=jax.ShapeDtypeStruct((bsz, t, CONF_WIDTH), F32),
        scratch_shapes=[pltpu.VMEM((tt + 4 * SUBLANES, CONF_WIDTH), F32)],
        compiler_params=_params(("arbitrary", "arbitrary")),
        name="cconv",
    )(uu, hist, w, b, lg, lb)


def _outproj_kernel(o_ref, c_ref, x_ref, wo_ref, wc_ref, g_ref, wr_ref, br_ref, cnt0_ref,
                    x1_ref, h2_ref, meta_ref, gate_ref, cnt_ref, run_ref):
    i = pl.program_id(0)
    tm = x_ref.shape[0]

    @pl.when(i == 0)
    def _():
        run_ref[...] = cnt0_ref[...]

    mix = _dotb(o_ref[...], wo_ref[...]) + _dotb(c_ref[...], wc_ref[...])
    x1 = x_ref[...] + mix
    x1_ref[...] = x1
    h2 = x1 * lax.rsqrt(jnp.mean(x1 * x1, axis=-1, keepdims=True) + RMS_EPS) * g_ref[...]
    h2_ref[...] = h2
    logits = _dotf(h2, wr_ref[...]) + br_ref[...]
    lane = lax.broadcasted_iota(I32, (tm, N_EXPERTS), 1)
    vals = logits
    idxs, tops = [], []
    for _ in range(TOP_K):
        m = jnp.max(vals, axis=-1, keepdims=True)
        idx = jnp.min(jnp.where(vals == m, lane, N_EXPERTS), axis=-1, keepdims=True)
        idxs.append(idx)
        tops.append(m)
        vals = jnp.where(lane == idx, -jnp.inf, vals)
    exps = [jnp.exp(v - tops[0]) for v in tops]
    den = exps[0] + exps[1] + exps[2] + exps[3]
    onehots = [(lane == idx).astype(F32) for idx in idxs]
    chosen = onehots[0] + onehots[1] + onehots[2] + onehots[3]
    r = lax.broadcasted_iota(I32, (tm, tm), 0)
    c = lax.broadcasted_iota(I32, (tm, tm), 1)
    before = jnp.dot((r > c).astype(BF16), chosen.astype(BF16), preferred_element_type=F32)
    before = before + run_ref[...]
    lane_out = lax.broadcasted_iota(I32, (tm, LANES), 1)
    meta = jnp.zeros((tm, LANES), I32)
    gate = jnp.zeros((tm, LANES), F32)
    for k in range(TOP_K):
        pos = jnp.sum(onehots[k] * before, axis=-1, keepdims=True).astype(I32)
        meta = jnp.where(lane_out == k, idxs[k], meta)
        meta = jnp.where(lane_out == TOP_K + k, pos, meta)
        gate = jnp.where(lane_out == k, exps[k] / den, gate)
    meta_ref[...] = meta
    gate_ref[...] = gate
    run_ref[...] = run_ref[...] + jnp.sum(chosen, axis=0, keepdims=True)
    cnt_ref[...] = run_ref[...]


def _outproj(o, c, x, wo, wc, g, wr, br, cnt0):
    t = x.shape[0]
    tm = TOKEN_TILE
    full = lambda shape: pl.BlockSpec(shape, lambda i: (0, 0))
    return pl.pallas_call(
        _outproj_kernel,
        grid=(t // tm,),
        in_specs=[
            pl.BlockSpec((tm, GDN_WIDTH), lambda i: (i, 0)),
            pl.BlockSpec((tm, CONF_WIDTH), lambda i: (i, 0)),
            pl.BlockSpec((tm, D_MODEL), lambda i: (i, 0)),
            full((GDN_WIDTH, D_MODEL)),
            full((CONF_WIDTH, D_MODEL)),
            full((1, D_MODEL)),
            full((D_MODEL, N_EXPERTS)),
            full((1, N_EXPERTS)),
            full((1, N_EXPERTS)),
        ],
        out_specs=[
            pl.BlockSpec((tm, D_MODEL), lambda i: (i, 0)),
            pl.BlockSpec((tm, D_MODEL), lambda i: (i, 0)),
            pl.BlockSpec((tm, LANES), lambda i: (i, 0)),
            pl.BlockSpec((tm, LANES), lambda i: (i, 0)),
            full((1, N_EXPERTS)),
        ],
        out_shape=[
            jax.ShapeDtypeStruct((t, D_MODEL), F32),
            jax.ShapeDtypeStruct((t, D_MODEL), F32),
            jax.ShapeDtypeStruct((t, LANES), I32),
            jax.ShapeDtypeStruct((t, LANES), F32),
            jax.ShapeDtypeStruct((1, N_EXPERTS), F32),
        ],
        scratch_shapes=[pltpu.VMEM((1, N_EXPERTS), F32)],
        compiler_params=_params(("arbitrary",)),
        name="outproj_router",
    )(o, c, x, wo, wc, g, wr, br, cnt0)


def _slot(pstart_ref, meta_ref, t, k):
    e = meta_ref[0, 0, t * 2 * TOP_K + k]
    p = meta_ref[0, 0, t * 2 * TOP_K + TOP_K + k]
    return pstart_ref[e] + p


def _dispatch_kernel(pstart_ref, h_ref, meta_ref, xs_in_ref, xs_ref, sem):
    del xs_in_ref
    tm = h_ref.shape[0]

    def row_copy(t, d):
        return pltpu.make_async_copy(h_ref.at[pl.ds(t, 1)], xs_ref.at[pl.ds(d, 1)], sem)

    def issue(t, carry):
        for k in range(TOP_K):
            row_copy(t, _slot(pstart_ref, meta_ref, t, k)).start()
        return carry

    def drain(t, carry):
        for k in range(TOP_K):
            row_copy(t, _slot(pstart_ref, meta_ref, t, k)).wait()
        return carry

    lax.fori_loop(0, tm, issue, 0)
    lax.fori_loop(0, tm, drain, 0)


def _dispatch(pstart, h2, meta3, xs):
    t = h2.shape[0]
    tm = ROUTE_TILE
    return pl.pallas_call(
        _dispatch_kernel,
        grid_spec=pltpu.PrefetchScalarGridSpec(
            num_scalar_prefetch=1,
            grid=(t // tm,),
            in_specs=[
                pl.BlockSpec((tm, D_MODEL), lambda i, ps: (i, 0)),
                pl.BlockSpec((1, 1, tm * 2 * TOP_K), lambda i, ps: (i, 0, 0),
                             memory_space=pltpu.SMEM),
                pl.BlockSpec(memory_space=pl.ANY),
            ],
            out_specs=pl.BlockSpec(memory_space=pl.ANY),
            scratch_shapes=[pltpu.SemaphoreType.DMA],
        ),
        out_shape=jax.ShapeDtypeStruct(xs.shape, xs.dtype),
        input_output_aliases={3: 0},
        compiler_params=_params(("arbitrary",)),
        name="dispatch",
    )(pstart, h2, meta3, xs)


def _expert_kernel(be_ref, rb_ref, nu_ref, x_ref, wgu_ref, bgu_ref, wd_ref, bdn_ref, y_ref):
    i = pl.program_id(0)

    @pl.when(i < nu_ref[0])
    def _():
        gu = _dotb(x_ref[...], wgu_ref[...]) + bgu_ref[...]
        x_glu = jnp.minimum(gu[:, :D_EXPERT], SWIGLU_LIMIT)
        x_lin = jnp.clip(gu[:, D_EXPERT:], -SWIGLU_LIMIT, SWIGLU_LIMIT)
        act = x_glu * _sigmoid(SWIGLU_ALPHA * x_glu) * (x_lin + 1.0)
        y_ref[...] = _dotb(act, wd_ref[...]) + bdn_ref[...]

    @pl.when(i >= nu_ref[0])
    def _():
        y_ref[...] = jnp.zeros(y_ref.shape, y_ref.dtype)


def _experts(block_expert, row_block, n_used, xs, wgu, bgu, wd, bdn):
    rows = xs.shape[0]
    bm = EXPERT_ROWS
    nb = rows // bm
    return pl.pallas_call(
        _expert_kernel,
        grid_spec=pltpu.PrefetchScalarGridSpec(
            num_scalar_prefetch=3,
            grid=(nb,),
            in_specs=[
                pl.BlockSpec((bm, D_MODEL), lambda i, be, rb, nu: (rb[i], 0)),
                pl.BlockSpec((None, D_MODEL, 2 * D_EXPERT), lambda i, be, rb, nu: (be[i], 0, 0)),
                pl.BlockSpec((None, 1, 2 * D_EXPERT), lambda i, be, rb, nu: (be[i], 0, 0)),
                pl.BlockSpec((None, D_EXPERT, D_MODEL), lambda i, be, rb, nu: (be[i], 0, 0)),
                pl.BlockSpec((None, 1, D_MODEL), lambda i, be, rb, nu: (be[i], 0, 0)),
            ],
            out_specs=pl.BlockSpec((bm, D_MODEL), lambda i, be, rb, nu: (i, 0)),
        ),
        out_shape=jax.ShapeDtypeStruct((rows, D_MODEL), F32),
        compiler_params=_params(("arbitrary",)),
        name="experts",
    )(block_expert, row_block, n_used, xs, wgu, bgu, wd, bdn)


def _combine_kernel(pstart_ref, x1_ref, meta_ref, gate_ref, yb_ref, gf_ref, y_ref, ybuf, sem):
    tm = x1_ref.shape[0]

    def row_copy(t, k, d):
        return pltpu.make_async_copy(yb_ref.at[pl.ds(d, 1)], ybuf.at[k, pl.ds(t, 1)], sem)

    def issue(t, carry):
        for k in range(TOP_K):
            row_copy(t, k, _slot(pstart_ref, meta_ref, t, k)).start()
        return carry

    def drain(t, carry):
        for k in range(TOP_K):
            row_copy(t, k, _slot(pstart_ref, meta_ref, t, k)).wait()
        return carry

    lax.fori_loop(0, tm, issue, 0)
    lax.fori_loop(0, tm, drain, 0)
    gate = gate_ref[...]
    acc = x1_ref[...]
    for k in range(TOP_K):
        acc = acc + gate[:, k:k + 1] * ybuf[k]
    y_ref[...] = acc * lax.rsqrt(jnp.mean(acc * acc, axis=-1, keepdims=True) + RMS_EPS) * gf_ref[...]


def _combine(pstart, x1, meta3, gate, yb, gfin):
    t = x1.shape[0]
    tm = ROUTE_TILE
    return pl.pallas_call(
        _combine_kernel,
        grid_spec=pltpu.PrefetchScalarGridSpec(
            num_scalar_prefetch=1,
            grid=(t // tm,),
            in_specs=[
                pl.BlockSpec((tm, D_MODEL), lambda i, ps: (i, 0)),
                pl.BlockSpec((1, 1, tm * 2 * TOP_K), lambda i, ps: (i, 0, 0),
                             memory_space=pltpu.SMEM),
                pl.BlockSpec((tm, LANES), lambda i, ps: (i, 0)),
                pl.BlockSpec(memory_space=pl.ANY),
                pl.BlockSpec((1, D_MODEL), lambda i, ps: (0, 0)),
            ],
            out_specs=pl.BlockSpec((tm, D_MODEL), lambda i, ps: (i, 0)),
            scratch_shapes=[pltpu.VMEM((TOP_K, tm, D_MODEL), F32), pltpu.SemaphoreType.DMA],
        ),
        out_shape=jax.ShapeDtypeStruct((t, D_MODEL), F32),
        compiler_params=_params(("arbitrary",)),
        name="combine",
    )(pstart, x1, meta3, gate, yb, gfin)


def _meta_for_smem(meta):
    t = meta.shape[0]
    return meta[:, :2 * TOP_K].reshape(t // ROUTE_TILE, 1, ROUTE_TILE * 2 * TOP_K)


def kernel(x_prompt, x_sample, state_gdn, state_qkv_conv, state_dwconv, norm_mix_g, w_in, w_conv_qkv,
           a_log, dt_bias, w_onorm, b_glu, w_dw, b_dw, ln_g, ln_b, w_out, norm_ffn_g, w_router,
           b_router, w_gate_up, b_gate_up, w_down, b_down, norm_final_g):
    bp, tp, _ = x_prompt.shape
    bs, ts, _ = x_sample.shape
    n_p, n_s = bp * tp, bs * ts

    wi = w_in[0]
    c_z = QKV_COLS
    c_bd = c_z + GDN_WIDTH
    c_glu = c_bd + 2 * GDN_HEADS
    wqkv = wi[:, :c_z].astype(BF16)
    wz = wi[:, c_z:c_bd].astype(BF16)
    wbd = jnp.pad(wi[:, c_bd:c_glu], ((0, 0), (0, LANES - 2 * GDN_HEADS))).astype(BF16)
    wglu = wi[:, c_glu:].astype(BF16)
    bglu = b_glu[0][None, :]
    lane_pad = (GDN_HEADS, LANES - 2 * GDN_HEADS)
    pa = jnp.pad(-jnp.exp(a_log[0].astype(F32)), lane_pad)[None, :]
    pb = jnp.pad(dt_bias[0].astype(F32), lane_pad)[None, :]
    gmix = norm_mix_g[0][None, :]
    wconv = w_conv_qkv[0]
    wonorm = w_onorm[0][None, :]
    wdw, bdw = w_dw[0], b_dw[0][None, :]
    lng, lnb = ln_g[0][None, :], ln_b[0][None, :]
    wo = w_out[0][:GDN_WIDTH].astype(BF16)
    wc = w_out[0][GDN_WIDTH:].astype(BF16)
    gffn = norm_ffn_g[0][None, :]
    wr, br = w_router[0], b_router[0][None, :]
    wgu = w_gate_up[0].astype(BF16)
    bgu = b_gate_up[0][:, None, :]
    wd = w_down[0].astype(BF16)
    bdn = b_down[0][:, None, :]
    gfin = norm_final_g[None, :]

    xp = x_prompt.reshape(n_p, D_MODEL)
    xs_tok = x_sample.reshape(n_s, D_MODEL)

    def mixers(x2d, bsz, t, hist_qkv, s0, hist_glu, chunk, gdn_tile, conv_tile):
        qkv, z, bd, bdt, uu = _inproj(x2d, gmix, wqkv, wz, wbd, wglu, bglu, pa, pb, chunk)
        qkv3 = qkv.reshape(bsz, t, QKV_COLS)
        uu3 = uu.reshape(bsz, t, CONF_WIDTH)
        bdt3 = bdt.reshape(2 * GDN_HEADS, bsz, t).transpose(1, 0, 2)
        o, s_new = _gdn(qkv3, z.reshape(bsz, t, GDN_WIDTH), bd.reshape(bsz, t, 2 * GDN_HEADS),
                        bdt3, hist_qkv, s0, wconv, wonorm, chunk, gdn_tile)
        cc = _cconv(uu3, hist_glu, wdw, bdw, lng, lnb, conv_tile)
        return o.reshape(bsz * t, GDN_WIDTH), cc.reshape(bsz * t, CONF_WIDTH), s_new, qkv3, uu3

    zero_qkv = jnp.zeros((bp, QKV_TAPS - 1, QKV_COLS), F32)
    zero_s = jnp.zeros((bp, GDN_HEADS, HEAD_DIM, HEAD_DIM), F32)
    zero_glu = jnp.zeros((bp, DW_TAPS - 1, CONF_WIDTH), F32)
    o_p, c_p, s_p, qkv_p, uu_p = mixers(xp, bp, tp, zero_qkv, zero_s, zero_glu,
                                        PROMPT_CHUNK, GDN_TILE, CONV_TILE)
    o_s, c_s, s_s, qkv_s, uu_s = mixers(xs_tok, bs, ts, state_qkv_conv[0], state_gdn[0],
                                        state_dwconv[0], ts, ts, ts)

    cnt0 = jnp.zeros((1, N_EXPERTS), F32)
    x1_p, h2_p, meta_p, gate_p, cnt_p = _outproj(o_p, c_p, xp, wo, wc, gffn, wr, br, cnt0)
    x1_s, h2_s, meta_s, gate_s, cnt_all = _outproj(o_s, c_s, xs_tok, wo, wc, gffn, wr, br, cnt_p)

    bm = EXPERT_ROWS
    n_assign = (n_p + n_s) * TOP_K
    n_blocks = -(-n_assign // bm) + N_EXPERTS
    counts = cnt_all[0].astype(I32)
    padded = (counts + bm - 1) // bm * bm
    pend = jnp.cumsum(padded)
    pstart = (pend - padded).astype(I32)
    n_used = (pend[-1] // bm).astype(I32)
    blk = jnp.minimum(jnp.arange(n_blocks, dtype=I32), n_used - 1)
    block_expert = jnp.minimum(
        jnp.searchsorted(pend, blk * bm, side='right'), N_EXPERTS - 1).astype(I32)

    meta3_p = _meta_for_smem(meta_p)
    meta3_s = _meta_for_smem(meta_s)
    xs = jnp.zeros((n_blocks * bm, D_MODEL), F32)
    xs = _dispatch(pstart, h2_p, meta3_p, xs)
    xs = _dispatch(pstart, h2_s, meta3_s, xs)
    yb = _experts(block_expert, blk, n_used.reshape(1), xs, wgu, bgu, wd, bdn)
    y_p = _combine(pstart, x1_p, meta3_p, gate_p, yb, gfin)
    y_s = _combine(pstart, x1_s, meta3_s, gate_s, yb, gfin)

    hist_glu_s = jnp.concatenate([state_dwconv[0], uu_s], axis=1)[:, -(DW_TAPS - 1):]
    return (y_p.reshape(bp, tp, D_MODEL),
            y_s.reshape(bs, ts, D_MODEL),
            s_p[None],
            qkv_p[:, -(QKV_TAPS - 1):][None],
            uu_p[:, -(DW_TAPS - 1):][None],
            s_s[None],
            qkv_s[:, -(QKV_TAPS - 1):][None],
            hist_glu_s[None])
```

```python
import functools
import math

import jax
import jax.numpy as jnp
from jax import lax
from jax.experimental import pallas as pl
from jax.experimental.pallas import tpu as pltpu

F32 = jnp.float32
BF16 = jnp.bfloat16
I32 = jnp.int32
HIGHEST = lax.Precision.HIGHEST

D_MODEL = 1024
GDN_HEADS = 4
HEAD_DIM = 128
GDN_WIDTH = GDN_HEADS * HEAD_DIM
QKV_COLS = 3 * GDN_WIDTH
CONF_WIDTH = 512
QKV_TAPS = 4
DW_TAPS = 31
N_EXPERTS = 32
TOP_K = 4
D_EXPERT = 1024
SWIGLU_LIMIT = 7.0
SWIGLU_ALPHA = 1.702
RMS_EPS = 1e-6
LN_EPS = 1e-5
L2_EPS = 1e-6
PROMPT_CHUNK = 64

LANES = 128
SUBLANES = 8
TOKEN_TILE = 512
GDN_TILE = 256
GDN_SAMPLE_ROWS = 4
CONV_TILE = 256
CONV_ROWS = 32
ROUTE_TILE = 256
EXPERT_ROWS = 256
VMEM_LIMIT = 48 * 1024 * 1024


def _dotb(a, b):
    return jnp.dot(a.astype(BF16), b.astype(BF16), preferred_element_type=F32)


def _dotf(a, b):
    return jnp.dot(a, b, precision=HIGHEST, preferred_element_type=F32)


def _sigmoid(x):
    return jax.nn.sigmoid(x)


def _params(sem):
    return pltpu.CompilerParams(dimension_semantics=sem, vmem_limit_bytes=VMEM_LIMIT)


def _inproj_kernel(chunk_shift, x_ref, g_ref, wqkv_ref, wz_ref, wbd_ref, wglu_ref, bglu_ref,
                   pa_ref, pb_ref, qkv_ref, z_ref, bd_ref, bdt_ref, uu_ref):
    x = x_ref[...]
    tm = x.shape[0]
    ms = jnp.mean(x * x, axis=-1, keepdims=True)
    h = (x * lax.rsqrt(ms + RMS_EPS) * g_ref[...]).astype(BF16)
    qkv_ref[...] = jnp.dot(h, wqkv_ref[...], preferred_element_type=F32)
    z_ref[...] = jnp.dot(h, wz_ref[...], preferred_element_type=F32)
    glu = jnp.dot(h, wglu_ref[...], preferred_element_type=F32) + bglu_ref[...]
    uu_ref[...] = glu[:, :CONF_WIDTH] * _sigmoid(glu[:, CONF_WIDTH:])
    bd = jnp.dot(h, wbd_ref[...], preferred_element_type=F32)
    beta = _sigmoid(bd)
    v = bd + pb_ref[...]
    softplus = jnp.maximum(v, 0.0) + jnp.log1p(jnp.exp(-jnp.abs(v)))
    g = pa_ref[...] * softplus
    r = lax.broadcasted_iota(I32, (tm, tm), 0)
    c = lax.broadcasted_iota(I32, (tm, tm), 1)
    tri = ((r >= c) & ((r >> chunk_shift) == (c >> chunk_shift))).astype(F32)
    gc = _dotf(tri, g)
    lane = lax.broadcasted_iota(I32, (tm, LANES), 1)
    res = jnp.where(lane < GDN_HEADS, beta, gc)
    bd_ref[...] = res[:, :2 * GDN_HEADS]
    bdt_ref[...] = res.T[:2 * GDN_HEADS, :]


def _inproj(x, g, wqkv, wz, wbd, wglu, bglu, pa, pb, chunk):
    t = x.shape[0]
    tm = TOKEN_TILE
    full = lambda shape: pl.BlockSpec(shape, lambda i: (0, 0))
    return pl.pallas_call(
        functools.partial(_inproj_kernel, int(math.log2(chunk))),
        grid=(t // tm,),
        in_specs=[
            pl.BlockSpec((tm, D_MODEL), lambda i: (i, 0)),
            full((1, D_MODEL)),
            full((D_MODEL, QKV_COLS)),
            full((D_MODEL, GDN_WIDTH)),
            full((D_MODEL, LANES)),
            full((D_MODEL, 2 * CONF_WIDTH)),
            full((1, 2 * CONF_WIDTH)),
            full((1, LANES)),
            full((1, LANES)),
        ],
        out_specs=[
            pl.BlockSpec((tm, QKV_COLS), lambda i: (i, 0)),
            pl.BlockSpec((tm, GDN_WIDTH), lambda i: (i, 0)),
            pl.BlockSpec((tm, 2 * GDN_HEADS), lambda i: (i, 0)),
            pl.BlockSpec((2 * GDN_HEADS, tm), lambda i: (0, i)),
            pl.BlockSpec((tm, CONF_WIDTH), lambda i: (i, 0)),
        ],
        out_shape=[
            jax.ShapeDtypeStruct((t, QKV_COLS), F32),
            jax.ShapeDtypeStruct((t, GDN_WIDTH), F32),
            jax.ShapeDtypeStruct((t, 2 * GDN_HEADS), F32),
            jax.ShapeDtypeStruct((2 * GDN_HEADS, t), F32),
            jax.ShapeDtypeStruct((t, CONF_WIDTH), F32),
        ],
        compiler_params=_params(("arbitrary",)),
        name="inproj",
    )(x, g, wqkv, wz, wbd, wglu, bglu, pa, pb)


def _split_bf16(a):
    hi = a.astype(BF16)
    lo = (a - hi.astype(F32)).astype(BF16)
    return hi, lo


def _dot3(a_hi, a_lo, b_hi, b_lo):
    dot = functools.partial(jnp.dot, preferred_element_type=F32)
    return dot(a_hi, b_hi) + dot(a_lo, b_hi) + dot(a_hi, b_lo)


def _unit_lower_inverses(ms, eye, n):
    levels = int(math.log2(n)) - 1
    ps = [eye - m for m in ms]
    pw = [_split_bf16(-m) for m in ms]
    pw = [_split_bf16(_dot3(h, l, h, l)) for h, l in pw]
    for level in range(1, levels + 1):
        last = level == levels
        nxt = []
        for u, (h, l) in enumerate(pw):
            ph, pl_ = _split_bf16(ps[u])
            if last:
                ps[u] = ps[u] + _dot3(ph, pl_, h, l)
            else:
                out = _dot3(jnp.concatenate([h, ph], axis=0), jnp.concatenate([l, pl_], axis=0), h, l)
                nxt.append(_split_bf16(out[:n]))
                ps[u] = ps[u] + out[n:]
        pw = nxt
    return ps


def _gdn_kernel(chunk, x_ref, z_ref, bd_ref, bdt_ref, hist_ref, s0_ref, wc_ref, won_ref,
                o_ref, s_ref, xbuf):
    i = pl.program_id(1)
    nb, tt = x_ref.shape[0], x_ref.shape[1]
    nchunk = tt // chunk
    pad = SUBLANES
    hist_rows = QKV_TAPS - 1

    @pl.when(i == 0)
    def _():
        xbuf[:, pad - hist_rows:pad, :] = hist_ref[...]
        s_ref[...] = s0_ref[...]

    xbuf[:, pad:pad + tt, :] = x_ref[...]

    ii = lax.broadcasted_iota(I32, (chunk, chunk), 0)
    jj = lax.broadcasted_iota(I32, (chunk, chunk), 1)
    lower_incl = ii >= jj
    lower_strict = ii > jj
    eye = (ii == jj).astype(F32)
    won = won_ref[...]

    def conv_silu(b, col0):
        cols = slice(col0, col0 + HEAD_DIM)
        acc = wc_ref[hist_rows:hist_rows + 1, cols] * xbuf[b, pad:pad + tt, cols]
        for j in range(hist_rows):
            r0 = pad - hist_rows + j
            acc = acc + wc_ref[j:j + 1, cols] * xbuf[b, r0:r0 + tt, cols]
        return acc * _sigmoid(acc)

    units = []
    for b in range(nb):
        bd = bd_ref[b]
        bdt = bdt_ref[b]
        for h in range(GDN_HEADS):
            q = conv_silu(b, h * HEAD_DIM)
            k = conv_silu(b, GDN_WIDTH + h * HEAD_DIM)
            v = conv_silu(b, 2 * GDN_WIDTH + h * HEAD_DIM)
            q = q * lax.rsqrt(jnp.sum(q * q, axis=-1, keepdims=True) + L2_EPS) * (HEAD_DIM ** -0.5)
            k = k * lax.rsqrt(jnp.sum(k * k, axis=-1, keepdims=True) + L2_EPS)
            kt = k.T
            beta = bd[:, h:h + 1]
            gcol = bd[:, GDN_HEADS + h:GDN_HEADS + h + 1]
            grow = bdt[GDN_HEADS + h:GDN_HEADS + h + 1, :]
            for c in range(nchunk):
                rows = slice(c * chunk, (c + 1) * chunk)
                qc, kc, vc, ktc = q[rows], k[rows], v[rows], kt[:, rows]
                bc, gc, gr = beta[rows], gcol[rows], grow[:, rows]
                dec = jnp.exp(jnp.where(lower_incl, gc - gr, -jnp.inf))
                kb = kc * bc
                eg = jnp.exp(gc)
                both = _dotb(jnp.concatenate([kb, qc], axis=0), ktc)
                units.append(dict(
                    b=b, h=h, c=c,
                    m=both[:chunk] * jnp.where(lower_strict, dec, 0.0),
                    qk=both[chunk:] * dec,
                    rhs=jnp.concatenate([vc * bc, kb * eg], axis=1),
                    qg=qc * eg,
                    kdt=ktc * jnp.exp(gr[:, chunk - 1:chunk] - gr),
                    d_last=jnp.exp(gc[chunk - 1:chunk, :]),
                ))

    tinvs = _unit_lower_inverses([u["m"] for u in units], eye, chunk)
    for u, tinv in zip(units, tinvs):
        th, tl = _split_bf16(tinv)
        rh, rl = _split_bf16(u["rhs"])
        sol = _dot3(th, tl, rh, rl)
        u["u"], u["w"] = sol[:, :HEAD_DIM], sol[:, HEAD_DIM:]

    state = {(b, h): s_ref[b, h] for b in range(nb) for h in range(GDN_HEADS)}
    by_key = {(u["b"], u["h"], u["c"]): u for u in units}
    for c in range(nchunk):
        rows = slice(c * chunk, (c + 1) * chunk)
        for b in range(nb):
            for h in range(GDN_HEADS):
                u = by_key[(b, h, c)]
                s = state[(b, h)]
                ws = _dotb(jnp.concatenate([u["w"], u["qg"]], axis=0), s)
                v_new = u["u"] - ws[:chunk]
                upd = _dotb(jnp.concatenate([u["qk"], u["kdt"]], axis=0), v_new)
                o_c = ws[chunk:] + upd[:chunk]
                state[(b, h)] = s * u["d_last"] + upd[chunk:]
                cols = slice(h * HEAD_DIM, (h + 1) * HEAD_DIM)
                on = o_c * lax.rsqrt(jnp.mean(o_c * o_c, axis=-1, keepdims=True) + RMS_EPS) * won
                zc = z_ref[b, rows, cols]
                o_ref[b, rows, cols] = on * (zc * _sigmoid(zc))
    for (b, h), s in state.items():
        s_ref[b, h] = s

    xbuf[:, pad - hist_rows:pad, :] = x_ref[:, tt - hist_rows:tt, :]


def _gdn(qkv, z, bd, bdt, hist, s0, wconv, wonorm, chunk, tt, nb):
    b, t, _ = qkv.shape
    nt = t // tt
    return pl.pallas_call(
        functools.partial(_gdn_kernel, chunk),
        grid=(b // nb, nt),
        in_specs=[
            pl.BlockSpec((nb, tt, QKV_COLS), lambda bi, i: (bi, i, 0)),
            pl.BlockSpec((nb, tt, GDN_WIDTH), lambda bi, i: (bi, i, 0)),
            pl.BlockSpec((nb, tt, 2 * GDN_HEADS), lambda bi, i: (bi, i, 0)),
            pl.BlockSpec((nb, 2 * GDN_HEADS, tt), lambda bi, i: (bi, 0, i)),
            pl.BlockSpec((nb, QKV_TAPS - 1, QKV_COLS), lambda bi, i: (bi, 0, 0)),
            pl.BlockSpec((nb, GDN_HEADS, HEAD_DIM, HEAD_DIM), lambda bi, i: (bi, 0, 0, 0)),
            pl.BlockSpec((QKV_TAPS, QKV_COLS), lambda bi, i: (0, 0)),
            pl.BlockSpec((1, HEAD_DIM), lambda bi, i: (0, 0)),
        ],
        out_specs=[
            pl.BlockSpec((nb, tt, GDN_WIDTH), lambda bi, i: (bi, i, 0)),
            pl.BlockSpec((nb, GDN_HEADS, HEAD_DIM, HEAD_DIM), lambda bi, i: (bi, 0, 0, 0)),
        ],
        out_shape=[
            jax.ShapeDtypeStruct((b, t, GDN_WIDTH), F32),
            jax.ShapeDtypeStruct((b, GDN_HEADS, HEAD_DIM, HEAD_DIM), F32),
        ],
        scratch_shapes=[pltpu.VMEM((nb, tt + SUBLANES, QKV_COLS), F32)],
        compiler_params=_params(("arbitrary", "arbitrary")),
        name="gdn",
    )(qkv, z, bd, bdt, hist, s0, wconv, wonorm)


def _cconv_kernel(carry, u_ref, hist_ref, w_ref, b_ref, lg_ref, lb_ref, c_ref, ubuf):
    i = pl.program_id(1)
    tt = u_ref.shape[0]
    hist_rows = DW_TAPS - 1
    pad = 4 * SUBLANES
    base = pad - hist_rows

    @pl.when(i == 0)
    def _():
        ubuf[base:pad, :] = hist_ref[...]

    ubuf[pad:pad + tt, :] = u_ref[...]
    rows = min(CONV_ROWS, tt)
    for r0 in range(0, tt, rows):
        acc = w_ref[0:1, :] * ubuf[base + r0:base + r0 + rows, :]
        for j in range(1, DW_TAPS):
            acc = acc + w_ref[j:j + 1, :] * ubuf[base + r0 + j:base + r0 + j + rows, :]
        cv = acc + b_ref[...]
        mu = jnp.mean(cv, axis=-1, keepdims=True)
        xc = cv - mu
        var = jnp.mean(xc * xc, axis=-1, keepdims=True)
        y = xc * lax.rsqrt(var + LN_EPS) * lg_ref[...] + lb_ref[...]
        c_ref[r0:r0 + rows, :] = y * _sigmoid(y)
    if carry:
        ubuf[base:pad, :] = u_ref[tt - hist_rows:tt, :]


def _cconv(uu, hist, w, b, lg, lb, tt):
    bsz, t, _ = uu.shape
    nt = t // tt
    vec = pl.BlockSpec((1, CONF_WIDTH), lambda bi, i: (0, 0))
    return pl.pallas_call(
        functools.partial(_cconv_kernel, nt > 1),
        grid=(bsz, nt),
        in_specs=[
            pl.BlockSpec((None, tt, CONF_WIDTH), lambda bi, i: (bi, i, 0)),
            pl.BlockSpec((None, DW_TAPS - 1, CONF_WIDTH), lambda bi, i: (bi, 0, 0)),
            pl.BlockSpec((DW_TAPS, CONF_WIDTH), lambda bi, i: (0, 0)),
            vec, vec, vec,
        ],
        out_specs=pl.BlockSpec((None, tt, CONF_WIDTH), lambda bi, i: (bi, i, 0)),
        out_shape=jax.ShapeDtypeStruct((bsz, t, CONF_WIDTH), F32),
        scratch_shapes=[pltpu.VMEM((tt + 4 * SUBLANES, CONF_WIDTH), F32)],
        compiler_params=_params(("arbitrary", "arbitrary")),
        name="cconv",
    )(uu, hist, w, b, lg, lb)


def _outproj_kernel(o_ref, c_ref, x_ref, wo_ref, wc_ref, g_ref, wr_ref, br_ref, cnt0_ref,
                    x1_ref, h2_ref, meta_ref, gate_ref, cnt_ref, run_ref):
    i = pl.program_id(0)
    tm = x_ref.shape[0]

    @pl.when(i == 0)
    def _():
        run_ref[...] = cnt0_ref[...]

    mix = _dotb(o_ref[...], wo_ref[...]) + _dotb(c_ref[...], wc_ref[...])
    x1 = x_ref[...] + mix
    x1_ref[...] = x1
    h2 = x1 * lax.rsqrt(jnp.mean(x1 * x1, axis=-1, keepdims=True) + RMS_EPS) * g_ref[...]
    h2_ref[...] = h2
    logits = _dotf(h2, wr_ref[...]) + br_ref[...]
    lane = lax.broadcasted_iota(I32, (tm, N_EXPERTS), 1)
    vals = logits
    idxs, tops = [], []
    for _ in range(TOP_K):
        m = jnp.max(vals, axis=-1, keepdims=True)
        idx = jnp.min(jnp.where(vals == m, lane, N_EXPERTS), axis=-1, keepdims=True)
        idxs.append(idx)
        tops.append(m)
        vals = jnp.where(lane == idx, -jnp.inf, vals)
    exps = [jnp.exp(v - tops[0]) for v in tops]
    den = exps[0] + exps[1] + exps[2] + exps[3]
    onehots = [(lane == idx).astype(F32) for idx in idxs]
    chosen = onehots[0] + onehots[1] + onehots[2] + onehots[3]
    r = lax.broadcasted_iota(I32, (tm, tm), 0)
    c = lax.broadcasted_iota(I32, (tm, tm), 1)
    before = jnp.dot((r > c).astype(BF16), chosen.astype(BF16), preferred_element_type=F32)
    before = before + run_ref[...]
    lane_out = lax.broadcasted_iota(I32, (tm, LANES), 1)
    meta = jnp.zeros((tm, LANES), I32)
    gate = jnp.zeros((tm, LANES), F32)
    for k in range(TOP_K):
        pos = jnp.sum(onehots[k] * before, axis=-1, keepdims=True).astype(I32)
        meta = jnp.where(lane_out == k, idxs[k], meta)
        meta = jnp.where(lane_out == TOP_K + k, pos, meta)
        gate = jnp.where(lane_out == k, exps[k] / den, gate)
    meta_ref[...] = meta
    gate_ref[...] = gate
    run_ref[...] = run_ref[...] + jnp.sum(chosen, axis=0, keepdims=True)
    cnt_ref[...] = run_ref[...]


def _outproj(o, c, x, wo, wc, g, wr, br, cnt0):
    t = x.shape[0]
    tm = TOKEN_TILE
    full = lambda shape: pl.BlockSpec(shape, lambda i: (0, 0))
    return pl.pallas_call(
        _outproj_kernel,
        grid=(t // tm,),
        in_specs=[
            pl.BlockSpec((tm, GDN_WIDTH), lambda i: (i, 0)),
            pl.BlockSpec((tm, CONF_WIDTH), lambda i: (i, 0)),
            pl.BlockSpec((tm, D_MODEL), lambda i: (i, 0)),
            full((GDN_WIDTH, D_MODEL)),
            full((CONF_WIDTH, D_MODEL)),
            full((1, D_MODEL)),
            full((D_MODEL, N_EXPERTS)),
            full((1, N_EXPERTS)),
            full((1, N_EXPERTS)),
        ],
        out_specs=[
            pl.BlockSpec((tm, D_MODEL), lambda i: (i, 0)),
            pl.BlockSpec((tm, D_MODEL), lambda i: (i, 0)),
            pl.BlockSpec((tm, LANES), lambda i: (i, 0)),
            pl.BlockSpec((tm, LANES), lambda i: (i, 0)),
            full((1, N_EXPERTS)),
        ],
        out_shape=[
            jax.ShapeDtypeStruct((t, D_MODEL), F32),
            jax.ShapeDtypeStruct((t, D_MODEL), F32),
            jax.ShapeDtypeStruct((t, LANES), I32),
            jax.ShapeDtypeStruct((t, LANES), F32),
            jax.ShapeDtypeStruct((1, N_EXPERTS), F32),
        ],
        scratch_shapes=[pltpu.VMEM((1, N_EXPERTS), F32)],
        compiler_params=_params(("arbitrary",)),
        name="outproj_router",
    )(o, c, x, wo, wc, g, wr, br, cnt0)


def _slot(pstart_ref, meta_ref, t, k):
    e = meta_ref[0, 0, t * 2 * TOP_K + k]
    p = meta_ref[0, 0, t * 2 * TOP_K + TOP_K + k]
    return pstart_ref[e] + p


def _dispatch_kernel(pstart_ref, h_ref, meta_ref, xs_in_ref, xs_ref, sem):
    del xs_in_ref
    tm = h_ref.shape[0]

    def row_copy(t, d):
        return pltpu.make_async_copy(h_ref.at[pl.ds(t, 1)], xs_ref.at[pl.ds(d, 1)], sem)

    def issue(t, carry):
        for k in range(TOP_K):
            row_copy(t, _slot(pstart_ref, meta_ref, t, k)).start()
        return carry

    def drain(t, carry):
        for k in range(TOP_K):
            row_copy(t, _slot(pstart_ref, meta_ref, t, k)).wait()
        return carry

    lax.fori_loop(0, tm, issue, 0)
    lax.fori_loop(0, tm, drain, 0)


def _dispatch(pstart, h2, meta3, xs):
    t = h2.shape[0]
    tm = ROUTE_TILE
    return pl.pallas_call(
        _dispatch_kernel,
        grid_spec=pltpu.PrefetchScalarGridSpec(
            num_scalar_prefetch=1,
            grid=(t // tm,),
            in_specs=[
                pl.BlockSpec((tm, D_MODEL), lambda i, ps: (i, 0)),
                pl.BlockSpec((1, 1, tm * 2 * TOP_K), lambda i, ps: (i, 0, 0),
                             memory_space=pltpu.SMEM),
                pl.BlockSpec(memory_space=pl.ANY),
            ],
            out_specs=pl.BlockSpec(memory_space=pl.ANY),
            scratch_shapes=[pltpu.SemaphoreType.DMA],
        ),
        out_shape=jax.ShapeDtypeStruct(xs.shape, xs.dtype),
        input_output_aliases={3: 0},
        compiler_params=_params(("arbitrary",)),
        name="dispatch",
    )(pstart, h2, meta3, xs)


def _expert_kernel(be_ref, rb_ref, nu_ref, x_ref, wgu_ref, bgu_ref, wd_ref, bdn_ref, y_ref):
    i = pl.program_id(0)

    @pl.when(i < nu_ref[0])
    def _():
        gu = _dotb(x_ref[...], wgu_ref[...]) + bgu_ref[...]
        x_glu = jnp.minimum(gu[:, :D_EXPERT], SWIGLU_LIMIT)
        x_lin = jnp.clip(gu[:, D_EXPERT:], -SWIGLU_LIMIT, SWIGLU_LIMIT)
        act = x_glu * _sigmoid(SWIGLU_ALPHA * x_glu) * (x_lin + 1.0)
        y_ref[...] = _dotb(act, wd_ref[...]) + bdn_ref[...]

    @pl.when(i >= nu_ref[0])
    def _():
        y_ref[...] = jnp.zeros(y_ref.shape, y_ref.dtype)


def _experts(block_expert, row_block, n_used, xs, wgu, bgu, wd, bdn):
    rows = xs.shape[0]
    bm = EXPERT_ROWS
    nb = rows // bm
    return pl.pallas_call(
        _expert_kernel,
        grid_spec=pltpu.PrefetchScalarGridSpec(
            num_scalar_prefetch=3,
            grid=(nb,),
            in_specs=[
                pl.BlockSpec((bm, D_MODEL), lambda i, be, rb, nu: (rb[i], 0)),
                pl.BlockSpec((None, D_MODEL, 2 * D_EXPERT), lambda i, be, rb, nu: (be[i], 0, 0)),
                pl.BlockSpec((None, 1, 2 * D_EXPERT), lambda i, be, rb, nu: (be[i], 0, 0)),
                pl.BlockSpec((None, D_EXPERT, D_MODEL), lambda i, be, rb, nu: (be[i], 0, 0)),
                pl.BlockSpec((None, 1, D_MODEL), lambda i, be, rb, nu: (be[i], 0, 0)),
            ],
            out_specs=pl.BlockSpec((bm, D_MODEL), lambda i, be, rb, nu: (i, 0)),
        ),
        out_shape=jax.ShapeDtypeStruct((rows, D_MODEL), F32),
        compiler_params=_params(("arbitrary",)),
        name="experts",
    )(block_expert, row_block, n_used, xs, wgu, bgu, wd, bdn)


def _combine_kernel(pstart_ref, x1_ref, meta_ref, gate_ref, yb_ref, gf_ref, y_ref, ybuf, sem):
    tm = x1_ref.shape[0]

    def row_copy(t, k, d):
        return pltpu.make_async_copy(yb_ref.at[pl.ds(d, 1)], ybuf.at[k, pl.ds(t, 1)], sem)

    def issue(t, carry):
        for k in range(TOP_K):
            row_copy(t, k, _slot(pstart_ref, meta_ref, t, k)).start()
        return carry

    def drain(t, carry):
        for k in range(TOP_K):
            row_copy(t, k, _slot(pstart_ref, meta_ref, t, k)).wait()
        return carry

    lax.fori_loop(0, tm, issue, 0)
    lax.fori_loop(0, tm, drain, 0)
    gate = gate_ref[...]
    acc = x1_ref[...]
    for k in range(TOP_K):
        acc = acc + gate[:, k:k + 1] * ybuf[k]
    y_ref[...] = acc * lax.rsqrt(jnp.mean(acc * acc, axis=-1, keepdims=True) + RMS_EPS) * gf_ref[...]


def _combine(pstart, x1, meta3, gate, yb, gfin):
    t = x1.shape[0]
    tm = ROUTE_TILE
    return pl.pallas_call(
        _combine_kernel,
        grid_spec=pltpu.PrefetchScalarGridSpec(
            num_scalar_prefetch=1,
            grid=(t // tm,),
            in_specs=[
                pl.BlockSpec((tm, D_MODEL), lambda i, ps: (i, 0)),
                pl.BlockSpec((1, 1, tm * 2 * TOP_K), lambda i, ps: (i, 0, 0),
                             memory_space=pltpu.SMEM),
                pl.BlockSpec((tm, LANES), lambda i, ps: (i, 0)),
                pl.BlockSpec(memory_space=pl.ANY),
                pl.BlockSpec((1, D_MODEL), lambda i, ps: (0, 0)),
            ],
            out_specs=pl.BlockSpec((tm, D_MODEL), lambda i, ps: (i, 0)),
            scratch_shapes=[pltpu.VMEM((TOP_K, tm, D_MODEL), F32), pltpu.SemaphoreType.DMA],
        ),
        out_shape=jax.ShapeDtypeStruct((t, D_MODEL), F32),
        compiler_params=_params(("arbitrary",)),
        name="combine",
    )(pstart, x1, meta3, gate, yb, gfin)


def _meta_for_smem(meta):
    t = meta.shape[0]
    return meta[:, :2 * TOP_K].reshape(t // ROUTE_TILE, 1, ROUTE_TILE * 2 * TOP_K)


def kernel(x_prompt, x_sample, state_gdn, state_qkv_conv, state_dwconv, norm_mix_g, w_in, w_conv_qkv,
           a_log, dt_bias, w_onorm, b_glu, w_dw, b_dw, ln_g, ln_b, w_out, norm_ffn_g, w_router,
           b_router, w_gate_up, b_gate_up, w_down, b_down, norm_final_g):
    bp, tp, _ = x_prompt.shape
    bs, ts, _ = x_sample.shape
    n_p, n_s = bp * tp, bs * ts

    wi = w_in[0]
    c_z = QKV_COLS
    c_bd = c_z + GDN_WIDTH
    c_glu = c_bd + 2 * GDN_HEADS
    wqkv = wi[:, :c_z].astype(BF16)
    wz = wi[:, c_z:c_bd].astype(BF16)
    wbd = jnp.pad(wi[:, c_bd:c_glu], ((0, 0), (0, LANES - 2 * GDN_HEADS))).astype(BF16)
    wglu = wi[:, c_glu:].astype(BF16)
    bglu = b_glu[0][None, :]
    lane_pad = (GDN_HEADS, LANES - 2 * GDN_HEADS)
    pa = jnp.pad(-jnp.exp(a_log[0].astype(F32)), lane_pad)[None, :]
    pb = jnp.pad(dt_bias[0].astype(F32), lane_pad)[None, :]
    gmix = norm_mix_g[0][None, :]
    wconv = w_conv_qkv[0]
    wonorm = w_onorm[0][None, :]
    wdw, bdw = w_dw[0], b_dw[0][None, :]
    lng, lnb = ln_g[0][None, :], ln_b[0][None, :]
    wo = w_out[0][:GDN_WIDTH].astype(BF16)
    wc = w_out[0][GDN_WIDTH:].astype(BF16)
    gffn = norm_ffn_g[0][None, :]
    wr, br = w_router[0], b_router[0][None, :]
    wgu = w_gate_up[0].astype(BF16)
    bgu = b_gate_up[0][:, None, :]
    wd = w_down[0].astype(BF16)
    bdn = b_down[0][:, None, :]
    gfin = norm_final_g[None, :]

    xp = x_prompt.reshape(n_p, D_MODEL)
    xs_tok = x_sample.reshape(n_s, D_MODEL)

    def mixers(x2d, bsz, t, hist_qkv, s0, hist_glu, chunk, gdn_tile, gdn_rows, conv_tile):
        qkv, z, bd, bdt, uu = _inproj(x2d, gmix, wqkv, wz, wbd, wglu, bglu, pa, pb, chunk)
        qkv3 = qkv.reshape(bsz, t, QKV_COLS)
        uu3 = uu.reshape(bsz, t, CONF_WIDTH)
        bdt3 = bdt.reshape(2 * GDN_HEADS, bsz, t).transpose(1, 0, 2)
        o, s_new = _gdn(qkv3, z.reshape(bsz, t, GDN_WIDTH), bd.reshape(bsz, t, 2 * GDN_HEADS),
                        bdt3, hist_qkv, s0, wconv, wonorm, chunk, gdn_tile, gdn_rows)
        cc = _cconv(uu3, hist_glu, wdw, bdw, lng, lnb, conv_tile)
        return o.reshape(bsz * t, GDN_WIDTH), cc.reshape(bsz * t, CONF_WIDTH), s_new, qkv3, uu3

    zero_qkv = jnp.zeros((bp, QKV_TAPS - 1, QKV_COLS), F32)
    zero_s = jnp.zeros((bp, GDN_HEADS, HEAD_DIM, HEAD_DIM), F32)
    zero_glu = jnp.zeros((bp, DW_TAPS - 1, CONF_WIDTH), F32)
    o_p, c_p, s_p, qkv_p, uu_p = mixers(xp, bp, tp, zero_qkv, zero_s, zero_glu,
                                        PROMPT_CHUNK, GDN_TILE, 1, CONV_TILE)
    o_s, c_s, s_s, qkv_s, uu_s = mixers(xs_tok, bs, ts, state_qkv_conv[0], state_gdn[0],
                                        state_dwconv[0], ts, ts, GDN_SAMPLE_ROWS, ts)

    cnt0 = jnp.zeros((1, N_EXPERTS), F32)
    x1_p, h2_p, meta_p, gate_p, cnt_p = _outproj(o_p, c_p, xp, wo, wc, gffn, wr, br, cnt0)
    x1_s, h2_s, meta_s, gate_s, cnt_all = _outproj(o_s, c_s, xs_tok, wo, wc, gffn, wr, br, cnt_p)

    bm = EXPERT_ROWS
    n_assign = (n_p + n_s) * TOP_K
    n_blocks = -(-n_assign // bm) + N_EXPERTS
    counts = cnt_all[0].astype(I32)
    padded = (counts + bm - 1) // bm * bm
    pend = jnp.cumsum(padded)
    pstart = (pend - padded).astype(I32)
    n_used = (pend[-1] // bm).astype(I32)
    blk = jnp.minimum(jnp.arange(n_blocks, dtype=I32), n_used - 1)
    block_expert = jnp.minimum(
        jnp.sum((pend[None, :] <= (blk * bm)[:, None]).astype(I32), axis=1), N_EXPERTS - 1)

    meta3_p = _meta_for_smem(meta_p)
    meta3_s = _meta_for_smem(meta_s)
    xs = jnp.zeros((n_blocks * bm, D_MODEL), F32)
    xs = _dispatch(pstart, h2_p, meta3_p, xs)
    xs = _dispatch(pstart, h2_s, meta3_s, xs)
    yb = _experts(block_expert, blk, n_used.reshape(1), xs, wgu, bgu, wd, bdn)
    y_p = _combine(pstart, x1_p, meta3_p, gate_p, yb, gfin)
    y_s = _combine(pstart, x1_s, meta3_s, gate_s, yb, gfin)

    hist_glu_s = jnp.concatenate([state_dwconv[0], uu_s], axis=1)[:, -(DW_TAPS - 1):]
    return (y_p.reshape(bp, tp, D_MODEL),
            y_s.reshape(bs, ts, D_MODEL),
            s_p[None],
            qkv_p[:, -(QKV_TAPS - 1):][None],
            uu_p[:, -(DW_TAPS - 1):][None],
            s_s[None],
            qkv_s[:, -(QKV_TAPS - 1):][None],
            hist_glu_s[None])
```

```python
import functools
import math

import jax
import jax.numpy as jnp
from jax import lax
from jax.experimental import pallas as pl
from jax.experimental.pallas import tpu as pltpu

F32 = jnp.float32
BF16 = jnp.bfloat16
I32 = jnp.int32
HIGHEST = lax.Precision.HIGHEST

D_MODEL = 1024
GDN_HEADS = 4
HEAD_DIM = 128
GDN_WIDTH = GDN_HEADS * HEAD_DIM
QKV_COLS = 3 * GDN_WIDTH
CONF_WIDTH = 512
QKV_TAPS = 4
DW_TAPS = 31
N_EXPERTS = 32
TOP_K = 4
D_EXPERT = 1024
SWIGLU_LIMIT = 7.0
SWIGLU_ALPHA = 1.702
RMS_EPS = 1e-6
LN_EPS = 1e-5
L2_EPS = 1e-6
PROMPT_CHUNK = 64

LANES = 128
SUBLANES = 8
TOKEN_TILE = 512
GDN_TILE = 256
GDN_SAMPLE_ROWS = 4
CONV_TILE = 256
CONV_ROWS = 32
EXPERT_ROWS = 256
GRANULE = SUBLANES
GAP_BITS = 5
TILE_ROWS = -(-(TOKEN_TILE * TOP_K + N_EXPERTS * (GRANULE - 1)) // 256) * 256
META_EXPERT, META_RANK, META_GATE, META_ROWS = 0, TOP_K, 2 * TOP_K, 16
assert EXPERT_ROWS // GRANULE <= 1 << GAP_BITS
VMEM_LIMIT = 48 * 1024 * 1024


def _dotb(a, b):
    return jnp.dot(a.astype(BF16), b.astype(BF16), preferred_element_type=F32)


def _dotf(a, b):
    return jnp.dot(a, b, precision=HIGHEST, preferred_element_type=F32)


def _sigmoid(x):
    return jax.nn.sigmoid(x)


def _params(sem):
    return pltpu.CompilerParams(dimension_semantics=sem, vmem_limit_bytes=VMEM_LIMIT)


def _inproj_kernel(chunk_shift, x_ref, g_ref, wqkv_ref, wz_ref, wbd_ref, wglu_ref, bglu_ref,
                   pa_ref, pb_ref, qkv_ref, z_ref, bd_ref, bdt_ref, uu_ref):
    x = x_ref[...]
    tm = x.shape[0]
    ms = jnp.mean(x * x, axis=-1, keepdims=True)
    h = (x * lax.rsqrt(ms + RMS_EPS) * g_ref[...]).astype(BF16)
    qkv_ref[...] = jnp.dot(h, wqkv_ref[...], preferred_element_type=F32)
    z_ref[...] = jnp.dot(h, wz_ref[...], preferred_element_type=F32)
    glu = jnp.dot(h, wglu_ref[...], preferred_element_type=F32) + bglu_ref[...]
    uu_ref[...] = glu[:, :CONF_WIDTH] * _sigmoid(glu[:, CONF_WIDTH:])
    bd = jnp.dot(h, wbd_ref[...], preferred_element_type=F32)
    beta = _sigmoid(bd)
    v = bd + pb_ref[...]
    softplus = jnp.maximum(v, 0.0) + jnp.log1p(jnp.exp(-jnp.abs(v)))
    g = pa_ref[...] * softplus
    r = lax.broadcasted_iota(I32, (tm, tm), 0)
    c = lax.broadcasted_iota(I32, (tm, tm), 1)
    tri = ((r >= c) & ((r >> chunk_shift) == (c >> chunk_shift))).astype(F32)
    gc = _dotf(tri, g)
    lane = lax.broadcasted_iota(I32, (tm, LANES), 1)
    res = jnp.where(lane < GDN_HEADS, beta, gc)
    bd_ref[...] = res[:, :2 * GDN_HEADS]
    bdt_ref[...] = res.T[:2 * GDN_HEADS, :]


def _inproj(x, g, wqkv, wz, wbd, wglu, bglu, pa, pb, chunk):
    t = x.shape[0]
    tm = TOKEN_TILE
    full = lambda shape: pl.BlockSpec(shape, lambda i: (0, 0))
    return pl.pallas_call(
        functools.partial(_inproj_kernel, int(math.log2(chunk))),
        grid=(t // tm,),
        in_specs=[
            pl.BlockSpec((tm, D_MODEL), lambda i: (i, 0)),
            full((1, D_MODEL)),
            full((D_MODEL, QKV_COLS)),
            full((D_MODEL, GDN_WIDTH)),
            full((D_MODEL, LANES)),
            full((D_MODEL, 2 * CONF_WIDTH)),
            full((1, 2 * CONF_WIDTH)),
            full((1, LANES)),
            full((1, LANES)),
        ],
        out_specs=[
            pl.BlockSpec((tm, QKV_COLS), lambda i: (i, 0)),
            pl.BlockSpec((tm, GDN_WIDTH), lambda i: (i, 0)),
            pl.BlockSpec((tm, 2 * GDN_HEADS), lambda i: (i, 0)),
            pl.BlockSpec((2 * GDN_HEADS, tm), lambda i: (0, i)),
            pl.BlockSpec((tm, CONF_WIDTH), lambda i: (i, 0)),
        ],
        out_shape=[
            jax.ShapeDtypeStruct((t, QKV_COLS), F32),
            jax.ShapeDtypeStruct((t, GDN_WIDTH), F32),
            jax.ShapeDtypeStruct((t, 2 * GDN_HEADS), F32),
            jax.ShapeDtypeStruct((2 * GDN_HEADS, t), F32),
            jax.ShapeDtypeStruct((t, CONF_WIDTH), F32),
        ],
        compiler_params=_params(("arbitrary",)),
        name="inproj",
    )(x, g, wqkv, wz, wbd, wglu, bglu, pa, pb)


def _split_bf16(a):
    hi = a.astype(BF16)
    lo = (a - hi.astype(F32)).astype(BF16)
    return hi, lo


def _dot3(a_hi, a_lo, b_hi, b_lo):
    dot = functools.partial(jnp.dot, preferred_element_type=F32)
    return dot(a_hi, b_hi) + dot(a_lo, b_hi) + dot(a_hi, b_lo)


def _unit_lower_inverses(ms, eye, n):
    levels = int(math.log2(n)) - 1
    ps = [eye - m for m in ms]
    pw = [_split_bf16(-m) for m in ms]
    pw = [_split_bf16(_dot3(h, l, h, l)) for h, l in pw]
    for level in range(1, levels + 1):
        last = level == levels
        nxt = []
        for u, (h, l) in enumerate(pw):
            ph, pl_ = _split_bf16(ps[u])
            if last:
                ps[u] = ps[u] + _dot3(ph, pl_, h, l)
            else:
                out = _dot3(jnp.concatenate([h, ph], axis=0), jnp.concatenate([l, pl_], axis=0), h, l)
                nxt.append(_split_bf16(out[:n]))
                ps[u] = ps[u] + out[n:]
        pw = nxt
    return ps


def _gdn_kernel(chunk, x_ref, z_ref, bd_ref, bdt_ref, hist_ref, s0_ref, wc_ref, won_ref,
                o_ref, s_ref, xbuf):
    i = pl.program_id(1)
    nb, tt = x_ref.shape[0], x_ref.shape[1]
    nchunk = tt // chunk
    pad = SUBLANES
    hist_rows = QKV_TAPS - 1

    @pl.when(i == 0)
    def _():
        xbuf[:, pad - hist_rows:pad, :] = hist_ref[...]
        s_ref[...] = s0_ref[...]

    xbuf[:, pad:pad + tt, :] = x_ref[...]

    ii = lax.broadcasted_iota(I32, (chunk, chunk), 0)
    jj = lax.broadcasted_iota(I32, (chunk, chunk), 1)
    lower_incl = ii >= jj
    lower_strict = ii > jj
    eye = (ii == jj).astype(F32)
    won = won_ref[...]

    def conv_silu(b, col0):
        cols = slice(col0, col0 + HEAD_DIM)
        acc = wc_ref[hist_rows:hist_rows + 1, cols] * xbuf[b, pad:pad + tt, cols]
        for j in range(hist_rows):
            r0 = pad - hist_rows + j
            acc = acc + wc_ref[j:j + 1, cols] * xbuf[b, r0:r0 + tt, cols]
        return acc * _sigmoid(acc)

    units = []
    for b in range(nb):
        bd = bd_ref[b]
        bdt = bdt_ref[b]
        for h in range(GDN_HEADS):
            q = conv_silu(b, h * HEAD_DIM)
            k = conv_silu(b, GDN_WIDTH + h * HEAD_DIM)
            v = conv_silu(b, 2 * GDN_WIDTH + h * HEAD_DIM)
            q = q * lax.rsqrt(jnp.sum(q * q, axis=-1, keepdims=True) + L2_EPS) * (HEAD_DIM ** -0.5)
            k = k * lax.rsqrt(jnp.sum(k * k, axis=-1, keepdims=True) + L2_EPS)
            kt = k.T
            beta = bd[:, h:h + 1]
            gcol = bd[:, GDN_HEADS + h:GDN_HEADS + h + 1]
            grow = bdt[GDN_HEADS + h:GDN_HEADS + h + 1, :]
            for c in range(nchunk):
                rows = slice(c * chunk, (c + 1) * chunk)
                qc, kc, vc, ktc = q[rows], k[rows], v[rows], kt[:, rows]
                bc, gc, gr = beta[rows], gcol[rows], grow[:, rows]
                dec = jnp.exp(jnp.where(lower_incl, gc - gr, -jnp.inf))
                kb = kc * bc
                eg = jnp.exp(gc)
                both = _dotb(jnp.concatenate([kb, qc], axis=0), ktc)
                units.append(dict(
                    b=b, h=h, c=c,
                    m=both[:chunk] * jnp.where(lower_strict, dec, 0.0),
                    qk=both[chunk:] * dec,
                    rhs=jnp.concatenate([vc * bc, kb * eg], axis=1),
                    qg=qc * eg,
                    kdt=ktc * jnp.exp(gr[:, chunk - 1:chunk] - gr),
                    d_last=jnp.exp(gc[chunk - 1:chunk, :]),
                ))

    tinvs = _unit_lower_inverses([u["m"] for u in units], eye, chunk)
    for u, tinv in zip(units, tinvs):
        th, tl = _split_bf16(tinv)
        rh, rl = _split_bf16(u["rhs"])
        sol = _dot3(th, tl, rh, rl)
        u["u"], u["w"] = sol[:, :HEAD_DIM], sol[:, HEAD_DIM:]

    state = {(b, h): s_ref[b, h] for b in range(nb) for h in range(GDN_HEADS)}
    by_key = {(u["b"], u["h"], u["c"]): u for u in units}
    for c in range(nchunk):
        rows = slice(c * chunk, (c + 1) * chunk)
        for b in range(nb):
            for h in range(GDN_HEADS):
                u = by_key[(b, h, c)]
                s = state[(b, h)]
                ws = _dotb(jnp.concatenate([u["w"], u["qg"]], axis=0), s)
                v_new = u["u"] - ws[:chunk]
                upd = _dotb(jnp.concatenate([u["qk"], u["kdt"]], axis=0), v_new)
                o_c = ws[chunk:] + upd[:chunk]
                state[(b, h)] = s * u["d_last"] + upd[chunk:]
                cols = slice(h * HEAD_DIM, (h + 1) * HEAD_DIM)
                on = o_c * lax.rsqrt(jnp.mean(o_c * o_c, axis=-1, keepdims=True) + RMS_EPS) * won
                zc = z_ref[b, rows, cols]
                o_ref[b, rows, cols] = on * (zc * _sigmoid(zc))
    for (b, h), s in state.items():
        s_ref[b, h] = s

    xbuf[:, pad - hist_rows:pad, :] = x_ref[:, tt - hist_rows:tt, :]


def _gdn(qkv, z, bd, bdt, hist, s0, wconv, wonorm, chunk, tt, nb):
    b, t, _ = qkv.shape
    nt = t // tt
    return pl.pallas_call(
        functools.partial(_gdn_kernel, chunk),
        grid=(b // nb, nt),
        in_specs=[
            pl.BlockSpec((nb, tt, QKV_COLS), lambda bi, i: (bi, i, 0)),
            pl.BlockSpec((nb, tt, GDN_WIDTH), lambda bi, i: (bi, i, 0)),
            pl.BlockSpec((nb, tt, 2 * GDN_HEADS), lambda bi, i: (bi, i, 0)),
            pl.BlockSpec((nb, 2 * GDN_HEADS, tt), lambda bi, i: (bi, 0, i)),
            pl.BlockSpec((nb, QKV_TAPS - 1, QKV_COLS), lambda bi, i: (bi, 0, 0)),
            pl.BlockSpec((nb, GDN_HEADS, HEAD_DIM, HEAD_DIM), lambda bi, i: (bi, 0, 0, 0)),
            pl.BlockSpec((QKV_TAPS, QKV_COLS), lambda bi, i: (0, 0)),
            pl.BlockSpec((1, HEAD_DIM), lambda bi, i: (0, 0)),
        ],
        out_specs=[
            pl.BlockSpec((nb, tt, GDN_WIDTH), lambda bi, i: (bi, i, 0)),
            pl.BlockSpec((nb, GDN_HEADS, HEAD_DIM, HEAD_DIM), lambda bi, i: (bi, 0, 0, 0)),
        ],
        out_shape=[
            jax.ShapeDtypeStruct((b, t, GDN_WIDTH), F32),
            jax.ShapeDtypeStruct((b, GDN_HEADS, HEAD_DIM, HEAD_DIM), F32),
        ],
        scratch_shapes=[pltpu.VMEM((nb, tt + SUBLANES, QKV_COLS), F32)],
        compiler_params=_params(("arbitrary", "arbitrary")),
        name="gdn",
    )(qkv, z, bd, bdt, hist, s0, wconv, wonorm)


def _cconv_kernel(carry, u_ref, hist_ref, w_ref, b_ref, lg_ref, lb_ref, c_ref, ubuf):
    i = pl.program_id(1)
    tt = u_ref.shape[0]
    hist_rows = DW_TAPS - 1
    pad = 4 * SUBLANES
    base = pad - hist_rows

    @pl.when(i == 0)
    def _():
        ubuf[base:pad, :] = hist_ref[...]

    ubuf[pad:pad + tt, :] = u_ref[...]
    rows = min(CONV_ROWS, tt)
    for r0 in range(0, tt, rows):
        acc = w_ref[0:1, :] * ubuf[base + r0:base + r0 + rows, :]
        for j in range(1, DW_TAPS):
            acc = acc + w_ref[j:j + 1, :] * ubuf[base + r0 + j:base + r0 + j + rows, :]
        cv = acc + b_ref[...]
        mu = jnp.mean(cv, axis=-1, keepdims=True)
        xc = cv - mu
        var = jnp.mean(xc * xc, axis=-1, keepdims=True)
        y = xc * lax.rsqrt(var + LN_EPS) * lg_ref[...] + lb_ref[...]
        c_ref[r0:r0 + rows, :] = y * _sigmoid(y)
    if carry:
        ubuf[base:pad, :] = u_ref[tt - hist_rows:tt, :]


def _cconv(uu, hist, w, b, lg, lb, tt):
    bsz, t, _ = uu.shape
    nt = t // tt
    vec = pl.BlockSpec((1, CONF_WIDTH), lambda bi, i: (0, 0))
    return pl.pallas_call(
        functools.partial(_cconv_kernel, nt > 1),
        grid=(bsz, nt),
        in_specs=[
            pl.BlockSpec((None, tt, CONF_WIDTH), lambda bi, i: (bi, i, 0)),
            pl.BlockSpec((None, DW_TAPS - 1, CONF_WIDTH), lambda bi, i: (bi, 0, 0)),
            pl.BlockSpec((DW_TAPS, CONF_WIDTH), lambda bi, i: (0, 0)),
            vec, vec, vec,
        ],
        out_specs=pl.BlockSpec((None, tt, CONF_WIDTH), lambda bi, i: (bi, i, 0)),
        out_shape=jax.ShapeDtypeStruct((bsz, t, CONF_WIDTH), F32),
        scratch_shapes=[pltpu.VMEM((tt + 4 * SUBLANES, CONF_WIDTH), F32)],
        compiler_params=_params(("arbitrary", "arbitrary")),
        name="cconv",
    )(uu, hist, w, b, lg, lb)


def _outproj_kernel(tiles_a, oa_ref, ca_ref, xa_ref, ob_ref, cb_ref, xb_ref, *rest):
    i = pl.program_id(0)

    @pl.when(i < tiles_a)
    def _():
        _outproj_tile(oa_ref, ca_ref, xa_ref, *rest)

    @pl.when(i >= tiles_a)
    def _():
        _outproj_tile(ob_ref, cb_ref, xb_ref, *rest)


def _outproj_tile(o_ref, c_ref, x_ref, wo_ref, wc_ref, g_ref, wr_ref, br_ref,
                  x1_ref, h2_ref, metac_ref, metar_ref, cnt_ref):
    tm = x_ref.shape[0]

    mix = _dotb(o_ref[...], wo_ref[...]) + _dotb(c_ref[...], wc_ref[...])
    x1 = x_ref[...] + mix
    x1_ref[...] = x1
    h2 = x1 * lax.rsqrt(jnp.mean(x1 * x1, axis=-1, keepdims=True) + RMS_EPS) * g_ref[...]
    h2_ref[...] = h2.astype(BF16)
    logits = _dotf(h2, wr_ref[...]) + br_ref[...]
    lane = lax.broadcasted_iota(I32, (tm, N_EXPERTS), 1)
    vals = logits
    idxs, tops = [], []
    for _ in range(TOP_K):
        m = jnp.max(vals, axis=-1, keepdims=True)
        idx = jnp.min(jnp.where(vals == m, lane, N_EXPERTS), axis=-1, keepdims=True)
        idxs.append(idx)
        tops.append(m)
        vals = jnp.where(lane == idx, -jnp.inf, vals)
    exps = [jnp.exp(v - tops[0]) for v in tops]
    den = exps[0] + exps[1] + exps[2] + exps[3]
    onehots = [(lane == idx).astype(F32) for idx in idxs]
    chosen = onehots[0] + onehots[1] + onehots[2] + onehots[3]
    r = lax.broadcasted_iota(I32, (tm, tm), 0)
    c = lax.broadcasted_iota(I32, (tm, tm), 1)
    before = jnp.dot((r > c).astype(BF16), chosen.astype(BF16), preferred_element_type=F32)
    lane_out = lax.broadcasted_iota(I32, (tm, LANES), 1)
    meta = jnp.zeros((tm, LANES), F32)
    for k in range(TOP_K):
        rank = jnp.sum(onehots[k] * before, axis=-1, keepdims=True)
        meta = jnp.where(lane_out == META_EXPERT + k, idxs[k].astype(F32), meta)
        meta = jnp.where(lane_out == META_RANK + k, rank, meta)
        meta = jnp.where(lane_out == META_GATE + k, exps[k] / den, meta)
    metac_ref[...] = meta
    metar_ref[...] = meta.T[:META_ROWS, :]
    counts = jnp.sum(chosen, axis=0, keepdims=True)
    cnt_ref[...] = jnp.concatenate(
        [counts, jnp.zeros((1, LANES - N_EXPERTS), F32)], axis=1)[None]


def _outproj(first, second, wo, wc, g, wr, br):
    t_a, t_b = first[2].shape[0], second[2].shape[0]
    tm = TOKEN_TILE
    tiles_a = t_a // tm
    t_total = t_a + t_b
    full = lambda shape: pl.BlockSpec(shape, lambda i: (0, 0))
    rows_a = lambda width: pl.BlockSpec((tm, width), lambda i: (jnp.minimum(i, tiles_a - 1), 0))
    rows_b = lambda width: pl.BlockSpec((tm, width), lambda i: (jnp.maximum(i - tiles_a, 0), 0))
    widths = (GDN_WIDTH, CONF_WIDTH, D_MODEL)
    return pl.pallas_call(
        functools.partial(_outproj_kernel, tiles_a),
        grid=(t_total // tm,),
        in_specs=[rows_a(w) for w in widths] + [rows_b(w) for w in widths] + [
            full((GDN_WIDTH, D_MODEL)),
            full((CONF_WIDTH, D_MODEL)),
            full((1, D_MODEL)),
            full((D_MODEL, N_EXPERTS)),
            full((1, N_EXPERTS)),
        ],
        out_specs=[
            pl.BlockSpec((tm, D_MODEL), lambda i: (i, 0)),
            pl.BlockSpec((tm, D_MODEL), lambda i: (i, 0)),
            pl.BlockSpec((tm, LANES), lambda i: (i, 0)),
            pl.BlockSpec((META_ROWS, tm), lambda i: (0, i)),
            pl.BlockSpec((1, 1, LANES), lambda i: (i, 0, 0)),
        ],
        out_shape=[
            jax.ShapeDtypeStruct((t_total, D_MODEL), F32),
            jax.ShapeDtypeStruct((t_total, D_MODEL), BF16),
            jax.ShapeDtypeStruct((t_total, LANES), F32),
            jax.ShapeDtypeStruct((META_ROWS, t_total), F32),
            jax.ShapeDtypeStruct((t_total // tm, 1, LANES), F32),
        ],
        compiler_params=_params(("arbitrary",)),
        name="outproj_router",
    )(*first, *second, wo, wc, g, wr, br)


def _segment_copies(i, seg_local_ref, seg_dst_ref, seg_n_ref, make_copy, wait):
    def per_expert(ex, carry):
        j = i * N_EXPERTS + ex
        local = pl.multiple_of(seg_local_ref[j], GRANULE)
        dst = pl.multiple_of(seg_dst_ref[j], GRANULE)

        def per_granule(g, c):
            off = pl.multiple_of(g * GRANULE, GRANULE)
            cp = make_copy(local + off, dst + off)
            cp.wait() if wait else cp.start()
            return c

        lax.fori_loop(0, seg_n_ref[j], per_granule, 0)
        return carry

    lax.fori_loop(0, N_EXPERTS, per_expert, 0)


def _gap_copies(gap_start_ref, gap_n_ref, make_copy, wait):
    def per_expert(ex, carry):
        n = gap_n_ref[ex]
        start = pl.multiple_of(gap_start_ref[ex], GRANULE)
        off = 0 * n
        for bit in reversed(range(GAP_BITS)):
            rows = GRANULE << bit

            @pl.when(((n >> bit) & 1) == 1)
            def _(rows=rows, off=off):
                cp = make_copy(rows, pl.multiple_of(start + off, GRANULE))
                cp.wait() if wait else cp.start()

            off = off + ((n >> bit) & 1) * rows
        return carry

    lax.fori_loop(0, N_EXPERTS, per_expert, 0)


def _local_rows(meta_e, seg_local_ref, i):
    base = jnp.zeros(meta_e.shape, F32)
    for ex in range(N_EXPERTS):
        base = jnp.where(meta_e == float(ex), seg_local_ref[i * N_EXPERTS + ex].astype(F32), base)
    return base


def _dispatch_kernel(seg_local_ref, seg_dst_ref, seg_n_ref, gap_start_ref, gap_n_ref, tail_ref,
                     h_ref, mr_ref, xs_ref, xloc, zbuf, sem, zsem):
    i = pl.program_id(0)
    tm = h_ref.shape[0]
    rt = xloc.shape[0]
    zrows = zbuf.shape[0]

    def gap_copy(rows, dst):
        return pltpu.make_async_copy(zbuf.at[pl.ds(0, rows)], xs_ref.at[pl.ds(dst, rows)], zsem)

    def tail_copies(wait):
        def body(j, c):
            cp = gap_copy(zrows, pl.multiple_of(tail_ref[0] + j * zrows, zrows))
            cp.wait() if wait else cp.start()
            return c
        lax.fori_loop(0, tail_ref[1], body, 0)

    @pl.when(i == 0)
    def _():
        zbuf[...] = jnp.zeros(zbuf.shape, zbuf.dtype)
        _gap_copies(gap_start_ref, gap_n_ref, gap_copy, wait=False)
        tail_copies(wait=False)

    mr = mr_ref[...]
    dl = (_local_rows(mr[META_EXPERT:META_EXPERT + TOP_K], seg_local_ref, i)
          + mr[META_RANK:META_RANK + TOP_K])
    rr = lax.broadcasted_iota(I32, (rt, tm), 0).astype(F32)
    hit = rr == dl[0:1]
    for k in range(1, TOP_K):
        hit = hit | (rr == dl[k:k + 1])
    xloc[...] = jnp.dot(hit.astype(BF16), h_ref[...], preferred_element_type=F32)

    def seg_copy(local, dst):
        return pltpu.make_async_copy(xloc.at[pl.ds(local, GRANULE)], xs_ref.at[pl.ds(dst, GRANULE)], sem)

    _segment_copies(i, seg_local_ref, seg_dst_ref, seg_n_ref, seg_copy, wait=False)
    _segment_copies(i, seg_local_ref, seg_dst_ref, seg_n_ref, seg_copy, wait=True)

    @pl.when(i == 0)
    def _():
        _gap_copies(gap_start_ref, gap_n_ref, gap_copy, wait=True)
        tail_copies(wait=True)


def _dispatch(tables, h2, metar, n_rows):
    t = h2.shape[0]
    tm = TOKEN_TILE
    return pl.pallas_call(
        _dispatch_kernel,
        grid_spec=pltpu.PrefetchScalarGridSpec(
            num_scalar_prefetch=6,
            grid=(t // tm,),
            in_specs=[
                pl.BlockSpec((tm, D_MODEL), lambda i, *_: (i, 0)),
                pl.BlockSpec((META_ROWS, tm), lambda i, *_: (0, i)),
            ],
            out_specs=pl.BlockSpec(memory_space=pl.ANY),
            scratch_shapes=[
                pltpu.VMEM((TILE_ROWS, D_MODEL), F32),
                pltpu.VMEM((GRANULE << (GAP_BITS - 1), D_MODEL), F32),
                pltpu.SemaphoreType.DMA,
                pltpu.SemaphoreType.DMA,
            ],
        ),
        out_shape=jax.ShapeDtypeStruct((n_rows, D_MODEL), F32),
        compiler_params=_params(("arbitrary",)),
        name="dispatch",
    )(*tables, h2, metar)


def _expert_kernel(be_ref, rb_ref, nu_ref, x_ref, wgu_ref, bgu_ref, wd_ref, bdn_ref, y_ref):
    i = pl.program_id(0)

    @pl.when(i < nu_ref[0])
    def _():
        gu = _dotb(x_ref[...], wgu_ref[...]) + bgu_ref[...]
        x_glu = jnp.minimum(gu[:, :D_EXPERT], SWIGLU_LIMIT)
        x_lin = jnp.clip(gu[:, D_EXPERT:], -SWIGLU_LIMIT, SWIGLU_LIMIT)
        act = x_glu * _sigmoid(SWIGLU_ALPHA * x_glu) * (x_lin + 1.0)
        y_ref[...] = _dotb(act, wd_ref[...]) + bdn_ref[...]

    @pl.when(i >= nu_ref[0])
    def _():
        y_ref[...] = jnp.zeros(y_ref.shape, y_ref.dtype)


def _experts(block_expert, row_block, n_used, xs, wgu, bgu, wd, bdn):
    rows = xs.shape[0]
    bm = EXPERT_ROWS
    nb = rows // bm
    return pl.pallas_call(
        _expert_kernel,
        grid_spec=pltpu.PrefetchScalarGridSpec(
            num_scalar_prefetch=3,
            grid=(nb,),
            in_specs=[
                pl.BlockSpec((bm, D_MODEL), lambda i, be, rb, nu: (rb[i], 0)),
                pl.BlockSpec((None, D_MODEL, 2 * D_EXPERT), lambda i, be, rb, nu: (be[i], 0, 0)),
                pl.BlockSpec((None, 1, 2 * D_EXPERT), lambda i, be, rb, nu: (be[i], 0, 0)),
                pl.BlockSpec((None, D_EXPERT, D_MODEL), lambda i, be, rb, nu: (be[i], 0, 0)),
                pl.BlockSpec((None, 1, D_MODEL), lambda i, be, rb, nu: (be[i], 0, 0)),
            ],
            out_specs=pl.BlockSpec((bm, D_MODEL), lambda i, be, rb, nu: (i, 0)),
        ),
        out_shape=jax.ShapeDtypeStruct((rows, D_MODEL), F32),
        compiler_params=_params(("arbitrary",)),
        name="experts",
    )(block_expert, row_block, n_used, xs, wgu, bgu, wd, bdn)


def _combine_kernel(tiles_a, seg_local_ref, seg_dst_ref, seg_n_ref, x1_ref, mc_ref, yb_ref, gf_ref,
                    ya_ref, yb_out_ref, ybuf, sem):
    i = pl.program_id(0)
    tm = x1_ref.shape[0]
    rt = ybuf.shape[0]

    @pl.when(i == 0)
    def _():
        ybuf[...] = jnp.zeros(ybuf.shape, ybuf.dtype)

    def seg_copy(local, src):
        return pltpu.make_async_copy(yb_ref.at[pl.ds(src, GRANULE)], ybuf.at[pl.ds(local, GRANULE)], sem)

    _segment_copies(i, seg_local_ref, seg_dst_ref, seg_n_ref, seg_copy, wait=False)
    mc = mc_ref[...]
    dl = (_local_rows(mc[:, META_EXPERT:META_EXPERT + TOP_K], seg_local_ref, i)
          + mc[:, META_RANK:META_RANK + TOP_K])
    gate = mc[:, META_GATE:META_GATE + TOP_K]
    rr = lax.broadcasted_iota(I32, (tm, rt), 1).astype(F32)
    weights = jnp.zeros((tm, rt), F32)
    for k in range(TOP_K):
        weights = weights + jnp.where(rr == dl[:, k:k + 1], gate[:, k:k + 1], 0.0)
    w_hi, w_lo = _split_bf16(weights)
    _segment_copies(i, seg_local_ref, seg_dst_ref, seg_n_ref, seg_copy, wait=True)
    yb16 = ybuf[...].astype(BF16)
    acc = (x1_ref[...] + jnp.dot(w_hi, yb16, preferred_element_type=F32)
           + jnp.dot(w_lo, yb16, preferred_element_type=F32))
    y = acc * lax.rsqrt(jnp.mean(acc * acc, axis=-1, keepdims=True) + RMS_EPS) * gf_ref[...]

    @pl.when(i < tiles_a)
    def _():
        ya_ref[...] = y

    @pl.when(i >= tiles_a)
    def _():
        yb_out_ref[...] = y


def _combine(tables, x1, metac, yb, gfin, n_a):
    t = x1.shape[0]
    tm = TOKEN_TILE
    tiles_a = n_a // tm
    return pl.pallas_call(
        functools.partial(_combine_kernel, tiles_a),
        grid_spec=pltpu.PrefetchScalarGridSpec(
            num_scalar_prefetch=3,
            grid=(t // tm,),
            in_specs=[
                pl.BlockSpec((tm, D_MODEL), lambda i, *_: (i, 0)),
                pl.BlockSpec((tm, LANES), lambda i, *_: (i, 0)),
                pl.BlockSpec(memory_space=pl.ANY),
                pl.BlockSpec((1, D_MODEL), lambda i, *_: (0, 0)),
            ],
            out_specs=[
                pl.BlockSpec((tm, D_MODEL), lambda i, *_: (jnp.minimum(i, tiles_a - 1), 0)),
                pl.BlockSpec((tm, D_MODEL), lambda i, *_: (jnp.maximum(i - tiles_a, 0), 0)),
            ],
            scratch_shapes=[pltpu.VMEM((TILE_ROWS, D_MODEL), F32), pltpu.SemaphoreType.DMA],
        ),
        out_shape=[jax.ShapeDtypeStruct((n_a, D_MODEL), F32),
                   jax.ShapeDtypeStruct((t - n_a, D_MODEL), F32)],
        compiler_params=_params(("arbitrary",)),
        name="combine",
    )(*tables, x1, metac, yb, gfin)


def kernel(x_prompt, x_sample, state_gdn, state_qkv_conv, state_dwconv, norm_mix_g, w_in, w_conv_qkv,
           a_log, dt_bias, w_onorm, b_glu, w_dw, b_dw, ln_g, ln_b, w_out, norm_ffn_g, w_router,
           b_router, w_gate_up, b_gate_up, w_down, b_down, norm_final_g):
    bp, tp, _ = x_prompt.shape
    bs, ts, _ = x_sample.shape
    n_p, n_s = bp * tp, bs * ts

    wi = w_in[0]
    c_z = QKV_COLS
    c_bd = c_z + GDN_WIDTH
    c_glu = c_bd + 2 * GDN_HEADS
    wqkv = wi[:, :c_z].astype(BF16)
    wz = wi[:, c_z:c_bd].astype(BF16)
    wbd = jnp.pad(wi[:, c_bd:c_glu], ((0, 0), (0, LANES - 2 * GDN_HEADS))).astype(BF16)
    wglu = wi[:, c_glu:].astype(BF16)
    bglu = b_glu[0][None, :]
    lane_pad = (GDN_HEADS, LANES - 2 * GDN_HEADS)
    pa = jnp.pad(-jnp.exp(a_log[0].astype(F32)), lane_pad)[None, :]
    pb = jnp.pad(dt_bias[0].astype(F32), lane_pad)[None, :]
    gmix = norm_mix_g[0][None, :]
    wconv = w_conv_qkv[0]
    wonorm = w_onorm[0][None, :]
    wdw, bdw = w_dw[0], b_dw[0][None, :]
    lng, lnb = ln_g[0][None, :], ln_b[0][None, :]
    wo = w_out[0][:GDN_WIDTH].astype(BF16)
    wc = w_out[0][GDN_WIDTH:].astype(BF16)
    gffn = norm_ffn_g[0][None, :]
    wr, br = w_router[0], b_router[0][None, :]
    wgu = w_gate_up[0].astype(BF16)
    bgu = b_gate_up[0][:, None, :]
    wd = w_down[0].astype(BF16)
    bdn = b_down[0][:, None, :]
    gfin = norm_final_g[None, :]

    xp = x_prompt.reshape(n_p, D_MODEL)
    xs_tok = x_sample.reshape(n_s, D_MODEL)

    def mixers(x2d, bsz, t, hist_qkv, s0, hist_glu, chunk, gdn_tile, gdn_rows, conv_tile):
        qkv, z, bd, bdt, uu = _inproj(x2d, gmix, wqkv, wz, wbd, wglu, bglu, pa, pb, chunk)
        qkv3 = qkv.reshape(bsz, t, QKV_COLS)
        uu3 = uu.reshape(bsz, t, CONF_WIDTH)
        bdt3 = bdt.reshape(2 * GDN_HEADS, bsz, t).transpose(1, 0, 2)
        o, s_new = _gdn(qkv3, z.reshape(bsz, t, GDN_WIDTH), bd.reshape(bsz, t, 2 * GDN_HEADS),
                        bdt3, hist_qkv, s0, wconv, wonorm, chunk, gdn_tile, gdn_rows)
        cc = _cconv(uu3, hist_glu, wdw, bdw, lng, lnb, conv_tile)
        return o.reshape(bsz * t, GDN_WIDTH), cc.reshape(bsz * t, CONF_WIDTH), s_new, qkv3, uu3

    zero_qkv = jnp.zeros((bp, QKV_TAPS - 1, QKV_COLS), F32)
    zero_s = jnp.zeros((bp, GDN_HEADS, HEAD_DIM, HEAD_DIM), F32)
    zero_glu = jnp.zeros((bp, DW_TAPS - 1, CONF_WIDTH), F32)
    o_p, c_p, s_p, qkv_p, uu_p = mixers(xp, bp, tp, zero_qkv, zero_s, zero_glu,
                                        PROMPT_CHUNK, GDN_TILE, 1, CONV_TILE)
    o_s, c_s, s_s, qkv_s, uu_s = mixers(xs_tok, bs, ts, state_qkv_conv[0], state_gdn[0],
                                        state_dwconv[0], ts, ts, GDN_SAMPLE_ROWS, ts)

    n_tok = n_p + n_s
    x1, h2, metac, metar, cnt_tile = _outproj((o_p, c_p, xp), (o_s, c_s, xs_tok),
                                              wo, wc, gffn, wr, br)

    bm = EXPERT_ROWS
    n_tiles = n_tok // TOKEN_TILE
    counts = cnt_tile[:, 0, :N_EXPERTS].astype(I32)
    seg_rows = (counts + GRANULE - 1) // GRANULE * GRANULE
    seg_local = jnp.cumsum(seg_rows, axis=1) - seg_rows
    seg_before = jnp.cumsum(seg_rows, axis=0) - seg_rows
    rows_e = jnp.sum(seg_rows, axis=0)
    padded = (rows_e + bm - 1) // bm * bm
    pend = jnp.cumsum(padded)
    pstart = pend - padded
    seg_dst = pstart[None, :] + seg_before
    n_used = jnp.maximum(pend[-1] // bm, 1).astype(I32)
    max_rows = n_tok * TOP_K + n_tiles * N_EXPERTS * (GRANULE - 1) + N_EXPERTS * (bm - 1)
    n_blocks = -(-max_rows // bm)
    blk = jnp.minimum(jnp.arange(n_blocks, dtype=I32), n_used - 1)
    block_expert = jnp.minimum(
        jnp.sum((pend[None, :] <= (blk * bm)[:, None]).astype(I32), axis=1), N_EXPERTS - 1)
    seg_tables = (seg_local.reshape(-1).astype(I32), seg_dst.reshape(-1).astype(I32),
                  (seg_rows // GRANULE).reshape(-1).astype(I32))
    gap_tables = ((pstart + rows_e).astype(I32), ((padded - rows_e) // GRANULE).astype(I32))
    zero_rows = GRANULE << (GAP_BITS - 1)
    tail_table = jnp.stack([pend[-1], (n_blocks * bm - pend[-1]) // zero_rows]).astype(I32)

    xs = _dispatch(seg_tables + gap_tables + (tail_table,), h2, metar, n_blocks * bm)
    yb = _experts(block_expert, blk, n_used.reshape(1), xs, wgu, bgu, wd, bdn)
    y_p, y_s = _combine(seg_tables, x1, metac, yb, gfin, n_p)

    hist_glu_s = jnp.concatenate([state_dwconv[0], uu_s], axis=1)[:, -(DW_TAPS - 1):]
    return (y_p.reshape(bp, tp, D_MODEL),
            y_s.reshape(bs, ts, D_MODEL),
            s_p[None],
            qkv_p[:, -(QKV_TAPS - 1):][None],
            uu_p[:, -(DW_TAPS - 1):][None],
            s_s[None],
            qkv_s[:, -(QKV_TAPS - 1):][None],
            hist_glu_s[None])
```

```python
import functools
import math

import jax
import jax.numpy as jnp
from jax import lax
from jax.experimental import pallas as pl
from jax.experimental.pallas import tpu as pltpu

F32 = jnp.float32
BF16 = jnp.bfloat16
I32 = jnp.int32
HIGHEST = lax.Precision.HIGHEST

D_MODEL = 1024
GDN_HEADS = 4
HEAD_DIM = 128
GDN_WIDTH = GDN_HEADS * HEAD_DIM
QKV_COLS = 3 * GDN_WIDTH
CONF_WIDTH = 512
QKV_TAPS = 4
DW_TAPS = 31
N_EXPERTS = 32
TOP_K = 4
D_EXPERT = 1024
SWIGLU_LIMIT = 7.0
SWIGLU_ALPHA = 1.702
RMS_EPS = 1e-6
LN_EPS = 1e-5
L2_EPS = 1e-6
PROMPT_CHUNK = 64

LANES = 128
SUBLANES = 8
TOKEN_TILE = 512
GDN_TILE = 256
GDN_SAMPLE_ROWS = 4
CONV_TILE = 256
CONV_ROWS = 32
EXPERT_ROWS = 256
GRANULE = SUBLANES
GAP_BITS = 5
TILE_ROWS = -(-(TOKEN_TILE * TOP_K + N_EXPERTS * (GRANULE - 1)) // 256) * 256
META_EXPERT, META_RANK, META_GATE, META_ROWS = 0, TOP_K, 2 * TOP_K, 16
assert EXPERT_ROWS // GRANULE <= 1 << GAP_BITS
VMEM_LIMIT = 48 * 1024 * 1024
EXPERT_VMEM_LIMIT = 56 * 1024 * 1024


def _dotb(a, b):
    return jnp.dot(a.astype(BF16), b.astype(BF16), preferred_element_type=F32)


def _dotf(a, b):
    return jnp.dot(a, b, precision=HIGHEST, preferred_element_type=F32)


def _sigmoid(x):
    return jax.nn.sigmoid(x)


def _params(sem):
    return pltpu.CompilerParams(dimension_semantics=sem, vmem_limit_bytes=VMEM_LIMIT)


def _inproj_kernel(chunk_shift, x_ref, g_ref, wqkv_ref, wz_ref, wbd_ref, wglu_ref, bglu_ref,
                   pa_ref, pb_ref, qkv_ref, z_ref, bd_ref, bdt_ref, uu_ref):
    x = x_ref[...]
    tm = x.shape[0]
    ms = jnp.mean(x * x, axis=-1, keepdims=True)
    h = (x * lax.rsqrt(ms + RMS_EPS) * g_ref[...]).astype(BF16)
    qkv_ref[...] = jnp.dot(h, wqkv_ref[...], preferred_element_type=F32)
    z_ref[...] = jnp.dot(h, wz_ref[...], preferred_element_type=F32)
    glu = jnp.dot(h, wglu_ref[...], preferred_element_type=F32) + bglu_ref[...]
    uu_ref[...] = glu[:, :CONF_WIDTH] * _sigmoid(glu[:, CONF_WIDTH:])
    bd = jnp.dot(h, wbd_ref[...], preferred_element_type=F32)
    beta = _sigmoid(bd)
    v = bd + pb_ref[...]
    softplus = jnp.maximum(v, 0.0) + jnp.log1p(jnp.exp(-jnp.abs(v)))
    g = pa_ref[...] * softplus
    r = lax.broadcasted_iota(I32, (tm, tm), 0)
    c = lax.broadcasted_iota(I32, (tm, tm), 1)
    tri = ((r >= c) & ((r >> chunk_shift) == (c >> chunk_shift))).astype(BF16)
    g_hi, g_rest = _split_bf16(g)
    g_mid = g - g_hi.astype(F32) - g_rest.astype(F32)
    pieces = jnp.concatenate([g_hi, g_rest, g_mid.astype(BF16)], axis=1)
    sums = jnp.dot(tri, pieces, preferred_element_type=F32)
    gc = sums[:, :LANES] + sums[:, LANES:2 * LANES] + sums[:, 2 * LANES:]
    lane = lax.broadcasted_iota(I32, (tm, LANES), 1)
    res = jnp.where(lane < GDN_HEADS, beta, gc)
    bd_ref[...] = res[:, :2 * GDN_HEADS]
    bdt_ref[...] = res.T[:2 * GDN_HEADS, :]


def _inproj(x, g, wqkv, wz, wbd, wglu, bglu, pa, pb, chunk):
    t = x.shape[0]
    tm = TOKEN_TILE
    full = lambda shape: pl.BlockSpec(shape, lambda i: (0, 0))
    return pl.pallas_call(
        functools.partial(_inproj_kernel, int(math.log2(chunk))),
        grid=(t // tm,),
        in_specs=[
            pl.BlockSpec((tm, D_MODEL), lambda i: (i, 0)),
            full((1, D_MODEL)),
            full((D_MODEL, QKV_COLS)),
            full((D_MODEL, GDN_WIDTH)),
            full((D_MODEL, LANES)),
            full((D_MODEL, 2 * CONF_WIDTH)),
            full((1, 2 * CONF_WIDTH)),
            full((1, LANES)),
            full((1, LANES)),
        ],
        out_specs=[
            pl.BlockSpec((tm, QKV_COLS), lambda i: (i, 0)),
            pl.BlockSpec((tm, GDN_WIDTH), lambda i: (i, 0)),
            pl.BlockSpec((tm, 2 * GDN_HEADS), lambda i: (i, 0)),
            pl.BlockSpec((2 * GDN_HEADS, tm), lambda i: (0, i)),
            pl.BlockSpec((tm, CONF_WIDTH), lambda i: (i, 0)),
        ],
        out_shape=[
            jax.ShapeDtypeStruct((t, QKV_COLS), F32),
            jax.ShapeDtypeStruct((t, GDN_WIDTH), F32),
            jax.ShapeDtypeStruct((t, 2 * GDN_HEADS), F32),
            jax.ShapeDtypeStruct((2 * GDN_HEADS, t), F32),
            jax.ShapeDtypeStruct((t, CONF_WIDTH), F32),
        ],
        compiler_params=_params(("arbitrary",)),
        name="inproj",
    )(x, g, wqkv, wz, wbd, wglu, bglu, pa, pb)


def _split_bf16(a):
    hi = a.astype(BF16)
    lo = (a - hi.astype(F32)).astype(BF16)
    return hi, lo


def _dot3(a_hi, a_lo, b_hi, b_lo):
    dot = functools.partial(jnp.dot, preferred_element_type=F32)
    return dot(a_hi, b_hi) + dot(a_lo, b_hi) + dot(a_hi, b_lo)


def _unit_lower_inverses(ms, eye, n):
    levels = int(math.log2(n)) - 1
    ps = [eye - m for m in ms]
    pw = [_split_bf16(-m) for m in ms]
    pw = [_split_bf16(_dot3(h, l, h, l)) for h, l in pw]
    for level in range(1, levels + 1):
        last = level == levels
        nxt = []
        for u, (h, l) in enumerate(pw):
            ph, pl_ = _split_bf16(ps[u])
            if last:
                ps[u] = ps[u] + _dot3(ph, pl_, h, l)
            else:
                out = _dot3(jnp.concatenate([h, ph], axis=0), jnp.concatenate([l, pl_], axis=0), h, l)
                nxt.append(_split_bf16(out[:n]))
                ps[u] = ps[u] + out[n:]
        pw = nxt
    return ps


def _gdn_kernel(chunk, x_ref, z_ref, bd_ref, bdt_ref, hist_ref, s0_ref, wc_ref, won_ref,
                o_ref, s_ref, xbuf):
    i = pl.program_id(1)
    nb, tt = x_ref.shape[0], x_ref.shape[1]
    nchunk = tt // chunk
    pad = SUBLANES
    hist_rows = QKV_TAPS - 1

    @pl.when(i == 0)
    def _():
        xbuf[:, pad - hist_rows:pad, :] = hist_ref[...]
        s_ref[...] = s0_ref[...]

    xbuf[:, pad:pad + tt, :] = x_ref[...]

    ii = lax.broadcasted_iota(I32, (chunk, chunk), 0)
    jj = lax.broadcasted_iota(I32, (chunk, chunk), 1)
    lower_incl = ii >= jj
    lower_strict = ii > jj
    eye = (ii == jj).astype(F32)
    won = won_ref[...]

    def conv_silu(b, col0):
        cols = slice(col0, col0 + HEAD_DIM)
        acc = wc_ref[hist_rows:hist_rows + 1, cols] * xbuf[b, pad:pad + tt, cols]
        for j in range(hist_rows):
            r0 = pad - hist_rows + j
            acc = acc + wc_ref[j:j + 1, cols] * xbuf[b, r0:r0 + tt, cols]
        return acc * _sigmoid(acc)

    units = []
    for b in range(nb):
        bd = bd_ref[b]
        bdt = bdt_ref[b]
        for h in range(GDN_HEADS):
            q = conv_silu(b, h * HEAD_DIM)
            k = conv_silu(b, GDN_WIDTH + h * HEAD_DIM)
            v = conv_silu(b, 2 * GDN_WIDTH + h * HEAD_DIM)
            q = q * lax.rsqrt(jnp.sum(q * q, axis=-1, keepdims=True) + L2_EPS) * (HEAD_DIM ** -0.5)
            k = k * lax.rsqrt(jnp.sum(k * k, axis=-1, keepdims=True) + L2_EPS)
            kt = k.T
            beta = bd[:, h:h + 1]
            gcol = bd[:, GDN_HEADS + h:GDN_HEADS + h + 1]
            grow = bdt[GDN_HEADS + h:GDN_HEADS + h + 1, :]
            for c in range(nchunk):
                rows = slice(c * chunk, (c + 1) * chunk)
                qc, kc, vc, ktc = q[rows], k[rows], v[rows], kt[:, rows]
                bc, gc, gr = beta[rows], gcol[rows], grow[:, rows]
                dec = jnp.exp(jnp.where(lower_incl, gc - gr, -jnp.inf))
                kb = kc * bc
                eg = jnp.exp(gc)
                both = _dotb(jnp.concatenate([kb, qc], axis=0), ktc)
                units.append(dict(
                    b=b, h=h, c=c,
                    m=both[:chunk] * jnp.where(lower_strict, dec, 0.0),
                    qk=both[chunk:] * dec,
                    rhs=jnp.concatenate([vc * bc, kb * eg], axis=1),
                    qg=qc * eg,
                    kdt=ktc * jnp.exp(gr[:, chunk - 1:chunk] - gr),
                    d_last=jnp.exp(gc[chunk - 1:chunk, :]),
                ))

    tinvs = _unit_lower_inverses([u["m"] for u in units], eye, chunk)
    for u, tinv in zip(units, tinvs):
        th, tl = _split_bf16(tinv)
        rh, rl = _split_bf16(u["rhs"])
        sol = _dot3(th, tl, rh, rl)
        u["u"], u["w"] = sol[:, :HEAD_DIM], sol[:, HEAD_DIM:]

    state = {(b, h): s_ref[b, h] for b in range(nb) for h in range(GDN_HEADS)}
    by_key = {(u["b"], u["h"], u["c"]): u for u in units}
    for c in range(nchunk):
        rows = slice(c * chunk, (c + 1) * chunk)
        for b in range(nb):
            for h in range(GDN_HEADS):
                u = by_key[(b, h, c)]
                s = state[(b, h)]
                ws = _dotb(jnp.concatenate([u["w"], u["qg"]], axis=0), s)
                v_new = u["u"] - ws[:chunk]
                upd = _dotb(jnp.concatenate([u["qk"], u["kdt"]], axis=0), v_new)
                o_c = ws[chunk:] + upd[:chunk]
                state[(b, h)] = s * u["d_last"] + upd[chunk:]
                cols = slice(h * HEAD_DIM, (h + 1) * HEAD_DIM)
                on = o_c * lax.rsqrt(jnp.mean(o_c * o_c, axis=-1, keepdims=True) + RMS_EPS) * won
                zc = z_ref[b, rows, cols]
                o_ref[b, rows, cols] = on * (zc * _sigmoid(zc))
    for (b, h), s in state.items():
        s_ref[b, h] = s

    xbuf[:, pad - hist_rows:pad, :] = x_ref[:, tt - hist_rows:tt, :]


def _gdn(qkv, z, bd, bdt, hist, s0, wconv, wonorm, chunk, tt, nb):
    b, t, _ = qkv.shape
    nt = t // tt
    return pl.pallas_call(
        functools.partial(_gdn_kernel, chunk),
        grid=(b // nb, nt),
        in_specs=[
            pl.BlockSpec((nb, tt, QKV_COLS), lambda bi, i: (bi, i, 0)),
            pl.BlockSpec((nb, tt, GDN_WIDTH), lambda bi, i: (bi, i, 0)),
            pl.BlockSpec((nb, tt, 2 * GDN_HEADS), lambda bi, i: (bi, i, 0)),
            pl.BlockSpec((nb, 2 * GDN_HEADS, tt), lambda bi, i: (bi, 0, i)),
            pl.BlockSpec((nb, QKV_TAPS - 1, QKV_COLS), lambda bi, i: (bi, 0, 0)),
            pl.BlockSpec((nb, GDN_HEADS, HEAD_DIM, HEAD_DIM), lambda bi, i: (bi, 0, 0, 0)),
            pl.BlockSpec((QKV_TAPS, QKV_COLS), lambda bi, i: (0, 0)),
            pl.BlockSpec((1, HEAD_DIM), lambda bi, i: (0, 0)),
        ],
        out_specs=[
            pl.BlockSpec((nb, tt, GDN_WIDTH), lambda bi, i: (bi, i, 0)),
            pl.BlockSpec((nb, GDN_HEADS, HEAD_DIM, HEAD_DIM), lambda bi, i: (bi, 0, 0, 0)),
        ],
        out_shape=[
            jax.ShapeDtypeStruct((b, t, GDN_WIDTH), F32),
            jax.ShapeDtypeStruct((b, GDN_HEADS, HEAD_DIM, HEAD_DIM), F32),
        ],
        scratch_shapes=[pltpu.VMEM((nb, tt + SUBLANES, QKV_COLS), F32)],
        compiler_params=_params(("arbitrary", "arbitrary")),
        name="gdn",
    )(qkv, z, bd, bdt, hist, s0, wconv, wonorm)


def _cconv_kernel(carry, u_ref, hist_ref, w_ref, b_ref, lg_ref, lb_ref, c_ref, ubuf):
    i = pl.program_id(1)
    tt = u_ref.shape[0]
    hist_rows = DW_TAPS - 1
    pad = 4 * SUBLANES
    base = pad - hist_rows

    @pl.when(i == 0)
    def _():
        ubuf[base:pad, :] = hist_ref[...]

    ubuf[pad:pad + tt, :] = u_ref[...]
    rows = min(CONV_ROWS, tt)
    for r0 in range(0, tt, rows):
        acc = w_ref[0:1, :] * ubuf[base + r0:base + r0 + rows, :]
        for j in range(1, DW_TAPS):
            acc = acc + w_ref[j:j + 1, :] * ubuf[base + r0 + j:base + r0 + j + rows, :]
        cv = acc + b_ref[...]
        mu = jnp.mean(cv, axis=-1, keepdims=True)
        xc = cv - mu
        var = jnp.mean(xc * xc, axis=-1, keepdims=True)
        y = xc * lax.rsqrt(var + LN_EPS) * lg_ref[...] + lb_ref[...]
        c_ref[r0:r0 + rows, :] = y * _sigmoid(y)
    if carry:
        ubuf[base:pad, :] = u_ref[tt - hist_rows:tt, :]


def _cconv(uu, hist, w, b, lg, lb, tt):
    bsz, t, _ = uu.shape
    nt = t // tt
    vec = pl.BlockSpec((1, CONF_WIDTH), lambda bi, i: (0, 0))
    return pl.pallas_call(
        functools.partial(_cconv_kernel, nt > 1),
        grid=(bsz, nt),
        in_specs=[
            pl.BlockSpec((None, tt, CONF_WIDTH), lambda bi, i: (bi, i, 0)),
            pl.BlockSpec((None, DW_TAPS - 1, CONF_WIDTH), lambda bi, i: (bi, 0, 0)),
            pl.BlockSpec((DW_TAPS, CONF_WIDTH), lambda bi, i: (0, 0)),
            vec, vec, vec,
        ],
        out_specs=pl.BlockSpec((None, tt, CONF_WIDTH), lambda bi, i: (bi, i, 0)),
        out_shape=jax.ShapeDtypeStruct((bsz, t, CONF_WIDTH), F32),
        scratch_shapes=[pltpu.VMEM((tt + 4 * SUBLANES, CONF_WIDTH), F32)],
        compiler_params=_params(("arbitrary", "arbitrary")),
        name="cconv",
    )(uu, hist, w, b, lg, lb)


def _outproj_kernel(tiles_a, oa_ref, ca_ref, xa_ref, ob_ref, cb_ref, xb_ref, *rest):
    i = pl.program_id(0)

    @pl.when(i < tiles_a)
    def _():
        _outproj_tile(oa_ref, ca_ref, xa_ref, *rest)

    @pl.when(i >= tiles_a)
    def _():
        _outproj_tile(ob_ref, cb_ref, xb_ref, *rest)


def _outproj_tile(o_ref, c_ref, x_ref, wo_ref, wc_ref, g_ref, wr_ref, br_ref, upper_ref,
                  x1_ref, h2_ref, metac_ref, metar_ref, cnt_ref):
    tm = x_ref.shape[0]

    mix = _dotb(o_ref[...], wo_ref[...]) + _dotb(c_ref[...], wc_ref[...])
    x1 = x_ref[...] + mix
    x1_ref[...] = x1
    h2 = x1 * lax.rsqrt(jnp.mean(x1 * x1, axis=-1, keepdims=True) + RMS_EPS) * g_ref[...]
    h2_ref[...] = h2.astype(BF16)
    logits = lax.dot_general(wr_ref[...], h2.astype(BF16), (((1,), (1,)), ((), ())),
                             preferred_element_type=F32) + br_ref[...]
    row = lax.broadcasted_iota(I32, (N_EXPERTS, tm), 0).astype(F32)
    vals = logits
    idxs, tops = [], []
    for _ in range(TOP_K):
        m = jnp.max(vals, axis=0, keepdims=True)
        idx = jnp.min(jnp.where(vals == m, row, float(N_EXPERTS)), axis=0, keepdims=True)
        idxs.append(idx)
        tops.append(m)
        vals = jnp.where(row == idx, -jnp.inf, vals)
    exps = [jnp.exp(v - tops[0]) for v in tops]
    den = exps[0] + exps[1] + exps[2] + exps[3]
    onehots = [(row == idx).astype(F32) for idx in idxs]
    chosen = onehots[0] + onehots[1] + onehots[2] + onehots[3]
    before = jnp.dot(chosen.astype(BF16), upper_ref[...], preferred_element_type=F32)
    ranks = [jnp.sum(oh * before, axis=0, keepdims=True) for oh in onehots]
    gates = [e / den for e in exps]
    meta = jnp.concatenate(idxs + ranks + gates + [jnp.zeros((LANES - 3 * TOP_K, tm), F32)], axis=0)
    metar_ref[...] = meta[:META_ROWS]
    metac_ref[...] = meta.T
    cnt_ref[...] = jnp.broadcast_to(jnp.sum(chosen, axis=1, keepdims=True), (N_EXPERTS, LANES))[None]


def _outproj(first, second, wo, wc, g, wr, br):
    t_a, t_b = first[2].shape[0], second[2].shape[0]
    tm = TOKEN_TILE
    upper = (jnp.arange(tm)[:, None] < jnp.arange(tm)[None, :]).astype(BF16)
    tiles_a = t_a // tm
    t_total = t_a + t_b
    full = lambda shape: pl.BlockSpec(shape, lambda i: (0, 0))
    rows_a = lambda width: pl.BlockSpec((tm, width), lambda i: (jnp.minimum(i, tiles_a - 1), 0))
    rows_b = lambda width: pl.BlockSpec((tm, width), lambda i: (jnp.maximum(i - tiles_a, 0), 0))
    widths = (GDN_WIDTH, CONF_WIDTH, D_MODEL)
    return pl.pallas_call(
        functools.partial(_outproj_kernel, tiles_a),
        grid=(t_total // tm,),
        in_specs=[rows_a(w) for w in widths] + [rows_b(w) for w in widths] + [
            full((GDN_WIDTH, D_MODEL)),
            full((CONF_WIDTH, D_MODEL)),
            full((1, D_MODEL)),
            full((N_EXPERTS, D_MODEL)),
            full((N_EXPERTS, 1)),
            full((tm, tm)),
        ],
        out_specs=[
            pl.BlockSpec((tm, D_MODEL), lambda i: (i, 0)),
            pl.BlockSpec((tm, D_MODEL), lambda i: (i, 0)),
            pl.BlockSpec((tm, LANES), lambda i: (i, 0)),
            pl.BlockSpec((META_ROWS, tm), lambda i: (0, i)),
            pl.BlockSpec((1, N_EXPERTS, LANES), lambda i: (i, 0, 0)),
        ],
        out_shape=[
            jax.ShapeDtypeStruct((t_total, D_MODEL), F32),
            jax.ShapeDtypeStruct((t_total, D_MODEL), BF16),
            jax.ShapeDtypeStruct((t_total, LANES), F32),
            jax.ShapeDtypeStruct((META_ROWS, t_total), F32),
            jax.ShapeDtypeStruct((t_total // tm, N_EXPERTS, LANES), F32),
        ],
        compiler_params=_params(("arbitrary",)),
        name="outproj_router",
    )(*first, *second, wo, wc, g, wr, br, upper)


def _segment_copies(i, seg_local_ref, seg_dst_ref, seg_n_ref, make_copy, wait):
    def per_expert(ex, carry):
        j = i * N_EXPERTS + ex
        local = pl.multiple_of(seg_local_ref[j], GRANULE)
        dst = pl.multiple_of(seg_dst_ref[j], GRANULE)

        def per_granule(g, c):
            off = pl.multiple_of(g * GRANULE, GRANULE)
            cp = make_copy(local + off, dst + off)
            cp.wait() if wait else cp.start()
            return c

        lax.fori_loop(0, seg_n_ref[j], per_granule, 0)
        return carry

    lax.fori_loop(0, N_EXPERTS, per_expert, 0)


def _gap_copies(gap_start_ref, gap_n_ref, make_copy, wait):
    def per_expert(ex, carry):
        n = gap_n_ref[ex]
        start = pl.multiple_of(gap_start_ref[ex], GRANULE)
        off = 0 * n
        for bit in reversed(range(GAP_BITS)):
            rows = GRANULE << bit

            @pl.when(((n >> bit) & 1) == 1)
            def _(rows=rows, off=off):
                cp = make_copy(rows, pl.multiple_of(start + off, GRANULE))
                cp.wait() if wait else cp.start()

            off = off + ((n >> bit) & 1) * rows
        return carry

    lax.fori_loop(0, N_EXPERTS, per_expert, 0)


def _local_rows(meta_e, seg_local_ref, i):
    base = jnp.zeros(meta_e.shape, F32)
    for ex in range(N_EXPERTS):
        base = jnp.where(meta_e == float(ex), seg_local_ref[i * N_EXPERTS + ex].astype(F32), base)
    return base


def _dispatch_kernel(seg_local_ref, seg_dst_ref, seg_n_ref, gap_start_ref, gap_n_ref, tail_ref,
                     h_ref, mr_ref, xs_ref, xloc, zbuf, sem, zsem):
    i = pl.program_id(0)
    tm = h_ref.shape[0]
    rt = xloc.shape[0]
    zrows = zbuf.shape[0]

    def gap_copy(rows, dst):
        return pltpu.make_async_copy(zbuf.at[pl.ds(0, rows)], xs_ref.at[pl.ds(dst, rows)], zsem)

    def tail_copies(wait):
        def body(j, c):
            cp = gap_copy(zrows, pl.multiple_of(tail_ref[0] + j * zrows, zrows))
            cp.wait() if wait else cp.start()
            return c
        lax.fori_loop(0, tail_ref[1], body, 0)

    @pl.when(i == 0)
    def _():
        zbuf[...] = jnp.zeros(zbuf.shape, zbuf.dtype)
        _gap_copies(gap_start_ref, gap_n_ref, gap_copy, wait=False)
        tail_copies(wait=False)

    mr = mr_ref[...]
    dl = (_local_rows(mr[META_EXPERT:META_EXPERT + TOP_K], seg_local_ref, i)
          + mr[META_RANK:META_RANK + TOP_K])
    rr = lax.broadcasted_iota(I32, (rt, tm), 0).astype(F32)
    hit = rr == dl[0:1]
    for k in range(1, TOP_K):
        hit = hit | (rr == dl[k:k + 1])
    xloc[...] = jnp.dot(hit.astype(BF16), h_ref[...], preferred_element_type=F32)

    def seg_copy(local, dst):
        return pltpu.make_async_copy(xloc.at[pl.ds(local, GRANULE)], xs_ref.at[pl.ds(dst, GRANULE)], sem)

    _segment_copies(i, seg_local_ref, seg_dst_ref, seg_n_ref, seg_copy, wait=False)
    _segment_copies(i, seg_local_ref, seg_dst_ref, seg_n_ref, seg_copy, wait=True)

    @pl.when(i == 0)
    def _():
        _gap_copies(gap_start_ref, gap_n_ref, gap_copy, wait=True)
        tail_copies(wait=True)


def _dispatch(tables, h2, metar, n_rows):
    t = h2.shape[0]
    tm = TOKEN_TILE
    return pl.pallas_call(
        _dispatch_kernel,
        grid_spec=pltpu.PrefetchScalarGridSpec(
            num_scalar_prefetch=6,
            grid=(t // tm,),
            in_specs=[
                pl.BlockSpec((tm, D_MODEL), lambda i, *_: (i, 0)),
                pl.BlockSpec((META_ROWS, tm), lambda i, *_: (0, i)),
            ],
            out_specs=pl.BlockSpec(memory_space=pl.ANY),
            scratch_shapes=[
                pltpu.VMEM((TILE_ROWS, D_MODEL), F32),
                pltpu.VMEM((GRANULE << (GAP_BITS - 1), D_MODEL), F32),
                pltpu.SemaphoreType.DMA,
                pltpu.SemaphoreType.DMA,
            ],
        ),
        out_shape=jax.ShapeDtypeStruct((n_rows, D_MODEL), F32),
        compiler_params=_params(("arbitrary",)),
        name="dispatch",
    )(*tables, h2, metar)


def _expert_kernel(be_ref, rb_ref, nu_ref, first_ref, slot_ref, next_ref,
                   x_ref, wgu_hbm, bgu_ref, wd_hbm, bdn_ref, y_ref,
                   wgu_f32, wd_f32, wgu_bf, wd_bf, sems):
    i = pl.program_id(0)

    def fetch(expert, slot):
        return (pltpu.make_async_copy(wgu_hbm.at[expert], wgu_f32.at[slot], sems.at[slot, 0]),
                pltpu.make_async_copy(wd_hbm.at[expert], wd_f32.at[slot], sems.at[slot, 1]))

    @pl.when(i == 0)
    def _():
        for cp in fetch(be_ref[0], 0):
            cp.start()

    @pl.when(first_ref[i] == 1)
    def _():
        slot = slot_ref[i]
        for cp in fetch(be_ref[i], slot):
            cp.wait()
        wgu_bf[...] = wgu_f32[slot].astype(BF16)
        wd_bf[...] = wd_f32[slot].astype(BF16)

        @pl.when(next_ref[i] >= 0)
        def _():
            for cp in fetch(next_ref[i], 1 - slot):
                cp.start()

    @pl.when(i < nu_ref[0])
    def _():
        gu = _dotb(x_ref[...], wgu_bf[...]) + bgu_ref[...]
        x_glu = jnp.minimum(gu[:, :D_EXPERT], SWIGLU_LIMIT)
        x_lin = jnp.clip(gu[:, D_EXPERT:], -SWIGLU_LIMIT, SWIGLU_LIMIT)
        act = x_glu * _sigmoid(SWIGLU_ALPHA * x_glu) * (x_lin + 1.0)
        y_ref[...] = _dotb(act, wd_bf[...]) + bdn_ref[...]

    @pl.when(i >= nu_ref[0])
    def _():
        y_ref[...] = jnp.zeros(y_ref.shape, y_ref.dtype)


def _experts(tables, xs, wgu, bgu, wd, bdn):
    rows = xs.shape[0]
    bm = EXPERT_ROWS
    nb = rows // bm
    return pl.pallas_call(
        _expert_kernel,
        grid_spec=pltpu.PrefetchScalarGridSpec(
            num_scalar_prefetch=6,
            grid=(nb,),
            in_specs=[
                pl.BlockSpec((bm, D_MODEL), lambda i, be, rb, *_: (rb[i], 0)),
                pl.BlockSpec(memory_space=pl.ANY),
                pl.BlockSpec((None, 1, 2 * D_EXPERT), lambda i, be, *_: (be[i], 0, 0)),
                pl.BlockSpec(memory_space=pl.ANY),
                pl.BlockSpec((None, 1, D_MODEL), lambda i, be, *_: (be[i], 0, 0)),
            ],
            out_specs=pl.BlockSpec((bm, D_MODEL), lambda i, *_: (i, 0)),
            scratch_shapes=[
                pltpu.VMEM((2, D_MODEL, 2 * D_EXPERT), F32),
                pltpu.VMEM((2, D_EXPERT, D_MODEL), F32),
                pltpu.VMEM((D_MODEL, 2 * D_EXPERT), BF16),
                pltpu.VMEM((D_EXPERT, D_MODEL), BF16),
                pltpu.SemaphoreType.DMA((2, 2)),
            ],
        ),
        out_shape=jax.ShapeDtypeStruct((rows, D_MODEL), F32),
        compiler_params=pltpu.CompilerParams(dimension_semantics=("arbitrary",),
                                             vmem_limit_bytes=EXPERT_VMEM_LIMIT),
        name="experts",
    )(*tables, xs, wgu, bgu, wd, bdn)


def _combine_kernel(tiles_a, seg_local_ref, seg_dst_ref, seg_n_ref, x1_ref, mc_ref, yb_ref, gf_ref,
                    ya_ref, yb_out_ref, ybuf, sem):
    i = pl.program_id(0)
    tm = x1_ref.shape[0]
    rt = ybuf.shape[0]

    @pl.when(i == 0)
    def _():
        ybuf[...] = jnp.zeros(ybuf.shape, ybuf.dtype)

    def seg_copy(local, src):
        return pltpu.make_async_copy(yb_ref.at[pl.ds(src, GRANULE)], ybuf.at[pl.ds(local, GRANULE)], sem)

    _segment_copies(i, seg_local_ref, seg_dst_ref, seg_n_ref, seg_copy, wait=False)
    mc = mc_ref[...]
    dl = (_local_rows(mc[:, META_EXPERT:META_EXPERT + TOP_K], seg_local_ref, i)
          + mc[:, META_RANK:META_RANK + TOP_K])
    gate = mc[:, META_GATE:META_GATE + TOP_K]
    rr = lax.broadcasted_iota(I32, (tm, rt), 1).astype(F32)
    weights = jnp.zeros((tm, rt), F32)
    for k in range(TOP_K):
        weights = weights + jnp.where(rr == dl[:, k:k + 1], gate[:, k:k + 1], 0.0)
    w_hi, w_lo = _split_bf16(weights)
    _segment_copies(i, seg_local_ref, seg_dst_ref, seg_n_ref, seg_copy, wait=True)
    yb16 = ybuf[...].astype(BF16)
    acc = (x1_ref[...] + jnp.dot(w_hi, yb16, preferred_element_type=F32)
           + jnp.dot(w_lo, yb16, preferred_element_type=F32))
    y = acc * lax.rsqrt(jnp.mean(acc * acc, axis=-1, keepdims=True) + RMS_EPS) * gf_ref[...]

    @pl.when(i < tiles_a)
    def _():
        ya_ref[...] = y

    @pl.when(i >= tiles_a)
    def _():
        yb_out_ref[...] = y


def _combine(tables, x1, metac, yb, gfin, n_a):
    t = x1.shape[0]
    tm = TOKEN_TILE
    tiles_a = n_a // tm
    return pl.pallas_call(
        functools.partial(_combine_kernel, tiles_a),
        grid_spec=pltpu.PrefetchScalarGridSpec(
            num_scalar_prefetch=3,
            grid=(t // tm,),
            in_specs=[
                pl.BlockSpec((tm, D_MODEL), lambda i, *_: (i, 0)),
                pl.BlockSpec((tm, LANES), lambda i, *_: (i, 0)),
                pl.BlockSpec(memory_space=pl.ANY),
                pl.BlockSpec((1, D_MODEL), lambda i, *_: (0, 0)),
            ],
            out_specs=[
                pl.BlockSpec((tm, D_MODEL), lambda i, *_: (jnp.minimum(i, tiles_a - 1), 0)),
                pl.BlockSpec((tm, D_MODEL), lambda i, *_: (jnp.maximum(i - tiles_a, 0), 0)),
            ],
            scratch_shapes=[pltpu.VMEM((TILE_ROWS, D_MODEL), F32), pltpu.SemaphoreType.DMA],
        ),
        out_shape=[jax.ShapeDtypeStruct((n_a, D_MODEL), F32),
                   jax.ShapeDtypeStruct((t - n_a, D_MODEL), F32)],
        compiler_params=_params(("arbitrary",)),
        name="combine",
    )(*tables, x1, metac, yb, gfin)


def kernel(x_prompt, x_sample, state_gdn, state_qkv_conv, state_dwconv, norm_mix_g, w_in, w_conv_qkv,
           a_log, dt_bias, w_onorm, b_glu, w_dw, b_dw, ln_g, ln_b, w_out, norm_ffn_g, w_router,
           b_router, w_gate_up, b_gate_up, w_down, b_down, norm_final_g):
    bp, tp, _ = x_prompt.shape
    bs, ts, _ = x_sample.shape
    n_p, n_s = bp * tp, bs * ts

    wi = w_in[0]
    c_z = QKV_COLS
    c_bd = c_z + GDN_WIDTH
    c_glu = c_bd + 2 * GDN_HEADS
    wqkv = wi[:, :c_z].astype(BF16)
    wz = wi[:, c_z:c_bd].astype(BF16)
    wbd = jnp.pad(wi[:, c_bd:c_glu], ((0, 0), (0, LANES - 2 * GDN_HEADS))).astype(BF16)
    wglu = wi[:, c_glu:].astype(BF16)
    bglu = b_glu[0][None, :]
    lane_pad = (GDN_HEADS, LANES - 2 * GDN_HEADS)
    pa = jnp.pad(-jnp.exp(a_log[0].astype(F32)), lane_pad)[None, :]
    pb = jnp.pad(dt_bias[0].astype(F32), lane_pad)[None, :]
    gmix = norm_mix_g[0][None, :]
    wconv = w_conv_qkv[0]
    wonorm = w_onorm[0][None, :]
    wdw, bdw = w_dw[0], b_dw[0][None, :]
    lng, lnb = ln_g[0][None, :], ln_b[0][None, :]
    wo = w_out[0][:GDN_WIDTH].astype(BF16)
    wc = w_out[0][GDN_WIDTH:].astype(BF16)
    gffn = norm_ffn_g[0][None, :]
    wr, br = w_router[0].T.astype(BF16), b_router[0].astype(F32)[:, None]
    bgu = b_gate_up[0][:, None, :]
    bdn = b_down[0][:, None, :]
    gfin = norm_final_g[None, :]

    xp = x_prompt.reshape(n_p, D_MODEL)
    xs_tok = x_sample.reshape(n_s, D_MODEL)

    def mixers(x2d, bsz, t, hist_qkv, s0, hist_glu, chunk, gdn_tile, gdn_rows, conv_tile):
        qkv, z, bd, bdt, uu = _inproj(x2d, gmix, wqkv, wz, wbd, wglu, bglu, pa, pb, chunk)
        qkv3 = qkv.reshape(bsz, t, QKV_COLS)
        uu3 = uu.reshape(bsz, t, CONF_WIDTH)
        bdt3 = bdt.reshape(2 * GDN_HEADS, bsz, t).transpose(1, 0, 2)
        o, s_new = _gdn(qkv3, z.reshape(bsz, t, GDN_WIDTH), bd.reshape(bsz, t, 2 * GDN_HEADS),
                        bdt3, hist_qkv, s0, wconv, wonorm, chunk, gdn_tile, gdn_rows)
        cc = _cconv(uu3, hist_glu, wdw, bdw, lng, lnb, conv_tile)
        return o.reshape(bsz * t, GDN_WIDTH), cc.reshape(bsz * t, CONF_WIDTH), s_new, qkv3, uu3

    zero_qkv = jnp.zeros((bp, QKV_TAPS - 1, QKV_COLS), F32)
    zero_s = jnp.zeros((bp, GDN_HEADS, HEAD_DIM, HEAD_DIM), F32)
    zero_glu = jnp.zeros((bp, DW_TAPS - 1, CONF_WIDTH), F32)
    o_p, c_p, s_p, qkv_p, uu_p = mixers(xp, bp, tp, zero_qkv, zero_s, zero_glu,
                                        PROMPT_CHUNK, GDN_TILE, 1, CONV_TILE)
    o_s, c_s, s_s, qkv_s, uu_s = mixers(xs_tok, bs, ts, state_qkv_conv[0], state_gdn[0],
                                        state_dwconv[0], ts, ts, GDN_SAMPLE_ROWS, ts)

    n_tok = n_p + n_s
    x1, h2, metac, metar, cnt_tile = _outproj((o_p, c_p, xp), (o_s, c_s, xs_tok),
                                              wo, wc, gffn, wr, br)

    bm = EXPERT_ROWS
    n_tiles = n_tok // TOKEN_TILE
    counts = cnt_tile[:, :, 0].astype(I32)
    seg_rows = (counts + GRANULE - 1) // GRANULE * GRANULE
    seg_local = jnp.cumsum(seg_rows, axis=1) - seg_rows
    seg_before = jnp.cumsum(seg_rows, axis=0) - seg_rows
    rows_e = jnp.sum(seg_rows, axis=0)
    padded = (rows_e + bm - 1) // bm * bm
    pend = jnp.cumsum(padded)
    pstart = pend - padded
    seg_dst = pstart[None, :] + seg_before
    n_used = jnp.maximum(pend[-1] // bm, 1).astype(I32)
    max_rows = n_tok * TOP_K + n_tiles * N_EXPERTS * (GRANULE - 1) + N_EXPERTS * (bm - 1)
    n_blocks = -(-max_rows // bm)
    blk = jnp.minimum(jnp.arange(n_blocks, dtype=I32), n_used - 1)
    block_expert = jnp.minimum(
        jnp.sum((pend[None, :] <= (blk * bm)[:, None]).astype(I32), axis=1), N_EXPERTS - 1)
    seg_tables = (seg_local.reshape(-1).astype(I32), seg_dst.reshape(-1).astype(I32),
                  (seg_rows // GRANULE).reshape(-1).astype(I32))
    gap_tables = ((pstart + rows_e).astype(I32), ((padded - rows_e) // GRANULE).astype(I32))
    zero_rows = GRANULE << (GAP_BITS - 1)
    tail_table = jnp.stack([pend[-1], (n_blocks * bm - pend[-1]) // zero_rows]).astype(I32)

    xs = _dispatch(seg_tables + gap_tables + (tail_table,), h2, metar, n_blocks * bm)
    ids = jnp.arange(N_EXPERTS, dtype=I32)
    active = padded > 0
    later = active[None, :] & (ids[None, :] > ids[:, None])
    next_active = jnp.min(jnp.where(later, ids[None, :], N_EXPERTS), axis=1)
    next_active = jnp.where(next_active == N_EXPERTS, -1, next_active)
    order = jnp.cumsum(active.astype(I32)) - 1
    steps = jnp.arange(n_blocks, dtype=I32)
    first = ((steps * bm == pstart[block_expert]) & (steps < n_used)).astype(I32)
    expert_tables = (block_expert, blk, n_used.reshape(1), first,
                     (order[block_expert] % 2).astype(I32), next_active[block_expert].astype(I32))
    yb = _experts(expert_tables, xs, w_gate_up[0], bgu, w_down[0], bdn)
    y_p, y_s = _combine(seg_tables, x1, metac, yb, gfin, n_p)

    hist_glu_s = jnp.concatenate([state_dwconv[0], uu_s], axis=1)[:, -(DW_TAPS - 1):]
    return (y_p.reshape(bp, tp, D_MODEL),
            y_s.reshape(bs, ts, D_MODEL),
            s_p[None],
            qkv_p[:, -(QKV_TAPS - 1):][None],
            uu_p[:, -(DW_TAPS - 1):][None],
            s_s[None],
            qkv_s[:, -(QKV_TAPS - 1):][None],
            hist_glu_s[None])
```

```python
import functools
import math

import jax
import jax.numpy as jnp
from jax import lax
from jax.experimental import pallas as pl
from jax.experimental.pallas import tpu as pltpu

F32 = jnp.float32
BF16 = jnp.bfloat16
I32 = jnp.int32
HIGHEST = lax.Precision.HIGHEST

D_MODEL = 1024
GDN_HEADS = 4
HEAD_DIM = 128
GDN_WIDTH = GDN_HEADS * HEAD_DIM
QKV_COLS = 3 * GDN_WIDTH
CONF_WIDTH = 512
QKV_TAPS = 4
DW_TAPS = 31
N_EXPERTS = 32
TOP_K = 4
D_EXPERT = 1024
SWIGLU_LIMIT = 7.0
SWIGLU_ALPHA = 1.702
RMS_EPS = 1e-6
LN_EPS = 1e-5
L2_EPS = 1e-6
PROMPT_CHUNK = 64

LANES = 128
SUBLANES = 8
TOKEN_TILE = 512
GDN_TILE = 256
GDN_SAMPLE_ROWS = 4
CONV_TILE = 256
CONV_ROWS = 32
EXPERT_ROWS = 256
GRANULE = SUBLANES
GAP_BITS = 5
TILE_ROWS = -(-(TOKEN_TILE * TOP_K + N_EXPERTS * (GRANULE - 1)) // 256) * 256
META_EXPERT, META_RANK, META_GATE, META_ROWS = 0, TOP_K, 2 * TOP_K, 16
assert EXPERT_ROWS // GRANULE <= 1 << GAP_BITS
VMEM_LIMIT = 48 * 1024 * 1024
EXPERT_VMEM_LIMIT = 56 * 1024 * 1024


def _dotb(a, b):
    return jnp.dot(a.astype(BF16), b.astype(BF16), preferred_element_type=F32)


def _dotf(a, b):
    return jnp.dot(a, b, precision=HIGHEST, preferred_element_type=F32)


def _sigmoid(x):
    return jax.nn.sigmoid(x)


def _params(sem):
    return pltpu.CompilerParams(dimension_semantics=sem, vmem_limit_bytes=VMEM_LIMIT)


def _inproj_kernel(chunk_shift, x_ref, g_ref, wqkv_ref, wz_ref, wbd_ref, wglu_ref, bglu_ref,
                   pa_ref, pb_ref, qkv_ref, z_ref, bd_ref, bdt_ref, uu_ref):
    x = x_ref[...]
    tm = x.shape[0]
    ms = jnp.mean(x * x, axis=-1, keepdims=True)
    h = (x * lax.rsqrt(ms + RMS_EPS) * g_ref[...]).astype(BF16)
    qkv_ref[...] = jnp.dot(h, wqkv_ref[...], preferred_element_type=F32)
    z_ref[...] = jnp.dot(h, wz_ref[...], preferred_element_type=F32)
    glu = jnp.dot(h, wglu_ref[...], preferred_element_type=F32) + bglu_ref[...]
    uu_ref[...] = glu[:, :CONF_WIDTH] * _sigmoid(glu[:, CONF_WIDTH:])
    bd = jnp.dot(h, wbd_ref[...], preferred_element_type=F32)
    beta = _sigmoid(bd)
    v = bd + pb_ref[...]
    softplus = jnp.maximum(v, 0.0) + jnp.log1p(jnp.exp(-jnp.abs(v)))
    g = pa_ref[...] * softplus
    r = lax.broadcasted_iota(I32, (tm, tm), 0)
    c = lax.broadcasted_iota(I32, (tm, tm), 1)
    tri = ((r >= c) & ((r >> chunk_shift) == (c >> chunk_shift))).astype(BF16)
    g_hi, g_rest = _split_bf16(g)
    g_mid = g - g_hi.astype(F32) - g_rest.astype(F32)
    pieces = jnp.concatenate([g_hi, g_rest, g_mid.astype(BF16)], axis=1)
    sums = jnp.dot(tri, pieces, preferred_element_type=F32)
    gc = sums[:, :LANES] + sums[:, LANES:2 * LANES] + sums[:, 2 * LANES:]
    lane = lax.broadcasted_iota(I32, (tm, LANES), 1)
    res = jnp.where(lane < GDN_HEADS, beta, gc)
    bd_ref[...] = res[:, :2 * GDN_HEADS]
    bdt_ref[...] = res.T[:2 * GDN_HEADS, :]


def _inproj(x, g, wqkv, wz, wbd, wglu, bglu, pa, pb, chunk):
    t = x.shape[0]
    tm = TOKEN_TILE
    full = lambda shape: pl.BlockSpec(shape, lambda i: (0, 0))
    return pl.pallas_call(
        functools.partial(_inproj_kernel, int(math.log2(chunk))),
        grid=(t // tm,),
        in_specs=[
            pl.BlockSpec((tm, D_MODEL), lambda i: (i, 0)),
            full((1, D_MODEL)),
            full((D_MODEL, QKV_COLS)),
            full((D_MODEL, GDN_WIDTH)),
            full((D_MODEL, LANES)),
            full((D_MODEL, 2 * CONF_WIDTH)),
            full((1, 2 * CONF_WIDTH)),
            full((1, LANES)),
            full((1, LANES)),
        ],
        out_specs=[
            pl.BlockSpec((tm, QKV_COLS), lambda i: (i, 0)),
            pl.BlockSpec((tm, GDN_WIDTH), lambda i: (i, 0)),
            pl.BlockSpec((tm, 2 * GDN_HEADS), lambda i: (i, 0)),
            pl.BlockSpec((2 * GDN_HEADS, tm), lambda i: (0, i)),
            pl.BlockSpec((tm, CONF_WIDTH), lambda i: (i, 0)),
        ],
        out_shape=[
            jax.ShapeDtypeStruct((t, QKV_COLS), F32),
            jax.ShapeDtypeStruct((t, GDN_WIDTH), F32),
            jax.ShapeDtypeStruct((t, 2 * GDN_HEADS), F32),
            jax.ShapeDtypeStruct((2 * GDN_HEADS, t), F32),
            jax.ShapeDtypeStruct((t, CONF_WIDTH), F32),
        ],
        compiler_params=_params(("arbitrary",)),
        name="inproj",
    )(x, g, wqkv, wz, wbd, wglu, bglu, pa, pb)


def _split_bf16(a):
    hi = a.astype(BF16)
    lo = (a - hi.astype(F32)).astype(BF16)
    return hi, lo


def _dot3(a_hi, a_lo, b_hi, b_lo):
    dot = functools.partial(jnp.dot, preferred_element_type=F32)
    return dot(a_hi, b_hi) + dot(a_lo, b_hi) + dot(a_hi, b_lo)


def _unit_lower_inverses(ms, eye, n):
    levels = int(math.log2(n)) - 1
    ps = [eye - m for m in ms]
    pw = [_split_bf16(-m) for m in ms]
    pw = [_split_bf16(_dot3(h, l, h, l)) for h, l in pw]
    for level in range(1, levels + 1):
        last = level == levels
        nxt = []
        for u, (h, l) in enumerate(pw):
            ph, pl_ = _split_bf16(ps[u])
            if last:
                ps[u] = ps[u] + _dot3(ph, pl_, h, l)
            else:
                out = _dot3(jnp.concatenate([h, ph], axis=0), jnp.concatenate([l, pl_], axis=0), h, l)
                nxt.append(_split_bf16(out[:n]))
                ps[u] = ps[u] + out[n:]
        pw = nxt
    return ps


def _gdn_kernel(chunk, x_ref, z_ref, bd_ref, bdt_ref, hist_ref, s0_ref, wc_ref, won_ref,
                o_ref, s_ref, xbuf):
    i = pl.program_id(1)
    nb, tt = x_ref.shape[0], x_ref.shape[1]
    nchunk = tt // chunk
    pad = SUBLANES
    hist_rows = QKV_TAPS - 1

    @pl.when(i == 0)
    def _():
        xbuf[:, pad - hist_rows:pad, :] = hist_ref[...]
        s_ref[...] = s0_ref[...]

    xbuf[:, pad:pad + tt, :] = x_ref[...]

    ii = lax.broadcasted_iota(I32, (chunk, chunk), 0)
    jj = lax.broadcasted_iota(I32, (chunk, chunk), 1)
    lower_incl = ii >= jj
    lower_strict = ii > jj
    eye = (ii == jj).astype(F32)
    won = won_ref[...]

    def conv_silu(b, col0):
        cols = slice(col0, col0 + HEAD_DIM)
        acc = wc_ref[hist_rows:hist_rows + 1, cols] * xbuf[b, pad:pad + tt, cols]
        for j in range(hist_rows):
            r0 = pad - hist_rows + j
            acc = acc + wc_ref[j:j + 1, cols] * xbuf[b, r0:r0 + tt, cols]
        return acc * _sigmoid(acc)

    units = []
    for b in range(nb):
        bd = bd_ref[b]
        bdt = bdt_ref[b]
        for h in range(GDN_HEADS):
            q = conv_silu(b, h * HEAD_DIM)
            k = conv_silu(b, GDN_WIDTH + h * HEAD_DIM)
            v = conv_silu(b, 2 * GDN_WIDTH + h * HEAD_DIM)
            q = q * lax.rsqrt(jnp.sum(q * q, axis=-1, keepdims=True) + L2_EPS) * (HEAD_DIM ** -0.5)
            k = k * lax.rsqrt(jnp.sum(k * k, axis=-1, keepdims=True) + L2_EPS)
            kt = k.T
            beta = bd[:, h:h + 1]
            gcol = bd[:, GDN_HEADS + h:GDN_HEADS + h + 1]
            grow = bdt[GDN_HEADS + h:GDN_HEADS + h + 1, :]
            for c in range(nchunk):
                rows = slice(c * chunk, (c + 1) * chunk)
                qc, kc, vc, ktc = q[rows], k[rows], v[rows], kt[:, rows]
                bc, gc, gr = beta[rows], gcol[rows], grow[:, rows]
                dec = jnp.exp(jnp.where(lower_incl, gc - gr, -jnp.inf))
                kb = kc * bc
                eg = jnp.exp(gc)
                both = _dotb(jnp.concatenate([kb, qc], axis=0), ktc)
                units.append(dict(
                    b=b, h=h, c=c,
                    m=both[:chunk] * jnp.where(lower_strict, dec, 0.0),
                    qk=both[chunk:] * dec,
                    rhs=jnp.concatenate([vc * bc, kb * eg], axis=1),
                    qg=qc * eg,
                    kdt=ktc * jnp.exp(gr[:, chunk - 1:chunk] - gr),
                    d_last=jnp.exp(gc[chunk - 1:chunk, :]),
                ))

    tinvs = _unit_lower_inverses([u["m"] for u in units], eye, chunk)
    for u, tinv in zip(units, tinvs):
        th, tl = _split_bf16(tinv)
        rh, rl = _split_bf16(u["rhs"])
        sol = _dot3(th, tl, rh, rl)
        u["u"], u["w"] = sol[:, :HEAD_DIM], sol[:, HEAD_DIM:]

    state = {(b, h): s_ref[b, h] for b in range(nb) for h in range(GDN_HEADS)}
    by_key = {(u["b"], u["h"], u["c"]): u for u in units}
    for c in range(nchunk):
        rows = slice(c * chunk, (c + 1) * chunk)
        for b in range(nb):
            for h in range(GDN_HEADS):
                u = by_key[(b, h, c)]
                s = state[(b, h)]
                ws = _dotb(jnp.concatenate([u["w"], u["qg"]], axis=0), s)
                v_new = u["u"] - ws[:chunk]
                upd = _dotb(jnp.concatenate([u["qk"], u["kdt"]], axis=0), v_new)
                o_c = ws[chunk:] + upd[:chunk]
                state[(b, h)] = s * u["d_last"] + upd[chunk:]
                cols = slice(h * HEAD_DIM, (h + 1) * HEAD_DIM)
                on = o_c * lax.rsqrt(jnp.mean(o_c * o_c, axis=-1, keepdims=True) + RMS_EPS) * won
                zc = z_ref[b, rows, cols]
                o_ref[b, rows, cols] = on * (zc * _sigmoid(zc))
    for (b, h), s in state.items():
        s_ref[b, h] = s

    xbuf[:, pad - hist_rows:pad, :] = x_ref[:, tt - hist_rows:tt, :]


def _gdn(qkv, z, bd, bdt, hist, s0, wconv, wonorm, chunk, tt, nb):
    b, t, _ = qkv.shape
    nt = t // tt
    return pl.pallas_call(
        functools.partial(_gdn_kernel, chunk),
        grid=(b // nb, nt),
        in_specs=[
            pl.BlockSpec((nb, tt, QKV_COLS), lambda bi, i: (bi, i, 0)),
            pl.BlockSpec((nb, tt, GDN_WIDTH), lambda bi, i: (bi, i, 0)),
            pl.BlockSpec((nb, tt, 2 * GDN_HEADS), lambda bi, i: (bi, i, 0)),
            pl.BlockSpec((nb, 2 * GDN_HEADS, tt), lambda bi, i: (bi, 0, i)),
            pl.BlockSpec((nb, QKV_TAPS - 1, QKV_COLS), lambda bi, i: (bi, 0, 0)),
            pl.BlockSpec((nb, GDN_HEADS, HEAD_DIM, HEAD_DIM), lambda bi, i: (bi, 0, 0, 0)),
            pl.BlockSpec((QKV_TAPS, QKV_COLS), lambda bi, i: (0, 0)),
            pl.BlockSpec((1, HEAD_DIM), lambda bi, i: (0, 0)),
        ],
        out_specs=[
            pl.BlockSpec((nb, tt, GDN_WIDTH), lambda bi, i: (bi, i, 0)),
            pl.BlockSpec((nb, GDN_HEADS, HEAD_DIM, HEAD_DIM), lambda bi, i: (bi, 0, 0, 0)),
        ],
        out_shape=[
            jax.ShapeDtypeStruct((b, t, GDN_WIDTH), F32),
            jax.ShapeDtypeStruct((b, GDN_HEADS, HEAD_DIM, HEAD_DIM), F32),
        ],
        scratch_shapes=[pltpu.VMEM((nb, tt + SUBLANES, QKV_COLS), F32)],
        compiler_params=_params(("arbitrary", "arbitrary")),
        name="gdn",
    )(qkv, z, bd, bdt, hist, s0, wconv, wonorm)


def _cconv_kernel(carry, u_ref, hist_ref, w_ref, b_ref, lg_ref, lb_ref, c_ref, ubuf):
    i = pl.program_id(1)
    tt = u_ref.shape[0]
    hist_rows = DW_TAPS - 1
    pad = 4 * SUBLANES
    base = pad - hist_rows

    @pl.when(i == 0)
    def _():
        ubuf[base:pad, :] = hist_ref[...]

    ubuf[pad:pad + tt, :] = u_ref[...]
    rows = min(CONV_ROWS, tt)
    for r0 in range(0, tt, rows):
        acc = w_ref[0:1, :] * ubuf[base + r0:base + r0 + rows, :]
        for j in range(1, DW_TAPS):
            acc = acc + w_ref[j:j + 1, :] * ubuf[base + r0 + j:base + r0 + j + rows, :]
        cv = acc + b_ref[...]
        mu = jnp.mean(cv, axis=-1, keepdims=True)
        xc = cv - mu
        var = jnp.mean(xc * xc, axis=-1, keepdims=True)
        y = xc * lax.rsqrt(var + LN_EPS) * lg_ref[...] + lb_ref[...]
        c_ref[r0:r0 + rows, :] = y * _sigmoid(y)
    if carry:
        ubuf[base:pad, :] = u_ref[tt - hist_rows:tt, :]


def _cconv(uu, hist, w, b, lg, lb, tt):
    bsz, t, _ = uu.shape
    nt = t // tt
    vec = pl.BlockSpec((1, CONF_WIDTH), lambda bi, i: (0, 0))
    return pl.pallas_call(
        functools.partial(_cconv_kernel, nt > 1),
        grid=(bsz, nt),
        in_specs=[
            pl.BlockSpec((None, tt, CONF_WIDTH), lambda bi, i: (bi, i, 0)),
            pl.BlockSpec((None, DW_TAPS - 1, CONF_WIDTH), lambda bi, i: (bi, 0, 0)),
            pl.BlockSpec((DW_TAPS, CONF_WIDTH), lambda bi, i: (0, 0)),
            vec, vec, vec,
        ],
        out_specs=pl.BlockSpec((None, tt, CONF_WIDTH), lambda bi, i: (bi, i, 0)),
        out_shape=jax.ShapeDtypeStruct((bsz, t, CONF_WIDTH), F32),
        scratch_shapes=[pltpu.VMEM((tt + 4 * SUBLANES, CONF_WIDTH), F32)],
        compiler_params=_params(("arbitrary", "arbitrary")),
        name="cconv",
    )(uu, hist, w, b, lg, lb)


def _outproj_kernel(tiles_a, oa_ref, ca_ref, xa_ref, ob_ref, cb_ref, xb_ref, *rest):
    i = pl.program_id(0)

    @pl.when(i < tiles_a)
    def _():
        _outproj_tile(oa_ref, ca_ref, xa_ref, *rest)

    @pl.when(i >= tiles_a)
    def _():
        _outproj_tile(ob_ref, cb_ref, xb_ref, *rest)


def _outproj_tile(o_ref, c_ref, x_ref, wo_ref, wc_ref, g_ref, wr_ref, br_ref, upper_ref,
                  x1_ref, h2_ref, metac_ref, metar_ref, cnt_ref):
    tm = x_ref.shape[0]

    mix = _dotb(o_ref[...], wo_ref[...]) + _dotb(c_ref[...], wc_ref[...])
    x1 = x_ref[...] + mix
    x1_ref[...] = x1
    h2 = x1 * lax.rsqrt(jnp.mean(x1 * x1, axis=-1, keepdims=True) + RMS_EPS) * g_ref[...]
    h2_ref[...] = h2.astype(BF16)
    logits = lax.dot_general(wr_ref[...], h2.astype(BF16), (((1,), (1,)), ((), ())),
                             preferred_element_type=F32) + br_ref[...]
    row = lax.broadcasted_iota(I32, (N_EXPERTS, tm), 0).astype(F32)
    vals = logits
    idxs, tops = [], []
    for _ in range(TOP_K):
        m = jnp.max(vals, axis=0, keepdims=True)
        idx = jnp.min(jnp.where(vals == m, row, float(N_EXPERTS)), axis=0, keepdims=True)
        idxs.append(idx)
        tops.append(m)
        vals = jnp.where(row == idx, -jnp.inf, vals)
    exps = [jnp.exp(v - tops[0]) for v in tops]
    den = exps[0] + exps[1] + exps[2] + exps[3]
    onehots = [(row == idx).astype(F32) for idx in idxs]
    chosen = onehots[0] + onehots[1] + onehots[2] + onehots[3]
    before = jnp.dot(chosen.astype(BF16), upper_ref[...], preferred_element_type=F32)
    ranks = [jnp.sum(oh * before, axis=0, keepdims=True) for oh in onehots]
    gates = [e / den for e in exps]
    meta = jnp.concatenate(idxs + ranks + gates + [jnp.zeros((LANES - 3 * TOP_K, tm), F32)], axis=0)
    metar_ref[...] = meta[:META_ROWS]
    metac_ref[...] = meta.T
    cnt_ref[...] = jnp.broadcast_to(jnp.sum(chosen, axis=1, keepdims=True), (N_EXPERTS, LANES))[None]


def _outproj(first, second, wo, wc, g, wr, br):
    t_a, t_b = first[2].shape[0], second[2].shape[0]
    tm = TOKEN_TILE
    upper = (jnp.arange(tm)[:, None] < jnp.arange(tm)[None, :]).astype(BF16)
    tiles_a = t_a // tm
    t_total = t_a + t_b
    full = lambda shape: pl.BlockSpec(shape, lambda i: (0, 0))
    rows_a = lambda width: pl.BlockSpec((tm, width), lambda i: (jnp.minimum(i, tiles_a - 1), 0))
    rows_b = lambda width: pl.BlockSpec((tm, width), lambda i: (jnp.maximum(i - tiles_a, 0), 0))
    widths = (GDN_WIDTH, CONF_WIDTH, D_MODEL)
    return pl.pallas_call(
        functools.partial(_outproj_kernel, tiles_a),
        grid=(t_total // tm,),
        in_specs=[rows_a(w) for w in widths] + [rows_b(w) for w in widths] + [
            full((GDN_WIDTH, D_MODEL)),
            full((CONF_WIDTH, D_MODEL)),
            full((1, D_MODEL)),
            full((N_EXPERTS, D_MODEL)),
            full((N_EXPERTS, 1)),
            full((tm, tm)),
        ],
        out_specs=[
            pl.BlockSpec((tm, D_MODEL), lambda i: (i, 0)),
            pl.BlockSpec((tm, D_MODEL), lambda i: (i, 0)),
            pl.BlockSpec((tm, LANES), lambda i: (i, 0)),
            pl.BlockSpec((META_ROWS, tm), lambda i: (0, i)),
            pl.BlockSpec((1, N_EXPERTS, LANES), lambda i: (i, 0, 0)),
        ],
        out_shape=[
            jax.ShapeDtypeStruct((t_total, D_MODEL), F32),
            jax.ShapeDtypeStruct((t_total, D_MODEL), BF16),
            jax.ShapeDtypeStruct((t_total, LANES), F32),
            jax.ShapeDtypeStruct((META_ROWS, t_total), F32),
            jax.ShapeDtypeStruct((t_total // tm, N_EXPERTS, LANES), F32),
        ],
        compiler_params=_params(("arbitrary",)),
        name="outproj_router",
    )(*first, *second, wo, wc, g, wr, br, upper)


def _segment_copies(i, seg_local_ref, seg_dst_ref, seg_n_ref, make_copy, wait):
    def per_expert(ex, carry):
        j = i * N_EXPERTS + ex
        local, dst, n = seg_local_ref[j], seg_dst_ref[j], seg_n_ref[j]
        done = 0
        for per_copy, count in ((8, n >> 3), (2, (n >> 1) & 3), (1, n & 1)):
            def body(g, c, per_copy=per_copy, done=done):
                off = (done + g * per_copy) * GRANULE
                cp = make_copy(per_copy * GRANULE, pl.multiple_of(local + off, GRANULE),
                               pl.multiple_of(dst + off, GRANULE))
                cp.wait() if wait else cp.start()
                return c

            lax.fori_loop(0, count, body, 0)
            done = done + count * per_copy
        return carry

    lax.fori_loop(0, N_EXPERTS, per_expert, 0)


def _gap_copies(gap_start_ref, gap_n_ref, make_copy, wait):
    def per_expert(ex, carry):
        n = gap_n_ref[ex]
        start = pl.multiple_of(gap_start_ref[ex], GRANULE)
        off = 0 * n
        for bit in reversed(range(GAP_BITS)):
            rows = GRANULE << bit

            @pl.when(((n >> bit) & 1) == 1)
            def _(rows=rows, off=off):
                cp = make_copy(rows, pl.multiple_of(start + off, GRANULE))
                cp.wait() if wait else cp.start()

            off = off + ((n >> bit) & 1) * rows
        return carry

    lax.fori_loop(0, N_EXPERTS, per_expert, 0)


def _local_rows(meta_e, seg_local_ref, i):
    base = jnp.zeros(meta_e.shape, F32)
    for ex in range(N_EXPERTS):
        base = jnp.where(meta_e == float(ex), seg_local_ref[i * N_EXPERTS + ex].astype(F32), base)
    return base


def _dispatch_kernel(seg_local_ref, seg_dst_ref, seg_n_ref, gap_start_ref, gap_n_ref, tail_ref,
                     h_ref, mr_ref, xs_ref, xloc, zbuf, sems, zsem):
    i = pl.program_id(0)
    last = pl.num_programs(0) - 1
    tm = h_ref.shape[0]
    rt = xloc.shape[1]
    zrows = zbuf.shape[0]
    slot = i % 2

    def gap_copy(rows, dst):
        return pltpu.make_async_copy(zbuf.at[pl.ds(0, rows)], xs_ref.at[pl.ds(dst, rows)], zsem)

    def tail_copies(wait):
        def body(j, c):
            cp = gap_copy(zrows, pl.multiple_of(tail_ref[0] + j * zrows, zrows))
            cp.wait() if wait else cp.start()
            return c
        lax.fori_loop(0, tail_ref[1], body, 0)

    @pl.when(i == 0)
    def _():
        zbuf[...] = jnp.zeros(zbuf.shape, zbuf.dtype)
        _gap_copies(gap_start_ref, gap_n_ref, gap_copy, wait=False)
        tail_copies(wait=False)

    mr = mr_ref[...]
    dl = (_local_rows(mr[META_EXPERT:META_EXPERT + TOP_K], seg_local_ref, i)
          + mr[META_RANK:META_RANK + TOP_K])
    rr = lax.broadcasted_iota(I32, (rt, tm), 0).astype(F32)
    hit = rr == dl[0:1]
    for k in range(1, TOP_K):
        hit = hit | (rr == dl[k:k + 1])
    xloc[slot] = jnp.dot(hit.astype(BF16), h_ref[...], preferred_element_type=F32)

    def seg_copy(buf):
        def make(rows, local, dst):
            return pltpu.make_async_copy(xloc.at[buf, pl.ds(local, rows)],
                                         xs_ref.at[pl.ds(dst, rows)], sems.at[buf])
        return make

    _segment_copies(i, seg_local_ref, seg_dst_ref, seg_n_ref, seg_copy(slot), wait=False)

    @pl.when(i > 0)
    def _():
        _segment_copies(i - 1, seg_local_ref, seg_dst_ref, seg_n_ref, seg_copy(1 - slot), wait=True)

    @pl.when(i == last)
    def _():
        _segment_copies(i, seg_local_ref, seg_dst_ref, seg_n_ref, seg_copy(slot), wait=True)

    @pl.when(i == 0)
    def _():
        _gap_copies(gap_start_ref, gap_n_ref, gap_copy, wait=True)
        tail_copies(wait=True)


def _dispatch(tables, h2, metar, n_rows):
    t = h2.shape[0]
    tm = TOKEN_TILE
    return pl.pallas_call(
        _dispatch_kernel,
        grid_spec=pltpu.PrefetchScalarGridSpec(
            num_scalar_prefetch=6,
            grid=(t // tm,),
            in_specs=[
                pl.BlockSpec((tm, D_MODEL), lambda i, *_: (i, 0)),
                pl.BlockSpec((META_ROWS, tm), lambda i, *_: (0, i)),
            ],
            out_specs=pl.BlockSpec(memory_space=pl.ANY),
            scratch_shapes=[
                pltpu.VMEM((2, TILE_ROWS, D_MODEL), F32),
                pltpu.VMEM((GRANULE << (GAP_BITS - 1), D_MODEL), F32),
                pltpu.SemaphoreType.DMA((2,)),
                pltpu.SemaphoreType.DMA,
            ],
        ),
        out_shape=jax.ShapeDtypeStruct((n_rows, D_MODEL), F32),
        compiler_params=_params(("arbitrary",)),
        name="dispatch",
    )(*tables, h2, metar)


def _expert_kernel(be_ref, rb_ref, nu_ref, first_ref, slot_ref, next_ref,
                   x_ref, wgu_hbm, bgu_ref, wd_hbm, bdn_ref, y_ref,
                   wgu_f32, wd_f32, wgu_bf, wd_bf, sems):
    i = pl.program_id(0)

    def fetch(expert, slot):
        return (pltpu.make_async_copy(wgu_hbm.at[expert], wgu_f32.at[slot], sems.at[slot, 0]),
                pltpu.make_async_copy(wd_hbm.at[expert], wd_f32.at[slot], sems.at[slot, 1]))

    @pl.when(i == 0)
    def _():
        for cp in fetch(be_ref[0], 0):
            cp.start()

    @pl.when(first_ref[i] == 1)
    def _():
        slot = slot_ref[i]
        for cp in fetch(be_ref[i], slot):
            cp.wait()
        wgu_bf[...] = wgu_f32[slot].astype(BF16)
        wd_bf[...] = wd_f32[slot].astype(BF16)

        @pl.when(next_ref[i] >= 0)
        def _():
            for cp in fetch(next_ref[i], 1 - slot):
                cp.start()

    @pl.when(i < nu_ref[0])
    def _():
        gu = _dotb(x_ref[...], wgu_bf[...]) + bgu_ref[...]
        x_glu = jnp.minimum(gu[:, :D_EXPERT], SWIGLU_LIMIT)
        x_lin = jnp.clip(gu[:, D_EXPERT:], -SWIGLU_LIMIT, SWIGLU_LIMIT)
        act = x_glu * _sigmoid(SWIGLU_ALPHA * x_glu) * (x_lin + 1.0)
        y_ref[...] = _dotb(act, wd_bf[...]) + bdn_ref[...]

    @pl.when(i >= nu_ref[0])
    def _():
        y_ref[...] = jnp.zeros(y_ref.shape, y_ref.dtype)


def _experts(tables, xs, wgu, bgu, wd, bdn):
    rows = xs.shape[0]
    bm = EXPERT_ROWS
    nb = rows // bm
    return pl.pallas_call(
        _expert_kernel,
        grid_spec=pltpu.PrefetchScalarGridSpec(
            num_scalar_prefetch=6,
            grid=(nb,),
            in_specs=[
                pl.BlockSpec((bm, D_MODEL), lambda i, be, rb, *_: (rb[i], 0)),
                pl.BlockSpec(memory_space=pl.ANY),
                pl.BlockSpec((None, 1, 2 * D_EXPERT), lambda i, be, *_: (be[i], 0, 0)),
                pl.BlockSpec(memory_space=pl.ANY),
                pl.BlockSpec((None, 1, D_MODEL), lambda i, be, *_: (be[i], 0, 0)),
            ],
            out_specs=pl.BlockSpec((bm, D_MODEL), lambda i, *_: (i, 0)),
            scratch_shapes=[
                pltpu.VMEM((2, D_MODEL, 2 * D_EXPERT), F32),
                pltpu.VMEM((2, D_EXPERT, D_MODEL), F32),
                pltpu.VMEM((D_MODEL, 2 * D_EXPERT), BF16),
                pltpu.VMEM((D_EXPERT, D_MODEL), BF16),
                pltpu.SemaphoreType.DMA((2, 2)),
            ],
        ),
        out_shape=jax.ShapeDtypeStruct((rows, D_MODEL), F32),
        compiler_params=pltpu.CompilerParams(dimension_semantics=("arbitrary",),
                                             vmem_limit_bytes=EXPERT_VMEM_LIMIT),
        name="experts",
    )(*tables, xs, wgu, bgu, wd, bdn)


def _combine_kernel(tiles_a, seg_local_ref, seg_dst_ref, seg_n_ref, x1_ref, mc_ref, yb_ref, gf_ref,
                    ya_ref, yb_out_ref, ybuf, sems):
    i = pl.program_id(0)
    n_tiles = pl.num_programs(0)
    tm = x1_ref.shape[0]
    rt = ybuf.shape[1]
    slot = i % 2

    def seg_copy(buf):
        def make(rows, local, src):
            return pltpu.make_async_copy(yb_ref.at[pl.ds(src, rows)],
                                         ybuf.at[buf, pl.ds(local, rows)], sems.at[buf])
        return make

    @pl.when(i == 0)
    def _():
        ybuf[...] = jnp.zeros(ybuf.shape, ybuf.dtype)
        _segment_copies(i, seg_local_ref, seg_dst_ref, seg_n_ref, seg_copy(slot), wait=False)

    @pl.when(i + 1 < n_tiles)
    def _():
        _segment_copies(i + 1, seg_local_ref, seg_dst_ref, seg_n_ref, seg_copy(1 - slot), wait=False)

    mc = mc_ref[...]
    dl = (_local_rows(mc[:, META_EXPERT:META_EXPERT + TOP_K], seg_local_ref, i)
          + mc[:, META_RANK:META_RANK + TOP_K])
    gate = mc[:, META_GATE:META_GATE + TOP_K]
    rr = lax.broadcasted_iota(I32, (tm, rt), 1).astype(F32)
    weights = jnp.where(rr == dl[:, 0:1], gate[:, 0:1], 0.0)
    for k in range(1, TOP_K):
        weights = jnp.where(rr == dl[:, k:k + 1], gate[:, k:k + 1], weights)
    _segment_copies(i, seg_local_ref, seg_dst_ref, seg_n_ref, seg_copy(slot), wait=True)
    acc = x1_ref[...] + jnp.dot(weights.astype(BF16), ybuf[slot].astype(BF16),
                                preferred_element_type=F32)
    y = acc * lax.rsqrt(jnp.mean(acc * acc, axis=-1, keepdims=True) + RMS_EPS) * gf_ref[...]

    @pl.when(i < tiles_a)
    def _():
        ya_ref[...] = y

    @pl.when(i >= tiles_a)
    def _():
        yb_out_ref[...] = y


def _combine(tables, x1, metac, yb, gfin, n_a):
    t = x1.shape[0]
    tm = TOKEN_TILE
    tiles_a = n_a // tm
    return pl.pallas_call(
        functools.partial(_combine_kernel, tiles_a),
        grid_spec=pltpu.PrefetchScalarGridSpec(
            num_scalar_prefetch=3,
            grid=(t // tm,),
            in_specs=[
                pl.BlockSpec((tm, D_MODEL), lambda i, *_: (i, 0)),
                pl.BlockSpec((tm, LANES), lambda i, *_: (i, 0)),
                pl.BlockSpec(memory_space=pl.ANY),
                pl.BlockSpec((1, D_MODEL), lambda i, *_: (0, 0)),
            ],
            out_specs=[
                pl.BlockSpec((tm, D_MODEL), lambda i, *_: (jnp.minimum(i, tiles_a - 1), 0)),
                pl.BlockSpec((tm, D_MODEL), lambda i, *_: (jnp.maximum(i - tiles_a, 0), 0)),
            ],
            scratch_shapes=[pltpu.VMEM((2, TILE_ROWS, D_MODEL), F32),
                            pltpu.SemaphoreType.DMA((2,))],
        ),
        out_shape=[jax.ShapeDtypeStruct((n_a, D_MODEL), F32),
                   jax.ShapeDtypeStruct((t - n_a, D_MODEL), F32)],
        compiler_params=pltpu.CompilerParams(dimension_semantics=("arbitrary",),
                                             vmem_limit_bytes=EXPERT_VMEM_LIMIT),
        name="combine",
    )(*tables, x1, metac, yb, gfin)


def kernel(x_prompt, x_sample, state_gdn, state_qkv_conv, state_dwconv, norm_mix_g, w_in, w_conv_qkv,
           a_log, dt_bias, w_onorm, b_glu, w_dw, b_dw, ln_g, ln_b, w_out, norm_ffn_g, w_router,
           b_router, w_gate_up, b_gate_up, w_down, b_down, norm_final_g):
    bp, tp, _ = x_prompt.shape
    bs, ts, _ = x_sample.shape
    n_p, n_s = bp * tp, bs * ts

    wi = w_in[0]
    c_z = QKV_COLS
    c_bd = c_z + GDN_WIDTH
    c_glu = c_bd + 2 * GDN_HEADS
    wqkv = wi[:, :c_z].astype(BF16)
    wz = wi[:, c_z:c_bd].astype(BF16)
    wbd = jnp.pad(wi[:, c_bd:c_glu], ((0, 0), (0, LANES - 2 * GDN_HEADS))).astype(BF16)
    wglu = wi[:, c_glu:].astype(BF16)
    bglu = b_glu[0][None, :]
    lane_pad = (GDN_HEADS, LANES - 2 * GDN_HEADS)
    pa = jnp.pad(-jnp.exp(a_log[0].astype(F32)), lane_pad)[None, :]
    pb = jnp.pad(dt_bias[0].astype(F32), lane_pad)[None, :]
    gmix = norm_mix_g[0][None, :]
    wconv = w_conv_qkv[0]
    wonorm = w_onorm[0][None, :]
    wdw, bdw = w_dw[0], b_dw[0][None, :]
    lng, lnb = ln_g[0][None, :], ln_b[0][None, :]
    wo = w_out[0][:GDN_WIDTH].astype(BF16)
    wc = w_out[0][GDN_WIDTH:].astype(BF16)
    gffn = norm_ffn_g[0][None, :]
    wr, br = w_router[0].T.astype(BF16), b_router[0].astype(F32)[:, None]
    bgu = b_gate_up[0][:, None, :]
    bdn = b_down[0][:, None, :]
    gfin = norm_final_g[None, :]

    xp = x_prompt.reshape(n_p, D_MODEL)
    xs_tok = x_sample.reshape(n_s, D_MODEL)

    def mixers(x2d, bsz, t, hist_qkv, s0, hist_glu, chunk, gdn_tile, gdn_rows, conv_tile):
        qkv, z, bd, bdt, uu = _inproj(x2d, gmix, wqkv, wz, wbd, wglu, bglu, pa, pb, chunk)
        qkv3 = qkv.reshape(bsz, t, QKV_COLS)
        uu3 = uu.reshape(bsz, t, CONF_WIDTH)
        bdt3 = bdt.reshape(2 * GDN_HEADS, bsz, t).transpose(1, 0, 2)
        o, s_new = _gdn(qkv3, z.reshape(bsz, t, GDN_WIDTH), bd.reshape(bsz, t, 2 * GDN_HEADS),
                        bdt3, hist_qkv, s0, wconv, wonorm, chunk, gdn_tile, gdn_rows)
        cc = _cconv(uu3, hist_glu, wdw, bdw, lng, lnb, conv_tile)
        return o.reshape(bsz * t, GDN_WIDTH), cc.reshape(bsz * t, CONF_WIDTH), s_new, qkv3, uu3

    zero_qkv = jnp.zeros((bp, QKV_TAPS - 1, QKV_COLS), F32)
    zero_s = jnp.zeros((bp, GDN_HEADS, HEAD_DIM, HEAD_DIM), F32)
    zero_glu = jnp.zeros((bp, DW_TAPS - 1, CONF_WIDTH), F32)
    o_p, c_p, s_p, qkv_p, uu_p = mixers(xp, bp, tp, zero_qkv, zero_s, zero_glu,
                                        PROMPT_CHUNK, GDN_TILE, 1, CONV_TILE)
    o_s, c_s, s_s, qkv_s, uu_s = mixers(xs_tok, bs, ts, state_qkv_conv[0], state_gdn[0],
                                        state_dwconv[0], ts, ts, GDN_SAMPLE_ROWS, ts)

    n_tok = n_p + n_s
    x1, h2, metac, metar, cnt_tile = _outproj((o_p, c_p, xp), (o_s, c_s, xs_tok),
                                              wo, wc, gffn, wr, br)

    bm = EXPERT_ROWS
    n_tiles = n_tok // TOKEN_TILE
    counts = cnt_tile[:, :, 0].astype(I32)
    seg_rows = (counts + GRANULE - 1) // GRANULE * GRANULE
    seg_local = jnp.cumsum(seg_rows, axis=1) - seg_rows
    seg_before = jnp.cumsum(seg_rows, axis=0) - seg_rows
    rows_e = jnp.sum(seg_rows, axis=0)
    padded = (rows_e + bm - 1) // bm * bm
    pend = jnp.cumsum(padded)
    pstart = pend - padded
    seg_dst = pstart[None, :] + seg_before
    n_used = jnp.maximum(pend[-1] // bm, 1).astype(I32)
    max_rows = n_tok * TOP_K + n_tiles * N_EXPERTS * (GRANULE - 1) + N_EXPERTS * (bm - 1)
    n_blocks = -(-max_rows // bm)
    blk = jnp.minimum(jnp.arange(n_blocks, dtype=I32), n_used - 1)
    block_expert = jnp.minimum(
        jnp.sum((pend[None, :] <= (blk * bm)[:, None]).astype(I32), axis=1), N_EXPERTS - 1)
    seg_tables = (seg_local.reshape(-1).astype(I32), seg_dst.reshape(-1).astype(I32),
                  (seg_rows // GRANULE).reshape(-1).astype(I32))
    gap_tables = ((pstart + rows_e).astype(I32), ((padded - rows_e) // GRANULE).astype(I32))
    zero_rows = GRANULE << (GAP_BITS - 1)
    tail_table = jnp.stack([pend[-1], (n_blocks * bm - pend[-1]) // zero_rows]).astype(I32)

    xs = _dispatch(seg_tables + gap_tables + (tail_table,), h2, metar, n_blocks * bm)
    ids = jnp.arange(N_EXPERTS, dtype=I32)
    active = padded > 0
    later = active[None, :] & (ids[None, :] > ids[:, None])
    next_active = jnp.min(jnp.where(later, ids[None, :], N_EXPERTS), axis=1)
    next_active = jnp.where(next_active == N_EXPERTS, -1, next_active)
    order = jnp.cumsum(active.astype(I32)) - 1
    steps = jnp.arange(n_blocks, dtype=I32)
    first = ((steps * bm == pstart[block_expert]) & (steps < n_used)).astype(I32)
    expert_tables = (block_expert, blk, n_used.reshape(1), first,
                     (order[block_expert] % 2).astype(I32), next_active[block_expert].astype(I32))
    yb = _experts(expert_tables, xs, w_gate_up[0], bgu, w_down[0], bdn)
    y_p, y_s = _combine(seg_tables, x1, metac, yb, gfin, n_p)

    hist_glu_s = jnp.concatenate([state_dwconv[0], uu_s], axis=1)[:, -(DW_TAPS - 1):]
    return (y_p.reshape(bp, tp, D_MODEL),
            y_s.reshape(bs, ts, D_MODEL),
            s_p[None],
            qkv_p[:, -(QKV_TAPS - 1):][None],
            uu_p[:, -(DW_TAPS - 1):][None],
            s_s[None],
            qkv_s[:, -(QKV_TAPS - 1):][None],
            hist_glu_s[None])
```

```python
import functools
import math

import jax
import jax.numpy as jnp
from jax import lax
from jax.experimental import pallas as pl
from jax.experimental.pallas import tpu as pltpu

F32 = jnp.float32
BF16 = jnp.bfloat16
I32 = jnp.int32
HIGHEST = lax.Precision.HIGHEST

D_MODEL = 1024
GDN_HEADS = 4
HEAD_DIM = 128
GDN_WIDTH = GDN_HEADS * HEAD_DIM
QKV_COLS = 3 * GDN_WIDTH
CONF_WIDTH = 512
QKV_TAPS = 4
DW_TAPS = 31
N_EXPERTS = 32
TOP_K = 4
D_EXPERT = 1024
SWIGLU_LIMIT = 7.0
SWIGLU_ALPHA = 1.702
RMS_EPS = 1e-6
LN_EPS = 1e-5
L2_EPS = 1e-6
PROMPT_CHUNK = 64

LANES = 128
SUBLANES = 8
TOKEN_TILE = 512
GDN_TILE = 256
GDN_SAMPLE_ROWS = 4
CONV_TILE = 256
CONV_ROWS = 32
EXPERT_ROWS = 256
GRANULE = SUBLANES
GAP_BITS = 5
TILE_ROWS = -(-(TOKEN_TILE * TOP_K + N_EXPERTS * (GRANULE - 1)) // 256) * 256
META_EXPERT, META_RANK, META_GATE, META_ROWS = 0, TOP_K, 2 * TOP_K, 16
assert EXPERT_ROWS // GRANULE <= 1 << GAP_BITS
SEL_CHUNK = (64, 256)
VMEM_LIMIT = 48 * 1024 * 1024
EXPERT_VMEM_LIMIT = 56 * 1024 * 1024


def _dotb(a, b):
    return jnp.dot(a.astype(BF16), b.astype(BF16), preferred_element_type=F32)


def _dotf(a, b):
    return jnp.dot(a, b, precision=HIGHEST, preferred_element_type=F32)


def _sigmoid(x):
    return jax.nn.sigmoid(x)


def _params(sem):
    return pltpu.CompilerParams(dimension_semantics=sem, vmem_limit_bytes=VMEM_LIMIT)


def _inproj_kernel(chunk_shift, x_ref, g_ref, wqkv_ref, wz_ref, wbd_ref, wglu_ref, bglu_ref,
                   pa_ref, pb_ref, qkv_ref, z_ref, bd_ref, bdt_ref, uu_ref):
    x = x_ref[...]
    tm = x.shape[0]
    ms = jnp.mean(x * x, axis=-1, keepdims=True)
    h = (x * lax.rsqrt(ms + RMS_EPS) * g_ref[...]).astype(BF16)
    qkv_ref[...] = jnp.dot(h, wqkv_ref[...], preferred_element_type=F32)
    z_ref[...] = jnp.dot(h, wz_ref[...], preferred_element_type=F32)
    glu = jnp.dot(h, wglu_ref[...], preferred_element_type=F32) + bglu_ref[...]
    uu_ref[...] = glu[:, :CONF_WIDTH] * _sigmoid(glu[:, CONF_WIDTH:])
    bd = jnp.dot(h, wbd_ref[...], preferred_element_type=F32)
    beta = _sigmoid(bd)
    v = bd + pb_ref[...]
    softplus = jnp.maximum(v, 0.0) + jnp.log1p(jnp.exp(-jnp.abs(v)))
    g = pa_ref[...] * softplus
    r = lax.broadcasted_iota(I32, (tm, tm), 0)
    c = lax.broadcasted_iota(I32, (tm, tm), 1)
    tri = ((r >= c) & ((r >> chunk_shift) == (c >> chunk_shift))).astype(BF16)
    g_hi, g_rest = _split_bf16(g)
    g_mid = g - g_hi.astype(F32) - g_rest.astype(F32)
    pieces = jnp.concatenate([g_hi, g_rest, g_mid.astype(BF16)], axis=1)
    sums = jnp.dot(tri, pieces, preferred_element_type=F32)
    gc = sums[:, :LANES] + sums[:, LANES:2 * LANES] + sums[:, 2 * LANES:]
    lane = lax.broadcasted_iota(I32, (tm, LANES), 1)
    res = jnp.where(lane < GDN_HEADS, beta, gc)
    bd_ref[...] = res[:, :2 * GDN_HEADS]
    bdt_ref[...] = res.T[:2 * GDN_HEADS, :]


def _inproj(x, g, wqkv, wz, wbd, wglu, bglu, pa, pb, chunk):
    t = x.shape[0]
    tm = TOKEN_TILE
    full = lambda shape: pl.BlockSpec(shape, lambda i: (0, 0))
    return pl.pallas_call(
        functools.partial(_inproj_kernel, int(math.log2(chunk))),
        grid=(t // tm,),
        in_specs=[
            pl.BlockSpec((tm, D_MODEL), lambda i: (i, 0)),
            full((1, D_MODEL)),
            full((D_MODEL, QKV_COLS)),
            full((D_MODEL, GDN_WIDTH)),
            full((D_MODEL, LANES)),
            full((D_MODEL, 2 * CONF_WIDTH)),
            full((1, 2 * CONF_WIDTH)),
            full((1, LANES)),
            full((1, LANES)),
        ],
        out_specs=[
            pl.BlockSpec((tm, QKV_COLS), lambda i: (i, 0)),
            pl.BlockSpec((tm, GDN_WIDTH), lambda i: (i, 0)),
            pl.BlockSpec((tm, 2 * GDN_HEADS), lambda i: (i, 0)),
            pl.BlockSpec((2 * GDN_HEADS, tm), lambda i: (0, i)),
            pl.BlockSpec((tm, CONF_WIDTH), lambda i: (i, 0)),
        ],
        out_shape=[
            jax.ShapeDtypeStruct((t, QKV_COLS), F32),
            jax.ShapeDtypeStruct((t, GDN_WIDTH), F32),
            jax.ShapeDtypeStruct((t, 2 * GDN_HEADS), F32),
            jax.ShapeDtypeStruct((2 * GDN_HEADS, t), F32),
            jax.ShapeDtypeStruct((t, CONF_WIDTH), F32),
        ],
        compiler_params=_params(("arbitrary",)),
        name="inproj",
    )(x, g, wqkv, wz, wbd, wglu, bglu, pa, pb)


def _split_bf16(a):
    hi = a.astype(BF16)
    lo = (a - hi.astype(F32)).astype(BF16)
    return hi, lo


def _dot3(a_hi, a_lo, b_hi, b_lo):
    dot = functools.partial(jnp.dot, preferred_element_type=F32)
    return dot(a_hi, b_hi) + dot(a_lo, b_hi) + dot(a_hi, b_lo)


def _unit_lower_inverses(ms, eye, n):
    levels = int(math.log2(n)) - 1
    ps = [eye - m for m in ms]
    pw = [_split_bf16(-m) for m in ms]
    pw = [_split_bf16(_dot3(h, l, h, l)) for h, l in pw]
    for level in range(1, levels + 1):
        last = level == levels
        nxt = []
        for u, (h, l) in enumerate(pw):
            ph, pl_ = _split_bf16(ps[u])
            if last:
                ps[u] = ps[u] + _dot3(ph, pl_, h, l)
            else:
                out = _dot3(jnp.concatenate([h, ph], axis=0), jnp.concatenate([l, pl_], axis=0), h, l)
                nxt.append(_split_bf16(out[:n]))
                ps[u] = ps[u] + out[n:]
        pw = nxt
    return ps


def _gdn_kernel(chunk, x_ref, z_ref, bd_ref, bdt_ref, hist_ref, s0_ref, wc_ref, won_ref,
                o_ref, s_ref, xbuf):
    i = pl.program_id(1)
    nb, tt = x_ref.shape[0], x_ref.shape[1]
    nchunk = tt // chunk
    pad = SUBLANES
    hist_rows = QKV_TAPS - 1

    @pl.when(i == 0)
    def _():
        xbuf[:, pad - hist_rows:pad, :] = hist_ref[...]
        s_ref[...] = s0_ref[...]

    xbuf[:, pad:pad + tt, :] = x_ref[...]

    ii = lax.broadcasted_iota(I32, (chunk, chunk), 0)
    jj = lax.broadcasted_iota(I32, (chunk, chunk), 1)
    lower_incl = ii >= jj
    lower_strict = ii > jj
    eye = (ii == jj).astype(F32)
    won = won_ref[...]

    def conv_silu(b, col0):
        cols = slice(col0, col0 + HEAD_DIM)
        acc = wc_ref[hist_rows:hist_rows + 1, cols] * xbuf[b, pad:pad + tt, cols]
        for j in range(hist_rows):
            r0 = pad - hist_rows + j
            acc = acc + wc_ref[j:j + 1, cols] * xbuf[b, r0:r0 + tt, cols]
        return acc * _sigmoid(acc)

    units = []
    for b in range(nb):
        bd = bd_ref[b]
        bdt = bdt_ref[b]
        for h in range(GDN_HEADS):
            q = conv_silu(b, h * HEAD_DIM)
            k = conv_silu(b, GDN_WIDTH + h * HEAD_DIM)
            v = conv_silu(b, 2 * GDN_WIDTH + h * HEAD_DIM)
            q = q * lax.rsqrt(jnp.sum(q * q, axis=-1, keepdims=True) + L2_EPS) * (HEAD_DIM ** -0.5)
            k = k * lax.rsqrt(jnp.sum(k * k, axis=-1, keepdims=True) + L2_EPS)
            kt = k.T
            beta = bd[:, h:h + 1]
            gcol = bd[:, GDN_HEADS + h:GDN_HEADS + h + 1]
            grow = bdt[GDN_HEADS + h:GDN_HEADS + h + 1, :]
            for c in range(nchunk):
                rows = slice(c * chunk, (c + 1) * chunk)
                qc, kc, vc, ktc = q[rows], k[rows], v[rows], kt[:, rows]
                bc, gc, gr = beta[rows], gcol[rows], grow[:, rows]
                dec = jnp.exp(jnp.where(lower_incl, gc - gr, -jnp.inf))
                kb = kc * bc
                eg = jnp.exp(gc)
                both = _dotb(jnp.concatenate([kb, qc], axis=0), ktc)
                units.append(dict(
                    b=b, h=h, c=c,
                    m=both[:chunk] * jnp.where(lower_strict, dec, 0.0),
                    qk=both[chunk:] * dec,
                    rhs=jnp.concatenate([vc * bc, kb * eg], axis=1),
                    qg=qc * eg,
                    kdt=ktc * jnp.exp(gr[:, chunk - 1:chunk] - gr),
                    d_last=jnp.exp(gc[chunk - 1:chunk, :]),
                ))

    tinvs = _unit_lower_inverses([u["m"] for u in units], eye, chunk)
    for u, tinv in zip(units, tinvs):
        th, tl = _split_bf16(tinv)
        rh, rl = _split_bf16(u["rhs"])
        sol = _dot3(th, tl, rh, rl)
        u["u"], u["w"] = sol[:, :HEAD_DIM], sol[:, HEAD_DIM:]

    state = {(b, h): s_ref[b, h] for b in range(nb) for h in range(GDN_HEADS)}
    by_key = {(u["b"], u["h"], u["c"]): u for u in units}
    for c in range(nchunk):
        rows = slice(c * chunk, (c + 1) * chunk)
        for b in range(nb):
            for h in range(GDN_HEADS):
                u = by_key[(b, h, c)]
                s = state[(b, h)]
                ws = _dotb(jnp.concatenate([u["w"], u["qg"]], axis=0), s)
                v_new = u["u"] - ws[:chunk]
                upd = _dotb(jnp.concatenate([u["qk"], u["kdt"]], axis=0), v_new)
                o_c = ws[chunk:] + upd[:chunk]
                state[(b, h)] = s * u["d_last"] + upd[chunk:]
                cols = slice(h * HEAD_DIM, (h + 1) * HEAD_DIM)
                on = o_c * lax.rsqrt(jnp.mean(o_c * o_c, axis=-1, keepdims=True) + RMS_EPS) * won
                zc = z_ref[b, rows, cols]
                o_ref[b, rows, cols] = on * (zc * _sigmoid(zc))
    for (b, h), s in state.items():
        s_ref[b, h] = s

    xbuf[:, pad - hist_rows:pad, :] = x_ref[:, tt - hist_rows:tt, :]


def _gdn(qkv, z, bd, bdt, hist, s0, wconv, wonorm, chunk, tt, nb):
    b, t, _ = qkv.shape
    nt = t // tt
    return pl.pallas_call(
        functools.partial(_gdn_kernel, chunk),
        grid=(b // nb, nt),
        in_specs=[
            pl.BlockSpec((nb, tt, QKV_COLS), lambda bi, i: (bi, i, 0)),
            pl.BlockSpec((nb, tt, GDN_WIDTH), lambda bi, i: (bi, i, 0)),
            pl.BlockSpec((nb, tt, 2 * GDN_HEADS), lambda bi, i: (bi, i, 0)),
            pl.BlockSpec((nb, 2 * GDN_HEADS, tt), lambda bi, i: (bi, 0, i)),
            pl.BlockSpec((nb, QKV_TAPS - 1, QKV_COLS), lambda bi, i: (bi, 0, 0)),
            pl.BlockSpec((nb, GDN_HEADS, HEAD_DIM, HEAD_DIM), lambda bi, i: (bi, 0, 0, 0)),
            pl.BlockSpec((QKV_TAPS, QKV_COLS), lambda bi, i: (0, 0)),
            pl.BlockSpec((1, HEAD_DIM), lambda bi, i: (0, 0)),
        ],
        out_specs=[
            pl.BlockSpec((nb, tt, GDN_WIDTH), lambda bi, i: (bi, i, 0)),
            pl.BlockSpec((nb, GDN_HEADS, HEAD_DIM, HEAD_DIM), lambda bi, i: (bi, 0, 0, 0)),
        ],
        out_shape=[
            jax.ShapeDtypeStruct((b, t, GDN_WIDTH), F32),
            jax.ShapeDtypeStruct((b, GDN_HEADS, HEAD_DIM, HEAD_DIM), F32),
        ],
        scratch_shapes=[pltpu.VMEM((nb, tt + SUBLANES, QKV_COLS), F32)],
        compiler_params=_params(("arbitrary", "arbitrary")),
        name="gdn",
    )(qkv, z, bd, bdt, hist, s0, wconv, wonorm)


def _cconv_kernel(carry, u_ref, hist_ref, w_ref, b_ref, lg_ref, lb_ref, c_ref, ubuf, shifted):
    i = pl.program_id(1)
    tt = u_ref.shape[0]
    hist_rows = DW_TAPS - 1
    pad = 4 * SUBLANES
    base = pad - hist_rows

    @pl.when(i == 0)
    def _():
        ubuf[base:pad, :] = hist_ref[...]

    ubuf[pad:pad + tt, :] = u_ref[...]
    span = shifted.shape[1]
    for s in range(1, SUBLANES):
        shifted[s - 1] = ubuf[s:s + span, :]

    def tap_rows(j, r0, rows):
        q, s = divmod(base + j, SUBLANES)
        start = q * SUBLANES + r0
        if s == 0:
            return ubuf[start:start + rows, :]
        return shifted[s - 1, start:start + rows, :]

    rows = min(CONV_ROWS, tt)
    for r0 in range(0, tt, rows):
        acc = w_ref[0:1, :] * tap_rows(0, r0, rows)
        for j in range(1, DW_TAPS):
            acc = acc + w_ref[j:j + 1, :] * tap_rows(j, r0, rows)
        cv = acc + b_ref[...]
        mu = jnp.mean(cv, axis=-1, keepdims=True)
        xc = cv - mu
        var = jnp.mean(xc * xc, axis=-1, keepdims=True)
        y = xc * lax.rsqrt(var + LN_EPS) * lg_ref[...] + lb_ref[...]
        c_ref[r0:r0 + rows, :] = y * _sigmoid(y)
    if carry:
        ubuf[base:pad, :] = u_ref[tt - hist_rows:tt, :]


def _cconv(uu, hist, w, b, lg, lb, tt):
    bsz, t, _ = uu.shape
    nt = t // tt
    vec = pl.BlockSpec((1, CONF_WIDTH), lambda bi, i: (0, 0))
    return pl.pallas_call(
        functools.partial(_cconv_kernel, nt > 1),
        grid=(bsz, nt),
        in_specs=[
            pl.BlockSpec((None, tt, CONF_WIDTH), lambda bi, i: (bi, i, 0)),
            pl.BlockSpec((None, DW_TAPS - 1, CONF_WIDTH), lambda bi, i: (bi, 0, 0)),
            pl.BlockSpec((DW_TAPS, CONF_WIDTH), lambda bi, i: (0, 0)),
            vec, vec, vec,
        ],
        out_specs=pl.BlockSpec((None, tt, CONF_WIDTH), lambda bi, i: (bi, i, 0)),
        out_shape=jax.ShapeDtypeStruct((bsz, t, CONF_WIDTH), F32),
        scratch_shapes=[pltpu.VMEM((tt + 4 * SUBLANES, CONF_WIDTH), F32),
                        pltpu.VMEM((SUBLANES - 1, tt + 3 * SUBLANES, CONF_WIDTH), F32)],
        compiler_params=_params(("arbitrary", "arbitrary")),
        name="cconv",
    )(uu, hist, w, b, lg, lb)


def _outproj_kernel(tiles_a, oa_ref, ca_ref, xa_ref, ob_ref, cb_ref, xb_ref, *rest):
    i = pl.program_id(0)

    @pl.when(i < tiles_a)
    def _():
        _outproj_tile(oa_ref, ca_ref, xa_ref, *rest)

    @pl.when(i >= tiles_a)
    def _():
        _outproj_tile(ob_ref, cb_ref, xb_ref, *rest)


def _outproj_tile(o_ref, c_ref, x_ref, wo_ref, wc_ref, g_ref, wr_ref, br_ref, upper_ref,
                  x1_ref, h2_ref, metar_ref, cnt_ref):
    tm = x_ref.shape[0]

    mix = _dotb(o_ref[...], wo_ref[...]) + _dotb(c_ref[...], wc_ref[...])
    x1 = x_ref[...] + mix
    x1_ref[...] = x1
    h2 = x1 * lax.rsqrt(jnp.mean(x1 * x1, axis=-1, keepdims=True) + RMS_EPS) * g_ref[...]
    h2_ref[...] = h2.astype(BF16)
    logits = lax.dot_general(wr_ref[...], h2.astype(BF16), (((1,), (1,)), ((), ())),
                             preferred_element_type=F32) + br_ref[...]
    row = lax.broadcasted_iota(I32, (N_EXPERTS, tm), 0).astype(F32)
    vals = logits
    idxs, tops = [], []
    for _ in range(TOP_K):
        m = jnp.max(vals, axis=0, keepdims=True)
        idx = jnp.min(jnp.where(vals == m, row, float(N_EXPERTS)), axis=0, keepdims=True)
        idxs.append(idx)
        tops.append(m)
        vals = jnp.where(row == idx, -jnp.inf, vals)
    exps = [jnp.exp(v - tops[0]) for v in tops]
    den = exps[0] + exps[1] + exps[2] + exps[3]
    onehots = [(row == idx).astype(F32) for idx in idxs]
    chosen = onehots[0] + onehots[1] + onehots[2] + onehots[3]
    before = jnp.dot(chosen.astype(BF16), upper_ref[...], preferred_element_type=F32)
    ranks = [jnp.sum(oh * before, axis=0, keepdims=True) for oh in onehots]
    gates = [e / den for e in exps]
    metar_ref[...] = jnp.concatenate(
        idxs + ranks + gates + [jnp.zeros((META_ROWS - 3 * TOP_K, tm), F32)], axis=0)
    cnt_ref[...] = jnp.broadcast_to(jnp.sum(chosen, axis=1, keepdims=True), (N_EXPERTS, LANES))[None]


def _outproj(first, second, wo, wc, g, wr, br):
    t_a, t_b = first[2].shape[0], second[2].shape[0]
    tm = TOKEN_TILE
    upper = (jnp.arange(tm)[:, None] < jnp.arange(tm)[None, :]).astype(BF16)
    tiles_a = t_a // tm
    t_total = t_a + t_b
    full = lambda shape: pl.BlockSpec(shape, lambda i: (0, 0))
    rows_a = lambda width: pl.BlockSpec((tm, width), lambda i: (jnp.minimum(i, tiles_a - 1), 0))
    rows_b = lambda width: pl.BlockSpec((tm, width), lambda i: (jnp.maximum(i - tiles_a, 0), 0))
    widths = (GDN_WIDTH, CONF_WIDTH, D_MODEL)
    return pl.pallas_call(
        functools.partial(_outproj_kernel, tiles_a),
        grid=(t_total // tm,),
        in_specs=[rows_a(w) for w in widths] + [rows_b(w) for w in widths] + [
            full((GDN_WIDTH, D_MODEL)),
            full((CONF_WIDTH, D_MODEL)),
            full((1, D_MODEL)),
            full((N_EXPERTS, D_MODEL)),
            full((N_EXPERTS, 1)),
            full((tm, tm)),
        ],
        out_specs=[
            pl.BlockSpec((tm, D_MODEL), lambda i: (i, 0)),
            pl.BlockSpec((tm, D_MODEL), lambda i: (i, 0)),
            pl.BlockSpec((META_ROWS, tm), lambda i: (0, i)),
            pl.BlockSpec((1, N_EXPERTS, LANES), lambda i: (i, 0, 0)),
        ],
        out_shape=[
            jax.ShapeDtypeStruct((t_total, D_MODEL), F32),
            jax.ShapeDtypeStruct((t_total, D_MODEL), BF16),
            jax.ShapeDtypeStruct((META_ROWS, t_total), F32),
            jax.ShapeDtypeStruct((t_total // tm, N_EXPERTS, LANES), F32),
        ],
        compiler_params=_params(("arbitrary",)),
        name="outproj_router",
    )(*first, *second, wo, wc, g, wr, br, upper)


def _segment_copies(i, seg_local_ref, seg_dst_ref, seg_n_ref, make_copy, wait):
    def per_expert(ex, carry):
        j = i * N_EXPERTS + ex
        local, dst, n = seg_local_ref[j], seg_dst_ref[j], seg_n_ref[j]
        done = 0
        for per_copy, count in ((8, n >> 3), (2, (n >> 1) & 3), (1, n & 1)):
            def body(g, c, per_copy=per_copy, done=done):
                off = (done + g * per_copy) * GRANULE
                cp = make_copy(per_copy * GRANULE, pl.multiple_of(local + off, GRANULE),
                               pl.multiple_of(dst + off, GRANULE))
                cp.wait() if wait else cp.start()
                return c

            lax.fori_loop(0, count, body, 0)
            done = done + count * per_copy
        return carry

    lax.fori_loop(0, N_EXPERTS, per_expert, 0)


def _gap_copies(gap_start_ref, gap_n_ref, make_copy, wait):
    def per_expert(ex, carry):
        n = gap_n_ref[ex]
        start = pl.multiple_of(gap_start_ref[ex], GRANULE)
        off = 0 * n
        for bit in reversed(range(GAP_BITS)):
            rows = GRANULE << bit

            @pl.when(((n >> bit) & 1) == 1)
            def _(rows=rows, off=off):
                cp = make_copy(rows, pl.multiple_of(start + off, GRANULE))
                cp.wait() if wait else cp.start()

            off = off + ((n >> bit) & 1) * rows
        return carry

    lax.fori_loop(0, N_EXPERTS, per_expert, 0)


def _local_rows(meta_e, seg_local_ref, i):
    base = jnp.zeros(meta_e.shape, F32)
    for ex in range(N_EXPERTS):
        base = jnp.where(meta_e == float(ex), seg_local_ref[i * N_EXPERTS + ex].astype(F32), base)
    return base


def _dispatch_kernel(seg_local_ref, seg_dst_ref, seg_n_ref, gap_start_ref, gap_n_ref, tail_ref,
                     h_ref, mr_ref, xs_ref, slots_ref, xloc, zbuf, sems, zsem):
    i = pl.program_id(0)
    last = pl.num_programs(0) - 1
    tm = h_ref.shape[0]
    rt = xloc.shape[1]
    zrows = zbuf.shape[0]
    slot = i % 2

    def gap_copy(rows, dst):
        return pltpu.make_async_copy(zbuf.at[pl.ds(0, rows)], xs_ref.at[pl.ds(dst, rows)], zsem)

    def tail_copies(wait):
        def body(j, c):
            cp = gap_copy(zrows, pl.multiple_of(tail_ref[0] + j * zrows, zrows))
            cp.wait() if wait else cp.start()
            return c
        lax.fori_loop(0, tail_ref[1], body, 0)

    @pl.when(i == 0)
    def _():
        zbuf[...] = jnp.zeros(zbuf.shape, zbuf.dtype)
        _gap_copies(gap_start_ref, gap_n_ref, gap_copy, wait=False)
        tail_copies(wait=False)

    mr = mr_ref[...]
    dl = (_local_rows(mr[META_EXPERT:META_EXPERT + TOP_K], seg_local_ref, i)
          + mr[META_RANK:META_RANK + TOP_K])
    slots_ref[...] = jnp.concatenate(
        [dl, mr[META_GATE:META_GATE + TOP_K], jnp.zeros((LANES - 2 * TOP_K, tm), F32)], axis=0).T
    rr = lax.broadcasted_iota(I32, (rt, tm), 0).astype(F32)
    hit = rr == dl[0:1]
    for k in range(1, TOP_K):
        hit = hit | (rr == dl[k:k + 1])
    xloc[slot] = jnp.dot(hit.astype(BF16), h_ref[...], preferred_element_type=F32)

    def seg_copy(buf):
        def make(rows, local, dst):
            return pltpu.make_async_copy(xloc.at[buf, pl.ds(local, rows)],
                                         xs_ref.at[pl.ds(dst, rows)], sems.at[buf])
        return make

    _segment_copies(i, seg_local_ref, seg_dst_ref, seg_n_ref, seg_copy(slot), wait=False)

    @pl.when(i > 0)
    def _():
        _segment_copies(i - 1, seg_local_ref, seg_dst_ref, seg_n_ref, seg_copy(1 - slot), wait=True)

    @pl.when(i == last)
    def _():
        _segment_copies(i, seg_local_ref, seg_dst_ref, seg_n_ref, seg_copy(slot), wait=True)

    @pl.when(i == 0)
    def _():
        _gap_copies(gap_start_ref, gap_n_ref, gap_copy, wait=True)
        tail_copies(wait=True)


def _dispatch(tables, h2, metar, n_rows):
    t = h2.shape[0]
    tm = TOKEN_TILE
    return pl.pallas_call(
        _dispatch_kernel,
        grid_spec=pltpu.PrefetchScalarGridSpec(
            num_scalar_prefetch=6,
            grid=(t // tm,),
            in_specs=[
                pl.BlockSpec((tm, D_MODEL), lambda i, *_: (i, 0)),
                pl.BlockSpec((META_ROWS, tm), lambda i, *_: (0, i)),
            ],
            out_specs=[pl.BlockSpec(memory_space=pl.ANY),
                       pl.BlockSpec((tm, LANES), lambda i, *_: (i, 0))],
            scratch_shapes=[
                pltpu.VMEM((2, TILE_ROWS, D_MODEL), F32),
                pltpu.VMEM((GRANULE << (GAP_BITS - 1), D_MODEL), F32),
                pltpu.SemaphoreType.DMA((2,)),
                pltpu.SemaphoreType.DMA,
            ],
        ),
        out_shape=[jax.ShapeDtypeStruct((n_rows, D_MODEL), F32),
                   jax.ShapeDtypeStruct((t, LANES), F32)],
        compiler_params=_params(("arbitrary",)),
        name="dispatch",
    )(*tables, h2, metar)


def _expert_kernel(be_ref, rb_ref, nu_ref, first_ref, slot_ref, next_ref,
                   x_ref, wgu_hbm, bgu_ref, wd_hbm, bdn_ref, y_ref,
                   wgu_f32, wd_f32, wgu_bf, wd_bf, sems):
    i = pl.program_id(0)

    def fetch(expert, slot):
        return (pltpu.make_async_copy(wgu_hbm.at[expert], wgu_f32.at[slot], sems.at[slot, 0]),
                pltpu.make_async_copy(wd_hbm.at[expert], wd_f32.at[slot], sems.at[slot, 1]))

    @pl.when(i == 0)
    def _():
        for cp in fetch(be_ref[0], 0):
            cp.start()

    @pl.when(first_ref[i] == 1)
    def _():
        slot = slot_ref[i]
        for cp in fetch(be_ref[i], slot):
            cp.wait()
        wgu_bf[...] = wgu_f32[slot].astype(BF16)
        wd_bf[...] = wd_f32[slot].astype(BF16)

        @pl.when(next_ref[i] >= 0)
        def _():
            for cp in fetch(next_ref[i], 1 - slot):
                cp.start()

    @pl.when(i < nu_ref[0])
    def _():
        gu = _dotb(x_ref[...], wgu_bf[...]) + bgu_ref[...]
        x_glu = jnp.minimum(gu[:, :D_EXPERT], SWIGLU_LIMIT)
        x_lin = jnp.clip(gu[:, D_EXPERT:], -SWIGLU_LIMIT, SWIGLU_LIMIT)
        act = x_glu * _sigmoid(SWIGLU_ALPHA * x_glu) * (x_lin + 1.0)
        y_ref[...] = _dotb(act, wd_bf[...]) + bdn_ref[...]

    @pl.when(i >= nu_ref[0])
    def _():
        y_ref[...] = jnp.zeros(y_ref.shape, y_ref.dtype)


def _experts(tables, xs, wgu, bgu, wd, bdn):
    rows = xs.shape[0]
    bm = EXPERT_ROWS
    nb = rows // bm
    return pl.pallas_call(
        _expert_kernel,
        grid_spec=pltpu.PrefetchScalarGridSpec(
            num_scalar_prefetch=6,
            grid=(nb,),
            in_specs=[
                pl.BlockSpec((bm, D_MODEL), lambda i, be, rb, *_: (rb[i], 0)),
                pl.BlockSpec(memory_space=pl.ANY),
                pl.BlockSpec((None, 1, 2 * D_EXPERT), lambda i, be, *_: (be[i], 0, 0)),
                pl.BlockSpec(memory_space=pl.ANY),
                pl.BlockSpec((None, 1, D_MODEL), lambda i, be, *_: (be[i], 0, 0)),
            ],
            out_specs=pl.BlockSpec((bm, D_MODEL), lambda i, *_: (i, 0)),
            scratch_shapes=[
                pltpu.VMEM((2, D_MODEL, 2 * D_EXPERT), F32),
                pltpu.VMEM((2, D_EXPERT, D_MODEL), F32),
                pltpu.VMEM((D_MODEL, 2 * D_EXPERT), BF16),
                pltpu.VMEM((D_EXPERT, D_MODEL), BF16),
                pltpu.SemaphoreType.DMA((2, 2)),
            ],
        ),
        out_shape=jax.ShapeDtypeStruct((rows, D_MODEL), F32),
        compiler_params=pltpu.CompilerParams(dimension_semantics=("arbitrary",),
                                             vmem_limit_bytes=EXPERT_VMEM_LIMIT),
        name="experts",
    )(*tables, xs, wgu, bgu, wd, bdn)


def _combine_kernel(tiles_a, seg_local_ref, seg_dst_ref, seg_n_ref, x1_ref, mc_ref, yb_ref, gf_ref,
                    ya_ref, yb_out_ref, ybuf, wsel, sems):
    i = pl.program_id(0)
    n_tiles = pl.num_programs(0)
    tm = x1_ref.shape[0]
    rt = ybuf.shape[1]
    slot = i % 2

    def seg_copy(buf):
        def make(rows, local, src):
            return pltpu.make_async_copy(yb_ref.at[pl.ds(src, rows)],
                                         ybuf.at[buf, pl.ds(local, rows)], sems.at[buf])
        return make

    @pl.when(i == 0)
    def _():
        ybuf[...] = jnp.zeros(ybuf.shape, ybuf.dtype)
        _segment_copies(i, seg_local_ref, seg_dst_ref, seg_n_ref, seg_copy(slot), wait=False)

    @pl.when(i + 1 < n_tiles)
    def _():
        _segment_copies(i + 1, seg_local_ref, seg_dst_ref, seg_n_ref, seg_copy(1 - slot), wait=False)

    dl = mc_ref[:, 0:TOP_K]
    gate = mc_ref[:, TOP_K:2 * TOP_K]
    rb, cb = SEL_CHUNK
    col = lax.broadcasted_iota(I32, (rb, cb), 1).astype(F32)
    for r0 in range(0, tm, rb):
        rows_k = [jnp.broadcast_to(dl[r0:r0 + rb, k:k + 1], (rb, cb)) for k in range(TOP_K)]
        gate_k = [jnp.broadcast_to(gate[r0:r0 + rb, k:k + 1], (rb, cb)) for k in range(TOP_K)]
        for c0 in range(0, rt, cb):
            rr = col + float(c0)
            w = jnp.where(rr == rows_k[0], gate_k[0], 0.0)
            for k in range(1, TOP_K):
                w = jnp.where(rr == rows_k[k], gate_k[k], w)
            wsel[r0:r0 + rb, c0:c0 + cb] = w.astype(BF16)
    _segment_copies(i, seg_local_ref, seg_dst_ref, seg_n_ref, seg_copy(slot), wait=True)
    acc = x1_ref[...] + jnp.dot(wsel[...], ybuf[slot].astype(BF16), preferred_element_type=F32)
    y = acc * lax.rsqrt(jnp.mean(acc * acc, axis=-1, keepdims=True) + RMS_EPS) * gf_ref[...]

    @pl.when(i < tiles_a)
    def _():
        ya_ref[...] = y

    @pl.when(i >= tiles_a)
    def _():
        yb_out_ref[...] = y


def _combine(tables, x1, slots, yb, gfin, n_a):
    t = x1.shape[0]
    tm = TOKEN_TILE
    tiles_a = n_a // tm
    return pl.pallas_call(
        functools.partial(_combine_kernel, tiles_a),
        grid_spec=pltpu.PrefetchScalarGridSpec(
            num_scalar_prefetch=3,
            grid=(t // tm,),
            in_specs=[
                pl.BlockSpec((tm, D_MODEL), lambda i, *_: (i, 0)),
                pl.BlockSpec((tm, LANES), lambda i, *_: (i, 0)),
                pl.BlockSpec(memory_space=pl.ANY),
                pl.BlockSpec((1, D_MODEL), lambda i, *_: (0, 0)),
            ],
            out_specs=[
                pl.BlockSpec((tm, D_MODEL), lambda i, *_: (jnp.minimum(i, tiles_a - 1), 0)),
                pl.BlockSpec((tm, D_MODEL), lambda i, *_: (jnp.maximum(i - tiles_a, 0), 0)),
            ],
            scratch_shapes=[pltpu.VMEM((2, TILE_ROWS, D_MODEL), F32),
                            pltpu.VMEM((tm, TILE_ROWS), BF16),
                            pltpu.SemaphoreType.DMA((2,))],
        ),
        out_shape=[jax.ShapeDtypeStruct((n_a, D_MODEL), F32),
                   jax.ShapeDtypeStruct((t - n_a, D_MODEL), F32)],
        compiler_params=pltpu.CompilerParams(dimension_semantics=("arbitrary",),
                                             vmem_limit_bytes=EXPERT_VMEM_LIMIT),
        name="combine",
    )(*tables, x1, slots, yb, gfin)


def kernel(x_prompt, x_sample, state_gdn, state_qkv_conv, state_dwconv, norm_mix_g, w_in, w_conv_qkv,
           a_log, dt_bias, w_onorm, b_glu, w_dw, b_dw, ln_g, ln_b, w_out, norm_ffn_g, w_router,
           b_router, w_gate_up, b_gate_up, w_down, b_down, norm_final_g):
    bp, tp, _ = x_prompt.shape
    bs, ts, _ = x_sample.shape
    n_p, n_s = bp * tp, bs * ts

    wi = w_in[0]
    c_z = QKV_COLS
    c_bd = c_z + GDN_WIDTH
    c_glu = c_bd + 2 * GDN_HEADS
    wqkv = wi[:, :c_z].astype(BF16)
    wz = wi[:, c_z:c_bd].astype(BF16)
    wbd = jnp.pad(wi[:, c_bd:c_glu], ((0, 0), (0, LANES - 2 * GDN_HEADS))).astype(BF16)
    wglu = wi[:, c_glu:].astype(BF16)
    bglu = b_glu[0][None, :]
    lane_pad = (GDN_HEADS, LANES - 2 * GDN_HEADS)
    pa = jnp.pad(-jnp.exp(a_log[0].astype(F32)), lane_pad)[None, :]
    pb = jnp.pad(dt_bias[0].astype(F32), lane_pad)[None, :]
    gmix = norm_mix_g[0][None, :]
    wconv = w_conv_qkv[0]
    wonorm = w_onorm[0][None, :]
    wdw, bdw = w_dw[0], b_dw[0][None, :]
    lng, lnb = ln_g[0][None, :], ln_b[0][None, :]
    wo = w_out[0][:GDN_WIDTH].astype(BF16)
    wc = w_out[0][GDN_WIDTH:].astype(BF16)
    gffn = norm_ffn_g[0][None, :]
    wr, br = w_router[0].T.astype(BF16), b_router[0].astype(F32)[:, None]
    bgu = b_gate_up[0][:, None, :]
    bdn = b_down[0][:, None, :]
    gfin = norm_final_g[None, :]

    xp = x_prompt.reshape(n_p, D_MODEL)
    xs_tok = x_sample.reshape(n_s, D_MODEL)

    def mixers(x2d, bsz, t, hist_qkv, s0, hist_glu, chunk, gdn_tile, gdn_rows, conv_tile):
        qkv, z, bd, bdt, uu = _inproj(x2d, gmix, wqkv, wz, wbd, wglu, bglu, pa, pb, chunk)
        qkv3 = qkv.reshape(bsz, t, QKV_COLS)
        uu3 = uu.reshape(bsz, t, CONF_WIDTH)
        bdt3 = bdt.reshape(2 * GDN_HEADS, bsz, t).transpose(1, 0, 2)
        o, s_new = _gdn(qkv3, z.reshape(bsz, t, GDN_WIDTH), bd.reshape(bsz, t, 2 * GDN_HEADS),
                        bdt3, hist_qkv, s0, wconv, wonorm, chunk, gdn_tile, gdn_rows)
        cc = _cconv(uu3, hist_glu, wdw, bdw, lng, lnb, conv_tile)
        return o.reshape(bsz * t, GDN_WIDTH), cc.reshape(bsz * t, CONF_WIDTH), s_new, qkv3, uu3

    zero_qkv = jnp.zeros((bp, QKV_TAPS - 1, QKV_COLS), F32)
    zero_s = jnp.zeros((bp, GDN_HEADS, HEAD_DIM, HEAD_DIM), F32)
    zero_glu = jnp.zeros((bp, DW_TAPS - 1, CONF_WIDTH), F32)
    o_p, c_p, s_p, qkv_p, uu_p = mixers(xp, bp, tp, zero_qkv, zero_s, zero_glu,
                                        PROMPT_CHUNK, GDN_TILE, 1, CONV_TILE)
    o_s, c_s, s_s, qkv_s, uu_s = mixers(xs_tok, bs, ts, state_qkv_conv[0], state_gdn[0],
                                        state_dwconv[0], ts, ts, GDN_SAMPLE_ROWS, ts)

    n_tok = n_p + n_s
    x1, h2, metar, cnt_tile = _outproj((o_p, c_p, xp), (o_s, c_s, xs_tok), wo, wc, gffn, wr, br)

    bm = EXPERT_ROWS
    n_tiles = n_tok // TOKEN_TILE
    counts = cnt_tile[:, :, 0].astype(I32)
    seg_rows = (counts + GRANULE - 1) // GRANULE * GRANULE
    ids = jnp.arange(N_EXPERTS, dtype=I32)
    tile_ids = jnp.arange(n_tiles, dtype=I32)
    earlier_e = (ids[:, None] < ids[None, :]).astype(I32)
    earlier_t = (tile_ids[None, :] < tile_ids[:, None]).astype(I32)
    seg_local = jnp.sum(seg_rows[:, :, None] * earlier_e[None], axis=1)
    seg_before = jnp.sum(earlier_t[:, :, None] * seg_rows[None], axis=1)
    rows_e = jnp.sum(seg_rows, axis=0)
    padded = (rows_e + bm - 1) // bm * bm
    pstart = jnp.sum(padded[:, None] * earlier_e, axis=0)
    pend = pstart + padded
    seg_dst = pstart[None, :] + seg_before
    n_used = jnp.maximum(pend[-1] // bm, 1).astype(I32)
    max_rows = n_tok * TOP_K + n_tiles * N_EXPERTS * (GRANULE - 1) + N_EXPERTS * (bm - 1)
    n_blocks = -(-max_rows // bm)
    blk = jnp.minimum(jnp.arange(n_blocks, dtype=I32), n_used - 1)
    block_expert = jnp.minimum(
        jnp.sum((pend[None, :] <= (blk * bm)[:, None]).astype(I32), axis=1), N_EXPERTS - 1)
    seg_tables = (seg_local.reshape(-1).astype(I32), seg_dst.reshape(-1).astype(I32),
                  (seg_rows // GRANULE).reshape(-1).astype(I32))
    gap_tables = ((pstart + rows_e).astype(I32), ((padded - rows_e) // GRANULE).astype(I32))
    zero_rows = GRANULE << (GAP_BITS - 1)
    tail_table = jnp.stack([pend[-1], (n_blocks * bm - pend[-1]) // zero_rows]).astype(I32)

    xs, slots = _dispatch(seg_tables + gap_tables + (tail_table,), h2, metar, n_blocks * bm)
    active = padded > 0
    later = active[None, :] & (ids[None, :] > ids[:, None])
    next_active = jnp.min(jnp.where(later, ids[None, :], N_EXPERTS), axis=1)
    next_active = jnp.where(next_active == N_EXPERTS, -1, next_active)
    order = jnp.sum(active.astype(I32)[:, None] * earlier_e, axis=0)
    steps = jnp.arange(n_blocks, dtype=I32)
    first = ((steps * bm == pstart[block_expert]) & (steps < n_used)).astype(I32)
    expert_tables = (block_expert, blk, n_used.reshape(1), first,
                     (order[block_expert] % 2).astype(I32), next_active[block_expert].astype(I32))
    yb = _experts(expert_tables, xs, w_gate_up[0], bgu, w_down[0], bdn)
    y_p, y_s = _combine(seg_tables, x1, slots, yb, gfin, n_p)

    hist_glu_s = jnp.concatenate([state_dwconv[0], uu_s], axis=1)[:, -(DW_TAPS - 1):]
    return (y_p.reshape(bp, tp, D_MODEL),
            y_s.reshape(bs, ts, D_MODEL),
            s_p[None],
            qkv_p[:, -(QKV_TAPS - 1):][None],
            uu_p[:, -(DW_TAPS - 1):][None],
            s_s[None],
            qkv_s[:, -(QKV_TAPS - 1):][None],
            hist_glu_s[None])
```

```python
import functools
import math

import jax
import jax.numpy as jnp
from jax import lax
from jax.experimental import pallas as pl
from jax.experimental.pallas import tpu as pltpu

F32 = jnp.float32
BF16 = jnp.bfloat16
I32 = jnp.int32
HIGHEST = lax.Precision.HIGHEST

D_MODEL = 1024
GDN_HEADS = 4
HEAD_DIM = 128
GDN_WIDTH = GDN_HEADS * HEAD_DIM
QKV_COLS = 3 * GDN_WIDTH
CONF_WIDTH = 512
QKV_TAPS = 4
DW_TAPS = 31
N_EXPERTS = 32
TOP_K = 4
D_EXPERT = 1024
SWIGLU_LIMIT = 7.0
SWIGLU_ALPHA = 1.702
RMS_EPS = 1e-6
LN_EPS = 1e-5
L2_EPS = 1e-6
PROMPT_CHUNK = 64

LANES = 128
SUBLANES = 8
TOKEN_TILE = 512
GDN_TILE = 256
GDN_SAMPLE_ROWS = 4
CONV_TILE = 256
CONV_ROWS = 32
EXPERT_ROWS = 512
GRANULE = SUBLANES
GAP_BITS = 6
TILE_ROWS = -(-(TOKEN_TILE * TOP_K + N_EXPERTS * (GRANULE - 1)) // 256) * 256
META_EXPERT, META_RANK, META_GATE, META_ROWS = 0, TOP_K, 2 * TOP_K, 16
assert EXPERT_ROWS // GRANULE <= 1 << GAP_BITS
SEL_CHUNK = (64, 256)
VMEM_LIMIT = 48 * 1024 * 1024
EXPERT_VMEM_LIMIT = 56 * 1024 * 1024


def _dotb(a, b):
    return jnp.dot(a.astype(BF16), b.astype(BF16), preferred_element_type=F32)


def _dotf(a, b):
    return jnp.dot(a, b, precision=HIGHEST, preferred_element_type=F32)


def _sigmoid(x):
    return jax.nn.sigmoid(x)


def _params(sem):
    return pltpu.CompilerParams(dimension_semantics=sem, vmem_limit_bytes=VMEM_LIMIT)


def _inproj_kernel(x_ref, g_ref, wqkv_ref, wz_ref, wbd_ref, wglu_ref, bglu_ref, pa_ref, pb_ref,
                   tri_ref, qkv_ref, z_ref, bd_ref, bdt_ref, uu_ref):
    x = x_ref[...]
    tm = x.shape[0]
    ms = jnp.mean(x * x, axis=-1, keepdims=True)
    h = (x * lax.rsqrt(ms + RMS_EPS) * g_ref[...]).astype(BF16)
    qkv_ref[...] = jnp.dot(h, wqkv_ref[...], preferred_element_type=F32)
    z_ref[...] = jnp.dot(h, wz_ref[...], preferred_element_type=F32)
    glu = jnp.dot(h, wglu_ref[...], preferred_element_type=F32) + bglu_ref[...]
    uu_ref[...] = glu[:, :CONF_WIDTH] * _sigmoid(glu[:, CONF_WIDTH:])
    bd = jnp.dot(h, wbd_ref[...], preferred_element_type=F32)
    beta = _sigmoid(bd)
    v = bd + pb_ref[...]
    softplus = jnp.maximum(v, 0.0) + jnp.log1p(jnp.exp(-jnp.abs(v)))
    g = pa_ref[...] * softplus
    g_hi, g_rest = _split_bf16(g)
    g_mid = g - g_hi.astype(F32) - g_rest.astype(F32)
    pieces = jnp.concatenate([g_hi, g_rest, g_mid.astype(BF16)], axis=1)
    sums = jnp.dot(tri_ref[...], pieces, preferred_element_type=F32)
    gc = sums[:, :LANES] + sums[:, LANES:2 * LANES] + sums[:, 2 * LANES:]
    lane = lax.broadcasted_iota(I32, (tm, LANES), 1)
    res = jnp.where(lane < GDN_HEADS, beta, gc)
    bd_ref[...] = res[:, :2 * GDN_HEADS]
    bdt_ref[...] = res.T[:2 * GDN_HEADS, :]


def _inproj(x, g, wqkv, wz, wbd, wglu, bglu, pa, pb, chunk, t_batch):
    t = x.shape[0]
    tm = TOKEN_TILE
    tiles_per_batch = t_batch // tm if t_batch % tm == 0 else 0
    pos = jnp.arange(tm)
    tri = ((pos[:, None] >= pos[None, :])
           & (pos[:, None] // chunk == pos[None, :] // chunk)).astype(BF16)
    full = lambda shape: pl.BlockSpec(shape, lambda i: (0, 0))
    if tiles_per_batch:
        bdt_spec = pl.BlockSpec((None, 2 * GDN_HEADS, tm),
                                lambda i: (i // tiles_per_batch, 0, i % tiles_per_batch))
        bdt_shape = jax.ShapeDtypeStruct((t // t_batch, 2 * GDN_HEADS, t_batch), F32)
    else:
        bdt_spec = pl.BlockSpec((2 * GDN_HEADS, tm), lambda i: (0, i))
        bdt_shape = jax.ShapeDtypeStruct((2 * GDN_HEADS, t), F32)
    return pl.pallas_call(
        _inproj_kernel,
        grid=(t // tm,),
        in_specs=[
            pl.BlockSpec((tm, D_MODEL), lambda i: (i, 0)),
            full((1, D_MODEL)),
            full((D_MODEL, QKV_COLS)),
            full((D_MODEL, GDN_WIDTH)),
            full((D_MODEL, LANES)),
            full((D_MODEL, 2 * CONF_WIDTH)),
            full((1, 2 * CONF_WIDTH)),
            full((1, LANES)),
            full((1, LANES)),
            full((tm, tm)),
        ],
        out_specs=[
            pl.BlockSpec((tm, QKV_COLS), lambda i: (i, 0)),
            pl.BlockSpec((tm, GDN_WIDTH), lambda i: (i, 0)),
            pl.BlockSpec((tm, 2 * GDN_HEADS), lambda i: (i, 0)),
            bdt_spec,
            pl.BlockSpec((tm, CONF_WIDTH), lambda i: (i, 0)),
        ],
        out_shape=[
            jax.ShapeDtypeStruct((t, QKV_COLS), F32),
            jax.ShapeDtypeStruct((t, GDN_WIDTH), F32),
            jax.ShapeDtypeStruct((t, 2 * GDN_HEADS), F32),
            bdt_shape,
            jax.ShapeDtypeStruct((t, CONF_WIDTH), F32),
        ],
        compiler_params=_params(("arbitrary",)),
        name="inproj",
    )(x, g, wqkv, wz, wbd, wglu, bglu, pa, pb, tri)


def _split_bf16(a):
    hi = a.astype(BF16)
    lo = (a - hi.astype(F32)).astype(BF16)
    return hi, lo


def _dot3(a_hi, a_lo, b_hi, b_lo):
    dot = functools.partial(jnp.dot, preferred_element_type=F32)
    return dot(a_hi, b_hi) + dot(a_lo, b_hi) + dot(a_hi, b_lo)


def _unit_lower_inverses(ms, eye, n):
    levels = int(math.log2(n)) - 1
    ps = [eye - m for m in ms]
    pw = [_split_bf16(-m) for m in ms]
    pw = [_split_bf16(_dot3(h, l, h, l)) for h, l in pw]
    for level in range(1, levels + 1):
        last = level == levels
        nxt = []
        for u, (h, l) in enumerate(pw):
            ph, pl_ = _split_bf16(ps[u])
            if last:
                ps[u] = ps[u] + _dot3(ph, pl_, h, l)
            else:
                out = _dot3(jnp.concatenate([h, ph], axis=0), jnp.concatenate([l, pl_], axis=0), h, l)
                nxt.append(_split_bf16(out[:n]))
                ps[u] = ps[u] + out[n:]
        pw = nxt
    return ps


def _gdn_kernel(chunk, x_ref, z_ref, bd_ref, bdt_ref, hist_ref, s0_ref, wc_ref, won_ref,
                o_ref, s_ref, xbuf):
    i = pl.program_id(1)
    nb, tt = x_ref.shape[0], x_ref.shape[1]
    nchunk = tt // chunk
    pad = SUBLANES
    hist_rows = QKV_TAPS - 1

    @pl.when(i == 0)
    def _():
        xbuf[:, pad - hist_rows:pad, :] = hist_ref[...]
        s_ref[...] = s0_ref[...]

    xbuf[:, pad:pad + tt, :] = x_ref[...]

    ii = lax.broadcasted_iota(I32, (chunk, chunk), 0)
    jj = lax.broadcasted_iota(I32, (chunk, chunk), 1)
    lower_incl = ii >= jj
    lower_strict = ii > jj
    eye = (ii == jj).astype(F32)
    won = won_ref[...]

    def conv_silu(b, col0):
        cols = slice(col0, col0 + HEAD_DIM)
        acc = wc_ref[hist_rows:hist_rows + 1, cols] * xbuf[b, pad:pad + tt, cols]
        for j in range(hist_rows):
            r0 = pad - hist_rows + j
            acc = acc + wc_ref[j:j + 1, cols] * xbuf[b, r0:r0 + tt, cols]
        return acc * _sigmoid(acc)

    units = []
    for b in range(nb):
        bd = bd_ref[b]
        bdt = bdt_ref[b]
        for h in range(GDN_HEADS):
            q = conv_silu(b, h * HEAD_DIM)
            k = conv_silu(b, GDN_WIDTH + h * HEAD_DIM)
            v = conv_silu(b, 2 * GDN_WIDTH + h * HEAD_DIM)
            q = q * lax.rsqrt(jnp.sum(q * q, axis=-1, keepdims=True) + L2_EPS) * (HEAD_DIM ** -0.5)
            k = k * lax.rsqrt(jnp.sum(k * k, axis=-1, keepdims=True) + L2_EPS)
            kt = k.T
            beta = bd[:, h:h + 1]
            gcol = bd[:, GDN_HEADS + h:GDN_HEADS + h + 1]
            grow = bdt[GDN_HEADS + h:GDN_HEADS + h + 1, :]
            for c in range(nchunk):
                rows = slice(c * chunk, (c + 1) * chunk)
                qc, kc, vc, ktc = q[rows], k[rows], v[rows], kt[:, rows]
                bc, gc, gr = beta[rows], gcol[rows], grow[:, rows]
                dec = jnp.exp(jnp.where(lower_incl, gc - gr, -jnp.inf))
                kb = kc * bc
                eg = jnp.exp(gc)
                both = _dotb(jnp.concatenate([kb, qc], axis=0), ktc)
                units.append(dict(
                    b=b, h=h, c=c,
                    m=both[:chunk] * jnp.where(lower_strict, dec, 0.0),
                    qk=both[chunk:] * dec,
                    rhs=jnp.concatenate([vc * bc, kb * eg], axis=1),
                    qg=qc * eg,
                    kdt=ktc * jnp.exp(gr[:, chunk - 1:chunk] - gr),
                    d_last=jnp.exp(gc[chunk - 1:chunk, :]),
                ))

    tinvs = _unit_lower_inverses([u["m"] for u in units], eye, chunk)
    for u, tinv in zip(units, tinvs):
        th, tl = _split_bf16(tinv)
        rh, rl = _split_bf16(u["rhs"])
        sol = _dot3(th, tl, rh, rl)
        u["u"], u["w"] = sol[:, :HEAD_DIM], sol[:, HEAD_DIM:]

    state = {(b, h): s_ref[b, h] for b in range(nb) for h in range(GDN_HEADS)}
    by_key = {(u["b"], u["h"], u["c"]): u for u in units}
    for c in range(nchunk):
        rows = slice(c * chunk, (c + 1) * chunk)
        for b in range(nb):
            for h in range(GDN_HEADS):
                u = by_key[(b, h, c)]
                s = state[(b, h)]
                ws = _dotb(jnp.concatenate([u["w"], u["qg"]], axis=0), s)
                v_new = u["u"] - ws[:chunk]
                upd = _dotb(jnp.concatenate([u["qk"], u["kdt"]], axis=0), v_new)
                o_c = ws[chunk:] + upd[:chunk]
                state[(b, h)] = s * u["d_last"] + upd[chunk:]
                cols = slice(h * HEAD_DIM, (h + 1) * HEAD_DIM)
                on = o_c * lax.rsqrt(jnp.mean(o_c * o_c, axis=-1, keepdims=True) + RMS_EPS) * won
                zc = z_ref[b, rows, cols]
                o_ref[b, rows, cols] = on * (zc * _sigmoid(zc))
    for (b, h), s in state.items():
        s_ref[b, h] = s

    xbuf[:, pad - hist_rows:pad, :] = x_ref[:, tt - hist_rows:tt, :]


def _gdn(qkv, z, bd, bdt, hist, s0, wconv, wonorm, chunk, tt, nb):
    b, t, _ = qkv.shape
    nt = t // tt
    return pl.pallas_call(
        functools.partial(_gdn_kernel, chunk),
        grid=(b // nb, nt),
        in_specs=[
            pl.BlockSpec((nb, tt, QKV_COLS), lambda bi, i: (bi, i, 0)),
            pl.BlockSpec((nb, tt, GDN_WIDTH), lambda bi, i: (bi, i, 0)),
            pl.BlockSpec((nb, tt, 2 * GDN_HEADS), lambda bi, i: (bi, i, 0)),
            pl.BlockSpec((nb, 2 * GDN_HEADS, tt), lambda bi, i: (bi, 0, i)),
            pl.BlockSpec((nb, QKV_TAPS - 1, QKV_COLS), lambda bi, i: (bi, 0, 0)),
            pl.BlockSpec((nb, GDN_HEADS, HEAD_DIM, HEAD_DIM), lambda bi, i: (bi, 0, 0, 0)),
            pl.BlockSpec((QKV_TAPS, QKV_COLS), lambda bi, i: (0, 0)),
            pl.BlockSpec((1, HEAD_DIM), lambda bi, i: (0, 0)),
        ],
        out_specs=[
            pl.BlockSpec((nb, tt, GDN_WIDTH), lambda bi, i: (bi, i, 0)),
            pl.BlockSpec((nb, GDN_HEADS, HEAD_DIM, HEAD_DIM), lambda bi, i: (bi, 0, 0, 0)),
        ],
        out_shape=[
            jax.ShapeDtypeStruct((b, t, GDN_WIDTH), F32),
            jax.ShapeDtypeStruct((b, GDN_HEADS, HEAD_DIM, HEAD_DIM), F32),
        ],
        scratch_shapes=[pltpu.VMEM((nb, tt + SUBLANES, QKV_COLS), F32)],
        compiler_params=_params(("arbitrary", "arbitrary")),
        name="gdn",
    )(qkv, z, bd, bdt, hist, s0, wconv, wonorm)


def _cconv_kernel(carry, u_ref, hist_ref, w_ref, b_ref, lg_ref, lb_ref, c_ref, ubuf, shifted):
    i = pl.program_id(1)
    tt = u_ref.shape[0]
    hist_rows = DW_TAPS - 1
    pad = 4 * SUBLANES
    base = pad - hist_rows

    @pl.when(i == 0)
    def _():
        ubuf[base:pad, :] = hist_ref[...]

    ubuf[pad:pad + tt, :] = u_ref[...]
    span = shifted.shape[1]
    for s in range(1, SUBLANES):
        shifted[s - 1] = ubuf[s:s + span, :]

    def tap_rows(j, r0, rows):
        q, s = divmod(base + j, SUBLANES)
        start = q * SUBLANES + r0
        if s == 0:
            return ubuf[start:start + rows, :]
        return shifted[s - 1, start:start + rows, :]

    rows = min(CONV_ROWS, tt)
    for r0 in range(0, tt, rows):
        acc = w_ref[0:1, :] * tap_rows(0, r0, rows)
        for j in range(1, DW_TAPS):
            acc = acc + w_ref[j:j + 1, :] * tap_rows(j, r0, rows)
        cv = acc + b_ref[...]
        mu = jnp.mean(cv, axis=-1, keepdims=True)
        xc = cv - mu
        var = jnp.mean(xc * xc, axis=-1, keepdims=True)
        y = xc * lax.rsqrt(var + LN_EPS) * lg_ref[...] + lb_ref[...]
        c_ref[r0:r0 + rows, :] = y * _sigmoid(y)
    if carry:
        ubuf[base:pad, :] = u_ref[tt - hist_rows:tt, :]


def _cconv(uu, hist, w, b, lg, lb, tt):
    bsz, t, _ = uu.shape
    nt = t // tt
    vec = pl.BlockSpec((1, CONF_WIDTH), lambda bi, i: (0, 0))
    return pl.pallas_call(
        functools.partial(_cconv_kernel, nt > 1),
        grid=(bsz, nt),
        in_specs=[
            pl.BlockSpec((None, tt, CONF_WIDTH), lambda bi, i: (bi, i, 0)),
            pl.BlockSpec((None, DW_TAPS - 1, CONF_WIDTH), lambda bi, i: (bi, 0, 0)),
            pl.BlockSpec((DW_TAPS, CONF_WIDTH), lambda bi, i: (0, 0)),
            vec, vec, vec,
        ],
        out_specs=pl.BlockSpec((None, tt, CONF_WIDTH), lambda bi, i: (bi, i, 0)),
        out_shape=jax.ShapeDtypeStruct((bsz, t, CONF_WIDTH), F32),
        scratch_shapes=[pltpu.VMEM((tt + 4 * SUBLANES, CONF_WIDTH), F32),
                        pltpu.VMEM((SUBLANES - 1, tt + 3 * SUBLANES, CONF_WIDTH), F32)],
        compiler_params=_params(("arbitrary", "arbitrary")),
        name="cconv",
    )(uu, hist, w, b, lg, lb)


def _outproj_kernel(tiles_a, oa_ref, ca_ref, xa_ref, ob_ref, cb_ref, xb_ref, *rest):
    i = pl.program_id(0)

    @pl.when(i < tiles_a)
    def _():
        _outproj_tile(oa_ref, ca_ref, xa_ref, *rest)

    @pl.when(i >= tiles_a)
    def _():
        _outproj_tile(ob_ref, cb_ref, xb_ref, *rest)


def _outproj_tile(o_ref, c_ref, x_ref, wo_ref, wc_ref, g_ref, wr_ref, br_ref, upper_ref,
                  x1_ref, h2_ref, metar_ref, cnt_ref):
    tm = x_ref.shape[0]

    mix = _dotb(o_ref[...], wo_ref[...]) + _dotb(c_ref[...], wc_ref[...])
    x1 = x_ref[...] + mix
    x1_ref[...] = x1
    h2 = x1 * lax.rsqrt(jnp.mean(x1 * x1, axis=-1, keepdims=True) + RMS_EPS) * g_ref[...]
    h2_ref[...] = h2.astype(BF16)
    logits = lax.dot_general(wr_ref[...], h2.astype(BF16), (((1,), (1,)), ((), ())),
                             preferred_element_type=F32) + br_ref[...]
    row = lax.broadcasted_iota(I32, (N_EXPERTS, tm), 0).astype(F32)
    vals = logits
    idxs, tops = [], []
    for _ in range(TOP_K):
        m = jnp.max(vals, axis=0, keepdims=True)
        idx = jnp.min(jnp.where(vals == m, row, float(N_EXPERTS)), axis=0, keepdims=True)
        idxs.append(idx)
        tops.append(m)
        vals = jnp.where(row == idx, -jnp.inf, vals)
    exps = [jnp.exp(v - tops[0]) for v in tops]
    den = exps[0] + exps[1] + exps[2] + exps[3]
    onehots = [(row == idx).astype(F32) for idx in idxs]
    chosen = onehots[0] + onehots[1] + onehots[2] + onehots[3]
    before = jnp.dot(chosen.astype(BF16), upper_ref[...], preferred_element_type=F32)
    ranks = [jnp.sum(oh * before, axis=0, keepdims=True) for oh in onehots]
    gates = [e / den for e in exps]
    metar_ref[...] = jnp.concatenate(
        idxs + ranks + gates + [jnp.zeros((META_ROWS - 3 * TOP_K, tm), F32)], axis=0)
    cnt_ref[...] = jnp.broadcast_to(jnp.sum(chosen, axis=1, keepdims=True), (N_EXPERTS, LANES))[None]


def _outproj(first, second, wo, wc, g, wr, br):
    t_a, t_b = first[2].shape[0], second[2].shape[0]
    tm = TOKEN_TILE
    upper = (jnp.arange(tm)[:, None] < jnp.arange(tm)[None, :]).astype(BF16)
    tiles_a = t_a // tm
    t_total = t_a + t_b
    full = lambda shape: pl.BlockSpec(shape, lambda i: (0, 0))
    rows_a = lambda width: pl.BlockSpec((tm, width), lambda i: (jnp.minimum(i, tiles_a - 1), 0))
    rows_b = lambda width: pl.BlockSpec((tm, width), lambda i: (jnp.maximum(i - tiles_a, 0), 0))
    widths = (GDN_WIDTH, CONF_WIDTH, D_MODEL)
    return pl.pallas_call(
        functools.partial(_outproj_kernel, tiles_a),
        grid=(t_total // tm,),
        in_specs=[rows_a(w) for w in widths] + [rows_b(w) for w in widths] + [
            full((GDN_WIDTH, D_MODEL)),
            full((CONF_WIDTH, D_MODEL)),
            full((1, D_MODEL)),
            full((N_EXPERTS, D_MODEL)),
            full((N_EXPERTS, 1)),
            full((tm, tm)),
        ],
        out_specs=[
            pl.BlockSpec((tm, D_MODEL), lambda i: (i, 0)),
            pl.BlockSpec((tm, D_MODEL), lambda i: (i, 0)),
            pl.BlockSpec((META_ROWS, tm), lambda i: (0, i)),
            pl.BlockSpec((1, N_EXPERTS, LANES), lambda i: (i, 0, 0)),
        ],
        out_shape=[
            jax.ShapeDtypeStruct((t_total, D_MODEL), F32),
            jax.ShapeDtypeStruct((t_total, D_MODEL), BF16),
            jax.ShapeDtypeStruct((META_ROWS, t_total), F32),
            jax.ShapeDtypeStruct((t_total // tm, N_EXPERTS, LANES), F32),
        ],
        compiler_params=_params(("arbitrary",)),
        name="outproj_router",
    )(*first, *second, wo, wc, g, wr, br, upper)


def _segment_copies(i, seg_local_ref, seg_dst_ref, seg_n_ref, make_copy, wait):
    def per_expert(ex, carry):
        j = i * N_EXPERTS + ex
        local, dst, n = seg_local_ref[j], seg_dst_ref[j], seg_n_ref[j]
        done = 0
        for per_copy, count in ((8, n >> 3), (2, (n >> 1) & 3), (1, n & 1)):
            def body(g, c, per_copy=per_copy, done=done):
                off = (done + g * per_copy) * GRANULE
                cp = make_copy(per_copy * GRANULE, pl.multiple_of(local + off, GRANULE),
                               pl.multiple_of(dst + off, GRANULE))
                cp.wait() if wait else cp.start()
                return c

            lax.fori_loop(0, count, body, 0)
            done = done + count * per_copy
        return carry

    lax.fori_loop(0, N_EXPERTS, per_expert, 0)


def _gap_copies(gap_start_ref, gap_n_ref, make_copy, wait):
    def per_expert(ex, carry):
        n = gap_n_ref[ex]
        start = pl.multiple_of(gap_start_ref[ex], GRANULE)
        off = 0 * n
        for bit in reversed(range(GAP_BITS)):
            rows = GRANULE << bit

            @pl.when(((n >> bit) & 1) == 1)
            def _(rows=rows, off=off):
                cp = make_copy(rows, pl.multiple_of(start + off, GRANULE))
                cp.wait() if wait else cp.start()

            off = off + ((n >> bit) & 1) * rows
        return carry

    lax.fori_loop(0, N_EXPERTS, per_expert, 0)


def _local_rows(meta_e, seg_local_ref, i):
    base = jnp.zeros(meta_e.shape, F32)
    for ex in range(N_EXPERTS):
        base = jnp.where(meta_e == float(ex), seg_local_ref[i * N_EXPERTS + ex].astype(F32), base)
    return base


def _dispatch_kernel(seg_local_ref, seg_dst_ref, seg_n_ref, gap_start_ref, gap_n_ref, tail_ref,
                     h_ref, mr_ref, xs_ref, slots_ref, xloc, zbuf, sems, zsem):
    i = pl.program_id(0)
    last = pl.num_programs(0) - 1
    tm = h_ref.shape[0]
    rt = xloc.shape[1]
    zrows = zbuf.shape[0]
    slot = i % 2

    def gap_copy(rows, dst):
        return pltpu.make_async_copy(zbuf.at[pl.ds(0, rows)], xs_ref.at[pl.ds(dst, rows)], zsem)

    def tail_copies(wait):
        def body(j, c):
            cp = gap_copy(zrows, pl.multiple_of(tail_ref[0] + j * zrows, zrows))
            cp.wait() if wait else cp.start()
            return c
        lax.fori_loop(0, tail_ref[1], body, 0)

    @pl.when(i == 0)
    def _():
        zbuf[...] = jnp.zeros(zbuf.shape, zbuf.dtype)
        _gap_copies(gap_start_ref, gap_n_ref, gap_copy, wait=False)
        tail_copies(wait=False)

    mr = mr_ref[...]
    dl = (_local_rows(mr[META_EXPERT:META_EXPERT + TOP_K], seg_local_ref, i)
          + mr[META_RANK:META_RANK + TOP_K])
    slots_ref[...] = jnp.concatenate(
        [dl, mr[META_GATE:META_GATE + TOP_K], jnp.zeros((LANES - 2 * TOP_K, tm), F32)], axis=0).T
    rr = lax.broadcasted_iota(I32, (rt, tm), 0).astype(F32)
    hit = rr == dl[0:1]
    for k in range(1, TOP_K):
        hit = hit | (rr == dl[k:k + 1])
    xloc[slot] = jnp.dot(hit.astype(BF16), h_ref[...], preferred_element_type=F32)

    def seg_copy(buf):
        def make(rows, local, dst):
            return pltpu.make_async_copy(xloc.at[buf, pl.ds(local, rows)],
                                         xs_ref.at[pl.ds(dst, rows)], sems.at[buf])
        return make

    _segment_copies(i, seg_local_ref, seg_dst_ref, seg_n_ref, seg_copy(slot), wait=False)

    @pl.when(i > 0)
    def _():
        _segment_copies(i - 1, seg_local_ref, seg_dst_ref, seg_n_ref, seg_copy(1 - slot), wait=True)

    @pl.when(i == last)
    def _():
        _segment_copies(i, seg_local_ref, seg_dst_ref, seg_n_ref, seg_copy(slot), wait=True)

    @pl.when(i == 0)
    def _():
        _gap_copies(gap_start_ref, gap_n_ref, gap_copy, wait=True)
        tail_copies(wait=True)


def _dispatch(tables, h2, metar, n_rows):
    t = h2.shape[0]
    tm = TOKEN_TILE
    return pl.pallas_call(
        _dispatch_kernel,
        grid_spec=pltpu.PrefetchScalarGridSpec(
            num_scalar_prefetch=6,
            grid=(t // tm,),
            in_specs=[
                pl.BlockSpec((tm, D_MODEL), lambda i, *_: (i, 0)),
                pl.BlockSpec((META_ROWS, tm), lambda i, *_: (0, i)),
            ],
            out_specs=[pl.BlockSpec(memory_space=pl.ANY),
                       pl.BlockSpec((tm, LANES), lambda i, *_: (i, 0))],
            scratch_shapes=[
                pltpu.VMEM((2, TILE_ROWS, D_MODEL), F32),
                pltpu.VMEM((GRANULE << (GAP_BITS - 1), D_MODEL), F32),
                pltpu.SemaphoreType.DMA((2,)),
                pltpu.SemaphoreType.DMA,
            ],
        ),
        out_shape=[jax.ShapeDtypeStruct((n_rows, D_MODEL), F32),
                   jax.ShapeDtypeStruct((t, LANES), F32)],
        compiler_params=_params(("arbitrary",)),
        name="dispatch",
    )(*tables, h2, metar)


def _expert_kernel(be_ref, rb_ref, nu_ref, first_ref, slot_ref, next_ref,
                   x_ref, wgu_hbm, bgu_ref, wd_hbm, bdn_ref, y_ref,
                   wgu_f32, wd_f32, wgu_bf, wd_bf, sems):
    i = pl.program_id(0)

    def fetch(expert, slot):
        return (pltpu.make_async_copy(wgu_hbm.at[expert], wgu_f32.at[slot], sems.at[slot, 0]),
                pltpu.make_async_copy(wd_hbm.at[expert], wd_f32.at[slot], sems.at[slot, 1]))

    @pl.when(i == 0)
    def _():
        for cp in fetch(be_ref[0], 0):
            cp.start()

    @pl.when(first_ref[i] == 1)
    def _():
        slot = slot_ref[i]
        for cp in fetch(be_ref[i], slot):
            cp.wait()
        wgu_bf[...] = wgu_f32[slot].astype(BF16)
        wd_bf[...] = wd_f32[slot].astype(BF16)

        @pl.when(next_ref[i] >= 0)
        def _():
            for cp in fetch(next_ref[i], 1 - slot):
                cp.start()

    @pl.when(i < nu_ref[0])
    def _():
        gu = _dotb(x_ref[...], wgu_bf[...]) + bgu_ref[...]
        x_glu = jnp.minimum(gu[:, :D_EXPERT], SWIGLU_LIMIT)
        x_lin = jnp.clip(gu[:, D_EXPERT:], -SWIGLU_LIMIT, SWIGLU_LIMIT)
        act = x_glu * _sigmoid(SWIGLU_ALPHA * x_glu) * (x_lin + 1.0)
        y_ref[...] = _dotb(act, wd_bf[...]) + bdn_ref[...]

    @pl.when(i >= nu_ref[0])
    def _():
        y_ref[...] = jnp.zeros(y_ref.shape, y_ref.dtype)


def _experts(tables, xs, wgu, bgu, wd, bdn):
    rows = xs.shape[0]
    bm = EXPERT_ROWS
    nb = rows // bm
    return pl.pallas_call(
        _expert_kernel,
        grid_spec=pltpu.PrefetchScalarGridSpec(
            num_scalar_prefetch=6,
            grid=(nb,),
            in_specs=[
                pl.BlockSpec((bm, D_MODEL), lambda i, be, rb, *_: (rb[i], 0)),
                pl.BlockSpec(memory_space=pl.ANY),
                pl.BlockSpec((None, 1, 2 * D_EXPERT), lambda i, be, *_: (be[i], 0, 0)),
                pl.BlockSpec(memory_space=pl.ANY),
                pl.BlockSpec((None, 1, D_MODEL), lambda i, be, *_: (be[i], 0, 0)),
            ],
            out_specs=pl.BlockSpec((bm, D_MODEL), lambda i, *_: (i, 0)),
            scratch_shapes=[
                pltpu.VMEM((2, D_MODEL, 2 * D_EXPERT), F32),
                pltpu.VMEM((2, D_EXPERT, D_MODEL), F32),
                pltpu.VMEM((D_MODEL, 2 * D_EXPERT), BF16),
                pltpu.VMEM((D_EXPERT, D_MODEL), BF16),
                pltpu.SemaphoreType.DMA((2, 2)),
            ],
        ),
        out_shape=jax.ShapeDtypeStruct((rows, D_MODEL), F32),
        compiler_params=pltpu.CompilerParams(dimension_semantics=("arbitrary",),
                                             vmem_limit_bytes=EXPERT_VMEM_LIMIT),
        name="experts",
    )(*tables, xs, wgu, bgu, wd, bdn)


def _combine_kernel(tiles_a, seg_local_ref, seg_dst_ref, seg_n_ref, x1_ref, mc_ref, yb_ref, gf_ref,
                    ya_ref, yb_out_ref, ybuf, wsel, sems):
    i = pl.program_id(0)
    n_tiles = pl.num_programs(0)
    tm = x1_ref.shape[0]
    rt = ybuf.shape[1]
    slot = i % 2

    def seg_copy(buf):
        def make(rows, local, src):
            return pltpu.make_async_copy(yb_ref.at[pl.ds(src, rows)],
                                         ybuf.at[buf, pl.ds(local, rows)], sems.at[buf])
        return make

    @pl.when(i == 0)
    def _():
        ybuf[...] = jnp.zeros(ybuf.shape, ybuf.dtype)
        _segment_copies(i, seg_local_ref, seg_dst_ref, seg_n_ref, seg_copy(slot), wait=False)

    @pl.when(i + 1 < n_tiles)
    def _():
        _segment_copies(i + 1, seg_local_ref, seg_dst_ref, seg_n_ref, seg_copy(1 - slot), wait=False)

    dl = mc_ref[:, 0:TOP_K]
    gate = mc_ref[:, TOP_K:2 * TOP_K]
    rb, cb = SEL_CHUNK
    col = lax.broadcasted_iota(I32, (rb, cb), 1).astype(F32)
    for r0 in range(0, tm, rb):
        rows_k = [jnp.broadcast_to(dl[r0:r0 + rb, k:k + 1], (rb, cb)) for k in range(TOP_K)]
        gate_k = [jnp.broadcast_to(gate[r0:r0 + rb, k:k + 1], (rb, cb)) for k in range(TOP_K)]
        for c0 in range(0, rt, cb):
            rr = col + float(c0)
            w = jnp.where(rr == rows_k[0], gate_k[0], 0.0)
            for k in range(1, TOP_K):
                w = jnp.where(rr == rows_k[k], gate_k[k], w)
            wsel[r0:r0 + rb, c0:c0 + cb] = w.astype(BF16)
    _segment_copies(i, seg_local_ref, seg_dst_ref, seg_n_ref, seg_copy(slot), wait=True)
    acc = x1_ref[...] + jnp.dot(wsel[...], ybuf[slot].astype(BF16), preferred_element_type=F32)
    y = acc * lax.rsqrt(jnp.mean(acc * acc, axis=-1, keepdims=True) + RMS_EPS) * gf_ref[...]

    @pl.when(i < tiles_a)
    def _():
        ya_ref[...] = y

    @pl.when(i >= tiles_a)
    def _():
        yb_out_ref[...] = y


def _combine(tables, x1, slots, yb, gfin, n_a):
    t = x1.shape[0]
    tm = TOKEN_TILE
    tiles_a = n_a // tm
    return pl.pallas_call(
        functools.partial(_combine_kernel, tiles_a),
        grid_spec=pltpu.PrefetchScalarGridSpec(
            num_scalar_prefetch=3,
            grid=(t // tm,),
            in_specs=[
                pl.BlockSpec((tm, D_MODEL), lambda i, *_: (i, 0)),
                pl.BlockSpec((tm, LANES), lambda i, *_: (i, 0)),
                pl.BlockSpec(memory_space=pl.ANY),
                pl.BlockSpec((1, D_MODEL), lambda i, *_: (0, 0)),
            ],
            out_specs=[
                pl.BlockSpec((tm, D_MODEL), lambda i, *_: (jnp.minimum(i, tiles_a - 1), 0)),
                pl.BlockSpec((tm, D_MODEL), lambda i, *_: (jnp.maximum(i - tiles_a, 0), 0)),
            ],
            scratch_shapes=[pltpu.VMEM((2, TILE_ROWS, D_MODEL), F32),
                            pltpu.VMEM((tm, TILE_ROWS), BF16),
                            pltpu.SemaphoreType.DMA((2,))],
        ),
        out_shape=[jax.ShapeDtypeStruct((n_a, D_MODEL), F32),
                   jax.ShapeDtypeStruct((t - n_a, D_MODEL), F32)],
        compiler_params=pltpu.CompilerParams(dimension_semantics=("arbitrary",),
                                             vmem_limit_bytes=EXPERT_VMEM_LIMIT),
        name="combine",
    )(*tables, x1, slots, yb, gfin)


def kernel(x_prompt, x_sample, state_gdn, state_qkv_conv, state_dwconv, norm_mix_g, w_in, w_conv_qkv,
           a_log, dt_bias, w_onorm, b_glu, w_dw, b_dw, ln_g, ln_b, w_out, norm_ffn_g, w_router,
           b_router, w_gate_up, b_gate_up, w_down, b_down, norm_final_g):
    bp, tp, _ = x_prompt.shape
    bs, ts, _ = x_sample.shape
    n_p, n_s = bp * tp, bs * ts

    wi = w_in[0]
    c_z = QKV_COLS
    c_bd = c_z + GDN_WIDTH
    c_glu = c_bd + 2 * GDN_HEADS
    wqkv = wi[:, :c_z].astype(BF16)
    wz = wi[:, c_z:c_bd].astype(BF16)
    wbd = jnp.pad(wi[:, c_bd:c_glu], ((0, 0), (0, LANES - 2 * GDN_HEADS))).astype(BF16)
    wglu = wi[:, c_glu:].astype(BF16)
    bglu = b_glu[0][None, :]
    lane_pad = (GDN_HEADS, LANES - 2 * GDN_HEADS)
    pa = jnp.pad(-jnp.exp(a_log[0].astype(F32)), lane_pad)[None, :]
    pb = jnp.pad(dt_bias[0].astype(F32), lane_pad)[None, :]
    gmix = norm_mix_g[0][None, :]
    wconv = w_conv_qkv[0]
    wonorm = w_onorm[0][None, :]
    wdw, bdw = w_dw[0], b_dw[0][None, :]
    lng, lnb = ln_g[0][None, :], ln_b[0][None, :]
    wo = w_out[0][:GDN_WIDTH].astype(BF16)
    wc = w_out[0][GDN_WIDTH:].astype(BF16)
    gffn = norm_ffn_g[0][None, :]
    wr, br = w_router[0].T.astype(BF16), b_router[0].astype(F32)[:, None]
    bgu = b_gate_up[0][:, None, :]
    bdn = b_down[0][:, None, :]
    gfin = norm_final_g[None, :]

    xp = x_prompt.reshape(n_p, D_MODEL)
    xs_tok = x_sample.reshape(n_s, D_MODEL)

    def mixers(x2d, bsz, t, hist_qkv, s0, hist_glu, chunk, gdn_tile, gdn_rows, conv_tile):
        qkv, z, bd, bdt, uu = _inproj(x2d, gmix, wqkv, wz, wbd, wglu, bglu, pa, pb, chunk, t)
        qkv3 = qkv.reshape(bsz, t, QKV_COLS)
        uu3 = uu.reshape(bsz, t, CONF_WIDTH)
        if bdt.ndim == 2:
            bdt = bdt.reshape(2 * GDN_HEADS, bsz, t).transpose(1, 0, 2)
        o, s_new = _gdn(qkv3, z.reshape(bsz, t, GDN_WIDTH), bd.reshape(bsz, t, 2 * GDN_HEADS),
                        bdt, hist_qkv, s0, wconv, wonorm, chunk, gdn_tile, gdn_rows)
        cc = _cconv(uu3, hist_glu, wdw, bdw, lng, lnb, conv_tile)
        return o.reshape(bsz * t, GDN_WIDTH), cc.reshape(bsz * t, CONF_WIDTH), s_new, qkv3, uu3

    zero_qkv = jnp.zeros((bp, QKV_TAPS - 1, QKV_COLS), F32)
    zero_s = jnp.zeros((bp, GDN_HEADS, HEAD_DIM, HEAD_DIM), F32)
    zero_glu = jnp.zeros((bp, DW_TAPS - 1, CONF_WIDTH), F32)
    o_p, c_p, s_p, qkv_p, uu_p = mixers(xp, bp, tp, zero_qkv, zero_s, zero_glu,
                                        PROMPT_CHUNK, GDN_TILE, 1, CONV_TILE)
    o_s, c_s, s_s, qkv_s, uu_s = mixers(xs_tok, bs, ts, state_qkv_conv[0], state_gdn[0],
                                        state_dwconv[0], ts, ts, GDN_SAMPLE_ROWS, ts)

    n_tok = n_p + n_s
    x1, h2, metar, cnt_tile = _outproj((o_p, c_p, xp), (o_s, c_s, xs_tok), wo, wc, gffn, wr, br)

    bm = EXPERT_ROWS
    n_tiles = n_tok // TOKEN_TILE
    counts = cnt_tile[:, :, 0].astype(I32)
    seg_rows = (counts + GRANULE - 1) // GRANULE * GRANULE
    ids = jnp.arange(N_EXPERTS, dtype=I32)
    tile_ids = jnp.arange(n_tiles, dtype=I32)
    earlier_e = (ids[:, None] < ids[None, :]).astype(I32)
    earlier_t = (tile_ids[None, :] < tile_ids[:, None]).astype(I32)
    seg_local = jnp.sum(seg_rows[:, :, None] * earlier_e[None], axis=1)
    seg_before = jnp.sum(earlier_t[:, :, None] * seg_rows[None], axis=1)
    rows_e = jnp.sum(seg_rows, axis=0)
    padded = (rows_e + bm - 1) // bm * bm
    pstart = jnp.sum(padded[:, None] * earlier_e, axis=0)
    pend = pstart + padded
    seg_dst = pstart[None, :] + seg_before
    n_used = jnp.maximum(pend[-1] // bm, 1).astype(I32)
    max_rows = n_tok * TOP_K + n_tiles * N_EXPERTS * (GRANULE - 1) + N_EXPERTS * (bm - 1)
    n_blocks = -(-max_rows // bm)
    blk = jnp.minimum(jnp.arange(n_blocks, dtype=I32), n_used - 1)
    block_expert = jnp.minimum(
        jnp.sum((pend[None, :] <= (blk * bm)[:, None]).astype(I32), axis=1), N_EXPERTS - 1)
    seg_tables = (seg_local.reshape(-1).astype(I32), seg_dst.reshape(-1).astype(I32),
                  (seg_rows // GRANULE).reshape(-1).astype(I32))
    gap_tables = ((pstart + rows_e).astype(I32), ((padded - rows_e) // GRANULE).astype(I32))
    zero_rows = GRANULE << (GAP_BITS - 1)
    tail_table = jnp.stack([pend[-1], (n_blocks * bm - pend[-1]) // zero_rows]).astype(I32)

    xs, slots = _dispatch(seg_tables + gap_tables + (tail_table,), h2, metar, n_blocks * bm)
    active = padded > 0
    later = active[None, :] & (ids[None, :] > ids[:, None])
    next_active = jnp.min(jnp.where(later, ids[None, :], N_EXPERTS), axis=1)
    next_active = jnp.where(next_active == N_EXPERTS, -1, next_active)
    order = jnp.sum(active.astype(I32)[:, None] * earlier_e, axis=0)
    steps = jnp.arange(n_blocks, dtype=I32)
    of_block = (block_expert[:, None] == ids[None, :]).astype(I32)
    per_block = lambda table: jnp.sum(of_block * table[None, :], axis=1)
    first = ((steps * bm == per_block(pstart)) & (steps < n_used)).astype(I32)
    expert_tables = (block_expert, blk, n_used.reshape(1), first,
                     (per_block(order) % 2).astype(I32), per_block(next_active).astype(I32))
    yb = _experts(expert_tables, xs, w_gate_up[0], bgu, w_down[0], bdn)
    y_p, y_s = _combine(seg_tables, x1, slots, yb, gfin, n_p)

    hist_glu_s = jnp.concatenate([state_dwconv[0], uu_s], axis=1)[:, -(DW_TAPS - 1):]
    return (y_p.reshape(bp, tp, D_MODEL),
            y_s.reshape(bs, ts, D_MODEL),
            s_p[None],
            qkv_p[:, -(QKV_TAPS - 1):][None],
            uu_p[:, -(DW_TAPS - 1):][None],
            s_s[None],
            qkv_s[:, -(QKV_TAPS - 1):][None],
            hist_glu_s[None])
```

```python
import functools
import math

import jax
import jax.numpy as jnp
from jax import lax
from jax.experimental import pallas as pl
from jax.experimental.pallas import tpu as pltpu

F32 = jnp.float32
BF16 = jnp.bfloat16
I32 = jnp.int32
HIGHEST = lax.Precision.HIGHEST

D_MODEL = 1024
GDN_HEADS = 4
HEAD_DIM = 128
GDN_WIDTH = GDN_HEADS * HEAD_DIM
QKV_COLS = 3 * GDN_WIDTH
CONF_WIDTH = 512
QKV_TAPS = 4
DW_TAPS = 31
N_EXPERTS = 32
TOP_K = 4
D_EXPERT = 1024
SWIGLU_LIMIT = 7.0
SWIGLU_ALPHA = 1.702
RMS_EPS = 1e-6
LN_EPS = 1e-5
L2_EPS = 1e-6
PROMPT_CHUNK = 64

LANES = 128
SUBLANES = 8
TOKEN_TILE = 512
GDN_TILE = 512
GDN_SAMPLE_ROWS = 4
CONV_TILE = 256
CONV_ROWS = 32
EXPERT_ROWS = 512
GRANULE = SUBLANES
GAP_BITS = 6
TILE_ROWS = -(-(TOKEN_TILE * TOP_K + N_EXPERTS * (GRANULE - 1)) // 256) * 256
META_EXPERT, META_RANK, META_GATE, META_ROWS = 0, TOP_K, 2 * TOP_K, 16
assert EXPERT_ROWS // GRANULE <= 1 << GAP_BITS
SEL_CHUNK = (64, 256)
VMEM_LIMIT = 48 * 1024 * 1024
EXPERT_VMEM_LIMIT = 56 * 1024 * 1024


def _dotb(a, b):
    return jnp.dot(a.astype(BF16), b.astype(BF16), preferred_element_type=F32)


def _dotf(a, b):
    return jnp.dot(a, b, precision=HIGHEST, preferred_element_type=F32)


def _sigmoid(x):
    return jax.nn.sigmoid(x)


def _params(sem):
    return pltpu.CompilerParams(dimension_semantics=sem, vmem_limit_bytes=VMEM_LIMIT)


def _inproj_kernel(x_ref, g_ref, wqkv_ref, wz_ref, wbd_ref, wglu_ref, bglu_ref, pa_ref, pb_ref,
                   tri_ref, qkv_ref, z_ref, bd_ref, bdt_ref, uu_ref):
    x = x_ref[...]
    tm = x.shape[0]
    ms = jnp.mean(x * x, axis=-1, keepdims=True)
    h = (x * lax.rsqrt(ms + RMS_EPS) * g_ref[...]).astype(BF16)
    qkv_ref[...] = jnp.dot(h, wqkv_ref[...], preferred_element_type=F32)
    z_ref[...] = jnp.dot(h, wz_ref[...], preferred_element_type=F32)
    glu = jnp.dot(h, wglu_ref[...], preferred_element_type=F32) + bglu_ref[...]
    uu_ref[...] = glu[:, :CONF_WIDTH] * _sigmoid(glu[:, CONF_WIDTH:])
    bd = jnp.dot(h, wbd_ref[...], preferred_element_type=F32)
    beta = _sigmoid(bd)
    v = bd + pb_ref[...]
    softplus = jnp.maximum(v, 0.0) + jnp.log1p(jnp.exp(-jnp.abs(v)))
    g = pa_ref[...] * softplus
    g_hi, g_rest = _split_bf16(g)
    g_mid = g - g_hi.astype(F32) - g_rest.astype(F32)
    pieces = jnp.concatenate([g_hi, g_rest, g_mid.astype(BF16)], axis=1)
    sums = jnp.dot(tri_ref[...], pieces, preferred_element_type=F32)
    gc = sums[:, :LANES] + sums[:, LANES:2 * LANES] + sums[:, 2 * LANES:]
    lane = lax.broadcasted_iota(I32, (tm, LANES), 1)
    res = jnp.where(lane < GDN_HEADS, beta, gc)
    bd_ref[...] = res[:, :2 * GDN_HEADS]
    bdt_ref[...] = res.T[:2 * GDN_HEADS, :]


def _inproj(x, g, wqkv, wz, wbd, wglu, bglu, pa, pb, chunk, t_batch):
    t = x.shape[0]
    tm = TOKEN_TILE
    tiles_per_batch = t_batch // tm if t_batch % tm == 0 else 0
    pos = jnp.arange(tm)
    tri = ((pos[:, None] >= pos[None, :])
           & (pos[:, None] // chunk == pos[None, :] // chunk)).astype(BF16)
    full = lambda shape: pl.BlockSpec(shape, lambda i: (0, 0))
    if tiles_per_batch:
        bdt_spec = pl.BlockSpec((None, 2 * GDN_HEADS, tm),
                                lambda i: (i // tiles_per_batch, 0, i % tiles_per_batch))
        bdt_shape = jax.ShapeDtypeStruct((t // t_batch, 2 * GDN_HEADS, t_batch), F32)
    else:
        bdt_spec = pl.BlockSpec((2 * GDN_HEADS, tm), lambda i: (0, i))
        bdt_shape = jax.ShapeDtypeStruct((2 * GDN_HEADS, t), F32)
    return pl.pallas_call(
        _inproj_kernel,
        grid=(t // tm,),
        in_specs=[
            pl.BlockSpec((tm, D_MODEL), lambda i: (i, 0)),
            full((1, D_MODEL)),
            full((D_MODEL, QKV_COLS)),
            full((D_MODEL, GDN_WIDTH)),
            full((D_MODEL, LANES)),
            full((D_MODEL, 2 * CONF_WIDTH)),
            full((1, 2 * CONF_WIDTH)),
            full((1, LANES)),
            full((1, LANES)),
            full((tm, tm)),
        ],
        out_specs=[
            pl.BlockSpec((tm, QKV_COLS), lambda i: (i, 0)),
            pl.BlockSpec((tm, GDN_WIDTH), lambda i: (i, 0)),
            pl.BlockSpec((tm, 2 * GDN_HEADS), lambda i: (i, 0)),
            bdt_spec,
            pl.BlockSpec((tm, CONF_WIDTH), lambda i: (i, 0)),
        ],
        out_shape=[
            jax.ShapeDtypeStruct((t, QKV_COLS), F32),
            jax.ShapeDtypeStruct((t, GDN_WIDTH), F32),
            jax.ShapeDtypeStruct((t, 2 * GDN_HEADS), F32),
            bdt_shape,
            jax.ShapeDtypeStruct((t, CONF_WIDTH), F32),
        ],
        compiler_params=_params(("arbitrary",)),
        name="inproj",
    )(x, g, wqkv, wz, wbd, wglu, bglu, pa, pb, tri)


def _split_bf16(a):
    hi = a.astype(BF16)
    lo = (a - hi.astype(F32)).astype(BF16)
    return hi, lo


def _unit_lower_inverses(ms, eye, n):
    levels = int(math.log2(n)) - 1
    ps = [eye - m for m in ms]
    pw = [(-m).astype(BF16) for m in ms]
    pw = [jnp.dot(p, p, preferred_element_type=F32).astype(BF16) for p in pw]
    for level in range(1, levels + 1):
        last = level == levels
        nxt = []
        for u, p in enumerate(pw):
            if last:
                ps[u] = ps[u] + jnp.dot(ps[u].astype(BF16), p, preferred_element_type=F32)
            else:
                out = jnp.dot(jnp.concatenate([p, ps[u].astype(BF16)], axis=0), p,
                              preferred_element_type=F32)
                nxt.append(out[:n].astype(BF16))
                ps[u] = ps[u] + out[n:]
        pw = nxt
    return ps


def _gdn_kernel(chunk, x_ref, z_ref, bd_ref, bdt_ref, hist_ref, s0_ref, wc_ref, won_ref,
                o_ref, s_ref, xbuf):
    i = pl.program_id(1)
    nb, tt = x_ref.shape[0], x_ref.shape[1]
    nchunk = tt // chunk
    pad = SUBLANES
    hist_rows = QKV_TAPS - 1

    @pl.when(i == 0)
    def _():
        xbuf[:, pad - hist_rows:pad, :] = hist_ref[...]
        s_ref[...] = s0_ref[...]

    xbuf[:, pad:pad + tt, :] = x_ref[...]

    ii = lax.broadcasted_iota(I32, (chunk, chunk), 0)
    jj = lax.broadcasted_iota(I32, (chunk, chunk), 1)
    lower_incl = ii >= jj
    lower_strict = ii > jj
    eye = (ii == jj).astype(F32)
    won = won_ref[...]

    def conv_silu(b, col0):
        cols = slice(col0, col0 + HEAD_DIM)
        acc = wc_ref[hist_rows:hist_rows + 1, cols] * xbuf[b, pad:pad + tt, cols]
        for j in range(hist_rows):
            r0 = pad - hist_rows + j
            acc = acc + wc_ref[j:j + 1, cols] * xbuf[b, r0:r0 + tt, cols]
        return acc * _sigmoid(acc)

    units = []
    for b in range(nb):
        bd = bd_ref[b]
        bdt = bdt_ref[b]
        for h in range(GDN_HEADS):
            q = conv_silu(b, h * HEAD_DIM)
            k = conv_silu(b, GDN_WIDTH + h * HEAD_DIM)
            v = conv_silu(b, 2 * GDN_WIDTH + h * HEAD_DIM)
            q = q * lax.rsqrt(jnp.sum(q * q, axis=-1, keepdims=True) + L2_EPS) * (HEAD_DIM ** -0.5)
            k = k * lax.rsqrt(jnp.sum(k * k, axis=-1, keepdims=True) + L2_EPS)
            kt = k.T
            beta = bd[:, h:h + 1]
            gcol = bd[:, GDN_HEADS + h:GDN_HEADS + h + 1]
            grow = bdt[GDN_HEADS + h:GDN_HEADS + h + 1, :]
            for c in range(nchunk):
                rows = slice(c * chunk, (c + 1) * chunk)
                qc, kc, vc, ktc = q[rows], k[rows], v[rows], kt[:, rows]
                bc, gc, gr = beta[rows], gcol[rows], grow[:, rows]
                dec = jnp.exp(jnp.where(lower_incl, gc - gr, -jnp.inf))
                kb = kc * bc
                eg = jnp.exp(gc)
                both = _dotb(jnp.concatenate([kb, qc], axis=0), ktc)
                units.append(dict(
                    b=b, h=h, c=c,
                    m=both[:chunk] * jnp.where(lower_strict, dec, 0.0),
                    qk=both[chunk:] * dec,
                    rhs=jnp.concatenate([vc * bc, kb * eg], axis=1),
                    qg=qc * eg,
                    kdt=ktc * jnp.exp(gr[:, chunk - 1:chunk] - gr),
                    d_last=jnp.exp(gc[chunk - 1:chunk, :]),
                ))

    tinvs = _unit_lower_inverses([u["m"] for u in units], eye, chunk)
    for u, tinv in zip(units, tinvs):
        sol = _dotb(tinv, u["rhs"])
        u["u"], u["w"] = sol[:, :HEAD_DIM], sol[:, HEAD_DIM:]

    state = {(b, h): s_ref[b, h] for b in range(nb) for h in range(GDN_HEADS)}
    by_key = {(u["b"], u["h"], u["c"]): u for u in units}
    for c in range(nchunk):
        rows = slice(c * chunk, (c + 1) * chunk)
        for b in range(nb):
            for h in range(GDN_HEADS):
                u = by_key[(b, h, c)]
                s = state[(b, h)]
                ws = _dotb(jnp.concatenate([u["w"], u["qg"]], axis=0), s)
                v_new = u["u"] - ws[:chunk]
                upd = _dotb(jnp.concatenate([u["qk"], u["kdt"]], axis=0), v_new)
                o_c = ws[chunk:] + upd[:chunk]
                state[(b, h)] = s * u["d_last"] + upd[chunk:]
                cols = slice(h * HEAD_DIM, (h + 1) * HEAD_DIM)
                on = o_c * lax.rsqrt(jnp.mean(o_c * o_c, axis=-1, keepdims=True) + RMS_EPS) * won
                zc = z_ref[b, rows, cols]
                o_ref[b, rows, cols] = on * (zc * _sigmoid(zc))
    for (b, h), s in state.items():
        s_ref[b, h] = s

    xbuf[:, pad - hist_rows:pad, :] = x_ref[:, tt - hist_rows:tt, :]


def _gdn(qkv, z, bd, bdt, hist, s0, wconv, wonorm, chunk, tt, nb):
    b, t, _ = qkv.shape
    nt = t // tt
    return pl.pallas_call(
        functools.partial(_gdn_kernel, chunk),
        grid=(b // nb, nt),
        in_specs=[
            pl.BlockSpec((nb, tt, QKV_COLS), lambda bi, i: (bi, i, 0)),
            pl.BlockSpec((nb, tt, GDN_WIDTH), lambda bi, i: (bi, i, 0)),
            pl.BlockSpec((nb, tt, 2 * GDN_HEADS), lambda bi, i: (bi, i, 0)),
            pl.BlockSpec((nb, 2 * GDN_HEADS, tt), lambda bi, i: (bi, 0, i)),
            pl.BlockSpec((nb, QKV_TAPS - 1, QKV_COLS), lambda bi, i: (bi, 0, 0)),
            pl.BlockSpec((nb, GDN_HEADS, HEAD_DIM, HEAD_DIM), lambda bi, i: (bi, 0, 0, 0)),
            pl.BlockSpec((QKV_TAPS, QKV_COLS), lambda bi, i: (0, 0)),
            pl.BlockSpec((1, HEAD_DIM), lambda bi, i: (0, 0)),
        ],
        out_specs=[
            pl.BlockSpec((nb, tt, GDN_WIDTH), lambda bi, i: (bi, i, 0)),
            pl.BlockSpec((nb, GDN_HEADS, HEAD_DIM, HEAD_DIM), lambda bi, i: (bi, 0, 0, 0)),
        ],
        out_shape=[
            jax.ShapeDtypeStruct((b, t, GDN_WIDTH), F32),
            jax.ShapeDtypeStruct((b, GDN_HEADS, HEAD_DIM, HEAD_DIM), F32),
        ],
        scratch_shapes=[pltpu.VMEM((nb, tt + SUBLANES, QKV_COLS), F32)],
        compiler_params=_params(("arbitrary", "arbitrary")),
        name="gdn",
    )(qkv, z, bd, bdt, hist, s0, wconv, wonorm)


def _cconv_kernel(carry, u_ref, hist_ref, w_ref, b_ref, lg_ref, lb_ref, c_ref, ubuf, shifted):
    i = pl.program_id(1)
    tt = u_ref.shape[0]
    hist_rows = DW_TAPS - 1
    pad = 4 * SUBLANES
    base = pad - hist_rows

    @pl.when(i == 0)
    def _():
        ubuf[base:pad, :] = hist_ref[...]

    ubuf[pad:pad + tt, :] = u_ref[...]
    span = shifted.shape[1]
    for s in range(1, SUBLANES):
        shifted[s - 1] = ubuf[s:s + span, :]

    def tap_rows(j, r0, rows):
        q, s = divmod(base + j, SUBLANES)
        start = q * SUBLANES + r0
        if s == 0:
            return ubuf[start:start + rows, :]
        return shifted[s - 1, start:start + rows, :]

    rows = min(CONV_ROWS, tt)
    for r0 in range(0, tt, rows):
        acc = w_ref[0:1, :] * tap_rows(0, r0, rows)
        for j in range(1, DW_TAPS):
            acc = acc + w_ref[j:j + 1, :] * tap_rows(j, r0, rows)
        cv = acc + b_ref[...]
        mu = jnp.mean(cv, axis=-1, keepdims=True)
        xc = cv - mu
        var = jnp.mean(xc * xc, axis=-1, keepdims=True)
        y = xc * lax.rsqrt(var + LN_EPS) * lg_ref[...] + lb_ref[...]
        c_ref[r0:r0 + rows, :] = y * _sigmoid(y)
    if carry:
        ubuf[base:pad, :] = u_ref[tt - hist_rows:tt, :]


def _cconv(uu, hist, w, b, lg, lb, tt):
    bsz, t, _ = uu.shape
    nt = t // tt
    vec = pl.BlockSpec((1, CONF_WIDTH), lambda bi, i: (0, 0))
    return pl.pallas_call(
        functools.partial(_cconv_kernel, nt > 1),
        grid=(bsz, nt),
        in_specs=[
            pl.BlockSpec((None, tt, CONF_WIDTH), lambda bi, i: (bi, i, 0)),
            pl.BlockSpec((None, DW_TAPS - 1, CONF_WIDTH), lambda bi, i: (bi, 0, 0)),
            pl.BlockSpec((DW_TAPS, CONF_WIDTH), lambda bi, i: (0, 0)),
            vec, vec, vec,
        ],
        out_specs=pl.BlockSpec((None, tt, CONF_WIDTH), lambda bi, i: (bi, i, 0)),
        out_shape=jax.ShapeDtypeStruct((bsz, t, CONF_WIDTH), F32),
        scratch_shapes=[pltpu.VMEM((tt + 4 * SUBLANES, CONF_WIDTH), F32),
                        pltpu.VMEM((SUBLANES - 1, tt + 3 * SUBLANES, CONF_WIDTH), F32)],
        compiler_params=_params(("arbitrary", "arbitrary")),
        name="cconv",
    )(uu, hist, w, b, lg, lb)


def _outproj_kernel(tiles_a, oa_ref, ca_ref, xa_ref, ob_ref, cb_ref, xb_ref, *rest):
    i = pl.program_id(0)

    @pl.when(i < tiles_a)
    def _():
        _outproj_tile(oa_ref, ca_ref, xa_ref, *rest)

    @pl.when(i >= tiles_a)
    def _():
        _outproj_tile(ob_ref, cb_ref, xb_ref, *rest)


def _outproj_tile(o_ref, c_ref, x_ref, wo_ref, wc_ref, g_ref, wr_ref, br_ref, upper_ref,
                  x1_ref, h2_ref, metar_ref, cnt_ref):
    tm = x_ref.shape[0]

    mix = _dotb(o_ref[...], wo_ref[...]) + _dotb(c_ref[...], wc_ref[...])
    x1 = x_ref[...] + mix
    x1_ref[...] = x1
    h2 = x1 * lax.rsqrt(jnp.mean(x1 * x1, axis=-1, keepdims=True) + RMS_EPS) * g_ref[...]
    h2_ref[...] = h2.astype(BF16)
    logits = lax.dot_general(wr_ref[...], h2.astype(BF16), (((1,), (1,)), ((), ())),
                             preferred_element_type=F32) + br_ref[...]
    row = lax.broadcasted_iota(I32, (N_EXPERTS, tm), 0).astype(F32)
    vals = logits
    idxs, tops = [], []
    for _ in range(TOP_K):
        m = jnp.max(vals, axis=0, keepdims=True)
        idx = jnp.min(jnp.where(vals == m, row, float(N_EXPERTS)), axis=0, keepdims=True)
        idxs.append(idx)
        tops.append(m)
        vals = jnp.where(row == idx, -jnp.inf, vals)
    exps = [jnp.exp(v - tops[0]) for v in tops]
    den = exps[0] + exps[1] + exps[2] + exps[3]
    onehots = [(row == idx).astype(F32) for idx in idxs]
    chosen = onehots[0] + onehots[1] + onehots[2] + onehots[3]
    before = jnp.dot(chosen.astype(BF16), upper_ref[...], preferred_element_type=F32)
    ranks = [jnp.sum(oh * before, axis=0, keepdims=True) for oh in onehots]
    gates = [e / den for e in exps]
    metar_ref[...] = jnp.concatenate(
        idxs + ranks + gates + [jnp.zeros((META_ROWS - 3 * TOP_K, tm), F32)], axis=0)
    cnt_ref[...] = jnp.broadcast_to(jnp.sum(chosen, axis=1, keepdims=True), (N_EXPERTS, LANES))[None]


def _outproj(first, second, wo, wc, g, wr, br):
    t_a, t_b = first[2].shape[0], second[2].shape[0]
    tm = TOKEN_TILE
    upper = (jnp.arange(tm)[:, None] < jnp.arange(tm)[None, :]).astype(BF16)
    tiles_a = t_a // tm
    t_total = t_a + t_b
    full = lambda shape: pl.BlockSpec(shape, lambda i: (0, 0))
    rows_a = lambda width: pl.BlockSpec((tm, width), lambda i: (jnp.minimum(i, tiles_a - 1), 0))
    rows_b = lambda width: pl.BlockSpec((tm, width), lambda i: (jnp.maximum(i - tiles_a, 0), 0))
    widths = (GDN_WIDTH, CONF_WIDTH, D_MODEL)
    return pl.pallas_call(
        functools.partial(_outproj_kernel, tiles_a),
        grid=(t_total // tm,),
        in_specs=[rows_a(w) for w in widths] + [rows_b(w) for w in widths] + [
            full((GDN_WIDTH, D_MODEL)),
            full((CONF_WIDTH, D_MODEL)),
            full((1, D_MODEL)),
            full((N_EXPERTS, D_MODEL)),
            full((N_EXPERTS, 1)),
            full((tm, tm)),
        ],
        out_specs=[
            pl.BlockSpec((tm, D_MODEL), lambda i: (i, 0)),
            pl.BlockSpec((tm, D_MODEL), lambda i: (i, 0)),
            pl.BlockSpec((META_ROWS, tm), lambda i: (0, i)),
            pl.BlockSpec((1, N_EXPERTS, LANES), lambda i: (i, 0, 0)),
        ],
        out_shape=[
            jax.ShapeDtypeStruct((t_total, D_MODEL), F32),
            jax.ShapeDtypeStruct((t_total, D_MODEL), BF16),
            jax.ShapeDtypeStruct((META_ROWS, t_total), F32),
            jax.ShapeDtypeStruct((t_total // tm, N_EXPERTS, LANES), F32),
        ],
        compiler_params=_params(("arbitrary",)),
        name="outproj_router",
    )(*first, *second, wo, wc, g, wr, br, upper)


def _segment_copies(i, seg_local_ref, seg_dst_ref, seg_n_ref, make_copy, wait):
    def per_expert(ex, carry):
        j = i * N_EXPERTS + ex
        local, dst, n = seg_local_ref[j], seg_dst_ref[j], seg_n_ref[j]
        done = 0
        for per_copy, count in ((8, n >> 3), (2, (n >> 1) & 3), (1, n & 1)):
            def body(g, c, per_copy=per_copy, done=done):
                off = (done + g * per_copy) * GRANULE
                cp = make_copy(per_copy * GRANULE, pl.multiple_of(local + off, GRANULE),
                               pl.multiple_of(dst + off, GRANULE))
                cp.wait() if wait else cp.start()
                return c

            lax.fori_loop(0, count, body, 0)
            done = done + count * per_copy
        return carry

    lax.fori_loop(0, N_EXPERTS, per_expert, 0)


def _gap_copies(gap_start_ref, gap_n_ref, make_copy, wait):
    def per_expert(ex, carry):
        n = gap_n_ref[ex]
        start = pl.multiple_of(gap_start_ref[ex], GRANULE)
        off = 0 * n
        for bit in reversed(range(GAP_BITS)):
            rows = GRANULE << bit

            @pl.when(((n >> bit) & 1) == 1)
            def _(rows=rows, off=off):
                cp = make_copy(rows, pl.multiple_of(start + off, GRANULE))
                cp.wait() if wait else cp.start()

            off = off + ((n >> bit) & 1) * rows
        return carry

    lax.fori_loop(0, N_EXPERTS, per_expert, 0)


def _local_rows(meta_e, seg_local_ref, i):
    base = jnp.zeros(meta_e.shape, F32)
    for ex in range(N_EXPERTS):
        base = jnp.where(meta_e == float(ex), seg_local_ref[i * N_EXPERTS + ex].astype(F32), base)
    return base


def _dispatch_kernel(seg_local_ref, seg_dst_ref, seg_n_ref, gap_start_ref, gap_n_ref, tail_ref,
                     h_ref, mr_ref, xs_ref, slots_ref, xloc, zbuf, sems, zsem):
    i = pl.program_id(0)
    last = pl.num_programs(0) - 1
    tm = h_ref.shape[0]
    rt = xloc.shape[1]
    zrows = zbuf.shape[0]
    slot = i % 2

    def gap_copy(rows, dst):
        return pltpu.make_async_copy(zbuf.at[pl.ds(0, rows)], xs_ref.at[pl.ds(dst, rows)], zsem)

    def tail_copies(wait):
        def body(j, c):
            cp = gap_copy(zrows, pl.multiple_of(tail_ref[0] + j * zrows, zrows))
            cp.wait() if wait else cp.start()
            return c
        lax.fori_loop(0, tail_ref[1], body, 0)

    @pl.when(i == 0)
    def _():
        zbuf[...] = jnp.zeros(zbuf.shape, zbuf.dtype)
        _gap_copies(gap_start_ref, gap_n_ref, gap_copy, wait=False)
        tail_copies(wait=False)

    mr = mr_ref[...]
    dl = (_local_rows(mr[META_EXPERT:META_EXPERT + TOP_K], seg_local_ref, i)
          + mr[META_RANK:META_RANK + TOP_K])
    slots_ref[...] = jnp.concatenate(
        [dl, mr[META_GATE:META_GATE + TOP_K], jnp.zeros((LANES - 2 * TOP_K, tm), F32)], axis=0).T
    rr = lax.broadcasted_iota(I32, (rt, tm), 0).astype(F32)
    hit = rr == dl[0:1]
    for k in range(1, TOP_K):
        hit = hit | (rr == dl[k:k + 1])
    xloc[slot] = jnp.dot(hit.astype(BF16), h_ref[...], preferred_element_type=F32)

    def seg_copy(buf):
        def make(rows, local, dst):
            return pltpu.make_async_copy(xloc.at[buf, pl.ds(local, rows)],
                                         xs_ref.at[pl.ds(dst, rows)], sems.at[buf])
        return make

    _segment_copies(i, seg_local_ref, seg_dst_ref, seg_n_ref, seg_copy(slot), wait=False)

    @pl.when(i > 0)
    def _():
        _segment_copies(i - 1, seg_local_ref, seg_dst_ref, seg_n_ref, seg_copy(1 - slot), wait=True)

    @pl.when(i == last)
    def _():
        _segment_copies(i, seg_local_ref, seg_dst_ref, seg_n_ref, seg_copy(slot), wait=True)

    @pl.when(i == 0)
    def _():
        _gap_copies(gap_start_ref, gap_n_ref, gap_copy, wait=True)
        tail_copies(wait=True)


def _dispatch(tables, h2, metar, n_rows):
    t = h2.shape[0]
    tm = TOKEN_TILE
    return pl.pallas_call(
        _dispatch_kernel,
        grid_spec=pltpu.PrefetchScalarGridSpec(
            num_scalar_prefetch=6,
            grid=(t // tm,),
            in_specs=[
                pl.BlockSpec((tm, D_MODEL), lambda i, *_: (i, 0)),
                pl.BlockSpec((META_ROWS, tm), lambda i, *_: (0, i)),
            ],
            out_specs=[pl.BlockSpec(memory_space=pl.ANY),
                       pl.BlockSpec((tm, LANES), lambda i, *_: (i, 0))],
            scratch_shapes=[
                pltpu.VMEM((2, TILE_ROWS, D_MODEL), F32),
                pltpu.VMEM((GRANULE << (GAP_BITS - 1), D_MODEL), F32),
                pltpu.SemaphoreType.DMA((2,)),
                pltpu.SemaphoreType.DMA,
            ],
        ),
        out_shape=[jax.ShapeDtypeStruct((n_rows, D_MODEL), F32),
                   jax.ShapeDtypeStruct((t, LANES), F32)],
        compiler_params=_params(("arbitrary",)),
        name="dispatch",
    )(*tables, h2, metar)


def _expert_kernel(be_ref, rb_ref, nu_ref, first_ref, slot_ref, next_ref,
                   x_ref, wgu_hbm, bgu_ref, wd_hbm, bdn_ref, y_ref,
                   wgu_f32, wd_f32, wgu_bf, wd_bf, sems):
    i = pl.program_id(0)

    def fetch(expert, slot):
        return (pltpu.make_async_copy(wgu_hbm.at[expert], wgu_f32.at[slot], sems.at[slot, 0]),
                pltpu.make_async_copy(wd_hbm.at[expert], wd_f32.at[slot], sems.at[slot, 1]))

    @pl.when(i == 0)
    def _():
        for cp in fetch(be_ref[0], 0):
            cp.start()

    @pl.when(first_ref[i] == 1)
    def _():
        slot = slot_ref[i]
        for cp in fetch(be_ref[i], slot):
            cp.wait()
        wgu_bf[...] = wgu_f32[slot].astype(BF16)
        wd_bf[...] = wd_f32[slot].astype(BF16)

        @pl.when(next_ref[i] >= 0)
        def _():
            for cp in fetch(next_ref[i], 1 - slot):
                cp.start()

    @pl.when(i < nu_ref[0])
    def _():
        gu = _dotb(x_ref[...], wgu_bf[...]) + bgu_ref[...]
        x_glu = jnp.minimum(gu[:, :D_EXPERT], SWIGLU_LIMIT)
        x_lin = jnp.clip(gu[:, D_EXPERT:], -SWIGLU_LIMIT, SWIGLU_LIMIT)
        act = x_glu * _sigmoid(SWIGLU_ALPHA * x_glu) * (x_lin + 1.0)
        y_ref[...] = _dotb(act, wd_bf[...]) + bdn_ref[...]

    @pl.when(i >= nu_ref[0])
    def _():
        y_ref[...] = jnp.zeros(y_ref.shape, y_ref.dtype)


def _experts(tables, xs, wgu, bgu, wd, bdn):
    rows = xs.shape[0]
    bm = EXPERT_ROWS
    nb = rows // bm
    return pl.pallas_call(
        _expert_kernel,
        grid_spec=pltpu.PrefetchScalarGridSpec(
            num_scalar_prefetch=6,
            grid=(nb,),
            in_specs=[
                pl.BlockSpec((bm, D_MODEL), lambda i, be, rb, *_: (rb[i], 0)),
                pl.BlockSpec(memory_space=pl.ANY),
                pl.BlockSpec((None, 1, 2 * D_EXPERT), lambda i, be, *_: (be[i], 0, 0)),
                pl.BlockSpec(memory_space=pl.ANY),
                pl.BlockSpec((None, 1, D_MODEL), lambda i, be, *_: (be[i], 0, 0)),
            ],
            out_specs=pl.BlockSpec((bm, D_MODEL), lambda i, *_: (i, 0)),
            scratch_shapes=[
                pltpu.VMEM((2, D_MODEL, 2 * D_EXPERT), F32),
                pltpu.VMEM((2, D_EXPERT, D_MODEL), F32),
                pltpu.VMEM((D_MODEL, 2 * D_EXPERT), BF16),
                pltpu.VMEM((D_EXPERT, D_MODEL), BF16),
                pltpu.SemaphoreType.DMA((2, 2)),
            ],
        ),
        out_shape=jax.ShapeDtypeStruct((rows, D_MODEL), F32),
        compiler_params=pltpu.CompilerParams(dimension_semantics=("arbitrary",),
                                             vmem_limit_bytes=EXPERT_VMEM_LIMIT),
        name="experts",
    )(*tables, xs, wgu, bgu, wd, bdn)


def _combine_kernel(tiles_a, seg_local_ref, seg_dst_ref, seg_n_ref, x1_ref, mc_ref, yb_ref, gf_ref,
                    ya_ref, yb_out_ref, ybuf, wsel, sems):
    i = pl.program_id(0)
    n_tiles = pl.num_programs(0)
    tm = x1_ref.shape[0]
    rt = ybuf.shape[1]
    slot = i % 2

    def seg_copy(buf):
        def make(rows, local, src):
            return pltpu.make_async_copy(yb_ref.at[pl.ds(src, rows)],
                                         ybuf.at[buf, pl.ds(local, rows)], sems.at[buf])
        return make

    @pl.when(i == 0)
    def _():
        ybuf[...] = jnp.zeros(ybuf.shape, ybuf.dtype)
        _segment_copies(i, seg_local_ref, seg_dst_ref, seg_n_ref, seg_copy(slot), wait=False)

    @pl.when(i + 1 < n_tiles)
    def _():
        _segment_copies(i + 1, seg_local_ref, seg_dst_ref, seg_n_ref, seg_copy(1 - slot), wait=False)

    dl = mc_ref[:, 0:TOP_K]
    gate = mc_ref[:, TOP_K:2 * TOP_K]
    rb, cb = SEL_CHUNK
    col = lax.broadcasted_iota(I32, (rb, cb), 1).astype(F32)
    for r0 in range(0, tm, rb):
        rows_k = [jnp.broadcast_to(dl[r0:r0 + rb, k:k + 1], (rb, cb)) for k in range(TOP_K)]
        gate_k = [jnp.broadcast_to(gate[r0:r0 + rb, k:k + 1], (rb, cb)) for k in range(TOP_K)]
        for c0 in range(0, rt, cb):
            rr = col + float(c0)
            w = jnp.where(rr == rows_k[0], gate_k[0], 0.0)
            for k in range(1, TOP_K):
                w = jnp.where(rr == rows_k[k], gate_k[k], w)
            wsel[r0:r0 + rb, c0:c0 + cb] = w.astype(BF16)
    _segment_copies(i, seg_local_ref, seg_dst_ref, seg_n_ref, seg_copy(slot), wait=True)
    acc = x1_ref[...] + jnp.dot(wsel[...], ybuf[slot].astype(BF16), preferred_element_type=F32)
    y = acc * lax.rsqrt(jnp.mean(acc * acc, axis=-1, keepdims=True) + RMS_EPS) * gf_ref[...]

    @pl.when(i < tiles_a)
    def _():
        ya_ref[...] = y

    @pl.when(i >= tiles_a)
    def _():
        yb_out_ref[...] = y


def _combine(tables, x1, slots, yb, gfin, n_a):
    t = x1.shape[0]
    tm = TOKEN_TILE
    tiles_a = n_a // tm
    return pl.pallas_call(
        functools.partial(_combine_kernel, tiles_a),
        grid_spec=pltpu.PrefetchScalarGridSpec(
            num_scalar_prefetch=3,
            grid=(t // tm,),
            in_specs=[
                pl.BlockSpec((tm, D_MODEL), lambda i, *_: (i, 0)),
                pl.BlockSpec((tm, LANES), lambda i, *_: (i, 0)),
                pl.BlockSpec(memory_space=pl.ANY),
                pl.BlockSpec((1, D_MODEL), lambda i, *_: (0, 0)),
            ],
            out_specs=[
                pl.BlockSpec((tm, D_MODEL), lambda i, *_: (jnp.minimum(i, tiles_a - 1), 0)),
                pl.BlockSpec((tm, D_MODEL), lambda i, *_: (jnp.maximum(i - tiles_a, 0), 0)),
            ],
            scratch_shapes=[pltpu.VMEM((2, TILE_ROWS, D_MODEL), F32),
                            pltpu.VMEM((tm, TILE_ROWS), BF16),
                            pltpu.SemaphoreType.DMA((2,))],
        ),
        out_shape=[jax.ShapeDtypeStruct((n_a, D_MODEL), F32),
                   jax.ShapeDtypeStruct((t - n_a, D_MODEL), F32)],
        compiler_params=pltpu.CompilerParams(dimension_semantics=("arbitrary",),
                                             vmem_limit_bytes=EXPERT_VMEM_LIMIT),
        name="combine",
    )(*tables, x1, slots, yb, gfin)


def kernel(x_prompt, x_sample, state_gdn, state_qkv_conv, state_dwconv, norm_mix_g, w_in, w_conv_qkv,
           a_log, dt_bias, w_onorm, b_glu, w_dw, b_dw, ln_g, ln_b, w_out, norm_ffn_g, w_router,
           b_router, w_gate_up, b_gate_up, w_down, b_down, norm_final_g):
    bp, tp, _ = x_prompt.shape
    bs, ts, _ = x_sample.shape
    n_p, n_s = bp * tp, bs * ts

    wi = w_in[0]
    c_z = QKV_COLS
    c_bd = c_z + GDN_WIDTH
    c_glu = c_bd + 2 * GDN_HEADS
    wqkv = wi[:, :c_z].astype(BF16)
    wz = wi[:, c_z:c_bd].astype(BF16)
    wbd = jnp.pad(wi[:, c_bd:c_glu], ((0, 0), (0, LANES - 2 * GDN_HEADS))).astype(BF16)
    wglu = wi[:, c_glu:].astype(BF16)
    bglu = b_glu[0][None, :]
    lane_pad = (GDN_HEADS, LANES - 2 * GDN_HEADS)
    pa = jnp.pad(-jnp.exp(a_log[0].astype(F32)), lane_pad)[None, :]
    pb = jnp.pad(dt_bias[0].astype(F32), lane_pad)[None, :]
    gmix = norm_mix_g[0][None, :]
    wconv = w_conv_qkv[0]
    wonorm = w_onorm[0][None, :]
    wdw, bdw = w_dw[0], b_dw[0][None, :]
    lng, lnb = ln_g[0][None, :], ln_b[0][None, :]
    wo = w_out[0][:GDN_WIDTH].astype(BF16)
    wc = w_out[0][GDN_WIDTH:].astype(BF16)
    gffn = norm_ffn_g[0][None, :]
    wr, br = w_router[0].T.astype(BF16), b_router[0].astype(F32)[:, None]
    bgu = b_gate_up[0][:, None, :]
    bdn = b_down[0][:, None, :]
    gfin = norm_final_g[None, :]

    xp = x_prompt.reshape(n_p, D_MODEL)
    xs_tok = x_sample.reshape(n_s, D_MODEL)

    def mixers(x2d, bsz, t, hist_qkv, s0, hist_glu, chunk, gdn_tile, gdn_rows, conv_tile):
        qkv, z, bd, bdt, uu = _inproj(x2d, gmix, wqkv, wz, wbd, wglu, bglu, pa, pb, chunk, t)
        qkv3 = qkv.reshape(bsz, t, QKV_COLS)
        uu3 = uu.reshape(bsz, t, CONF_WIDTH)
        if bdt.ndim == 2:
            bdt = bdt.reshape(2 * GDN_HEADS, bsz, t).transpose(1, 0, 2)
        o, s_new = _gdn(qkv3, z.reshape(bsz, t, GDN_WIDTH), bd.reshape(bsz, t, 2 * GDN_HEADS),
                        bdt, hist_qkv, s0, wconv, wonorm, chunk, gdn_tile, gdn_rows)
        cc = _cconv(uu3, hist_glu, wdw, bdw, lng, lnb, conv_tile)
        return o.reshape(bsz * t, GDN_WIDTH), cc.reshape(bsz * t, CONF_WIDTH), s_new, qkv3, uu3

    zero_qkv = jnp.zeros((bp, QKV_TAPS - 1, QKV_COLS), F32)
    zero_s = jnp.zeros((bp, GDN_HEADS, HEAD_DIM, HEAD_DIM), F32)
    zero_glu = jnp.zeros((bp, DW_TAPS - 1, CONF_WIDTH), F32)
    o_p, c_p, s_p, qkv_p, uu_p = mixers(xp, bp, tp, zero_qkv, zero_s, zero_glu,
                                        PROMPT_CHUNK, GDN_TILE, 1, CONV_TILE)
    o_s, c_s, s_s, qkv_s, uu_s = mixers(xs_tok, bs, ts, state_qkv_conv[0], state_gdn[0],
                                        state_dwconv[0], ts, ts, GDN_SAMPLE_ROWS, ts)

    n_tok = n_p + n_s
    x1, h2, metar, cnt_tile = _outproj((o_p, c_p, xp), (o_s, c_s, xs_tok), wo, wc, gffn, wr, br)

    bm = EXPERT_ROWS
    n_tiles = n_tok // TOKEN_TILE
    counts = cnt_tile[:, :, 0].astype(I32)
    seg_rows = (counts + GRANULE - 1) // GRANULE * GRANULE
    ids = jnp.arange(N_EXPERTS, dtype=I32)
    tile_ids = jnp.arange(n_tiles, dtype=I32)
    earlier_e = (ids[:, None] < ids[None, :]).astype(I32)
    earlier_t = (tile_ids[None, :] < tile_ids[:, None]).astype(I32)
    seg_local = jnp.sum(seg_rows[:, :, None] * earlier_e[None], axis=1)
    seg_before = jnp.sum(earlier_t[:, :, None] * seg_rows[None], axis=1)
    rows_e = jnp.sum(seg_rows, axis=0)
    padded = (rows_e + bm - 1) // bm * bm
    pstart = jnp.sum(padded[:, None] * earlier_e, axis=0)
    pend = pstart + padded
    seg_dst = pstart[None, :] + seg_before
    n_used = jnp.maximum(pend[-1] // bm, 1).astype(I32)
    max_rows = n_tok * TOP_K + n_tiles * N_EXPERTS * (GRANULE - 1) + N_EXPERTS * (bm - 1)
    n_blocks = -(-max_rows // bm)
    blk = jnp.minimum(jnp.arange(n_blocks, dtype=I32), n_used - 1)
    block_expert = jnp.minimum(
        jnp.sum((pend[None, :] <= (blk * bm)[:, None]).astype(I32), axis=1), N_EXPERTS - 1)
    seg_tables = (seg_local.reshape(-1).astype(I32), seg_dst.reshape(-1).astype(I32),
                  (seg_rows // GRANULE).reshape(-1).astype(I32))
    gap_tables = ((pstart + rows_e).astype(I32), ((padded - rows_e) // GRANULE).astype(I32))
    zero_rows = GRANULE << (GAP_BITS - 1)
    tail_table = jnp.stack([pend[-1], (n_blocks * bm - pend[-1]) // zero_rows]).astype(I32)

    xs, slots = _dispatch(seg_tables + gap_tables + (tail_table,), h2, metar, n_blocks * bm)
    active = padded > 0
    later = active[None, :] & (ids[None, :] > ids[:, None])
    next_active = jnp.min(jnp.where(later, ids[None, :], N_EXPERTS), axis=1)
    next_active = jnp.where(next_active == N_EXPERTS, -1, next_active)
    order = jnp.sum(active.astype(I32)[:, None] * earlier_e, axis=0)
    steps = jnp.arange(n_blocks, dtype=I32)
    of_block = (block_expert[:, None] == ids[None, :]).astype(I32)
    per_block = lambda table: jnp.sum(of_block * table[None, :], axis=1)
    first = ((steps * bm == per_block(pstart)) & (steps < n_used)).astype(I32)
    expert_tables = (block_expert, blk, n_used.reshape(1), first,
                     (per_block(order) % 2).astype(I32), per_block(next_active).astype(I32))
    yb = _experts(expert_tables, xs, w_gate_up[0], bgu, w_down[0], bdn)
    y_p, y_s = _combine(seg_tables, x1, slots, yb, gfin, n_p)

    hist_glu_s = jnp.concatenate([state_dwconv[0], uu_s], axis=1)[:, -(DW_TAPS - 1):]
    return (y_p.reshape(bp, tp, D_MODEL),
            y_s.reshape(bs, ts, D_MODEL),
            s_p[None],
            qkv_p[:, -(QKV_TAPS - 1):][None],
            uu_p[:, -(DW_TAPS - 1):][None],
            s_s[None],
            qkv_s[:, -(QKV_TAPS - 1):][None],
            hist_glu_s[None])
```

```python
import functools
import math

import jax
import jax.numpy as jnp
from jax import lax
from jax.experimental import pallas as pl
from jax.experimental.pallas import tpu as pltpu

F32 = jnp.float32
BF16 = jnp.bfloat16
I32 = jnp.int32
HIGHEST = lax.Precision.HIGHEST

D_MODEL = 1024
GDN_HEADS = 4
HEAD_DIM = 128
GDN_WIDTH = GDN_HEADS * HEAD_DIM
QKV_COLS = 3 * GDN_WIDTH
CONF_WIDTH = 512
QKV_TAPS = 4
DW_TAPS = 31
N_EXPERTS = 32
TOP_K = 4
D_EXPERT = 1024
SWIGLU_LIMIT = 7.0
SWIGLU_ALPHA = 1.702
RMS_EPS = 1e-6
LN_EPS = 1e-5
L2_EPS = 1e-6
PROMPT_CHUNK = 64

LANES = 128
SUBLANES = 8
TOKEN_TILE = 512
GDN_TILE = 512
GDN_SAMPLE_ROWS = 4
CONV_TILE = 512
CONV_ROWS = 32
EXPERT_ROWS = 512
GRANULE = SUBLANES
GAP_BITS = 6
TILE_ROWS = -(-(TOKEN_TILE * TOP_K + N_EXPERTS * (GRANULE - 1)) // 256) * 256
META_EXPERT, META_RANK, META_GATE, META_ROWS = 0, TOP_K, 2 * TOP_K, 16
assert EXPERT_ROWS // GRANULE <= 1 << GAP_BITS
SEL_CHUNK = (64, 256)
VMEM_LIMIT = 48 * 1024 * 1024
EXPERT_VMEM_LIMIT = 56 * 1024 * 1024


def _dotb(a, b):
    return jnp.dot(a.astype(BF16), b.astype(BF16), preferred_element_type=F32)


def _dotf(a, b):
    return jnp.dot(a, b, precision=HIGHEST, preferred_element_type=F32)


def _sigmoid(x):
    return jax.nn.sigmoid(x)


def _params(sem):
    return pltpu.CompilerParams(dimension_semantics=sem, vmem_limit_bytes=VMEM_LIMIT)


def _inproj_kernel(x_ref, g_ref, wqkv_ref, wz_ref, wbd_ref, wglu_ref, bglu_ref, pa_ref, pb_ref,
                   tri_ref, qkv_ref, z_ref, bd_ref, bdt_ref, uu_ref):
    x = x_ref[...]
    tm = x.shape[0]
    ms = jnp.mean(x * x, axis=-1, keepdims=True)
    h = (x * lax.rsqrt(ms + RMS_EPS) * g_ref[...]).astype(BF16)
    qkv_ref[...] = jnp.dot(h, wqkv_ref[...], preferred_element_type=F32)
    z_ref[...] = jnp.dot(h, wz_ref[...], preferred_element_type=F32)
    glu = jnp.dot(h, wglu_ref[...], preferred_element_type=F32) + bglu_ref[...]
    uu_ref[...] = glu[:, :CONF_WIDTH] * _sigmoid(glu[:, CONF_WIDTH:])
    bd = jnp.dot(h, wbd_ref[...], preferred_element_type=F32)
    beta = _sigmoid(bd)
    v = bd + pb_ref[...]
    softplus = jnp.maximum(v, 0.0) + jnp.log1p(jnp.exp(-jnp.abs(v)))
    g = pa_ref[...] * softplus
    g_hi, g_rest = _split_bf16(g)
    g_mid = g - g_hi.astype(F32) - g_rest.astype(F32)
    pieces = jnp.concatenate([g_hi, g_rest, g_mid.astype(BF16)], axis=1)
    sums = jnp.dot(tri_ref[...], pieces, preferred_element_type=F32)
    gc = sums[:, :LANES] + sums[:, LANES:2 * LANES] + sums[:, 2 * LANES:]
    lane = lax.broadcasted_iota(I32, (tm, LANES), 1)
    res = jnp.where(lane < GDN_HEADS, beta, gc)
    bd_ref[...] = res[:, :2 * GDN_HEADS]
    bdt_ref[...] = res.T[:2 * GDN_HEADS, :]


def _inproj(x, g, wqkv, wz, wbd, wglu, bglu, pa, pb, chunk, t_batch):
    t = x.shape[0]
    tm = TOKEN_TILE
    tiles_per_batch = t_batch // tm if t_batch % tm == 0 else 0
    pos = jnp.arange(tm)
    tri = ((pos[:, None] >= pos[None, :])
           & (pos[:, None] // chunk == pos[None, :] // chunk)).astype(BF16)
    full = lambda shape: pl.BlockSpec(shape, lambda i: (0, 0))
    if tiles_per_batch:
        bdt_spec = pl.BlockSpec((None, 2 * GDN_HEADS, tm),
                                lambda i: (i // tiles_per_batch, 0, i % tiles_per_batch))
        bdt_shape = jax.ShapeDtypeStruct((t // t_batch, 2 * GDN_HEADS, t_batch), F32)
    else:
        bdt_spec = pl.BlockSpec((2 * GDN_HEADS, tm), lambda i: (0, i))
        bdt_shape = jax.ShapeDtypeStruct((2 * GDN_HEADS, t), F32)
    return pl.pallas_call(
        _inproj_kernel,
        grid=(t // tm,),
        in_specs=[
            pl.BlockSpec((tm, D_MODEL), lambda i: (i, 0)),
            full((1, D_MODEL)),
            full((D_MODEL, QKV_COLS)),
            full((D_MODEL, GDN_WIDTH)),
            full((D_MODEL, LANES)),
            full((D_MODEL, 2 * CONF_WIDTH)),
            full((1, 2 * CONF_WIDTH)),
            full((1, LANES)),
            full((1, LANES)),
            full((tm, tm)),
        ],
        out_specs=[
            pl.BlockSpec((tm, QKV_COLS), lambda i: (i, 0)),
            pl.BlockSpec((tm, GDN_WIDTH), lambda i: (i, 0)),
            pl.BlockSpec((tm, 2 * GDN_HEADS), lambda i: (i, 0)),
            bdt_spec,
            pl.BlockSpec((tm, CONF_WIDTH), lambda i: (i, 0)),
        ],
        out_shape=[
            jax.ShapeDtypeStruct((t, QKV_COLS), F32),
            jax.ShapeDtypeStruct((t, GDN_WIDTH), F32),
            jax.ShapeDtypeStruct((t, 2 * GDN_HEADS), F32),
            bdt_shape,
            jax.ShapeDtypeStruct((t, CONF_WIDTH), F32),
        ],
        compiler_params=_params(("arbitrary",)),
        name="inproj",
    )(x, g, wqkv, wz, wbd, wglu, bglu, pa, pb, tri)


def _split_bf16(a):
    hi = a.astype(BF16)
    lo = (a - hi.astype(F32)).astype(BF16)
    return hi, lo


def _unit_lower_inverses(ms, eye, n):
    levels = int(math.log2(n)) - 1
    ps = [eye - m for m in ms]
    pw = [(-m).astype(BF16) for m in ms]
    pw = [jnp.dot(p, p, preferred_element_type=F32).astype(BF16) for p in pw]
    for level in range(1, levels + 1):
        last = level == levels
        nxt = []
        for u, p in enumerate(pw):
            if last:
                ps[u] = ps[u] + jnp.dot(ps[u].astype(BF16), p, preferred_element_type=F32)
            else:
                out = jnp.dot(jnp.concatenate([p, ps[u].astype(BF16)], axis=0), p,
                              preferred_element_type=F32)
                nxt.append(out[:n].astype(BF16))
                ps[u] = ps[u] + out[n:]
        pw = nxt
    return ps


def _gdn_kernel(chunk, x_ref, z_ref, bd_ref, bdt_ref, hist_ref, s0_ref, wc_ref, won_ref,
                o_ref, s_ref, xbuf):
    i = pl.program_id(1)
    nb, tt = x_ref.shape[0], x_ref.shape[1]
    nchunk = tt // chunk
    pad = SUBLANES
    hist_rows = QKV_TAPS - 1

    @pl.when(i == 0)
    def _():
        xbuf[:, pad - hist_rows:pad, :] = hist_ref[...]
        s_ref[...] = s0_ref[...]

    xbuf[:, pad:pad + tt, :] = x_ref[...]

    ii = lax.broadcasted_iota(I32, (chunk, chunk), 0)
    jj = lax.broadcasted_iota(I32, (chunk, chunk), 1)
    lower_incl = ii >= jj
    lower_strict = ii > jj
    eye = (ii == jj).astype(F32)
    won = won_ref[...]

    def conv_silu(b, col0):
        cols = slice(col0, col0 + HEAD_DIM)
        acc = wc_ref[hist_rows:hist_rows + 1, cols] * xbuf[b, pad:pad + tt, cols]
        for j in range(hist_rows):
            r0 = pad - hist_rows + j
            acc = acc + wc_ref[j:j + 1, cols] * xbuf[b, r0:r0 + tt, cols]
        return acc * _sigmoid(acc)

    units = []
    for b in range(nb):
        bd = bd_ref[b]
        bdt = bdt_ref[b]
        for h in range(GDN_HEADS):
            q = conv_silu(b, h * HEAD_DIM)
            k = conv_silu(b, GDN_WIDTH + h * HEAD_DIM)
            v = conv_silu(b, 2 * GDN_WIDTH + h * HEAD_DIM)
            q = q * lax.rsqrt(jnp.sum(q * q, axis=-1, keepdims=True) + L2_EPS) * (HEAD_DIM ** -0.5)
            k = k * lax.rsqrt(jnp.sum(k * k, axis=-1, keepdims=True) + L2_EPS)
            kt = k.T
            beta = bd[:, h:h + 1]
            gcol = bd[:, GDN_HEADS + h:GDN_HEADS + h + 1]
            grow = bdt[GDN_HEADS + h:GDN_HEADS + h + 1, :]
            for c in range(nchunk):
                rows = slice(c * chunk, (c + 1) * chunk)
                qc, kc, vc, ktc = q[rows], k[rows], v[rows], kt[:, rows]
                bc, gc, gr = beta[rows], gcol[rows], grow[:, rows]
                dec = jnp.exp(jnp.where(lower_incl, gc - gr, -jnp.inf))
                kb = kc * bc
                eg = jnp.exp(gc)
                both = _dotb(jnp.concatenate([kb, qc], axis=0), ktc)
                units.append(dict(
                    b=b, h=h, c=c,
                    m=both[:chunk] * jnp.where(lower_strict, dec, 0.0),
                    qk=both[chunk:] * dec,
                    rhs=jnp.concatenate([vc * bc, kb * eg], axis=1),
                    qg=qc * eg,
                    kdt=ktc * jnp.exp(gr[:, chunk - 1:chunk] - gr),
                    d_last=jnp.exp(gc[chunk - 1:chunk, :]),
                ))

    tinvs = _unit_lower_inverses([u["m"] for u in units], eye, chunk)
    for u, tinv in zip(units, tinvs):
        sol = _dotb(tinv, u["rhs"])
        u["u"], u["w"] = sol[:, :HEAD_DIM], sol[:, HEAD_DIM:]

    state = {(b, h): s_ref[b, h] for b in range(nb) for h in range(GDN_HEADS)}
    by_key = {(u["b"], u["h"], u["c"]): u for u in units}
    for c in range(nchunk):
        rows = slice(c * chunk, (c + 1) * chunk)
        for b in range(nb):
            for h in range(GDN_HEADS):
                u = by_key[(b, h, c)]
                s = state[(b, h)]
                ws = _dotb(jnp.concatenate([u["w"], u["qg"]], axis=0), s)
                v_new = u["u"] - ws[:chunk]
                upd = _dotb(jnp.concatenate([u["qk"], u["kdt"]], axis=0), v_new)
                o_c = ws[chunk:] + upd[:chunk]
                state[(b, h)] = s * u["d_last"] + upd[chunk:]
                cols = slice(h * HEAD_DIM, (h + 1) * HEAD_DIM)
                on = o_c * lax.rsqrt(jnp.mean(o_c * o_c, axis=-1, keepdims=True) + RMS_EPS) * won
                zc = z_ref[b, rows, cols]
                o_ref[b, rows, cols] = on * (zc * _sigmoid(zc))
    for (b, h), s in state.items():
        s_ref[b, h] = s

    xbuf[:, pad - hist_rows:pad, :] = x_ref[:, tt - hist_rows:tt, :]


def _gdn(qkv, z, bd, bdt, hist, s0, wconv, wonorm, chunk, tt, nb):
    b, t, _ = qkv.shape
    nt = t // tt
    return pl.pallas_call(
        functools.partial(_gdn_kernel, chunk),
        grid=(b // nb, nt),
        in_specs=[
            pl.BlockSpec((nb, tt, QKV_COLS), lambda bi, i: (bi, i, 0)),
            pl.BlockSpec((nb, tt, GDN_WIDTH), lambda bi, i: (bi, i, 0)),
            pl.BlockSpec((nb, tt, 2 * GDN_HEADS), lambda bi, i: (bi, i, 0)),
            pl.BlockSpec((nb, 2 * GDN_HEADS, tt), lambda bi, i: (bi, 0, i)),
            pl.BlockSpec((nb, QKV_TAPS - 1, QKV_COLS), lambda bi, i: (bi, 0, 0)),
            pl.BlockSpec((nb, GDN_HEADS, HEAD_DIM, HEAD_DIM), lambda bi, i: (bi, 0, 0, 0)),
            pl.BlockSpec((QKV_TAPS, QKV_COLS), lambda bi, i: (0, 0)),
            pl.BlockSpec((1, HEAD_DIM), lambda bi, i: (0, 0)),
        ],
        out_specs=[
            pl.BlockSpec((nb, tt, GDN_WIDTH), lambda bi, i: (bi, i, 0)),
            pl.BlockSpec((nb, GDN_HEADS, HEAD_DIM, HEAD_DIM), lambda bi, i: (bi, 0, 0, 0)),
        ],
        out_shape=[
            jax.ShapeDtypeStruct((b, t, GDN_WIDTH), F32),
            jax.ShapeDtypeStruct((b, GDN_HEADS, HEAD_DIM, HEAD_DIM), F32),
        ],
        scratch_shapes=[pltpu.VMEM((nb, tt + SUBLANES, QKV_COLS), F32)],
        compiler_params=_params(("arbitrary", "arbitrary")),
        name="gdn",
    )(qkv, z, bd, bdt, hist, s0, wconv, wonorm)


def _cconv_kernel(carry, u_ref, hist_ref, w_ref, b_ref, lg_ref, lb_ref, c_ref, ubuf, shifted):
    i = pl.program_id(1)
    tt = u_ref.shape[0]
    hist_rows = DW_TAPS - 1
    pad = 4 * SUBLANES
    base = pad - hist_rows

    @pl.when(i == 0)
    def _():
        ubuf[base:pad, :] = hist_ref[...]

    ubuf[pad:pad + tt, :] = u_ref[...]
    span = shifted.shape[1]
    for s in range(1, SUBLANES):
        shifted[s - 1] = ubuf[s:s + span, :]

    def tap_rows(j, r0, rows):
        q, s = divmod(base + j, SUBLANES)
        start = q * SUBLANES + r0
        if s == 0:
            return ubuf[start:start + rows, :]
        return shifted[s - 1, start:start + rows, :]

    rows = min(CONV_ROWS, tt)
    for r0 in range(0, tt, rows):
        acc = w_ref[0:1, :] * tap_rows(0, r0, rows)
        for j in range(1, DW_TAPS):
            acc = acc + w_ref[j:j + 1, :] * tap_rows(j, r0, rows)
        cv = acc + b_ref[...]
        mu = jnp.mean(cv, axis=-1, keepdims=True)
        xc = cv - mu
        var = jnp.mean(xc * xc, axis=-1, keepdims=True)
        y = xc * lax.rsqrt(var + LN_EPS) * lg_ref[...] + lb_ref[...]
        c_ref[r0:r0 + rows, :] = y * _sigmoid(y)
    if carry:
        ubuf[base:pad, :] = u_ref[tt - hist_rows:tt, :]


def _cconv(uu, hist, w, b, lg, lb, tt):
    bsz, t, _ = uu.shape
    nt = t // tt
    vec = pl.BlockSpec((1, CONF_WIDTH), lambda bi, i: (0, 0))
    return pl.pallas_call(
        functools.partial(_cconv_kernel, nt > 1),
        grid=(bsz, nt),
        in_specs=[
            pl.BlockSpec((None, tt, CONF_WIDTH), lambda bi, i: (bi, i, 0)),
            pl.BlockSpec((None, DW_TAPS - 1, CONF_WIDTH), lambda bi, i: (bi, 0, 0)),
            pl.BlockSpec((DW_TAPS, CONF_WIDTH), lambda bi, i: (0, 0)),
            vec, vec, vec,
        ],
        out_specs=pl.BlockSpec((None, tt, CONF_WIDTH), lambda bi, i: (bi, i, 0)),
        out_shape=jax.ShapeDtypeStruct((bsz, t, CONF_WIDTH), F32),
        scratch_shapes=[pltpu.VMEM((tt + 4 * SUBLANES, CONF_WIDTH), F32),
                        pltpu.VMEM((SUBLANES - 1, tt + 3 * SUBLANES, CONF_WIDTH), F32)],
        compiler_params=_params(("arbitrary", "arbitrary")),
        name="cconv",
    )(uu, hist, w, b, lg, lb)


def _outproj_kernel(tiles_a, oa_ref, ca_ref, xa_ref, ob_ref, cb_ref, xb_ref, *rest):
    i = pl.program_id(0)

    @pl.when(i < tiles_a)
    def _():
        _outproj_tile(oa_ref, ca_ref, xa_ref, *rest)

    @pl.when(i >= tiles_a)
    def _():
        _outproj_tile(ob_ref, cb_ref, xb_ref, *rest)


def _outproj_tile(o_ref, c_ref, x_ref, wo_ref, wc_ref, g_ref, wr_ref, br_ref, upper_ref,
                  x1_ref, h2_ref, metar_ref, cnt_ref):
    tm = x_ref.shape[0]

    mix = _dotb(o_ref[...], wo_ref[...]) + _dotb(c_ref[...], wc_ref[...])
    x1 = x_ref[...] + mix
    x1_ref[...] = x1
    h2 = x1 * lax.rsqrt(jnp.mean(x1 * x1, axis=-1, keepdims=True) + RMS_EPS) * g_ref[...]
    h2_ref[...] = h2.astype(BF16)
    logits = lax.dot_general(wr_ref[...], h2.astype(BF16), (((1,), (1,)), ((), ())),
                             preferred_element_type=F32) + br_ref[...]
    row = lax.broadcasted_iota(I32, (N_EXPERTS, tm), 0).astype(F32)
    vals = logits
    idxs, tops = [], []
    for _ in range(TOP_K):
        m = jnp.max(vals, axis=0, keepdims=True)
        idx = jnp.min(jnp.where(vals == m, row, float(N_EXPERTS)), axis=0, keepdims=True)
        idxs.append(idx)
        tops.append(m)
        vals = jnp.where(row == idx, -jnp.inf, vals)
    exps = [jnp.exp(v - tops[0]) for v in tops]
    den = exps[0] + exps[1] + exps[2] + exps[3]
    onehots = [(row == idx).astype(F32) for idx in idxs]
    chosen = onehots[0] + onehots[1] + onehots[2] + onehots[3]
    before = jnp.dot(chosen.astype(BF16), upper_ref[...], preferred_element_type=F32)
    ranks = [jnp.sum(oh * before, axis=0, keepdims=True) for oh in onehots]
    gates = [e / den for e in exps]
    metar_ref[...] = jnp.concatenate(
        idxs + ranks + gates + [jnp.zeros((META_ROWS - 3 * TOP_K, tm), F32)], axis=0)
    cnt_ref[...] = jnp.broadcast_to(jnp.sum(chosen, axis=1, keepdims=True), (N_EXPERTS, LANES))[None]


def _outproj(first, second, wo, wc, g, wr, br):
    t_a, t_b = first[2].shape[0], second[2].shape[0]
    tm = TOKEN_TILE
    upper = (jnp.arange(tm)[:, None] < jnp.arange(tm)[None, :]).astype(BF16)
    tiles_a = t_a // tm
    t_total = t_a + t_b
    full = lambda shape: pl.BlockSpec(shape, lambda i: (0, 0))
    rows_a = lambda width: pl.BlockSpec((tm, width), lambda i: (jnp.minimum(i, tiles_a - 1), 0))
    rows_b = lambda width: pl.BlockSpec((tm, width), lambda i: (jnp.maximum(i - tiles_a, 0), 0))
    widths = (GDN_WIDTH, CONF_WIDTH, D_MODEL)
    return pl.pallas_call(
        functools.partial(_outproj_kernel, tiles_a),
        grid=(t_total // tm,),
        in_specs=[rows_a(w) for w in widths] + [rows_b(w) for w in widths] + [
            full((GDN_WIDTH, D_MODEL)),
            full((CONF_WIDTH, D_MODEL)),
            full((1, D_MODEL)),
            full((N_EXPERTS, D_MODEL)),
            full((N_EXPERTS, 1)),
            full((tm, tm)),
        ],
        out_specs=[
            pl.BlockSpec((tm, D_MODEL), lambda i: (i, 0)),
            pl.BlockSpec((tm, D_MODEL), lambda i: (i, 0)),
            pl.BlockSpec((META_ROWS, tm), lambda i: (0, i)),
            pl.BlockSpec((1, N_EXPERTS, LANES), lambda i: (i, 0, 0)),
        ],
        out_shape=[
            jax.ShapeDtypeStruct((t_total, D_MODEL), F32),
            jax.ShapeDtypeStruct((t_total, D_MODEL), BF16),
            jax.ShapeDtypeStruct((META_ROWS, t_total), F32),
            jax.ShapeDtypeStruct((t_total // tm, N_EXPERTS, LANES), F32),
        ],
        compiler_params=_params(("arbitrary",)),
        name="outproj_router",
    )(*first, *second, wo, wc, g, wr, br, upper)


def _segment_copies(i, seg_local_ref, seg_dst_ref, seg_n_ref, make_copy, wait):
    def per_expert(ex, carry):
        j = i * N_EXPERTS + ex
        local, dst, n = seg_local_ref[j], seg_dst_ref[j], seg_n_ref[j]
        done = 0
        for per_copy, count in ((8, n >> 3), (2, (n >> 1) & 3), (1, n & 1)):
            def body(g, c, per_copy=per_copy, done=done):
                off = (done + g * per_copy) * GRANULE
                cp = make_copy(per_copy * GRANULE, pl.multiple_of(local + off, GRANULE),
                               pl.multiple_of(dst + off, GRANULE))
                cp.wait() if wait else cp.start()
                return c

            lax.fori_loop(0, count, body, 0)
            done = done + count * per_copy
        return carry

    lax.fori_loop(0, N_EXPERTS, per_expert, 0)


def _gap_copies(gap_start_ref, gap_n_ref, make_copy, wait):
    def per_expert(ex, carry):
        n = gap_n_ref[ex]
        start = pl.multiple_of(gap_start_ref[ex], GRANULE)
        off = 0 * n
        for bit in reversed(range(GAP_BITS)):
            rows = GRANULE << bit

            @pl.when(((n >> bit) & 1) == 1)
            def _(rows=rows, off=off):
                cp = make_copy(rows, pl.multiple_of(start + off, GRANULE))
                cp.wait() if wait else cp.start()

            off = off + ((n >> bit) & 1) * rows
        return carry

    lax.fori_loop(0, N_EXPERTS, per_expert, 0)


def _local_rows(meta_e, seg_local_ref, i):
    base = jnp.zeros(meta_e.shape, F32)
    for ex in range(N_EXPERTS):
        base = jnp.where(meta_e == float(ex), seg_local_ref[i * N_EXPERTS + ex].astype(F32), base)
    return base


def _dispatch_kernel(seg_local_ref, seg_dst_ref, seg_n_ref, gap_start_ref, gap_n_ref, tail_ref,
                     h_ref, mr_ref, xs_ref, slots_ref, xloc, zbuf, sems, zsem):
    i = pl.program_id(0)
    last = pl.num_programs(0) - 1
    tm = h_ref.shape[0]
    rt = xloc.shape[1]
    zrows = zbuf.shape[0]
    slot = i % 2

    def gap_copy(rows, dst):
        return pltpu.make_async_copy(zbuf.at[pl.ds(0, rows)], xs_ref.at[pl.ds(dst, rows)], zsem)

    def tail_copies(wait):
        def body(j, c):
            cp = gap_copy(zrows, pl.multiple_of(tail_ref[0] + j * zrows, zrows))
            cp.wait() if wait else cp.start()
            return c
        lax.fori_loop(0, tail_ref[1], body, 0)

    @pl.when(i == 0)
    def _():
        zbuf[...] = jnp.zeros(zbuf.shape, zbuf.dtype)
        _gap_copies(gap_start_ref, gap_n_ref, gap_copy, wait=False)
        tail_copies(wait=False)

    mr = mr_ref[...]
    dl = (_local_rows(mr[META_EXPERT:META_EXPERT + TOP_K], seg_local_ref, i)
          + mr[META_RANK:META_RANK + TOP_K])
    slots_ref[...] = jnp.concatenate(
        [dl, mr[META_GATE:META_GATE + TOP_K], jnp.zeros((LANES - 2 * TOP_K, tm), F32)], axis=0).T
    rr = lax.broadcasted_iota(I32, (rt, tm), 0).astype(F32)
    hit = rr == dl[0:1]
    for k in range(1, TOP_K):
        hit = hit | (rr == dl[k:k + 1])
    xloc[slot] = jnp.dot(hit.astype(BF16), h_ref[...], preferred_element_type=F32)

    def seg_copy(buf):
        def make(rows, local, dst):
            return pltpu.make_async_copy(xloc.at[buf, pl.ds(local, rows)],
                                         xs_ref.at[pl.ds(dst, rows)], sems.at[buf])
        return make

    _segment_copies(i, seg_local_ref, seg_dst_ref, seg_n_ref, seg_copy(slot), wait=False)

    @pl.when(i > 0)
    def _():
        _segment_copies(i - 1, seg_local_ref, seg_dst_ref, seg_n_ref, seg_copy(1 - slot), wait=True)

    @pl.when(i == last)
    def _():
        _segment_copies(i, seg_local_ref, seg_dst_ref, seg_n_ref, seg_copy(slot), wait=True)

    @pl.when(i == 0)
    def _():
        _gap_copies(gap_start_ref, gap_n_ref, gap_copy, wait=True)
        tail_copies(wait=True)


def _dispatch(tables, h2, metar, n_rows):
    t = h2.shape[0]
    tm = TOKEN_TILE
    return pl.pallas_call(
        _dispatch_kernel,
        grid_spec=pltpu.PrefetchScalarGridSpec(
            num_scalar_prefetch=6,
            grid=(t // tm,),
            in_specs=[
                pl.BlockSpec((tm, D_MODEL), lambda i, *_: (i, 0)),
                pl.BlockSpec((META_ROWS, tm), lambda i, *_: (0, i)),
            ],
            out_specs=[pl.BlockSpec(memory_space=pl.ANY),
                       pl.BlockSpec((tm, LANES), lambda i, *_: (i, 0))],
            scratch_shapes=[
                pltpu.VMEM((2, TILE_ROWS, D_MODEL), F32),
                pltpu.VMEM((GRANULE << (GAP_BITS - 1), D_MODEL), F32),
                pltpu.SemaphoreType.DMA((2,)),
                pltpu.SemaphoreType.DMA,
            ],
        ),
        out_shape=[jax.ShapeDtypeStruct((n_rows, D_MODEL), F32),
                   jax.ShapeDtypeStruct((t, LANES), F32)],
        compiler_params=_params(("arbitrary",)),
        name="dispatch",
    )(*tables, h2, metar)


def _expert_kernel(be_ref, rb_ref, nu_ref, first_ref, slot_ref, next_ref,
                   x_ref, wgu_hbm, bgu_ref, wd_hbm, bdn_ref, y_ref,
                   wgu_f32, wd_f32, wgu_bf, wd_bf, sems):
    i = pl.program_id(0)

    def fetch(expert, slot):
        return (pltpu.make_async_copy(wgu_hbm.at[expert], wgu_f32.at[slot], sems.at[slot, 0]),
                pltpu.make_async_copy(wd_hbm.at[expert], wd_f32.at[slot], sems.at[slot, 1]))

    @pl.when(i == 0)
    def _():
        for cp in fetch(be_ref[0], 0):
            cp.start()

    @pl.when(first_ref[i] == 1)
    def _():
        slot = slot_ref[i]
        for cp in fetch(be_ref[i], slot):
            cp.wait()
        wgu_bf[...] = wgu_f32[slot].astype(BF16)
        wd_bf[...] = wd_f32[slot].astype(BF16)

        @pl.when(next_ref[i] >= 0)
        def _():
            for cp in fetch(next_ref[i], 1 - slot):
                cp.start()

    @pl.when(i < nu_ref[0])
    def _():
        gu = _dotb(x_ref[...], wgu_bf[...]) + bgu_ref[...]
        x_glu = jnp.minimum(gu[:, :D_EXPERT], SWIGLU_LIMIT)
        x_lin = jnp.clip(gu[:, D_EXPERT:], -SWIGLU_LIMIT, SWIGLU_LIMIT)
        act = x_glu * _sigmoid(SWIGLU_ALPHA * x_glu) * (x_lin + 1.0)
        y_ref[...] = _dotb(act, wd_bf[...]) + bdn_ref[...]

    @pl.when(i >= nu_ref[0])
    def _():
        y_ref[...] = jnp.zeros(y_ref.shape, y_ref.dtype)


def _experts(tables, xs, wgu, bgu, wd, bdn):
    rows = xs.shape[0]
    bm = EXPERT_ROWS
    nb = rows // bm
    return pl.pallas_call(
        _expert_kernel,
        grid_spec=pltpu.PrefetchScalarGridSpec(
            num_scalar_prefetch=6,
            grid=(nb,),
            in_specs=[
                pl.BlockSpec((bm, D_MODEL), lambda i, be, rb, *_: (rb[i], 0)),
                pl.BlockSpec(memory_space=pl.ANY),
                pl.BlockSpec((None, 1, 2 * D_EXPERT), lambda i, be, *_: (be[i], 0, 0)),
                pl.BlockSpec(memory_space=pl.ANY),
                pl.BlockSpec((None, 1, D_MODEL), lambda i, be, *_: (be[i], 0, 0)),
            ],
            out_specs=pl.BlockSpec((bm, D_MODEL), lambda i, *_: (i, 0)),
            scratch_shapes=[
                pltpu.VMEM((2, D_MODEL, 2 * D_EXPERT), F32),
                pltpu.VMEM((2, D_EXPERT, D_MODEL), F32),
                pltpu.VMEM((D_MODEL, 2 * D_EXPERT), BF16),
                pltpu.VMEM((D_EXPERT, D_MODEL), BF16),
                pltpu.SemaphoreType.DMA((2, 2)),
            ],
        ),
        out_shape=jax.ShapeDtypeStruct((rows, D_MODEL), F32),
        compiler_params=pltpu.CompilerParams(dimension_semantics=("arbitrary",),
                                             vmem_limit_bytes=EXPERT_VMEM_LIMIT),
        name="experts",
    )(*tables, xs, wgu, bgu, wd, bdn)


def _combine_kernel(tiles_a, seg_local_ref, seg_dst_ref, seg_n_ref, x1_ref, mc_ref, yb_ref, gf_ref,
                    ya_ref, yb_out_ref, ybuf, wsel, sems):
    i = pl.program_id(0)
    n_tiles = pl.num_programs(0)
    tm = x1_ref.shape[0]
    rt = ybuf.shape[1]
    slot = i % 2

    def seg_copy(buf):
        def make(rows, local, src):
            return pltpu.make_async_copy(yb_ref.at[pl.ds(src, rows)],
                                         ybuf.at[buf, pl.ds(local, rows)], sems.at[buf])
        return make

    @pl.when(i == 0)
    def _():
        ybuf[...] = jnp.zeros(ybuf.shape, ybuf.dtype)
        _segment_copies(i, seg_local_ref, seg_dst_ref, seg_n_ref, seg_copy(slot), wait=False)

    @pl.when(i + 1 < n_tiles)
    def _():
        _segment_copies(i + 1, seg_local_ref, seg_dst_ref, seg_n_ref, seg_copy(1 - slot), wait=False)

    dl = mc_ref[:, 0:TOP_K]
    gate = mc_ref[:, TOP_K:2 * TOP_K]
    rb, cb = SEL_CHUNK
    col = lax.broadcasted_iota(I32, (rb, cb), 1).astype(F32)
    for r0 in range(0, tm, rb):
        rows_k = [jnp.broadcast_to(dl[r0:r0 + rb, k:k + 1], (rb, cb)) for k in range(TOP_K)]
        gate_k = [jnp.broadcast_to(gate[r0:r0 + rb, k:k + 1], (rb, cb)) for k in range(TOP_K)]
        for c0 in range(0, rt, cb):
            rr = col + float(c0)
            w = jnp.where(rr == rows_k[0], gate_k[0], 0.0)
            for k in range(1, TOP_K):
                w = jnp.where(rr == rows_k[k], gate_k[k], w)
            wsel[r0:r0 + rb, c0:c0 + cb] = w.astype(BF16)
    _segment_copies(i, seg_local_ref, seg_dst_ref, seg_n_ref, seg_copy(slot), wait=True)
    acc = x1_ref[...] + jnp.dot(wsel[...], ybuf[slot].astype(BF16), preferred_element_type=F32)
    y = acc * lax.rsqrt(jnp.mean(acc * acc, axis=-1, keepdims=True) + RMS_EPS) * gf_ref[...]

    @pl.when(i < tiles_a)
    def _():
        ya_ref[...] = y

    @pl.when(i >= tiles_a)
    def _():
        yb_out_ref[...] = y


def _combine(tables, x1, slots, yb, gfin, n_a):
    t = x1.shape[0]
    tm = TOKEN_TILE
    tiles_a = n_a // tm
    return pl.pallas_call(
        functools.partial(_combine_kernel, tiles_a),
        grid_spec=pltpu.PrefetchScalarGridSpec(
            num_scalar_prefetch=3,
            grid=(t // tm,),
            in_specs=[
                pl.BlockSpec((tm, D_MODEL), lambda i, *_: (i, 0)),
                pl.BlockSpec((tm, LANES), lambda i, *_: (i, 0)),
                pl.BlockSpec(memory_space=pl.ANY),
                pl.BlockSpec((1, D_MODEL), lambda i, *_: (0, 0)),
            ],
            out_specs=[
                pl.BlockSpec((tm, D_MODEL), lambda i, *_: (jnp.minimum(i, tiles_a - 1), 0)),
                pl.BlockSpec((tm, D_MODEL), lambda i, *_: (jnp.maximum(i - tiles_a, 0), 0)),
            ],
            scratch_shapes=[pltpu.VMEM((2, TILE_ROWS, D_MODEL), F32),
                            pltpu.VMEM((tm, TILE_ROWS), BF16),
                            pltpu.SemaphoreType.DMA((2,))],
        ),
        out_shape=[jax.ShapeDtypeStruct((n_a, D_MODEL), F32),
                   jax.ShapeDtypeStruct((t - n_a, D_MODEL), F32)],
        compiler_params=pltpu.CompilerParams(dimension_semantics=("arbitrary",),
                                             vmem_limit_bytes=EXPERT_VMEM_LIMIT),
        name="combine",
    )(*tables, x1, slots, yb, gfin)


def kernel(x_prompt, x_sample, state_gdn, state_qkv_conv, state_dwconv, norm_mix_g, w_in, w_conv_qkv,
           a_log, dt_bias, w_onorm, b_glu, w_dw, b_dw, ln_g, ln_b, w_out, norm_ffn_g, w_router,
           b_router, w_gate_up, b_gate_up, w_down, b_down, norm_final_g):
    bp, tp, _ = x_prompt.shape
    bs, ts, _ = x_sample.shape
    n_p, n_s = bp * tp, bs * ts

    wi = w_in[0]
    c_z = QKV_COLS
    c_bd = c_z + GDN_WIDTH
    c_glu = c_bd + 2 * GDN_HEADS
    wqkv = wi[:, :c_z].astype(BF16)
    wz = wi[:, c_z:c_bd].astype(BF16)
    wbd = jnp.pad(wi[:, c_bd:c_glu], ((0, 0), (0, LANES - 2 * GDN_HEADS))).astype(BF16)
    wglu = wi[:, c_glu:].astype(BF16)
    bglu = b_glu[0][None, :]
    lane_pad = (GDN_HEADS, LANES - 2 * GDN_HEADS)
    pa = jnp.pad(-jnp.exp(a_log[0].astype(F32)), lane_pad)[None, :]
    pb = jnp.pad(dt_bias[0].astype(F32), lane_pad)[None, :]
    gmix = norm_mix_g[0][None, :]
    wconv = w_conv_qkv[0]
    wonorm = w_onorm[0][None, :]
    wdw, bdw = w_dw[0], b_dw[0][None, :]
    lng, lnb = ln_g[0][None, :], ln_b[0][None, :]
    wo = w_out[0][:GDN_WIDTH].astype(BF16)
    wc = w_out[0][GDN_WIDTH:].astype(BF16)
    gffn = norm_ffn_g[0][None, :]
    wr, br = w_router[0].T.astype(BF16), b_router[0].astype(F32)[:, None]
    bgu = b_gate_up[0][:, None, :]
    bdn = b_down[0][:, None, :]
    gfin = norm_final_g[None, :]

    xp = x_prompt.reshape(n_p, D_MODEL)
    xs_tok = x_sample.reshape(n_s, D_MODEL)

    def mixers(x2d, bsz, t, hist_qkv, s0, hist_glu, chunk, gdn_tile, gdn_rows, conv_tile):
        qkv, z, bd, bdt, uu = _inproj(x2d, gmix, wqkv, wz, wbd, wglu, bglu, pa, pb, chunk, t)
        qkv3 = qkv.reshape(bsz, t, QKV_COLS)
        uu3 = uu.reshape(bsz, t, CONF_WIDTH)
        if bdt.ndim == 2:
            bdt = bdt.reshape(2 * GDN_HEADS, bsz, t).transpose(1, 0, 2)
        o, s_new = _gdn(qkv3, z.reshape(bsz, t, GDN_WIDTH), bd.reshape(bsz, t, 2 * GDN_HEADS),
                        bdt, hist_qkv, s0, wconv, wonorm, chunk, gdn_tile, gdn_rows)
        cc = _cconv(uu3, hist_glu, wdw, bdw, lng, lnb, conv_tile)
        return o.reshape(bsz * t, GDN_WIDTH), cc.reshape(bsz * t, CONF_WIDTH), s_new, qkv3, uu3

    zero_qkv = jnp.zeros((bp, QKV_TAPS - 1, QKV_COLS), F32)
    zero_s = jnp.zeros((bp, GDN_HEADS, HEAD_DIM, HEAD_DIM), F32)
    zero_glu = jnp.zeros((bp, DW_TAPS - 1, CONF_WIDTH), F32)
    o_p, c_p, s_p, qkv_p, uu_p = mixers(xp, bp, tp, zero_qkv, zero_s, zero_glu,
                                        PROMPT_CHUNK, GDN_TILE, 1, CONV_TILE)
    o_s, c_s, s_s, qkv_s, uu_s = mixers(xs_tok, bs, ts, state_qkv_conv[0], state_gdn[0],
                                        state_dwconv[0], ts, ts, GDN_SAMPLE_ROWS, ts)

    n_tok = n_p + n_s
    x1, h2, metar, cnt_tile = _outproj((o_p, c_p, xp), (o_s, c_s, xs_tok), wo, wc, gffn, wr, br)

    bm = EXPERT_ROWS
    n_tiles = n_tok // TOKEN_TILE
    counts = cnt_tile[:, :, 0].astype(I32)
    seg_rows = (counts + GRANULE - 1) // GRANULE * GRANULE
    ids = jnp.arange(N_EXPERTS, dtype=I32)
    tile_ids = jnp.arange(n_tiles, dtype=I32)
    earlier_e = (ids[:, None] < ids[None, :]).astype(I32)
    earlier_t = (tile_ids[None, :] < tile_ids[:, None]).astype(I32)
    seg_local = jnp.sum(seg_rows[:, :, None] * earlier_e[None], axis=1)
    seg_before = jnp.sum(earlier_t[:, :, None] * seg_rows[None], axis=1)
    rows_e = jnp.sum(seg_rows, axis=0)
    padded = (rows_e + bm - 1) // bm * bm
    pstart = jnp.sum(padded[:, None] * earlier_e, axis=0)
    pend = pstart + padded
    seg_dst = pstart[None, :] + seg_before
    n_used = jnp.maximum(pend[-1] // bm, 1).astype(I32)
    max_rows = n_tok * TOP_K + n_tiles * N_EXPERTS * (GRANULE - 1) + N_EXPERTS * (bm - 1)
    n_blocks = -(-max_rows // bm)
    blk = jnp.minimum(jnp.arange(n_blocks, dtype=I32), n_used - 1)
    block_expert = jnp.minimum(
        jnp.sum((pend[None, :] <= (blk * bm)[:, None]).astype(I32), axis=1), N_EXPERTS - 1)
    seg_tables = (seg_local.reshape(-1).astype(I32), seg_dst.reshape(-1).astype(I32),
                  (seg_rows // GRANULE).reshape(-1).astype(I32))
    gap_tables = ((pstart + rows_e).astype(I32), ((padded - rows_e) // GRANULE).astype(I32))
    zero_rows = GRANULE << (GAP_BITS - 1)
    tail_table = jnp.stack([pend[-1], (n_blocks * bm - pend[-1]) // zero_rows]).astype(I32)

    xs, slots = _dispatch(seg_tables + gap_tables + (tail_table,), h2, metar, n_blocks * bm)
    active = padded > 0
    later = active[None, :] & (ids[None, :] > ids[:, None])
    next_active = jnp.min(jnp.where(later, ids[None, :], N_EXPERTS), axis=1)
    next_active = jnp.where(next_active == N_EXPERTS, -1, next_active)
    order = jnp.sum(active.astype(I32)[:, None] * earlier_e, axis=0)
    steps = jnp.arange(n_blocks, dtype=I32)
    of_block = (block_expert[:, None] == ids[None, :]).astype(I32)
    per_block = lambda table: jnp.sum(of_block * table[None, :], axis=1)
    first = ((steps * bm == per_block(pstart)) & (steps < n_used)).astype(I32)
    expert_tables = (block_expert, blk, n_used.reshape(1), first,
                     (per_block(order) % 2).astype(I32), per_block(next_active).astype(I32))
    yb = _experts(expert_tables, xs, w_gate_up[0], bgu, w_down[0], bdn)
    y_p, y_s = _combine(seg_tables, x1, slots, yb, gfin, n_p)

    hist_glu_s = jnp.concatenate([state_dwconv[0], uu_s], axis=1)[:, -(DW_TAPS - 1):]
    return (y_p.reshape(bp, tp, D_MODEL),
            y_s.reshape(bs, ts, D_MODEL),
            s_p[None],
            qkv_p[:, -(QKV_TAPS - 1):][None],
            uu_p[:, -(DW_TAPS - 1):][None],
            s_s[None],
            qkv_s[:, -(QKV_TAPS - 1):][None],
            hist_glu_s[None])
```

```python
import functools
import math

import jax
import jax.numpy as jnp
from jax import lax
from jax.experimental import pallas as pl
from jax.experimental.pallas import tpu as pltpu

F32 = jnp.float32
BF16 = jnp.bfloat16
I32 = jnp.int32
HIGHEST = lax.Precision.HIGHEST

D_MODEL = 1024
GDN_HEADS = 4
HEAD_DIM = 128
GDN_WIDTH = GDN_HEADS * HEAD_DIM
QKV_COLS = 3 * GDN_WIDTH
CONF_WIDTH = 512
QKV_TAPS = 4
DW_TAPS = 31
N_EXPERTS = 32
TOP_K = 4
D_EXPERT = 1024
SWIGLU_LIMIT = 7.0
SWIGLU_ALPHA = 1.702
RMS_EPS = 1e-6
LN_EPS = 1e-5
L2_EPS = 1e-6
PROMPT_CHUNK = 64

LANES = 128
SUBLANES = 8
TOKEN_TILE = 512
GDN_TILE = 512
GDN_SAMPLE_ROWS = 4
CONV_TILE = 512
CONV_ROWS = 32
EXPERT_ROWS = 512
GRANULE = SUBLANES
GAP_BITS = 6
TILE_ROWS = -(-(TOKEN_TILE * TOP_K + N_EXPERTS * (GRANULE - 1)) // 256) * 256
META_EXPERT, META_RANK, META_GATE, META_ROWS = 0, TOP_K, 2 * TOP_K, 16
assert EXPERT_ROWS // GRANULE <= 1 << GAP_BITS
SEL_CHUNK = (64, 256)
VMEM_LIMIT = 48 * 1024 * 1024
EXPERT_VMEM_LIMIT = 56 * 1024 * 1024


def _dotb(a, b):
    return jnp.dot(a.astype(BF16), b.astype(BF16), preferred_element_type=F32)


def _dotf(a, b):
    return jnp.dot(a, b, precision=HIGHEST, preferred_element_type=F32)


def _sigmoid(x):
    return jax.nn.sigmoid(x)


def _params(sem):
    return pltpu.CompilerParams(dimension_semantics=sem, vmem_limit_bytes=VMEM_LIMIT)


def _inproj_kernel(x_ref, g_ref, wqkv_ref, wz_ref, wbd_ref, wglu_ref, bglu_ref, pa_ref, pb_ref,
                   tri_ref, qkv_ref, z_ref, bd_ref, bdt_ref, uu_ref):
    x = x_ref[...]
    tm = x.shape[0]
    ms = jnp.mean(x * x, axis=-1, keepdims=True)
    h = (x * lax.rsqrt(ms + RMS_EPS) * g_ref[...]).astype(BF16)
    qkv_ref[...] = jnp.dot(h, wqkv_ref[...], preferred_element_type=F32)
    z_ref[...] = jnp.dot(h, wz_ref[...], preferred_element_type=F32)
    glu = jnp.dot(h, wglu_ref[...], preferred_element_type=F32) + bglu_ref[...]
    uu_ref[...] = glu[:, :CONF_WIDTH] * _sigmoid(glu[:, CONF_WIDTH:])
    bd = jnp.dot(h, wbd_ref[...], preferred_element_type=F32)
    beta = _sigmoid(bd)
    v = bd + pb_ref[...]
    softplus = jnp.maximum(v, 0.0) + jnp.log1p(jnp.exp(-jnp.abs(v)))
    g = pa_ref[...] * softplus
    g_hi, g_rest = _split_bf16(g)
    g_mid = g - g_hi.astype(F32) - g_rest.astype(F32)
    pieces = jnp.concatenate([g_hi, g_rest, g_mid.astype(BF16)], axis=1)
    sums = jnp.dot(tri_ref[...], pieces, preferred_element_type=F32)
    gc = sums[:, :LANES] + sums[:, LANES:2 * LANES] + sums[:, 2 * LANES:]
    lane = lax.broadcasted_iota(I32, (tm, LANES), 1)
    res = jnp.where(lane < GDN_HEADS, beta, gc)
    bd_ref[...] = res[:, :2 * GDN_HEADS]
    bdt_ref[...] = res.T[:2 * GDN_HEADS, :]


def _inproj(x, g, wqkv, wz, wbd, wglu, bglu, pa, pb, chunk, t_batch):
    t = x.shape[0]
    tm = TOKEN_TILE
    tiles_per_batch = t_batch // tm if t_batch % tm == 0 else 0
    pos = jnp.arange(tm)
    tri = ((pos[:, None] >= pos[None, :])
           & (pos[:, None] // chunk == pos[None, :] // chunk)).astype(BF16)
    full = lambda shape: pl.BlockSpec(shape, lambda i: (0, 0))
    if tiles_per_batch:
        bdt_spec = pl.BlockSpec((None, 2 * GDN_HEADS, tm),
                                lambda i: (i // tiles_per_batch, 0, i % tiles_per_batch))
        bdt_shape = jax.ShapeDtypeStruct((t // t_batch, 2 * GDN_HEADS, t_batch), F32)
    else:
        bdt_spec = pl.BlockSpec((2 * GDN_HEADS, tm), lambda i: (0, i))
        bdt_shape = jax.ShapeDtypeStruct((2 * GDN_HEADS, t), F32)
    return pl.pallas_call(
        _inproj_kernel,
        grid=(t // tm,),
        in_specs=[
            pl.BlockSpec((tm, D_MODEL), lambda i: (i, 0)),
            full((1, D_MODEL)),
            full((D_MODEL, QKV_COLS)),
            full((D_MODEL, GDN_WIDTH)),
            full((D_MODEL, LANES)),
            full((D_MODEL, 2 * CONF_WIDTH)),
            full((1, 2 * CONF_WIDTH)),
            full((1, LANES)),
            full((1, LANES)),
            full((tm, tm)),
        ],
        out_specs=[
            pl.BlockSpec((tm, QKV_COLS), lambda i: (i, 0)),
            pl.BlockSpec((tm, GDN_WIDTH), lambda i: (i, 0)),
            pl.BlockSpec((tm, 2 * GDN_HEADS), lambda i: (i, 0)),
            bdt_spec,
            pl.BlockSpec((tm, CONF_WIDTH), lambda i: (i, 0)),
        ],
        out_shape=[
            jax.ShapeDtypeStruct((t, QKV_COLS), F32),
            jax.ShapeDtypeStruct((t, GDN_WIDTH), F32),
            jax.ShapeDtypeStruct((t, 2 * GDN_HEADS), F32),
            bdt_shape,
            jax.ShapeDtypeStruct((t, CONF_WIDTH), F32),
        ],
        compiler_params=_params(("arbitrary",)),
        name="inproj",
    )(x, g, wqkv, wz, wbd, wglu, bglu, pa, pb, tri)


def _split_bf16(a):
    hi = a.astype(BF16)
    lo = (a - hi.astype(F32)).astype(BF16)
    return hi, lo


def _unit_lower_inverses(ms, eye, n):
    levels = int(math.log2(n)) - 1
    ps = [eye - m for m in ms]
    pw = [(-m).astype(BF16) for m in ms]
    pw = [jnp.dot(p, p, preferred_element_type=F32).astype(BF16) for p in pw]
    for level in range(1, levels + 1):
        last = level == levels
        nxt = []
        for u, p in enumerate(pw):
            if last:
                ps[u] = ps[u] + jnp.dot(ps[u].astype(BF16), p, preferred_element_type=F32)
            else:
                out = jnp.dot(jnp.concatenate([p, ps[u].astype(BF16)], axis=0), p,
                              preferred_element_type=F32)
                nxt.append(out[:n].astype(BF16))
                ps[u] = ps[u] + out[n:]
        pw = nxt
    return ps


def _gdn_kernel(chunk, x_ref, z_ref, bd_ref, bdt_ref, hist_ref, s0_ref, wc_ref, won_ref,
                o_ref, s_ref, xbuf):
    i = pl.program_id(1)
    nb, tt = x_ref.shape[0], x_ref.shape[1]
    nchunk = tt // chunk
    pad = SUBLANES
    hist_rows = QKV_TAPS - 1

    @pl.when(i == 0)
    def _():
        xbuf[:, pad - hist_rows:pad, :] = hist_ref[...]
        s_ref[...] = s0_ref[...]

    xbuf[:, pad:pad + tt, :] = x_ref[...]

    ii = lax.broadcasted_iota(I32, (chunk, chunk), 0)
    jj = lax.broadcasted_iota(I32, (chunk, chunk), 1)
    lower_incl = ii >= jj
    lower_strict = ii > jj
    eye = (ii == jj).astype(F32)
    won = won_ref[...]

    def conv_silu(b, col0):
        cols = slice(col0, col0 + HEAD_DIM)
        acc = wc_ref[hist_rows:hist_rows + 1, cols] * xbuf[b, pad:pad + tt, cols]
        for j in range(hist_rows):
            r0 = pad - hist_rows + j
            acc = acc + wc_ref[j:j + 1, cols] * xbuf[b, r0:r0 + tt, cols]
        return acc * _sigmoid(acc)

    units = []
    for b in range(nb):
        bd = bd_ref[b]
        bdt = bdt_ref[b]
        for h in range(GDN_HEADS):
            q = conv_silu(b, h * HEAD_DIM)
            k = conv_silu(b, GDN_WIDTH + h * HEAD_DIM)
            v = conv_silu(b, 2 * GDN_WIDTH + h * HEAD_DIM)
            q = q * lax.rsqrt(jnp.sum(q * q, axis=-1, keepdims=True) + L2_EPS) * (HEAD_DIM ** -0.5)
            k = k * lax.rsqrt(jnp.sum(k * k, axis=-1, keepdims=True) + L2_EPS)
            kt = k.T
            beta = bd[:, h:h + 1]
            gcol = bd[:, GDN_HEADS + h:GDN_HEADS + h + 1]
            grow = bdt[GDN_HEADS + h:GDN_HEADS + h + 1, :]
            for c in range(nchunk):
                rows = slice(c * chunk, (c + 1) * chunk)
                qc, kc, vc, ktc = q[rows], k[rows], v[rows], kt[:, rows]
                bc, gc, gr = beta[rows], gcol[rows], grow[:, rows]
                dec = jnp.exp(jnp.where(lower_incl, gc - gr, -jnp.inf))
                kb = kc * bc
                eg = jnp.exp(gc)
                both = _dotb(jnp.concatenate([kb, qc], axis=0), ktc)
                units.append(dict(
                    b=b, h=h, c=c,
                    m=both[:chunk] * jnp.where(lower_strict, dec, 0.0),
                    qk=both[chunk:] * dec,
                    rhs=jnp.concatenate([vc * bc, kb * eg], axis=1),
                    qg=qc * eg,
                    kdt=ktc * jnp.exp(gr[:, chunk - 1:chunk] - gr),
                    d_last=jnp.exp(gc[chunk - 1:chunk, :]),
                ))

    tinvs = _unit_lower_inverses([u["m"] for u in units], eye, chunk)
    for u, tinv in zip(units, tinvs):
        sol = _dotb(tinv, u["rhs"])
        u["u"], u["w"] = sol[:, :HEAD_DIM], sol[:, HEAD_DIM:]

    state = {(b, h): s_ref[b, h] for b in range(nb) for h in range(GDN_HEADS)}
    by_key = {(u["b"], u["h"], u["c"]): u for u in units}
    for c in range(nchunk):
        rows = slice(c * chunk, (c + 1) * chunk)
        for b in range(nb):
            for h in range(GDN_HEADS):
                u = by_key[(b, h, c)]
                s = state[(b, h)]
                ws = _dotb(jnp.concatenate([u["w"], u["qg"]], axis=0), s)
                v_new = u["u"] - ws[:chunk]
                upd = _dotb(jnp.concatenate([u["qk"], u["kdt"]], axis=0), v_new)
                o_c = ws[chunk:] + upd[:chunk]
                state[(b, h)] = s * u["d_last"] + upd[chunk:]
                cols = slice(h * HEAD_DIM, (h + 1) * HEAD_DIM)
                on = o_c * lax.rsqrt(jnp.mean(o_c * o_c, axis=-1, keepdims=True) + RMS_EPS) * won
                zc = z_ref[b, rows, cols]
                o_ref[b, rows, cols] = on * (zc * _sigmoid(zc))
    for (b, h), s in state.items():
        s_ref[b, h] = s

    xbuf[:, pad - hist_rows:pad, :] = x_ref[:, tt - hist_rows:tt, :]


def _gdn(qkv, z, bd, bdt, hist, s0, wconv, wonorm, chunk, tt, nb):
    b, t, _ = qkv.shape
    nt = t // tt
    return pl.pallas_call(
        functools.partial(_gdn_kernel, chunk),
        grid=(b // nb, nt),
        in_specs=[
            pl.BlockSpec((nb, tt, QKV_COLS), lambda bi, i: (bi, i, 0)),
            pl.BlockSpec((nb, tt, GDN_WIDTH), lambda bi, i: (bi, i, 0)),
            pl.BlockSpec((nb, tt, 2 * GDN_HEADS), lambda bi, i: (bi, i, 0)),
            pl.BlockSpec((nb, 2 * GDN_HEADS, tt), lambda bi, i: (bi, 0, i)),
            pl.BlockSpec((nb, QKV_TAPS - 1, QKV_COLS), lambda bi, i: (bi, 0, 0)),
            pl.BlockSpec((nb, GDN_HEADS, HEAD_DIM, HEAD_DIM), lambda bi, i: (bi, 0, 0, 0)),
            pl.BlockSpec((QKV_TAPS, QKV_COLS), lambda bi, i: (0, 0)),
            pl.BlockSpec((1, HEAD_DIM), lambda bi, i: (0, 0)),
        ],
        out_specs=[
            pl.BlockSpec((nb, tt, GDN_WIDTH), lambda bi, i: (bi, i, 0)),
            pl.BlockSpec((nb, GDN_HEADS, HEAD_DIM, HEAD_DIM), lambda bi, i: (bi, 0, 0, 0)),
        ],
        out_shape=[
            jax.ShapeDtypeStruct((b, t, GDN_WIDTH), F32),
            jax.ShapeDtypeStruct((b, GDN_HEADS, HEAD_DIM, HEAD_DIM), F32),
        ],
        scratch_shapes=[pltpu.VMEM((nb, tt + SUBLANES, QKV_COLS), F32)],
        compiler_params=_params(("arbitrary", "arbitrary")),
        name="gdn",
    )(qkv, z, bd, bdt, hist, s0, wconv, wonorm)


def _cconv_kernel(carry, u_ref, hist_ref, w_ref, b_ref, lg_ref, lb_ref, c_ref, ubuf, shifted):
    i = pl.program_id(1)
    tt = u_ref.shape[0]
    hist_rows = DW_TAPS - 1
    pad = 4 * SUBLANES
    base = pad - hist_rows

    @pl.when(i == 0)
    def _():
        ubuf[base:pad, :] = hist_ref[...]

    ubuf[pad:pad + tt, :] = u_ref[...]
    span = shifted.shape[1]
    for s in range(1, SUBLANES):
        shifted[s - 1] = ubuf[s:s + span, :]

    def tap_rows(j, r0, rows):
        q, s = divmod(base + j, SUBLANES)
        start = q * SUBLANES + r0
        if s == 0:
            return ubuf[start:start + rows, :]
        return shifted[s - 1, start:start + rows, :]

    rows = min(CONV_ROWS, tt)
    for r0 in range(0, tt, rows):
        acc = w_ref[0:1, :] * tap_rows(0, r0, rows)
        for j in range(1, DW_TAPS):
            acc = acc + w_ref[j:j + 1, :] * tap_rows(j, r0, rows)
        cv = acc + b_ref[...]
        mu = jnp.mean(cv, axis=-1, keepdims=True)
        xc = cv - mu
        var = jnp.mean(xc * xc, axis=-1, keepdims=True)
        y = xc * lax.rsqrt(var + LN_EPS) * lg_ref[...] + lb_ref[...]
        c_ref[r0:r0 + rows, :] = y * _sigmoid(y)
    if carry:
        ubuf[base:pad, :] = u_ref[tt - hist_rows:tt, :]


def _cconv(uu, hist, w, b, lg, lb, tt):
    bsz, t, _ = uu.shape
    nt = t // tt
    vec = pl.BlockSpec((1, CONF_WIDTH), lambda bi, i: (0, 0))
    return pl.pallas_call(
        functools.partial(_cconv_kernel, nt > 1),
        grid=(bsz, nt),
        in_specs=[
            pl.BlockSpec((None, tt, CONF_WIDTH), lambda bi, i: (bi, i, 0)),
            pl.BlockSpec((None, DW_TAPS - 1, CONF_WIDTH), lambda bi, i: (bi, 0, 0)),
            pl.BlockSpec((DW_TAPS, CONF_WIDTH), lambda bi, i: (0, 0)),
            vec, vec, vec,
        ],
        out_specs=pl.BlockSpec((None, tt, CONF_WIDTH), lambda bi, i: (bi, i, 0)),
        out_shape=jax.ShapeDtypeStruct((bsz, t, CONF_WIDTH), F32),
        scratch_shapes=[pltpu.VMEM((tt + 4 * SUBLANES, CONF_WIDTH), F32),
                        pltpu.VMEM((SUBLANES - 1, tt + 3 * SUBLANES, CONF_WIDTH), F32)],
        compiler_params=_params(("arbitrary", "arbitrary")),
        name="cconv",
    )(uu, hist, w, b, lg, lb)


def _outproj_kernel(tiles_a, oa_ref, ca_ref, xa_ref, ob_ref, cb_ref, xb_ref, *rest):
    i = pl.program_id(0)

    @pl.when(i < tiles_a)
    def _():
        _outproj_tile(oa_ref, ca_ref, xa_ref, *rest)

    @pl.when(i >= tiles_a)
    def _():
        _outproj_tile(ob_ref, cb_ref, xb_ref, *rest)


def _outproj_tile(o_ref, c_ref, x_ref, wo_ref, wc_ref, g_ref, wr_ref, br_ref, upper_ref,
                  x1_ref, h2_ref, metar_ref, cnt_ref):
    tm = x_ref.shape[0]

    mix = _dotb(o_ref[...], wo_ref[...]) + _dotb(c_ref[...], wc_ref[...])
    x1 = x_ref[...] + mix
    x1_ref[...] = x1
    h2 = x1 * lax.rsqrt(jnp.mean(x1 * x1, axis=-1, keepdims=True) + RMS_EPS) * g_ref[...]
    h2_ref[...] = h2.astype(BF16)
    logits = lax.dot_general(wr_ref[...], h2.astype(BF16), (((1,), (1,)), ((), ())),
                             preferred_element_type=F32) + br_ref[...]
    row = lax.broadcasted_iota(I32, (N_EXPERTS, tm), 0).astype(F32)
    vals = logits
    idxs, tops = [], []
    for _ in range(TOP_K):
        m = jnp.max(vals, axis=0, keepdims=True)
        idx = jnp.min(jnp.where(vals == m, row, float(N_EXPERTS)), axis=0, keepdims=True)
        idxs.append(idx)
        tops.append(m)
        vals = jnp.where(row == idx, -jnp.inf, vals)
    exps = [jnp.exp(v - tops[0]) for v in tops]
    den = exps[0] + exps[1] + exps[2] + exps[3]
    onehots = [(row == idx).astype(F32) for idx in idxs]
    chosen = onehots[0] + onehots[1] + onehots[2] + onehots[3]
    before = jnp.dot(chosen.astype(BF16), upper_ref[...], preferred_element_type=F32)
    ranks = [jnp.sum(oh * before, axis=0, keepdims=True) for oh in onehots]
    gates = [e / den for e in exps]
    metar_ref[...] = jnp.concatenate(
        idxs + ranks + gates + [jnp.zeros((META_ROWS - 3 * TOP_K, tm), F32)], axis=0)
    cnt_ref[...] = jnp.broadcast_to(jnp.sum(chosen, axis=1, keepdims=True), (N_EXPERTS, LANES))[None]


def _outproj(first, second, wo, wc, g, wr, br):
    t_a, t_b = first[2].shape[0], second[2].shape[0]
    tm = TOKEN_TILE
    upper = (jnp.arange(tm)[:, None] < jnp.arange(tm)[None, :]).astype(BF16)
    tiles_a = t_a // tm
    t_total = t_a + t_b
    full = lambda shape: pl.BlockSpec(shape, lambda i: (0, 0))
    rows_a = lambda width: pl.BlockSpec((tm, width), lambda i: (jnp.minimum(i, tiles_a - 1), 0))
    rows_b = lambda width: pl.BlockSpec((tm, width), lambda i: (jnp.maximum(i - tiles_a, 0), 0))
    widths = (GDN_WIDTH, CONF_WIDTH, D_MODEL)
    return pl.pallas_call(
        functools.partial(_outproj_kernel, tiles_a),
        grid=(t_total // tm,),
        in_specs=[rows_a(w) for w in widths] + [rows_b(w) for w in widths] + [
            full((GDN_WIDTH, D_MODEL)),
            full((CONF_WIDTH, D_MODEL)),
            full((1, D_MODEL)),
            full((N_EXPERTS, D_MODEL)),
            full((N_EXPERTS, 1)),
            full((tm, tm)),
        ],
        out_specs=[
            pl.BlockSpec((tm, D_MODEL), lambda i: (i, 0)),
            pl.BlockSpec((tm, D_MODEL), lambda i: (i, 0)),
            pl.BlockSpec((META_ROWS, tm), lambda i: (0, i)),
            pl.BlockSpec((1, N_EXPERTS, LANES), lambda i: (i, 0, 0)),
        ],
        out_shape=[
            jax.ShapeDtypeStruct((t_total, D_MODEL), F32),
            jax.ShapeDtypeStruct((t_total, D_MODEL), BF16),
            jax.ShapeDtypeStruct((META_ROWS, t_total), F32),
            jax.ShapeDtypeStruct((t_total // tm, N_EXPERTS, LANES), F32),
        ],
        compiler_params=_params(("arbitrary",)),
        name="outproj_router",
    )(*first, *second, wo, wc, g, wr, br, upper)


def _segment_copies(i, seg_local_ref, seg_dst_ref, seg_n_ref, make_copy, wait):
    def copy(rows, local, dst):
        cp = make_copy(rows, pl.multiple_of(local, GRANULE), pl.multiple_of(dst, GRANULE))
        cp.wait() if wait else cp.start()

    def per_expert(ex, carry):
        j = i * N_EXPERTS + ex
        local, dst, n = seg_local_ref[j], seg_dst_ref[j], seg_n_ref[j]
        big = 8 * GRANULE

        def body(g, c):
            copy(big, local + g * big, dst + g * big)
            return c

        lax.fori_loop(0, n >> 3, body, 0)
        off = (n >> 3) * big
        for bit in (2, 1, 0):
            rows = GRANULE << bit
            has = (n >> bit) & 1

            @pl.when(has == 1)
            def _(rows=rows, off=off):
                copy(rows, local + off, dst + off)

            off = off + has * rows
        return carry

    lax.fori_loop(0, N_EXPERTS, per_expert, 0)


def _gap_copies(gap_start_ref, gap_n_ref, make_copy, wait):
    def per_expert(ex, carry):
        n = gap_n_ref[ex]
        start = pl.multiple_of(gap_start_ref[ex], GRANULE)
        off = 0 * n
        for bit in reversed(range(GAP_BITS)):
            rows = GRANULE << bit

            @pl.when(((n >> bit) & 1) == 1)
            def _(rows=rows, off=off):
                cp = make_copy(rows, pl.multiple_of(start + off, GRANULE))
                cp.wait() if wait else cp.start()

            off = off + ((n >> bit) & 1) * rows
        return carry

    lax.fori_loop(0, N_EXPERTS, per_expert, 0)


def _local_rows(meta_e, seg_local_ref, i):
    base = jnp.zeros(meta_e.shape, F32)
    for ex in range(N_EXPERTS):
        base = jnp.where(meta_e == float(ex), seg_local_ref[i * N_EXPERTS + ex].astype(F32), base)
    return base


def _dispatch_kernel(seg_local_ref, seg_dst_ref, seg_n_ref, gap_start_ref, gap_n_ref, tail_ref,
                     h_ref, mr_ref, xs_ref, slots_ref, xloc, zbuf, sems, zsem):
    i = pl.program_id(0)
    last = pl.num_programs(0) - 1
    tm = h_ref.shape[0]
    rt = xloc.shape[1]
    zrows = zbuf.shape[0]
    slot = i % 2

    def gap_copy(rows, dst):
        return pltpu.make_async_copy(zbuf.at[pl.ds(0, rows)], xs_ref.at[pl.ds(dst, rows)], zsem)

    def tail_copies(wait):
        def body(j, c):
            cp = gap_copy(zrows, pl.multiple_of(tail_ref[0] + j * zrows, zrows))
            cp.wait() if wait else cp.start()
            return c
        lax.fori_loop(0, tail_ref[1], body, 0)

    @pl.when(i == 0)
    def _():
        zbuf[...] = jnp.zeros(zbuf.shape, zbuf.dtype)
        _gap_copies(gap_start_ref, gap_n_ref, gap_copy, wait=False)
        tail_copies(wait=False)

    mr = mr_ref[...]
    dl = (_local_rows(mr[META_EXPERT:META_EXPERT + TOP_K], seg_local_ref, i)
          + mr[META_RANK:META_RANK + TOP_K])
    slots_ref[...] = jnp.concatenate(
        [dl, mr[META_GATE:META_GATE + TOP_K], jnp.zeros((LANES - 2 * TOP_K, tm), F32)], axis=0).T
    rr = lax.broadcasted_iota(I32, (rt, tm), 0).astype(F32)
    hit = rr == dl[0:1]
    for k in range(1, TOP_K):
        hit = hit | (rr == dl[k:k + 1])
    xloc[slot] = jnp.dot(hit.astype(BF16), h_ref[...], preferred_element_type=F32)

    def seg_copy(buf):
        def make(rows, local, dst):
            return pltpu.make_async_copy(xloc.at[buf, pl.ds(local, rows)],
                                         xs_ref.at[pl.ds(dst, rows)], sems.at[buf])
        return make

    _segment_copies(i, seg_local_ref, seg_dst_ref, seg_n_ref, seg_copy(slot), wait=False)

    @pl.when(i > 0)
    def _():
        _segment_copies(i - 1, seg_local_ref, seg_dst_ref, seg_n_ref, seg_copy(1 - slot), wait=True)

    @pl.when(i == last)
    def _():
        _segment_copies(i, seg_local_ref, seg_dst_ref, seg_n_ref, seg_copy(slot), wait=True)

    @pl.when(i == 0)
    def _():
        _gap_copies(gap_start_ref, gap_n_ref, gap_copy, wait=True)
        tail_copies(wait=True)


def _dispatch(tables, h2, metar, n_rows):
    t = h2.shape[0]
    tm = TOKEN_TILE
    return pl.pallas_call(
        _dispatch_kernel,
        grid_spec=pltpu.PrefetchScalarGridSpec(
            num_scalar_prefetch=6,
            grid=(t // tm,),
            in_specs=[
                pl.BlockSpec((tm, D_MODEL), lambda i, *_: (i, 0)),
                pl.BlockSpec((META_ROWS, tm), lambda i, *_: (0, i)),
            ],
            out_specs=[pl.BlockSpec(memory_space=pl.ANY),
                       pl.BlockSpec((tm, LANES), lambda i, *_: (i, 0))],
            scratch_shapes=[
                pltpu.VMEM((2, TILE_ROWS, D_MODEL), F32),
                pltpu.VMEM((GRANULE << (GAP_BITS - 1), D_MODEL), F32),
                pltpu.SemaphoreType.DMA((2,)),
                pltpu.SemaphoreType.DMA,
            ],
        ),
        out_shape=[jax.ShapeDtypeStruct((n_rows, D_MODEL), F32),
                   jax.ShapeDtypeStruct((t, LANES), F32)],
        compiler_params=_params(("arbitrary",)),
        name="dispatch",
    )(*tables, h2, metar)


def _expert_kernel(be_ref, rb_ref, used_ref, first_ref, slot_ref, next_ref,
                   x_ref, wgu_hbm, bgu_ref, wd_hbm, bdn_ref, y_ref,
                   wgu_f32, wd_f32, wgu_bf, wd_bf, sems):
    i = pl.program_id(0)

    def fetch(expert, slot):
        return (pltpu.make_async_copy(wgu_hbm.at[expert], wgu_f32.at[slot], sems.at[slot, 0]),
                pltpu.make_async_copy(wd_hbm.at[expert], wd_f32.at[slot], sems.at[slot, 1]))

    @pl.when(i == 0)
    def _():
        for cp in fetch(be_ref[0], 0):
            cp.start()

    @pl.when(first_ref[i] == 1)
    def _():
        slot = slot_ref[i]
        for cp in fetch(be_ref[i], slot):
            cp.wait()
        wgu_bf[...] = wgu_f32[slot].astype(BF16)
        wd_bf[...] = wd_f32[slot].astype(BF16)

        @pl.when(next_ref[i] >= 0)
        def _():
            for cp in fetch(next_ref[i], 1 - slot):
                cp.start()

    def ffn(x):
        gu = _dotb(x, wgu_bf[...]) + bgu_ref[...]
        x_glu = jnp.minimum(gu[:, :D_EXPERT], SWIGLU_LIMIT)
        x_lin = jnp.clip(gu[:, D_EXPERT:], -SWIGLU_LIMIT, SWIGLU_LIMIT)
        act = x_glu * _sigmoid(SWIGLU_ALPHA * x_glu) * (x_lin + 1.0)
        return _dotb(act, wd_bf[...]) + bdn_ref[...]

    used = used_ref[i]
    half = x_ref.shape[0] // 2

    @pl.when(used > half)
    def _():
        y_ref[...] = ffn(x_ref[...])

    @pl.when((used > 0) & (used <= half))
    def _():
        y_ref[:half, :] = ffn(x_ref[:half, :])
        y_ref[half:, :] = jnp.zeros((half, y_ref.shape[1]), y_ref.dtype)

    @pl.when(used == 0)
    def _():
        y_ref[...] = jnp.zeros(y_ref.shape, y_ref.dtype)


def _experts(tables, xs, wgu, bgu, wd, bdn):
    rows = xs.shape[0]
    bm = EXPERT_ROWS
    nb = rows // bm
    return pl.pallas_call(
        _expert_kernel,
        grid_spec=pltpu.PrefetchScalarGridSpec(
            num_scalar_prefetch=6,
            grid=(nb,),
            in_specs=[
                pl.BlockSpec((bm, D_MODEL), lambda i, be, rb, *_: (rb[i], 0)),
                pl.BlockSpec(memory_space=pl.ANY),
                pl.BlockSpec((None, 1, 2 * D_EXPERT), lambda i, be, *_: (be[i], 0, 0)),
                pl.BlockSpec(memory_space=pl.ANY),
                pl.BlockSpec((None, 1, D_MODEL), lambda i, be, *_: (be[i], 0, 0)),
            ],
            out_specs=pl.BlockSpec((bm, D_MODEL), lambda i, *_: (i, 0)),
            scratch_shapes=[
                pltpu.VMEM((2, D_MODEL, 2 * D_EXPERT), F32),
                pltpu.VMEM((2, D_EXPERT, D_MODEL), F32),
                pltpu.VMEM((D_MODEL, 2 * D_EXPERT), BF16),
                pltpu.VMEM((D_EXPERT, D_MODEL), BF16),
                pltpu.SemaphoreType.DMA((2, 2)),
            ],
        ),
        out_shape=jax.ShapeDtypeStruct((rows, D_MODEL), F32),
        compiler_params=pltpu.CompilerParams(dimension_semantics=("arbitrary",),
                                             vmem_limit_bytes=EXPERT_VMEM_LIMIT),
        name="experts",
    )(*tables, xs, wgu, bgu, wd, bdn)


def _combine_kernel(tiles_a, seg_local_ref, seg_dst_ref, seg_n_ref, x1_ref, mc_ref, yb_ref, gf_ref,
                    ya_ref, yb_out_ref, ybuf, wsel, sems):
    i = pl.program_id(0)
    n_tiles = pl.num_programs(0)
    tm = x1_ref.shape[0]
    rt = ybuf.shape[1]
    slot = i % 2

    def seg_copy(buf):
        def make(rows, local, src):
            return pltpu.make_async_copy(yb_ref.at[pl.ds(src, rows)],
                                         ybuf.at[buf, pl.ds(local, rows)], sems.at[buf])
        return make

    @pl.when(i == 0)
    def _():
        ybuf[...] = jnp.zeros(ybuf.shape, ybuf.dtype)
        _segment_copies(i, seg_local_ref, seg_dst_ref, seg_n_ref, seg_copy(slot), wait=False)

    @pl.when(i + 1 < n_tiles)
    def _():
        _segment_copies(i + 1, seg_local_ref, seg_dst_ref, seg_n_ref, seg_copy(1 - slot), wait=False)

    dl = mc_ref[:, 0:TOP_K]
    gate = mc_ref[:, TOP_K:2 * TOP_K]
    rb, cb = SEL_CHUNK
    col = lax.broadcasted_iota(I32, (rb, cb), 1).astype(F32)
    for r0 in range(0, tm, rb):
        rows_k = [jnp.broadcast_to(dl[r0:r0 + rb, k:k + 1], (rb, cb)) for k in range(TOP_K)]
        gate_k = [jnp.broadcast_to(gate[r0:r0 + rb, k:k + 1], (rb, cb)) for k in range(TOP_K)]
        for c0 in range(0, rt, cb):
            rr = col + float(c0)
            w = jnp.where(rr == rows_k[0], gate_k[0], 0.0)
            for k in range(1, TOP_K):
                w = jnp.where(rr == rows_k[k], gate_k[k], w)
            wsel[r0:r0 + rb, c0:c0 + cb] = w.astype(BF16)
    _segment_copies(i, seg_local_ref, seg_dst_ref, seg_n_ref, seg_copy(slot), wait=True)
    acc = x1_ref[...] + jnp.dot(wsel[...], ybuf[slot].astype(BF16), preferred_element_type=F32)
    y = acc * lax.rsqrt(jnp.mean(acc * acc, axis=-1, keepdims=True) + RMS_EPS) * gf_ref[...]

    @pl.when(i < tiles_a)
    def _():
        ya_ref[...] = y

    @pl.when(i >= tiles_a)
    def _():
        yb_out_ref[...] = y


def _combine(tables, x1, slots, yb, gfin, n_a):
    t = x1.shape[0]
    tm = TOKEN_TILE
    tiles_a = n_a // tm
    return pl.pallas_call(
        functools.partial(_combine_kernel, tiles_a),
        grid_spec=pltpu.PrefetchScalarGridSpec(
            num_scalar_prefetch=3,
            grid=(t // tm,),
            in_specs=[
                pl.BlockSpec((tm, D_MODEL), lambda i, *_: (i, 0)),
                pl.BlockSpec((tm, LANES), lambda i, *_: (i, 0)),
                pl.BlockSpec(memory_space=pl.ANY),
                pl.BlockSpec((1, D_MODEL), lambda i, *_: (0, 0)),
            ],
            out_specs=[
                pl.BlockSpec((tm, D_MODEL), lambda i, *_: (jnp.minimum(i, tiles_a - 1), 0)),
                pl.BlockSpec((tm, D_MODEL), lambda i, *_: (jnp.maximum(i - tiles_a, 0), 0)),
            ],
            scratch_shapes=[pltpu.VMEM((2, TILE_ROWS, D_MODEL), F32),
                            pltpu.VMEM((tm, TILE_ROWS), BF16),
                            pltpu.SemaphoreType.DMA((2,))],
        ),
        out_shape=[jax.ShapeDtypeStruct((n_a, D_MODEL), F32),
                   jax.ShapeDtypeStruct((t - n_a, D_MODEL), F32)],
        compiler_params=pltpu.CompilerParams(dimension_semantics=("arbitrary",),
                                             vmem_limit_bytes=EXPERT_VMEM_LIMIT),
        name="combine",
    )(*tables, x1, slots, yb, gfin)


def kernel(x_prompt, x_sample, state_gdn, state_qkv_conv, state_dwconv, norm_mix_g, w_in, w_conv_qkv,
           a_log, dt_bias, w_onorm, b_glu, w_dw, b_dw, ln_g, ln_b, w_out, norm_ffn_g, w_router,
           b_router, w_gate_up, b_gate_up, w_down, b_down, norm_final_g):
    bp, tp, _ = x_prompt.shape
    bs, ts, _ = x_sample.shape
    n_p, n_s = bp * tp, bs * ts

    wi = w_in[0]
    c_z = QKV_COLS
    c_bd = c_z + GDN_WIDTH
    c_glu = c_bd + 2 * GDN_HEADS
    wqkv = wi[:, :c_z].astype(BF16)
    wz = wi[:, c_z:c_bd].astype(BF16)
    wbd = jnp.pad(wi[:, c_bd:c_glu], ((0, 0), (0, LANES - 2 * GDN_HEADS))).astype(BF16)
    wglu = wi[:, c_glu:].astype(BF16)
    bglu = b_glu[0][None, :]
    lane_pad = (GDN_HEADS, LANES - 2 * GDN_HEADS)
    pa = jnp.pad(-jnp.exp(a_log[0].astype(F32)), lane_pad)[None, :]
    pb = jnp.pad(dt_bias[0].astype(F32), lane_pad)[None, :]
    gmix = norm_mix_g[0][None, :]
    wconv = w_conv_qkv[0]
    wonorm = w_onorm[0][None, :]
    wdw, bdw = w_dw[0], b_dw[0][None, :]
    lng, lnb = ln_g[0][None, :], ln_b[0][None, :]
    wo = w_out[0][:GDN_WIDTH].astype(BF16)
    wc = w_out[0][GDN_WIDTH:].astype(BF16)
    gffn = norm_ffn_g[0][None, :]
    wr, br = w_router[0].T.astype(BF16), b_router[0].astype(F32)[:, None]
    bgu = b_gate_up[0][:, None, :]
    bdn = b_down[0][:, None, :]
    gfin = norm_final_g[None, :]

    xp = x_prompt.reshape(n_p, D_MODEL)
    xs_tok = x_sample.reshape(n_s, D_MODEL)

    def mixers(x2d, bsz, t, hist_qkv, s0, hist_glu, chunk, gdn_tile, gdn_rows, conv_tile):
        qkv, z, bd, bdt, uu = _inproj(x2d, gmix, wqkv, wz, wbd, wglu, bglu, pa, pb, chunk, t)
        qkv3 = qkv.reshape(bsz, t, QKV_COLS)
        uu3 = uu.reshape(bsz, t, CONF_WIDTH)
        if bdt.ndim == 2:
            bdt = bdt.reshape(2 * GDN_HEADS, bsz, t).transpose(1, 0, 2)
        o, s_new = _gdn(qkv3, z.reshape(bsz, t, GDN_WIDTH), bd.reshape(bsz, t, 2 * GDN_HEADS),
                        bdt, hist_qkv, s0, wconv, wonorm, chunk, gdn_tile, gdn_rows)
        cc = _cconv(uu3, hist_glu, wdw, bdw, lng, lnb, conv_tile)
        return o.reshape(bsz * t, GDN_WIDTH), cc.reshape(bsz * t, CONF_WIDTH), s_new, qkv3, uu3

    zero_qkv = jnp.zeros((bp, QKV_TAPS - 1, QKV_COLS), F32)
    zero_s = jnp.zeros((bp, GDN_HEADS, HEAD_DIM, HEAD_DIM), F32)
    zero_glu = jnp.zeros((bp, DW_TAPS - 1, CONF_WIDTH), F32)
    o_p, c_p, s_p, qkv_p, uu_p = mixers(xp, bp, tp, zero_qkv, zero_s, zero_glu,
                                        PROMPT_CHUNK, GDN_TILE, 1, CONV_TILE)
    o_s, c_s, s_s, qkv_s, uu_s = mixers(xs_tok, bs, ts, state_qkv_conv[0], state_gdn[0],
                                        state_dwconv[0], ts, ts, GDN_SAMPLE_ROWS, ts)

    n_tok = n_p + n_s
    x1, h2, metar, cnt_tile = _outproj((o_p, c_p, xp), (o_s, c_s, xs_tok), wo, wc, gffn, wr, br)

    bm = EXPERT_ROWS
    n_tiles = n_tok // TOKEN_TILE
    counts = cnt_tile[:, :, 0].astype(I32)
    seg_rows = (counts + GRANULE - 1) // GRANULE * GRANULE
    ids = jnp.arange(N_EXPERTS, dtype=I32)
    tile_ids = jnp.arange(n_tiles, dtype=I32)
    earlier_e = (ids[:, None] < ids[None, :]).astype(I32)
    earlier_t = (tile_ids[None, :] < tile_ids[:, None]).astype(I32)
    seg_local = jnp.sum(seg_rows[:, :, None] * earlier_e[None], axis=1)
    seg_before = jnp.sum(earlier_t[:, :, None] * seg_rows[None], axis=1)
    rows_e = jnp.sum(seg_rows, axis=0)
    padded = (rows_e + bm - 1) // bm * bm
    pstart = jnp.sum(padded[:, None] * earlier_e, axis=0)
    pend = pstart + padded
    seg_dst = pstart[None, :] + seg_before
    n_used = jnp.maximum(pend[-1] // bm, 1).astype(I32)
    max_rows = n_tok * TOP_K + n_tiles * N_EXPERTS * (GRANULE - 1) + N_EXPERTS * (bm - 1)
    n_blocks = -(-max_rows // bm)
    blk = jnp.minimum(jnp.arange(n_blocks, dtype=I32), n_used - 1)
    block_expert = jnp.minimum(
        jnp.sum((pend[None, :] <= (blk * bm)[:, None]).astype(I32), axis=1), N_EXPERTS - 1)
    seg_tables = (seg_local.reshape(-1).astype(I32), seg_dst.reshape(-1).astype(I32),
                  (seg_rows // GRANULE).reshape(-1).astype(I32))
    gap_tables = ((pstart + rows_e).astype(I32), ((padded - rows_e) // GRANULE).astype(I32))
    zero_rows = GRANULE << (GAP_BITS - 1)
    tail_table = jnp.stack([pend[-1], (n_blocks * bm - pend[-1]) // zero_rows]).astype(I32)

    xs, slots = _dispatch(seg_tables + gap_tables + (tail_table,), h2, metar, n_blocks * bm)
    active = padded > 0
    later = active[None, :] & (ids[None, :] > ids[:, None])
    next_active = jnp.min(jnp.where(later, ids[None, :], N_EXPERTS), axis=1)
    next_active = jnp.where(next_active == N_EXPERTS, -1, next_active)
    order = jnp.sum(active.astype(I32)[:, None] * earlier_e, axis=0)
    steps = jnp.arange(n_blocks, dtype=I32)
    of_block = (block_expert[:, None] == ids[None, :]).astype(I32)
    per_block = lambda table: jnp.sum(of_block * table[None, :], axis=1)
    first = ((steps * bm == per_block(pstart)) & (steps < n_used)).astype(I32)
    used = jnp.clip(per_block(pstart + rows_e) - steps * bm, 0, bm)
    used = jnp.where(steps < n_used, used, 0).astype(I32)
    expert_tables = (block_expert, blk, used, first,
                     (per_block(order) % 2).astype(I32), per_block(next_active).astype(I32))
    yb = _experts(expert_tables, xs, w_gate_up[0], bgu, w_down[0], bdn)
    y_p, y_s = _combine(seg_tables, x1, slots, yb, gfin, n_p)

    hist_glu_s = jnp.concatenate([state_dwconv[0], uu_s], axis=1)[:, -(DW_TAPS - 1):]
    return (y_p.reshape(bp, tp, D_MODEL),
            y_s.reshape(bs, ts, D_MODEL),
            s_p[None],
            qkv_p[:, -(QKV_TAPS - 1):][None],
            uu_p[:, -(DW_TAPS - 1):][None],
            s_s[None],
            qkv_s[:, -(QKV_TAPS - 1):][None],
            hist_glu_s[None])
```

```python
import functools
import math

import jax
import jax.numpy as jnp
from jax import lax
from jax.experimental import pallas as pl
from jax.experimental.pallas import tpu as pltpu

F32 = jnp.float32
BF16 = jnp.bfloat16
I32 = jnp.int32
HIGHEST = lax.Precision.HIGHEST

D_MODEL = 1024
GDN_HEADS = 4
HEAD_DIM = 128
GDN_WIDTH = GDN_HEADS * HEAD_DIM
QKV_COLS = 3 * GDN_WIDTH
CONF_WIDTH = 512
QKV_TAPS = 4
DW_TAPS = 31
N_EXPERTS = 32
TOP_K = 4
D_EXPERT = 1024
SWIGLU_LIMIT = 7.0
SWIGLU_ALPHA = 1.702
RMS_EPS = 1e-6
LN_EPS = 1e-5
L2_EPS = 1e-6
PROMPT_CHUNK = 64

LANES = 128
SUBLANES = 8
TOKEN_TILE = 512
GDN_TILE = 512
GDN_SAMPLE_ROWS = 4
CONV_TILE = 512
CONV_ROWS = 32
EXPERT_ROWS = 512
EXPERT_PARTS = 4
GRANULE = SUBLANES
GAP_BITS = 6
TILE_ROWS = -(-(TOKEN_TILE * TOP_K + N_EXPERTS * (GRANULE - 1)) // 256) * 256
META_EXPERT, META_RANK, META_GATE, META_ROWS = 0, TOP_K, 2 * TOP_K, 16
assert EXPERT_ROWS // GRANULE <= 1 << GAP_BITS
SEL_CHUNK = (64, 256)
VMEM_LIMIT = 48 * 1024 * 1024
EXPERT_VMEM_LIMIT = 56 * 1024 * 1024


def _dotb(a, b):
    return jnp.dot(a.astype(BF16), b.astype(BF16), preferred_element_type=F32)


def _dotf(a, b):
    return jnp.dot(a, b, precision=HIGHEST, preferred_element_type=F32)


def _sigmoid(x):
    return jax.nn.sigmoid(x)


def _params(sem):
    return pltpu.CompilerParams(dimension_semantics=sem, vmem_limit_bytes=VMEM_LIMIT)


def _inproj_kernel(x_ref, g_ref, wqkv_ref, wz_ref, wbd_ref, wglu_ref, bglu_ref, pa_ref, pb_ref,
                   tri_ref, qkv_ref, z_ref, bd_ref, bdt_ref, uu_ref):
    x = x_ref[...]
    tm = x.shape[0]
    ms = jnp.mean(x * x, axis=-1, keepdims=True)
    h = (x * lax.rsqrt(ms + RMS_EPS) * g_ref[...]).astype(BF16)
    qkv_ref[...] = jnp.dot(h, wqkv_ref[...], preferred_element_type=F32)
    z_ref[...] = jnp.dot(h, wz_ref[...], preferred_element_type=F32)
    glu = jnp.dot(h, wglu_ref[...], preferred_element_type=F32) + bglu_ref[...]
    uu_ref[...] = glu[:, :CONF_WIDTH] * _sigmoid(glu[:, CONF_WIDTH:])
    bd = jnp.dot(h, wbd_ref[...], preferred_element_type=F32)
    beta = _sigmoid(bd)
    v = bd + pb_ref[...]
    softplus = jnp.maximum(v, 0.0) + jnp.log1p(jnp.exp(-jnp.abs(v)))
    g = pa_ref[...] * softplus
    g_hi, g_rest = _split_bf16(g)
    g_mid = g - g_hi.astype(F32) - g_rest.astype(F32)
    pieces = jnp.concatenate([g_hi, g_rest, g_mid.astype(BF16)], axis=1)
    sums = jnp.dot(tri_ref[...], pieces, preferred_element_type=F32)
    gc = sums[:, :LANES] + sums[:, LANES:2 * LANES] + sums[:, 2 * LANES:]
    lane = lax.broadcasted_iota(I32, (tm, LANES), 1)
    res = jnp.where(lane < GDN_HEADS, beta, gc)
    bd_ref[...] = res[:, :2 * GDN_HEADS]
    bdt_ref[...] = res.T[:2 * GDN_HEADS, :]


def _inproj(x, g, wqkv, wz, wbd, wglu, bglu, pa, pb, chunk, t_batch):
    t = x.shape[0]
    tm = TOKEN_TILE
    tiles_per_batch = t_batch // tm if t_batch % tm == 0 else 0
    pos = jnp.arange(tm)
    tri = ((pos[:, None] >= pos[None, :])
           & (pos[:, None] // chunk == pos[None, :] // chunk)).astype(BF16)
    full = lambda shape: pl.BlockSpec(shape, lambda i: (0, 0))
    if tiles_per_batch:
        bdt_spec = pl.BlockSpec((None, 2 * GDN_HEADS, tm),
                                lambda i: (i // tiles_per_batch, 0, i % tiles_per_batch))
        bdt_shape = jax.ShapeDtypeStruct((t // t_batch, 2 * GDN_HEADS, t_batch), F32)
    else:
        bdt_spec = pl.BlockSpec((2 * GDN_HEADS, tm), lambda i: (0, i))
        bdt_shape = jax.ShapeDtypeStruct((2 * GDN_HEADS, t), F32)
    return pl.pallas_call(
        _inproj_kernel,
        grid=(t // tm,),
        in_specs=[
            pl.BlockSpec((tm, D_MODEL), lambda i: (i, 0)),
            full((1, D_MODEL)),
            full((D_MODEL, QKV_COLS)),
            full((D_MODEL, GDN_WIDTH)),
            full((D_MODEL, LANES)),
            full((D_MODEL, 2 * CONF_WIDTH)),
            full((1, 2 * CONF_WIDTH)),
            full((1, LANES)),
            full((1, LANES)),
            full((tm, tm)),
        ],
        out_specs=[
            pl.BlockSpec((tm, QKV_COLS), lambda i: (i, 0)),
            pl.BlockSpec((tm, GDN_WIDTH), lambda i: (i, 0)),
            pl.BlockSpec((tm, 2 * GDN_HEADS), lambda i: (i, 0)),
            bdt_spec,
            pl.BlockSpec((tm, CONF_WIDTH), lambda i: (i, 0)),
        ],
        out_shape=[
            jax.ShapeDtypeStruct((t, QKV_COLS), F32),
            jax.ShapeDtypeStruct((t, GDN_WIDTH), F32),
            jax.ShapeDtypeStruct((t, 2 * GDN_HEADS), F32),
            bdt_shape,
            jax.ShapeDtypeStruct((t, CONF_WIDTH), F32),
        ],
        compiler_params=_params(("arbitrary",)),
        name="inproj",
    )(x, g, wqkv, wz, wbd, wglu, bglu, pa, pb, tri)


def _split_bf16(a):
    hi = a.astype(BF16)
    lo = (a - hi.astype(F32)).astype(BF16)
    return hi, lo


def _unit_lower_inverses(ms, eye, n):
    levels = int(math.log2(n)) - 1
    ps = [eye - m for m in ms]
    pw = [(-m).astype(BF16) for m in ms]
    pw = [jnp.dot(p, p, preferred_element_type=F32).astype(BF16) for p in pw]
    for level in range(1, levels + 1):
        last = level == levels
        nxt = []
        for u, p in enumerate(pw):
            if last:
                ps[u] = ps[u] + jnp.dot(ps[u].astype(BF16), p, preferred_element_type=F32)
            else:
                out = jnp.dot(jnp.concatenate([p, ps[u].astype(BF16)], axis=0), p,
                              preferred_element_type=F32)
                nxt.append(out[:n].astype(BF16))
                ps[u] = ps[u] + out[n:]
        pw = nxt
    return ps


def _gdn_kernel(chunk, x_ref, z_ref, bd_ref, bdt_ref, hist_ref, s0_ref, wc_ref, won_ref,
                o_ref, s_ref, xbuf):
    i = pl.program_id(1)
    nb, tt = x_ref.shape[0], x_ref.shape[1]
    nchunk = tt // chunk
    pad = SUBLANES
    hist_rows = QKV_TAPS - 1

    @pl.when(i == 0)
    def _():
        xbuf[:, pad - hist_rows:pad, :] = hist_ref[...]
        s_ref[...] = s0_ref[...]

    xbuf[:, pad:pad + tt, :] = x_ref[...]

    ii = lax.broadcasted_iota(I32, (chunk, chunk), 0)
    jj = lax.broadcasted_iota(I32, (chunk, chunk), 1)
    lower_incl = ii >= jj
    lower_strict = ii > jj
    eye = (ii == jj).astype(F32)
    won = won_ref[...]

    def conv_silu(b, col0):
        cols = slice(col0, col0 + HEAD_DIM)
        acc = wc_ref[hist_rows:hist_rows + 1, cols] * xbuf[b, pad:pad + tt, cols]
        for j in range(hist_rows):
            r0 = pad - hist_rows + j
            acc = acc + wc_ref[j:j + 1, cols] * xbuf[b, r0:r0 + tt, cols]
        return acc * _sigmoid(acc)

    units = []
    for b in range(nb):
        bd = bd_ref[b]
        bdt = bdt_ref[b]
        for h in range(GDN_HEADS):
            q = conv_silu(b, h * HEAD_DIM)
            k = conv_silu(b, GDN_WIDTH + h * HEAD_DIM)
            v = conv_silu(b, 2 * GDN_WIDTH + h * HEAD_DIM)
            q = q * lax.rsqrt(jnp.sum(q * q, axis=-1, keepdims=True) + L2_EPS) * (HEAD_DIM ** -0.5)
            k = k * lax.rsqrt(jnp.sum(k * k, axis=-1, keepdims=True) + L2_EPS)
            kt = k.T
            beta = bd[:, h:h + 1]
            gcol = bd[:, GDN_HEADS + h:GDN_HEADS + h + 1]
            grow = bdt[GDN_HEADS + h:GDN_HEADS + h + 1, :]
            for c in range(nchunk):
                rows = slice(c * chunk, (c + 1) * chunk)
                qc, kc, vc, ktc = q[rows], k[rows], v[rows], kt[:, rows]
                bc, gc, gr = beta[rows], gcol[rows], grow[:, rows]
                dec = jnp.exp(jnp.where(lower_incl, gc - gr, -jnp.inf))
                kb = kc * bc
                eg = jnp.exp(gc)
                both = _dotb(jnp.concatenate([kb, qc], axis=0), ktc)
                units.append(dict(
                    b=b, h=h, c=c,
                    m=both[:chunk] * jnp.where(lower_strict, dec, 0.0),
                    qk=both[chunk:] * dec,
                    rhs=jnp.concatenate([vc * bc, kb * eg], axis=1),
                    qg=qc * eg,
                    kdt=ktc * jnp.exp(gr[:, chunk - 1:chunk] - gr),
                    d_last=jnp.exp(gc[chunk - 1:chunk, :]),
                ))

    tinvs = _unit_lower_inverses([u["m"] for u in units], eye, chunk)
    for u, tinv in zip(units, tinvs):
        sol = _dotb(tinv, u["rhs"])
        u["u"], u["w"] = sol[:, :HEAD_DIM], sol[:, HEAD_DIM:]

    state = {(b, h): s_ref[b, h] for b in range(nb) for h in range(GDN_HEADS)}
    by_key = {(u["b"], u["h"], u["c"]): u for u in units}
    for c in range(nchunk):
        rows = slice(c * chunk, (c + 1) * chunk)
        for b in range(nb):
            for h in range(GDN_HEADS):
                u = by_key[(b, h, c)]
                s = state[(b, h)]
                ws = _dotb(jnp.concatenate([u["w"], u["qg"]], axis=0), s)
                v_new = u["u"] - ws[:chunk]
                upd = _dotb(jnp.concatenate([u["qk"], u["kdt"]], axis=0), v_new)
                o_c = ws[chunk:] + upd[:chunk]
                state[(b, h)] = s * u["d_last"] + upd[chunk:]
                cols = slice(h * HEAD_DIM, (h + 1) * HEAD_DIM)
                on = o_c * lax.rsqrt(jnp.mean(o_c * o_c, axis=-1, keepdims=True) + RMS_EPS) * won
                zc = z_ref[b, rows, cols]
                o_ref[b, rows, cols] = on * (zc * _sigmoid(zc))
    for (b, h), s in state.items():
        s_ref[b, h] = s

    xbuf[:, pad - hist_rows:pad, :] = x_ref[:, tt - hist_rows:tt, :]


def _gdn(qkv, z, bd, bdt, hist, s0, wconv, wonorm, chunk, tt, nb):
    b, t, _ = qkv.shape
    nt = t // tt
    return pl.pallas_call(
        functools.partial(_gdn_kernel, chunk),
        grid=(b // nb, nt),
        in_specs=[
            pl.BlockSpec((nb, tt, QKV_COLS), lambda bi, i: (bi, i, 0)),
            pl.BlockSpec((nb, tt, GDN_WIDTH), lambda bi, i: (bi, i, 0)),
            pl.BlockSpec((nb, tt, 2 * GDN_HEADS), lambda bi, i: (bi, i, 0)),
            pl.BlockSpec((nb, 2 * GDN_HEADS, tt), lambda bi, i: (bi, 0, i)),
            pl.BlockSpec((nb, QKV_TAPS - 1, QKV_COLS), lambda bi, i: (bi, 0, 0)),
            pl.BlockSpec((nb, GDN_HEADS, HEAD_DIM, HEAD_DIM), lambda bi, i: (bi, 0, 0, 0)),
            pl.BlockSpec((QKV_TAPS, QKV_COLS), lambda bi, i: (0, 0)),
            pl.BlockSpec((1, HEAD_DIM), lambda bi, i: (0, 0)),
        ],
        out_specs=[
            pl.BlockSpec((nb, tt, GDN_WIDTH), lambda bi, i: (bi, i, 0)),
            pl.BlockSpec((nb, GDN_HEADS, HEAD_DIM, HEAD_DIM), lambda bi, i: (bi, 0, 0, 0)),
        ],
        out_shape=[
            jax.ShapeDtypeStruct((b, t, GDN_WIDTH), F32),
            jax.ShapeDtypeStruct((b, GDN_HEADS, HEAD_DIM, HEAD_DIM), F32),
        ],
        scratch_shapes=[pltpu.VMEM((nb, tt + SUBLANES, QKV_COLS), F32)],
        compiler_params=_params(("arbitrary", "arbitrary")),
        name="gdn",
    )(qkv, z, bd, bdt, hist, s0, wconv, wonorm)


def _cconv_kernel(carry, u_ref, hist_ref, w_ref, b_ref, lg_ref, lb_ref, c_ref, ubuf, shifted):
    i = pl.program_id(1)
    tt = u_ref.shape[0]
    hist_rows = DW_TAPS - 1
    pad = 4 * SUBLANES
    base = pad - hist_rows

    @pl.when(i == 0)
    def _():
        ubuf[base:pad, :] = hist_ref[...]

    ubuf[pad:pad + tt, :] = u_ref[...]
    span = shifted.shape[1]
    for s in range(1, SUBLANES):
        shifted[s - 1] = ubuf[s:s + span, :]

    def tap_rows(j, r0, rows):
        q, s = divmod(base + j, SUBLANES)
        start = q * SUBLANES + r0
        if s == 0:
            return ubuf[start:start + rows, :]
        return shifted[s - 1, start:start + rows, :]

    rows = min(CONV_ROWS, tt)
    for r0 in range(0, tt, rows):
        acc = w_ref[0:1, :] * tap_rows(0, r0, rows)
        for j in range(1, DW_TAPS):
            acc = acc + w_ref[j:j + 1, :] * tap_rows(j, r0, rows)
        cv = acc + b_ref[...]
        mu = jnp.mean(cv, axis=-1, keepdims=True)
        xc = cv - mu
        var = jnp.mean(xc * xc, axis=-1, keepdims=True)
        y = xc * lax.rsqrt(var + LN_EPS) * lg_ref[...] + lb_ref[...]
        c_ref[r0:r0 + rows, :] = y * _sigmoid(y)
    if carry:
        ubuf[base:pad, :] = u_ref[tt - hist_rows:tt, :]


def _cconv(uu, hist, w, b, lg, lb, tt):
    bsz, t, _ = uu.shape
    nt = t // tt
    vec = pl.BlockSpec((1, CONF_WIDTH), lambda bi, i: (0, 0))
    return pl.pallas_call(
        functools.partial(_cconv_kernel, nt > 1),
        grid=(bsz, nt),
        in_specs=[
            pl.BlockSpec((None, tt, CONF_WIDTH), lambda bi, i: (bi, i, 0)),
            pl.BlockSpec((None, DW_TAPS - 1, CONF_WIDTH), lambda bi, i: (bi, 0, 0)),
            pl.BlockSpec((DW_TAPS, CONF_WIDTH), lambda bi, i: (0, 0)),
            vec, vec, vec,
        ],
        out_specs=pl.BlockSpec((None, tt, CONF_WIDTH), lambda bi, i: (bi, i, 0)),
        out_shape=jax.ShapeDtypeStruct((bsz, t, CONF_WIDTH), F32),
        scratch_shapes=[pltpu.VMEM((tt + 4 * SUBLANES, CONF_WIDTH), F32),
                        pltpu.VMEM((SUBLANES - 1, tt + 3 * SUBLANES, CONF_WIDTH), F32)],
        compiler_params=_params(("arbitrary", "arbitrary")),
        name="cconv",
    )(uu, hist, w, b, lg, lb)


def _outproj_kernel(tiles_a, oa_ref, ca_ref, xa_ref, ob_ref, cb_ref, xb_ref, *rest):
    i = pl.program_id(0)

    @pl.when(i < tiles_a)
    def _():
        _outproj_tile(oa_ref, ca_ref, xa_ref, *rest)

    @pl.when(i >= tiles_a)
    def _():
        _outproj_tile(ob_ref, cb_ref, xb_ref, *rest)


def _outproj_tile(o_ref, c_ref, x_ref, wo_ref, wc_ref, g_ref, wr_ref, br_ref, upper_ref,
                  x1_ref, h2_ref, metar_ref, cnt_ref):
    tm = x_ref.shape[0]

    mix = _dotb(o_ref[...], wo_ref[...]) + _dotb(c_ref[...], wc_ref[...])
    x1 = x_ref[...] + mix
    x1_ref[...] = x1
    h2 = x1 * lax.rsqrt(jnp.mean(x1 * x1, axis=-1, keepdims=True) + RMS_EPS) * g_ref[...]
    h2_ref[...] = h2.astype(BF16)
    logits = lax.dot_general(wr_ref[...], h2.astype(BF16), (((1,), (1,)), ((), ())),
                             preferred_element_type=F32) + br_ref[...]
    row = lax.broadcasted_iota(I32, (N_EXPERTS, tm), 0).astype(F32)
    vals = logits
    idxs, tops = [], []
    for _ in range(TOP_K):
        m = jnp.max(vals, axis=0, keepdims=True)
        idx = jnp.min(jnp.where(vals == m, row, float(N_EXPERTS)), axis=0, keepdims=True)
        idxs.append(idx)
        tops.append(m)
        vals = jnp.where(row == idx, -jnp.inf, vals)
    exps = [jnp.exp(v - tops[0]) for v in tops]
    den = exps[0] + exps[1] + exps[2] + exps[3]
    onehots = [(row == idx).astype(F32) for idx in idxs]
    chosen = onehots[0] + onehots[1] + onehots[2] + onehots[3]
    before = jnp.dot(chosen.astype(BF16), upper_ref[...], preferred_element_type=F32)
    ranks = [jnp.sum(oh * before, axis=0, keepdims=True) for oh in onehots]
    gates = [e / den for e in exps]
    metar_ref[...] = jnp.concatenate(
        idxs + ranks + gates + [jnp.zeros((META_ROWS - 3 * TOP_K, tm), F32)], axis=0)
    cnt_ref[...] = jnp.broadcast_to(jnp.sum(chosen, axis=1, keepdims=True), (N_EXPERTS, LANES))[None]


def _outproj(first, second, wo, wc, g, wr, br):
    t_a, t_b = first[2].shape[0], second[2].shape[0]
    tm = TOKEN_TILE
    upper = (jnp.arange(tm)[:, None] < jnp.arange(tm)[None, :]).astype(BF16)
    tiles_a = t_a // tm
    t_total = t_a + t_b
    full = lambda shape: pl.BlockSpec(shape, lambda i: (0, 0))
    rows_a = lambda width: pl.BlockSpec((tm, width), lambda i: (jnp.minimum(i, tiles_a - 1), 0))
    rows_b = lambda width: pl.BlockSpec((tm, width), lambda i: (jnp.maximum(i - tiles_a, 0), 0))
    widths = (GDN_WIDTH, CONF_WIDTH, D_MODEL)
    return pl.pallas_call(
        functools.partial(_outproj_kernel, tiles_a),
        grid=(t_total // tm,),
        in_specs=[rows_a(w) for w in widths] + [rows_b(w) for w in widths] + [
            full((GDN_WIDTH, D_MODEL)),
            full((CONF_WIDTH, D_MODEL)),
            full((1, D_MODEL)),
            full((N_EXPERTS, D_MODEL)),
            full((N_EXPERTS, 1)),
            full((tm, tm)),
        ],
        out_specs=[
            pl.BlockSpec((tm, D_MODEL), lambda i: (i, 0)),
            pl.BlockSpec((tm, D_MODEL), lambda i: (i, 0)),
            pl.BlockSpec((META_ROWS, tm), lambda i: (0, i)),
            pl.BlockSpec((1, N_EXPERTS, LANES), lambda i: (i, 0, 0)),
        ],
        out_shape=[
            jax.ShapeDtypeStruct((t_total, D_MODEL), F32),
            jax.ShapeDtypeStruct((t_total, D_MODEL), BF16),
            jax.ShapeDtypeStruct((META_ROWS, t_total), F32),
            jax.ShapeDtypeStruct((t_total // tm, N_EXPERTS, LANES), F32),
        ],
        compiler_params=_params(("arbitrary",)),
        name="outproj_router",
    )(*first, *second, wo, wc, g, wr, br, upper)


def _segment_copies(i, seg_local_ref, seg_dst_ref, seg_n_ref, make_copy, wait):
    def copy(rows, local, dst):
        cp = make_copy(rows, pl.multiple_of(local, GRANULE), pl.multiple_of(dst, GRANULE))
        cp.wait() if wait else cp.start()

    def per_expert(ex, carry):
        j = i * N_EXPERTS + ex
        local, dst, n = seg_local_ref[j], seg_dst_ref[j], seg_n_ref[j]
        big = 8 * GRANULE

        def body(g, c):
            copy(big, local + g * big, dst + g * big)
            return c

        lax.fori_loop(0, n >> 3, body, 0)
        off = (n >> 3) * big
        for bit in (2, 1, 0):
            rows = GRANULE << bit
            has = (n >> bit) & 1

            @pl.when(has == 1)
            def _(rows=rows, off=off):
                copy(rows, local + off, dst + off)

            off = off + has * rows
        return carry

    lax.fori_loop(0, N_EXPERTS, per_expert, 0)


def _gap_copies(gap_start_ref, gap_n_ref, make_copy, wait):
    def per_expert(ex, carry):
        n = gap_n_ref[ex]
        start = pl.multiple_of(gap_start_ref[ex], GRANULE)
        off = 0 * n
        for bit in reversed(range(GAP_BITS)):
            rows = GRANULE << bit

            @pl.when(((n >> bit) & 1) == 1)
            def _(rows=rows, off=off):
                cp = make_copy(rows, pl.multiple_of(start + off, GRANULE))
                cp.wait() if wait else cp.start()

            off = off + ((n >> bit) & 1) * rows
        return carry

    lax.fori_loop(0, N_EXPERTS, per_expert, 0)


def _local_rows(meta_e, seg_local_ref, i):
    base = jnp.zeros(meta_e.shape, F32)
    for ex in range(N_EXPERTS):
        base = jnp.where(meta_e == float(ex), seg_local_ref[i * N_EXPERTS + ex].astype(F32), base)
    return base


def _dispatch_kernel(seg_local_ref, seg_dst_ref, seg_n_ref, gap_start_ref, gap_n_ref, tail_ref,
                     h_ref, mr_ref, xs_ref, slots_ref, xloc, zbuf, sems, zsem):
    i = pl.program_id(0)
    last = pl.num_programs(0) - 1
    tm = h_ref.shape[0]
    rt = xloc.shape[1]
    zrows = zbuf.shape[0]
    slot = i % 2

    def gap_copy(rows, dst):
        return pltpu.make_async_copy(zbuf.at[pl.ds(0, rows)], xs_ref.at[pl.ds(dst, rows)], zsem)

    def tail_copies(wait):
        def body(j, c):
            cp = gap_copy(zrows, pl.multiple_of(tail_ref[0] + j * zrows, zrows))
            cp.wait() if wait else cp.start()
            return c
        lax.fori_loop(0, tail_ref[1], body, 0)

    @pl.when(i == 0)
    def _():
        zbuf[...] = jnp.zeros(zbuf.shape, zbuf.dtype)
        _gap_copies(gap_start_ref, gap_n_ref, gap_copy, wait=False)
        tail_copies(wait=False)

    mr = mr_ref[...]
    dl = (_local_rows(mr[META_EXPERT:META_EXPERT + TOP_K], seg_local_ref, i)
          + mr[META_RANK:META_RANK + TOP_K])
    slots_ref[...] = jnp.concatenate(
        [dl, mr[META_GATE:META_GATE + TOP_K], jnp.zeros((LANES - 2 * TOP_K, tm), F32)], axis=0).T
    rr = lax.broadcasted_iota(I32, (rt, tm), 0).astype(F32)
    hit = rr == dl[0:1]
    for k in range(1, TOP_K):
        hit = hit | (rr == dl[k:k + 1])
    xloc[slot] = jnp.dot(hit.astype(BF16), h_ref[...], preferred_element_type=F32)

    def seg_copy(buf):
        def make(rows, local, dst):
            return pltpu.make_async_copy(xloc.at[buf, pl.ds(local, rows)],
                                         xs_ref.at[pl.ds(dst, rows)], sems.at[buf])
        return make

    _segment_copies(i, seg_local_ref, seg_dst_ref, seg_n_ref, seg_copy(slot), wait=False)

    @pl.when(i > 0)
    def _():
        _segment_copies(i - 1, seg_local_ref, seg_dst_ref, seg_n_ref, seg_copy(1 - slot), wait=True)

    @pl.when(i == last)
    def _():
        _segment_copies(i, seg_local_ref, seg_dst_ref, seg_n_ref, seg_copy(slot), wait=True)

    @pl.when(i == 0)
    def _():
        _gap_copies(gap_start_ref, gap_n_ref, gap_copy, wait=True)
        tail_copies(wait=True)


def _dispatch(tables, h2, metar, n_rows):
    t = h2.shape[0]
    tm = TOKEN_TILE
    return pl.pallas_call(
        _dispatch_kernel,
        grid_spec=pltpu.PrefetchScalarGridSpec(
            num_scalar_prefetch=6,
            grid=(t // tm,),
            in_specs=[
                pl.BlockSpec((tm, D_MODEL), lambda i, *_: (i, 0)),
                pl.BlockSpec((META_ROWS, tm), lambda i, *_: (0, i)),
            ],
            out_specs=[pl.BlockSpec(memory_space=pl.ANY),
                       pl.BlockSpec((tm, LANES), lambda i, *_: (i, 0))],
            scratch_shapes=[
                pltpu.VMEM((2, TILE_ROWS, D_MODEL), F32),
                pltpu.VMEM((GRANULE << (GAP_BITS - 1), D_MODEL), F32),
                pltpu.SemaphoreType.DMA((2,)),
                pltpu.SemaphoreType.DMA,
            ],
        ),
        out_shape=[jax.ShapeDtypeStruct((n_rows, D_MODEL), F32),
                   jax.ShapeDtypeStruct((t, LANES), F32)],
        compiler_params=_params(("arbitrary",)),
        name="dispatch",
    )(*tables, h2, metar)


def _expert_kernel(be_ref, rb_ref, used_ref, first_ref, slot_ref, next_ref,
                   x_ref, wgu_hbm, bgu_ref, wd_hbm, bdn_ref, y_ref,
                   wgu_f32, wd_f32, wgu_bf, wd_bf, sems):
    i = pl.program_id(0)

    def fetch(expert, slot):
        return (pltpu.make_async_copy(wgu_hbm.at[expert], wgu_f32.at[slot], sems.at[slot, 0]),
                pltpu.make_async_copy(wd_hbm.at[expert], wd_f32.at[slot], sems.at[slot, 1]))

    @pl.when(i == 0)
    def _():
        for cp in fetch(be_ref[0], 0):
            cp.start()

    @pl.when(first_ref[i] == 1)
    def _():
        slot = slot_ref[i]
        for cp in fetch(be_ref[i], slot):
            cp.wait()
        wgu_bf[...] = wgu_f32[slot].astype(BF16)
        wd_bf[...] = wd_f32[slot].astype(BF16)

        @pl.when(next_ref[i] >= 0)
        def _():
            for cp in fetch(next_ref[i], 1 - slot):
                cp.start()

    def ffn(x):
        gu = _dotb(x, wgu_bf[...]) + bgu_ref[...]
        x_glu = jnp.minimum(gu[:, :D_EXPERT], SWIGLU_LIMIT)
        x_lin = jnp.clip(gu[:, D_EXPERT:], -SWIGLU_LIMIT, SWIGLU_LIMIT)
        act = x_glu * _sigmoid(SWIGLU_ALPHA * x_glu) * (x_lin + 1.0)
        return _dotb(act, wd_bf[...]) + bdn_ref[...]

    bm = x_ref.shape[0]
    quarter = bm // EXPERT_PARTS
    parts = (used_ref[i] + quarter - 1) // quarter
    for p in range(EXPERT_PARTS + 1):
        @pl.when(parts == p)
        def _(rows=p * quarter):
            if rows:
                y_ref[:rows, :] = ffn(x_ref[:rows, :])
            if rows < bm:
                y_ref[rows:, :] = jnp.zeros((bm - rows, y_ref.shape[1]), y_ref.dtype)


def _experts(tables, xs, wgu, bgu, wd, bdn):
    rows = xs.shape[0]
    bm = EXPERT_ROWS
    nb = rows // bm
    return pl.pallas_call(
        _expert_kernel,
        grid_spec=pltpu.PrefetchScalarGridSpec(
            num_scalar_prefetch=6,
            grid=(nb,),
            in_specs=[
                pl.BlockSpec((bm, D_MODEL), lambda i, be, rb, *_: (rb[i], 0)),
                pl.BlockSpec(memory_space=pl.ANY),
                pl.BlockSpec((None, 1, 2 * D_EXPERT), lambda i, be, *_: (be[i], 0, 0)),
                pl.BlockSpec(memory_space=pl.ANY),
                pl.BlockSpec((None, 1, D_MODEL), lambda i, be, *_: (be[i], 0, 0)),
            ],
            out_specs=pl.BlockSpec((bm, D_MODEL), lambda i, *_: (i, 0)),
            scratch_shapes=[
                pltpu.VMEM((2, D_MODEL, 2 * D_EXPERT), F32),
                pltpu.VMEM((2, D_EXPERT, D_MODEL), F32),
                pltpu.VMEM((D_MODEL, 2 * D_EXPERT), BF16),
                pltpu.VMEM((D_EXPERT, D_MODEL), BF16),
                pltpu.SemaphoreType.DMA((2, 2)),
            ],
        ),
        out_shape=jax.ShapeDtypeStruct((rows, D_MODEL), F32),
        compiler_params=pltpu.CompilerParams(dimension_semantics=("arbitrary",),
                                             vmem_limit_bytes=EXPERT_VMEM_LIMIT),
        name="experts",
    )(*tables, xs, wgu, bgu, wd, bdn)


def _combine_kernel(tiles_a, seg_local_ref, seg_dst_ref, seg_n_ref, x1_ref, mc_ref, yb_ref, gf_ref,
                    ya_ref, yb_out_ref, ybuf, wsel, sems):
    i = pl.program_id(0)
    n_tiles = pl.num_programs(0)
    tm = x1_ref.shape[0]
    rt = ybuf.shape[1]
    slot = i % 2

    def seg_copy(buf):
        def make(rows, local, src):
            return pltpu.make_async_copy(yb_ref.at[pl.ds(src, rows)],
                                         ybuf.at[buf, pl.ds(local, rows)], sems.at[buf])
        return make

    @pl.when(i == 0)
    def _():
        ybuf[...] = jnp.zeros(ybuf.shape, ybuf.dtype)
        _segment_copies(i, seg_local_ref, seg_dst_ref, seg_n_ref, seg_copy(slot), wait=False)

    @pl.when(i + 1 < n_tiles)
    def _():
        _segment_copies(i + 1, seg_local_ref, seg_dst_ref, seg_n_ref, seg_copy(1 - slot), wait=False)

    dl = mc_ref[:, 0:TOP_K]
    gate = mc_ref[:, TOP_K:2 * TOP_K]
    rb, cb = SEL_CHUNK
    col = lax.broadcasted_iota(I32, (rb, cb), 1).astype(F32)
    for r0 in range(0, tm, rb):
        rows_k = [jnp.broadcast_to(dl[r0:r0 + rb, k:k + 1], (rb, cb)) for k in range(TOP_K)]
        gate_k = [jnp.broadcast_to(gate[r0:r0 + rb, k:k + 1], (rb, cb)) for k in range(TOP_K)]
        for c0 in range(0, rt, cb):
            rr = col + float(c0)
            w = jnp.where(rr == rows_k[0], gate_k[0], 0.0)
            for k in range(1, TOP_K):
                w = jnp.where(rr == rows_k[k], gate_k[k], w)
            wsel[r0:r0 + rb, c0:c0 + cb] = w.astype(BF16)
    _segment_copies(i, seg_local_ref, seg_dst_ref, seg_n_ref, seg_copy(slot), wait=True)
    acc = x1_ref[...] + jnp.dot(wsel[...], ybuf[slot].astype(BF16), preferred_element_type=F32)
    y = acc * lax.rsqrt(jnp.mean(acc * acc, axis=-1, keepdims=True) + RMS_EPS) * gf_ref[...]

    @pl.when(i < tiles_a)
    def _():
        ya_ref[...] = y

    @pl.when(i >= tiles_a)
    def _():
        yb_out_ref[...] = y


def _combine(tables, x1, slots, yb, gfin, n_a):
    t = x1.shape[0]
    tm = TOKEN_TILE
    tiles_a = n_a // tm
    return pl.pallas_call(
        functools.partial(_combine_kernel, tiles_a),
        grid_spec=pltpu.PrefetchScalarGridSpec(
            num_scalar_prefetch=3,
            grid=(t // tm,),
            in_specs=[
                pl.BlockSpec((tm, D_MODEL), lambda i, *_: (i, 0)),
                pl.BlockSpec((tm, LANES), lambda i, *_: (i, 0)),
                pl.BlockSpec(memory_space=pl.ANY),
                pl.BlockSpec((1, D_MODEL), lambda i, *_: (0, 0)),
            ],
            out_specs=[
                pl.BlockSpec((tm, D_MODEL), lambda i, *_: (jnp.minimum(i, tiles_a - 1), 0)),
                pl.BlockSpec((tm, D_MODEL), lambda i, *_: (jnp.maximum(i - tiles_a, 0), 0)),
            ],
            scratch_shapes=[pltpu.VMEM((2, TILE_ROWS, D_MODEL), F32),
                            pltpu.VMEM((tm, TILE_ROWS), BF16),
                            pltpu.SemaphoreType.DMA((2,))],
        ),
        out_shape=[jax.ShapeDtypeStruct((n_a, D_MODEL), F32),
                   jax.ShapeDtypeStruct((t - n_a, D_MODEL), F32)],
        compiler_params=pltpu.CompilerParams(dimension_semantics=("arbitrary",),
                                             vmem_limit_bytes=EXPERT_VMEM_LIMIT),
        name="combine",
    )(*tables, x1, slots, yb, gfin)


def kernel(x_prompt, x_sample, state_gdn, state_qkv_conv, state_dwconv, norm_mix_g, w_in, w_conv_qkv,
           a_log, dt_bias, w_onorm, b_glu, w_dw, b_dw, ln_g, ln_b, w_out, norm_ffn_g, w_router,
           b_router, w_gate_up, b_gate_up, w_down, b_down, norm_final_g):
    bp, tp, _ = x_prompt.shape
    bs, ts, _ = x_sample.shape
    n_p, n_s = bp * tp, bs * ts

    wi = w_in[0]
    c_z = QKV_COLS
    c_bd = c_z + GDN_WIDTH
    c_glu = c_bd + 2 * GDN_HEADS
    wqkv = wi[:, :c_z].astype(BF16)
    wz = wi[:, c_z:c_bd].astype(BF16)
    wbd = jnp.pad(wi[:, c_bd:c_glu], ((0, 0), (0, LANES - 2 * GDN_HEADS))).astype(BF16)
    wglu = wi[:, c_glu:].astype(BF16)
    bglu = b_glu[0][None, :]
    lane_pad = (GDN_HEADS, LANES - 2 * GDN_HEADS)
    pa = jnp.pad(-jnp.exp(a_log[0].astype(F32)), lane_pad)[None, :]
    pb = jnp.pad(dt_bias[0].astype(F32), lane_pad)[None, :]
    gmix = norm_mix_g[0][None, :]
    wconv = w_conv_qkv[0]
    wonorm = w_onorm[0][None, :]
    wdw, bdw = w_dw[0], b_dw[0][None, :]
    lng, lnb = ln_g[0][None, :], ln_b[0][None, :]
    wo = w_out[0][:GDN_WIDTH].astype(BF16)
    wc = w_out[0][GDN_WIDTH:].astype(BF16)
    gffn = norm_ffn_g[0][None, :]
    wr, br = w_router[0].T.astype(BF16), b_router[0].astype(F32)[:, None]
    bgu = b_gate_up[0][:, None, :]
    bdn = b_down[0][:, None, :]
    gfin = norm_final_g[None, :]

    xp = x_prompt.reshape(n_p, D_MODEL)
    xs_tok = x_sample.reshape(n_s, D_MODEL)

    def mixers(x2d, bsz, t, hist_qkv, s0, hist_glu, chunk, gdn_tile, gdn_rows, conv_tile):
        qkv, z, bd, bdt, uu = _inproj(x2d, gmix, wqkv, wz, wbd, wglu, bglu, pa, pb, chunk, t)
        qkv3 = qkv.reshape(bsz, t, QKV_COLS)
        uu3 = uu.reshape(bsz, t, CONF_WIDTH)
        if bdt.ndim == 2:
            bdt = bdt.reshape(2 * GDN_HEADS, bsz, t).transpose(1, 0, 2)
        o, s_new = _gdn(qkv3, z.reshape(bsz, t, GDN_WIDTH), bd.reshape(bsz, t, 2 * GDN_HEADS),
                        bdt, hist_qkv, s0, wconv, wonorm, chunk, gdn_tile, gdn_rows)
        cc = _cconv(uu3, hist_glu, wdw, bdw, lng, lnb, conv_tile)
        return o.reshape(bsz * t, GDN_WIDTH), cc.reshape(bsz * t, CONF_WIDTH), s_new, qkv3, uu3

    zero_qkv = jnp.zeros((bp, QKV_TAPS - 1, QKV_COLS), F32)
    zero_s = jnp.zeros((bp, GDN_HEADS, HEAD_DIM, HEAD_DIM), F32)
    zero_glu = jnp.zeros((bp, DW_TAPS - 1, CONF_WIDTH), F32)
    o_p, c_p, s_p, qkv_p, uu_p = mixers(xp, bp, tp, zero_qkv, zero_s, zero_glu,
                                        PROMPT_CHUNK, GDN_TILE, 1, CONV_TILE)
    o_s, c_s, s_s, qkv_s, uu_s = mixers(xs_tok, bs, ts, state_qkv_conv[0], state_gdn[0],
                                        state_dwconv[0], ts, ts, GDN_SAMPLE_ROWS, ts)

    n_tok = n_p + n_s
    x1, h2, metar, cnt_tile = _outproj((o_p, c_p, xp), (o_s, c_s, xs_tok), wo, wc, gffn, wr, br)

    bm = EXPERT_ROWS
    n_tiles = n_tok // TOKEN_TILE
    counts = cnt_tile[:, :, 0].astype(I32)
    seg_rows = (counts + GRANULE - 1) // GRANULE * GRANULE
    ids = jnp.arange(N_EXPERTS, dtype=I32)
    tile_ids = jnp.arange(n_tiles, dtype=I32)
    earlier_e = (ids[:, None] < ids[None, :]).astype(I32)
    earlier_t = (tile_ids[None, :] < tile_ids[:, None]).astype(I32)
    seg_local = jnp.sum(seg_rows[:, :, None] * earlier_e[None], axis=1)
    seg_before = jnp.sum(earlier_t[:, :, None] * seg_rows[None], axis=1)
    rows_e = jnp.sum(seg_rows, axis=0)
    padded = (rows_e + bm - 1) // bm * bm
    pstart = jnp.sum(padded[:, None] * earlier_e, axis=0)
    pend = pstart + padded
    seg_dst = pstart[None, :] + seg_before
    n_used = jnp.maximum(pend[-1] // bm, 1).astype(I32)
    max_rows = n_tok * TOP_K + n_tiles * N_EXPERTS * (GRANULE - 1) + N_EXPERTS * (bm - 1)
    n_blocks = -(-max_rows // bm)
    blk = jnp.minimum(jnp.arange(n_blocks, dtype=I32), n_used - 1)
    block_expert = jnp.minimum(
        jnp.sum((pend[None, :] <= (blk * bm)[:, None]).astype(I32), axis=1), N_EXPERTS - 1)
    seg_tables = (seg_local.reshape(-1).astype(I32), seg_dst.reshape(-1).astype(I32),
                  (seg_rows // GRANULE).reshape(-1).astype(I32))
    gap_tables = ((pstart + rows_e).astype(I32), ((padded - rows_e) // GRANULE).astype(I32))
    zero_rows = GRANULE << (GAP_BITS - 1)
    tail_table = jnp.stack([pend[-1], (n_blocks * bm - pend[-1]) // zero_rows]).astype(I32)

    xs, slots = _dispatch(seg_tables + gap_tables + (tail_table,), h2, metar, n_blocks * bm)
    active = padded > 0
    later = active[None, :] & (ids[None, :] > ids[:, None])
    next_active = jnp.min(jnp.where(later, ids[None, :], N_EXPERTS), axis=1)
    next_active = jnp.where(next_active == N_EXPERTS, -1, next_active)
    order = jnp.sum(active.astype(I32)[:, None] * earlier_e, axis=0)
    steps = jnp.arange(n_blocks, dtype=I32)
    of_block = (block_expert[:, None] == ids[None, :]).astype(I32)
    per_block = lambda table: jnp.sum(of_block * table[None, :], axis=1)
    first = ((steps * bm == per_block(pstart)) & (steps < n_used)).astype(I32)
    used = jnp.clip(per_block(pstart + rows_e) - steps * bm, 0, bm)
    used = jnp.where(steps < n_used, used, 0).astype(I32)
    expert_tables = (block_expert, blk, used, first,
                     (per_block(order) % 2).astype(I32), per_block(next_active).astype(I32))
    yb = _experts(expert_tables, xs, w_gate_up[0], bgu, w_down[0], bdn)
    y_p, y_s = _combine(seg_tables, x1, slots, yb, gfin, n_p)

    hist_glu_s = jnp.concatenate([state_dwconv[0], uu_s], axis=1)[:, -(DW_TAPS - 1):]
    return (y_p.reshape(bp, tp, D_MODEL),
            y_s.reshape(bs, ts, D_MODEL),
            s_p[None],
            qkv_p[:, -(QKV_TAPS - 1):][None],
            uu_p[:, -(DW_TAPS - 1):][None],
            s_s[None],
            qkv_s[:, -(QKV_TAPS - 1):][None],
            hist_glu_s[None])
```

```python
import functools
import math

import jax
import jax.numpy as jnp
from jax import lax
from jax.experimental import pallas as pl
from jax.experimental.pallas import tpu as pltpu

F32 = jnp.float32
BF16 = jnp.bfloat16
I32 = jnp.int32
HIGHEST = lax.Precision.HIGHEST

D_MODEL = 1024
GDN_HEADS = 4
HEAD_DIM = 128
GDN_WIDTH = GDN_HEADS * HEAD_DIM
QKV_COLS = 3 * GDN_WIDTH
CONF_WIDTH = 512
QKV_TAPS = 4
DW_TAPS = 31
N_EXPERTS = 32
TOP_K = 4
D_EXPERT = 1024
SWIGLU_LIMIT = 7.0
SWIGLU_ALPHA = 1.702
RMS_EPS = 1e-6
LN_EPS = 1e-5
L2_EPS = 1e-6
PROMPT_CHUNK = 64

LANES = 128
SUBLANES = 8
TOKEN_TILE = 512
GDN_TILE = 512
GDN_SAMPLE_ROWS = 4
CONV_TILE = 512
CONV_ROWS = 32
EXPERT_ROWS = 512
GRANULE = SUBLANES
GAP_BITS = 6
TILE_ROWS = -(-(TOKEN_TILE * TOP_K + N_EXPERTS * (GRANULE - 1)) // 256) * 256
META_EXPERT, META_RANK, META_GATE, META_ROWS = 0, TOP_K, 2 * TOP_K, 16
assert EXPERT_ROWS // GRANULE <= 1 << GAP_BITS
SEL_CHUNK = (64, 256)
VMEM_LIMIT = 48 * 1024 * 1024
EXPERT_VMEM_LIMIT = 56 * 1024 * 1024


def _dotb(a, b):
    return jnp.dot(a.astype(BF16), b.astype(BF16), preferred_element_type=F32)


def _dotf(a, b):
    return jnp.dot(a, b, precision=HIGHEST, preferred_element_type=F32)


def _sigmoid(x):
    return jax.nn.sigmoid(x)


def _params(sem):
    return pltpu.CompilerParams(dimension_semantics=sem, vmem_limit_bytes=VMEM_LIMIT)


def _inproj_kernel(x_ref, g_ref, wqkv_ref, wz_ref, wbd_ref, wglu_ref, bglu_ref, pa_ref, pb_ref,
                   tri_ref, qkv_ref, z_ref, bd_ref, bdt_ref, uu_ref):
    x = x_ref[...]
    tm = x.shape[0]
    ms = jnp.mean(x * x, axis=-1, keepdims=True)
    h = (x * lax.rsqrt(ms + RMS_EPS) * g_ref[...]).astype(BF16)
    qkv_ref[...] = jnp.dot(h, wqkv_ref[...], preferred_element_type=F32)
    z_ref[...] = jnp.dot(h, wz_ref[...], preferred_element_type=F32)
    glu = jnp.dot(h, wglu_ref[...], preferred_element_type=F32) + bglu_ref[...]
    uu_ref[...] = glu[:, :CONF_WIDTH] * _sigmoid(glu[:, CONF_WIDTH:])
    bd = jnp.dot(h, wbd_ref[...], preferred_element_type=F32)
    beta = _sigmoid(bd)
    v = bd + pb_ref[...]
    softplus = jnp.maximum(v, 0.0) + jnp.log1p(jnp.exp(-jnp.abs(v)))
    g = pa_ref[...] * softplus
    g_hi, g_rest = _split_bf16(g)
    g_mid = g - g_hi.astype(F32) - g_rest.astype(F32)
    pieces = jnp.concatenate([g_hi, g_rest, g_mid.astype(BF16)], axis=1)
    sums = jnp.dot(tri_ref[...], pieces, preferred_element_type=F32)
    gc = sums[:, :LANES] + sums[:, LANES:2 * LANES] + sums[:, 2 * LANES:]
    lane = lax.broadcasted_iota(I32, (tm, LANES), 1)
    res = jnp.where(lane < GDN_HEADS, beta, gc)
    bd_ref[...] = res[:, :2 * GDN_HEADS]
    bdt_ref[...] = res.T[:2 * GDN_HEADS, :]


def _inproj(x, g, wqkv, wz, wbd, wglu, bglu, pa, pb, chunk, t_batch):
    t = x.shape[0]
    tm = TOKEN_TILE
    tiles_per_batch = t_batch // tm if t_batch % tm == 0 else 0
    pos = jnp.arange(tm)
    tri = ((pos[:, None] >= pos[None, :])
           & (pos[:, None] // chunk == pos[None, :] // chunk)).astype(BF16)
    full = lambda shape: pl.BlockSpec(shape, lambda i: (0, 0))
    if tiles_per_batch:
        bdt_spec = pl.BlockSpec((None, 2 * GDN_HEADS, tm),
                                lambda i: (i // tiles_per_batch, 0, i % tiles_per_batch))
        bdt_shape = jax.ShapeDtypeStruct((t // t_batch, 2 * GDN_HEADS, t_batch), F32)
    else:
        bdt_spec = pl.BlockSpec((2 * GDN_HEADS, tm), lambda i: (0, i))
        bdt_shape = jax.ShapeDtypeStruct((2 * GDN_HEADS, t), F32)
    return pl.pallas_call(
        _inproj_kernel,
        grid=(t // tm,),
        in_specs=[
            pl.BlockSpec((tm, D_MODEL), lambda i: (i, 0)),
            full((1, D_MODEL)),
            full((D_MODEL, QKV_COLS)),
            full((D_MODEL, GDN_WIDTH)),
            full((D_MODEL, LANES)),
            full((D_MODEL, 2 * CONF_WIDTH)),
            full((1, 2 * CONF_WIDTH)),
            full((1, LANES)),
            full((1, LANES)),
            full((tm, tm)),
        ],
        out_specs=[
            pl.BlockSpec((tm, QKV_COLS), lambda i: (i, 0)),
            pl.BlockSpec((tm, GDN_WIDTH), lambda i: (i, 0)),
            pl.BlockSpec((tm, 2 * GDN_HEADS), lambda i: (i, 0)),
            bdt_spec,
            pl.BlockSpec((tm, CONF_WIDTH), lambda i: (i, 0)),
        ],
        out_shape=[
            jax.ShapeDtypeStruct((t, QKV_COLS), F32),
            jax.ShapeDtypeStruct((t, GDN_WIDTH), F32),
            jax.ShapeDtypeStruct((t, 2 * GDN_HEADS), F32),
            bdt_shape,
            jax.ShapeDtypeStruct((t, CONF_WIDTH), F32),
        ],
        compiler_params=_params(("arbitrary",)),
        name="inproj",
    )(x, g, wqkv, wz, wbd, wglu, bglu, pa, pb, tri)


def _split_bf16(a):
    hi = a.astype(BF16)
    lo = (a - hi.astype(F32)).astype(BF16)
    return hi, lo


def _unit_lower_inverses(ms, eye, n):
    levels = int(math.log2(n)) - 1
    ps = [eye - m for m in ms]
    pw = [(-m).astype(BF16) for m in ms]
    pw = [jnp.dot(p, p, preferred_element_type=F32).astype(BF16) for p in pw]
    for level in range(1, levels + 1):
        last = level == levels
        nxt = []
        for u, p in enumerate(pw):
            if last:
                ps[u] = ps[u] + jnp.dot(ps[u].astype(BF16), p, preferred_element_type=F32)
            else:
                out = jnp.dot(jnp.concatenate([p, ps[u].astype(BF16)], axis=0), p,
                              preferred_element_type=F32)
                nxt.append(out[:n].astype(BF16))
                ps[u] = ps[u] + out[n:]
        pw = nxt
    return ps


def _gdn_kernel(chunk, x_ref, z_ref, bd_ref, bdt_ref, hist_ref, s0_ref, wc_ref, won_ref,
                o_ref, s_ref, xbuf):
    i = pl.program_id(1)
    nb, tt = x_ref.shape[0], x_ref.shape[1]
    nchunk = tt // chunk
    pad = SUBLANES
    hist_rows = QKV_TAPS - 1

    @pl.when(i == 0)
    def _():
        xbuf[:, pad - hist_rows:pad, :] = hist_ref[...]
        s_ref[...] = s0_ref[...]

    xbuf[:, pad:pad + tt, :] = x_ref[...]

    ii = lax.broadcasted_iota(I32, (chunk, chunk), 0)
    jj = lax.broadcasted_iota(I32, (chunk, chunk), 1)
    lower_incl = ii >= jj
    lower_strict = ii > jj
    eye = (ii == jj).astype(F32)
    won = won_ref[...]

    def conv_silu(b, col0):
        cols = slice(col0, col0 + HEAD_DIM)
        acc = wc_ref[hist_rows:hist_rows + 1, cols] * xbuf[b, pad:pad + tt, cols]
        for j in range(hist_rows):
            r0 = pad - hist_rows + j
            acc = acc + wc_ref[j:j + 1, cols] * xbuf[b, r0:r0 + tt, cols]
        return acc * _sigmoid(acc)

    units = []
    for b in range(nb):
        bd = bd_ref[b]
        bdt = bdt_ref[b]
        for h in range(GDN_HEADS):
            q = conv_silu(b, h * HEAD_DIM)
            k = conv_silu(b, GDN_WIDTH + h * HEAD_DIM)
            v = conv_silu(b, 2 * GDN_WIDTH + h * HEAD_DIM)
            q = q * lax.rsqrt(jnp.sum(q * q, axis=-1, keepdims=True) + L2_EPS) * (HEAD_DIM ** -0.5)
            k = k * lax.rsqrt(jnp.sum(k * k, axis=-1, keepdims=True) + L2_EPS)
            kt = k.T
            beta = bd[:, h:h + 1]
            gcol = bd[:, GDN_HEADS + h:GDN_HEADS + h + 1]
            grow = bdt[GDN_HEADS + h:GDN_HEADS + h + 1, :]
            for c in range(nchunk):
                rows = slice(c * chunk, (c + 1) * chunk)
                qc, kc, vc, ktc = q[rows], k[rows], v[rows], kt[:, rows]
                bc, gc, gr = beta[rows], gcol[rows], grow[:, rows]
                dec = jnp.exp(jnp.where(lower_incl, gc - gr, -jnp.inf))
                kb = kc * bc
                eg = jnp.exp(gc)
                both = _dotb(jnp.concatenate([kb, qc], axis=0), ktc)
                units.append(dict(
                    b=b, h=h, c=c,
                    m=both[:chunk] * jnp.where(lower_strict, dec, 0.0),
                    qk=both[chunk:] * dec,
                    rhs=jnp.concatenate([vc * bc, kb * eg], axis=1),
                    qg=qc * eg,
                    kdt=ktc * jnp.exp(gr[:, chunk - 1:chunk] - gr),
                    d_last=jnp.exp(gc[chunk - 1:chunk, :]),
                ))

    tinvs = _unit_lower_inverses([u["m"] for u in units], eye, chunk)
    for u, tinv in zip(units, tinvs):
        sol = _dotb(tinv, u["rhs"])
        u["u"], u["w"] = sol[:, :HEAD_DIM], sol[:, HEAD_DIM:]

    state = {(b, h): s_ref[b, h] for b in range(nb) for h in range(GDN_HEADS)}
    by_key = {(u["b"], u["h"], u["c"]): u for u in units}
    for c in range(nchunk):
        rows = slice(c * chunk, (c + 1) * chunk)
        for b in range(nb):
            for h in range(GDN_HEADS):
                u = by_key[(b, h, c)]
                s = state[(b, h)]
                ws = _dotb(jnp.concatenate([u["w"], u["qg"]], axis=0), s)
                v_new = u["u"] - ws[:chunk]
                upd = _dotb(jnp.concatenate([u["qk"], u["kdt"]], axis=0), v_new)
                o_c = ws[chunk:] + upd[:chunk]
                state[(b, h)] = s * u["d_last"] + upd[chunk:]
                cols = slice(h * HEAD_DIM, (h + 1) * HEAD_DIM)
                on = o_c * lax.rsqrt(jnp.mean(o_c * o_c, axis=-1, keepdims=True) + RMS_EPS) * won
                zc = z_ref[b, rows, cols]
                o_ref[b, rows, cols] = (on * (zc * _sigmoid(zc))).astype(o_ref.dtype)
    for (b, h), s in state.items():
        s_ref[b, h] = s

    xbuf[:, pad - hist_rows:pad, :] = x_ref[:, tt - hist_rows:tt, :]


def _gdn(qkv, z, bd, bdt, hist, s0, wconv, wonorm, chunk, tt, nb):
    b, t, _ = qkv.shape
    nt = t // tt
    return pl.pallas_call(
        functools.partial(_gdn_kernel, chunk),
        grid=(b // nb, nt),
        in_specs=[
            pl.BlockSpec((nb, tt, QKV_COLS), lambda bi, i: (bi, i, 0)),
            pl.BlockSpec((nb, tt, GDN_WIDTH), lambda bi, i: (bi, i, 0)),
            pl.BlockSpec((nb, tt, 2 * GDN_HEADS), lambda bi, i: (bi, i, 0)),
            pl.BlockSpec((nb, 2 * GDN_HEADS, tt), lambda bi, i: (bi, 0, i)),
            pl.BlockSpec((nb, QKV_TAPS - 1, QKV_COLS), lambda bi, i: (bi, 0, 0)),
            pl.BlockSpec((nb, GDN_HEADS, HEAD_DIM, HEAD_DIM), lambda bi, i: (bi, 0, 0, 0)),
            pl.BlockSpec((QKV_TAPS, QKV_COLS), lambda bi, i: (0, 0)),
            pl.BlockSpec((1, HEAD_DIM), lambda bi, i: (0, 0)),
        ],
        out_specs=[
            pl.BlockSpec((nb, tt, GDN_WIDTH), lambda bi, i: (bi, i, 0)),
            pl.BlockSpec((nb, GDN_HEADS, HEAD_DIM, HEAD_DIM), lambda bi, i: (bi, 0, 0, 0)),
        ],
        out_shape=[
            jax.ShapeDtypeStruct((b, t, GDN_WIDTH), BF16),
            jax.ShapeDtypeStruct((b, GDN_HEADS, HEAD_DIM, HEAD_DIM), F32),
        ],
        scratch_shapes=[pltpu.VMEM((nb, tt + SUBLANES, QKV_COLS), F32)],
        compiler_params=_params(("arbitrary", "arbitrary")),
        name="gdn",
    )(qkv, z, bd, bdt, hist, s0, wconv, wonorm)


def _cconv_kernel(carry, u_ref, hist_ref, w_ref, b_ref, lg_ref, lb_ref, c_ref, ubuf, shifted):
    i = pl.program_id(1)
    tt = u_ref.shape[0]
    hist_rows = DW_TAPS - 1
    pad = 4 * SUBLANES
    base = pad - hist_rows

    @pl.when(i == 0)
    def _():
        ubuf[base:pad, :] = hist_ref[...]

    ubuf[pad:pad + tt, :] = u_ref[...]
    span = shifted.shape[1]
    for s in range(1, SUBLANES):
        shifted[s - 1] = ubuf[s:s + span, :]

    def tap_rows(j, r0, rows):
        q, s = divmod(base + j, SUBLANES)
        start = q * SUBLANES + r0
        if s == 0:
            return ubuf[start:start + rows, :]
        return shifted[s - 1, start:start + rows, :]

    rows = min(CONV_ROWS, tt)
    for r0 in range(0, tt, rows):
        acc = w_ref[0:1, :] * tap_rows(0, r0, rows)
        for j in range(1, DW_TAPS):
            acc = acc + w_ref[j:j + 1, :] * tap_rows(j, r0, rows)
        cv = acc + b_ref[...]
        mu = jnp.mean(cv, axis=-1, keepdims=True)
        xc = cv - mu
        var = jnp.mean(xc * xc, axis=-1, keepdims=True)
        y = xc * lax.rsqrt(var + LN_EPS) * lg_ref[...] + lb_ref[...]
        c_ref[r0:r0 + rows, :] = (y * _sigmoid(y)).astype(c_ref.dtype)
    if carry:
        ubuf[base:pad, :] = u_ref[tt - hist_rows:tt, :]


def _cconv(uu, hist, w, b, lg, lb, tt):
    bsz, t, _ = uu.shape
    nt = t // tt
    vec = pl.BlockSpec((1, CONF_WIDTH), lambda bi, i: (0, 0))
    return pl.pallas_call(
        functools.partial(_cconv_kernel, nt > 1),
        grid=(bsz, nt),
        in_specs=[
            pl.BlockSpec((None, tt, CONF_WIDTH), lambda bi, i: (bi, i, 0)),
            pl.BlockSpec((None, DW_TAPS - 1, CONF_WIDTH), lambda bi, i: (bi, 0, 0)),
            pl.BlockSpec((DW_TAPS, CONF_WIDTH), lambda bi, i: (0, 0)),
            vec, vec, vec,
        ],
        out_specs=pl.BlockSpec((None, tt, CONF_WIDTH), lambda bi, i: (bi, i, 0)),
        out_shape=jax.ShapeDtypeStruct((bsz, t, CONF_WIDTH), BF16),
        scratch_shapes=[pltpu.VMEM((tt + 4 * SUBLANES, CONF_WIDTH), F32),
                        pltpu.VMEM((SUBLANES - 1, tt + 3 * SUBLANES, CONF_WIDTH), F32)],
        compiler_params=_params(("arbitrary", "arbitrary")),
        name="cconv",
    )(uu, hist, w, b, lg, lb)


def _outproj_kernel(tiles_a, oa_ref, ca_ref, xa_ref, ob_ref, cb_ref, xb_ref, *rest):
    i = pl.program_id(0)

    @pl.when(i < tiles_a)
    def _():
        _outproj_tile(oa_ref, ca_ref, xa_ref, *rest)

    @pl.when(i >= tiles_a)
    def _():
        _outproj_tile(ob_ref, cb_ref, xb_ref, *rest)


def _outproj_tile(o_ref, c_ref, x_ref, wo_ref, wc_ref, g_ref, wr_ref, br_ref, upper_ref,
                  x1_ref, h2_ref, metar_ref, cnt_ref):
    tm = x_ref.shape[0]

    mix = _dotb(o_ref[...], wo_ref[...]) + _dotb(c_ref[...], wc_ref[...])
    x1 = x_ref[...] + mix
    x1_ref[...] = x1
    h2 = x1 * lax.rsqrt(jnp.mean(x1 * x1, axis=-1, keepdims=True) + RMS_EPS) * g_ref[...]
    h2_ref[...] = h2.astype(BF16)
    logits = lax.dot_general(wr_ref[...], h2.astype(BF16), (((1,), (1,)), ((), ())),
                             preferred_element_type=F32) + br_ref[...]
    row = lax.broadcasted_iota(I32, (N_EXPERTS, tm), 0).astype(F32)
    vals = logits
    idxs, tops = [], []
    for _ in range(TOP_K):
        m = jnp.max(vals, axis=0, keepdims=True)
        idx = jnp.min(jnp.where(vals == m, row, float(N_EXPERTS)), axis=0, keepdims=True)
        idxs.append(idx)
        tops.append(m)
        vals = jnp.where(row == idx, -jnp.inf, vals)
    exps = [jnp.exp(v - tops[0]) for v in tops]
    den = exps[0] + exps[1] + exps[2] + exps[3]
    onehots = [(row == idx).astype(F32) for idx in idxs]
    chosen = onehots[0] + onehots[1] + onehots[2] + onehots[3]
    before = jnp.dot(chosen.astype(BF16), upper_ref[...], preferred_element_type=F32)
    ranks = [jnp.sum(oh * before, axis=0, keepdims=True) for oh in onehots]
    gates = [e / den for e in exps]
    metar_ref[...] = jnp.concatenate(
        idxs + ranks + gates + [jnp.zeros((META_ROWS - 3 * TOP_K, tm), F32)], axis=0)
    cnt_ref[...] = jnp.broadcast_to(jnp.sum(chosen, axis=1, keepdims=True), (N_EXPERTS, LANES))[None]


def _outproj(first, second, wo, wc, g, wr, br):
    t_a, t_b = first[2].shape[0], second[2].shape[0]
    tm = TOKEN_TILE
    upper = (jnp.arange(tm)[:, None] < jnp.arange(tm)[None, :]).astype(BF16)
    tiles_a = t_a // tm
    t_total = t_a + t_b
    full = lambda shape: pl.BlockSpec(shape, lambda i: (0, 0))
    rows_a = lambda width: pl.BlockSpec((tm, width), lambda i: (jnp.minimum(i, tiles_a - 1), 0))
    rows_b = lambda width: pl.BlockSpec((tm, width), lambda i: (jnp.maximum(i - tiles_a, 0), 0))
    widths = (GDN_WIDTH, CONF_WIDTH, D_MODEL)
    return pl.pallas_call(
        functools.partial(_outproj_kernel, tiles_a),
        grid=(t_total // tm,),
        in_specs=[rows_a(w) for w in widths] + [rows_b(w) for w in widths] + [
            full((GDN_WIDTH, D_MODEL)),
            full((CONF_WIDTH, D_MODEL)),
            full((1, D_MODEL)),
            full((N_EXPERTS, D_MODEL)),
            full((N_EXPERTS, 1)),
            full((tm, tm)),
        ],
        out_specs=[
            pl.BlockSpec((tm, D_MODEL), lambda i: (i, 0)),
            pl.BlockSpec((tm, D_MODEL), lambda i: (i, 0)),
            pl.BlockSpec((META_ROWS, tm), lambda i: (0, i)),
            pl.BlockSpec((1, N_EXPERTS, LANES), lambda i: (i, 0, 0)),
        ],
        out_shape=[
            jax.ShapeDtypeStruct((t_total, D_MODEL), F32),
            jax.ShapeDtypeStruct((t_total, D_MODEL), BF16),
            jax.ShapeDtypeStruct((META_ROWS, t_total), F32),
            jax.ShapeDtypeStruct((t_total // tm, N_EXPERTS, LANES), F32),
        ],
        compiler_params=_params(("arbitrary",)),
        name="outproj_router",
    )(*first, *second, wo, wc, g, wr, br, upper)


def _segment_copies(i, seg_local_ref, seg_dst_ref, seg_n_ref, make_copy, wait):
    def copy(rows, local, dst):
        cp = make_copy(rows, pl.multiple_of(local, GRANULE), pl.multiple_of(dst, GRANULE))
        cp.wait() if wait else cp.start()

    def per_expert(ex, carry):
        j = i * N_EXPERTS + ex
        local, dst, n = seg_local_ref[j], seg_dst_ref[j], seg_n_ref[j]
        big = 8 * GRANULE

        def body(g, c):
            copy(big, local + g * big, dst + g * big)
            return c

        lax.fori_loop(0, n >> 3, body, 0)
        off = (n >> 3) * big
        for bit in (2, 1, 0):
            rows = GRANULE << bit
            has = (n >> bit) & 1

            @pl.when(has == 1)
            def _(rows=rows, off=off):
                copy(rows, local + off, dst + off)

            off = off + has * rows
        return carry

    lax.fori_loop(0, N_EXPERTS, per_expert, 0)


def _gap_copies(gap_start_ref, gap_n_ref, make_copy, wait):
    def per_expert(ex, carry):
        n = gap_n_ref[ex]
        start = pl.multiple_of(gap_start_ref[ex], GRANULE)
        off = 0 * n
        for bit in reversed(range(GAP_BITS)):
            rows = GRANULE << bit

            @pl.when(((n >> bit) & 1) == 1)
            def _(rows=rows, off=off):
                cp = make_copy(rows, pl.multiple_of(start + off, GRANULE))
                cp.wait() if wait else cp.start()

            off = off + ((n >> bit) & 1) * rows
        return carry

    lax.fori_loop(0, N_EXPERTS, per_expert, 0)


def _local_rows(meta_e, seg_local_ref, i):
    base = jnp.zeros(meta_e.shape, F32)
    for ex in range(N_EXPERTS):
        base = jnp.where(meta_e == float(ex), seg_local_ref[i * N_EXPERTS + ex].astype(F32), base)
    return base


def _dispatch_kernel(seg_local_ref, seg_dst_ref, seg_n_ref, gap_start_ref, gap_n_ref, tail_ref,
                     h_ref, mr_ref, xs_ref, slots_ref, xloc, zbuf, sems, zsem):
    i = pl.program_id(0)
    last = pl.num_programs(0) - 1
    tm = h_ref.shape[0]
    rt = xloc.shape[1]
    zrows = zbuf.shape[0]
    slot = i % 2

    def gap_copy(rows, dst):
        return pltpu.make_async_copy(zbuf.at[pl.ds(0, rows)], xs_ref.at[pl.ds(dst, rows)], zsem)

    def tail_copies(wait):
        def body(j, c):
            cp = gap_copy(zrows, pl.multiple_of(tail_ref[0] + j * zrows, zrows))
            cp.wait() if wait else cp.start()
            return c
        lax.fori_loop(0, tail_ref[1], body, 0)

    @pl.when(i == 0)
    def _():
        zbuf[...] = jnp.zeros(zbuf.shape, zbuf.dtype)
        _gap_copies(gap_start_ref, gap_n_ref, gap_copy, wait=False)
        tail_copies(wait=False)

    mr = mr_ref[...]
    dl = (_local_rows(mr[META_EXPERT:META_EXPERT + TOP_K], seg_local_ref, i)
          + mr[META_RANK:META_RANK + TOP_K])
    slots_ref[...] = jnp.concatenate(
        [dl, mr[META_GATE:META_GATE + TOP_K], jnp.zeros((LANES - 2 * TOP_K, tm), F32)], axis=0).T
    rr = lax.broadcasted_iota(I32, (rt, tm), 0).astype(F32)
    hit = rr == dl[0:1]
    for k in range(1, TOP_K):
        hit = hit | (rr == dl[k:k + 1])
    xloc[slot] = jnp.dot(hit.astype(BF16), h_ref[...], preferred_element_type=F32)

    def seg_copy(buf):
        def make(rows, local, dst):
            return pltpu.make_async_copy(xloc.at[buf, pl.ds(local, rows)],
                                         xs_ref.at[pl.ds(dst, rows)], sems.at[buf])
        return make

    _segment_copies(i, seg_local_ref, seg_dst_ref, seg_n_ref, seg_copy(slot), wait=False)

    @pl.when(i > 0)
    def _():
        _segment_copies(i - 1, seg_local_ref, seg_dst_ref, seg_n_ref, seg_copy(1 - slot), wait=True)

    @pl.when(i == last)
    def _():
        _segment_copies(i, seg_local_ref, seg_dst_ref, seg_n_ref, seg_copy(slot), wait=True)

    @pl.when(i == 0)
    def _():
        _gap_copies(gap_start_ref, gap_n_ref, gap_copy, wait=True)
        tail_copies(wait=True)


def _dispatch(tables, h2, metar, n_rows):
    t = h2.shape[0]
    tm = TOKEN_TILE
    return pl.pallas_call(
        _dispatch_kernel,
        grid_spec=pltpu.PrefetchScalarGridSpec(
            num_scalar_prefetch=6,
            grid=(t // tm,),
            in_specs=[
                pl.BlockSpec((tm, D_MODEL), lambda i, *_: (i, 0)),
                pl.BlockSpec((META_ROWS, tm), lambda i, *_: (0, i)),
            ],
            out_specs=[pl.BlockSpec(memory_space=pl.ANY),
                       pl.BlockSpec((tm, LANES), lambda i, *_: (i, 0))],
            scratch_shapes=[
                pltpu.VMEM((2, TILE_ROWS, D_MODEL), F32),
                pltpu.VMEM((GRANULE << (GAP_BITS - 1), D_MODEL), F32),
                pltpu.SemaphoreType.DMA((2,)),
                pltpu.SemaphoreType.DMA,
            ],
        ),
        out_shape=[jax.ShapeDtypeStruct((n_rows, D_MODEL), F32),
                   jax.ShapeDtypeStruct((t, LANES), F32)],
        compiler_params=_params(("arbitrary",)),
        name="dispatch",
    )(*tables, h2, metar)


def _expert_kernel(be_ref, rb_ref, used_ref, first_ref, slot_ref, next_ref,
                   x_ref, wgu_hbm, bgu_ref, wd_hbm, bdn_ref, y_ref,
                   wgu_f32, wd_f32, wgu_bf, wd_bf, sems):
    i = pl.program_id(0)

    def fetch(expert, slot):
        return (pltpu.make_async_copy(wgu_hbm.at[expert], wgu_f32.at[slot], sems.at[slot, 0]),
                pltpu.make_async_copy(wd_hbm.at[expert], wd_f32.at[slot], sems.at[slot, 1]))

    @pl.when(i == 0)
    def _():
        for cp in fetch(be_ref[0], 0):
            cp.start()

    @pl.when(first_ref[i] == 1)
    def _():
        slot = slot_ref[i]
        for cp in fetch(be_ref[i], slot):
            cp.wait()
        wgu_bf[...] = wgu_f32[slot].astype(BF16)
        wd_bf[...] = wd_f32[slot].astype(BF16)

        @pl.when(next_ref[i] >= 0)
        def _():
            for cp in fetch(next_ref[i], 1 - slot):
                cp.start()

    def ffn(x):
        gu = _dotb(x, wgu_bf[...]) + bgu_ref[...]
        x_glu = jnp.minimum(gu[:, :D_EXPERT], SWIGLU_LIMIT)
        x_lin = jnp.clip(gu[:, D_EXPERT:], -SWIGLU_LIMIT, SWIGLU_LIMIT)
        act = x_glu * _sigmoid(SWIGLU_ALPHA * x_glu) * (x_lin + 1.0)
        return _dotb(act, wd_bf[...]) + bdn_ref[...]

    used = used_ref[i]
    half = x_ref.shape[0] // 2

    @pl.when(used > half)
    def _():
        y_ref[...] = ffn(x_ref[...])

    @pl.when((used > 0) & (used <= half))
    def _():
        y_ref[:half, :] = ffn(x_ref[:half, :])
        y_ref[half:, :] = jnp.zeros((half, y_ref.shape[1]), y_ref.dtype)


def _experts(tables, xs, wgu, bgu, wd, bdn):
    rows = xs.shape[0]
    bm = EXPERT_ROWS
    nb = rows // bm
    return pl.pallas_call(
        _expert_kernel,
        grid_spec=pltpu.PrefetchScalarGridSpec(
            num_scalar_prefetch=6,
            grid=(nb,),
            in_specs=[
                pl.BlockSpec((bm, D_MODEL), lambda i, be, rb, *_: (rb[i], 0)),
                pl.BlockSpec(memory_space=pl.ANY),
                pl.BlockSpec((None, 1, 2 * D_EXPERT), lambda i, be, *_: (be[i], 0, 0)),
                pl.BlockSpec(memory_space=pl.ANY),
                pl.BlockSpec((None, 1, D_MODEL), lambda i, be, *_: (be[i], 0, 0)),
            ],
            out_specs=pl.BlockSpec((bm, D_MODEL), lambda i, be, rb, *_: (rb[i], 0)),
            scratch_shapes=[
                pltpu.VMEM((2, D_MODEL, 2 * D_EXPERT), F32),
                pltpu.VMEM((2, D_EXPERT, D_MODEL), F32),
                pltpu.VMEM((D_MODEL, 2 * D_EXPERT), BF16),
                pltpu.VMEM((D_EXPERT, D_MODEL), BF16),
                pltpu.SemaphoreType.DMA((2, 2)),
            ],
        ),
        out_shape=jax.ShapeDtypeStruct((rows, D_MODEL), F32),
        input_output_aliases={6: 0},
        compiler_params=pltpu.CompilerParams(dimension_semantics=("arbitrary",),
                                             vmem_limit_bytes=EXPERT_VMEM_LIMIT),
        name="experts",
    )(*tables, xs, wgu, bgu, wd, bdn)


def _combine_kernel(tiles_a, seg_local_ref, seg_dst_ref, seg_n_ref, x1_ref, mc_ref, yb_ref, gf_ref,
                    ya_ref, yb_out_ref, ybuf, wsel, sems):
    i = pl.program_id(0)
    n_tiles = pl.num_programs(0)
    tm = x1_ref.shape[0]
    rt = ybuf.shape[1]
    slot = i % 2

    def seg_copy(buf):
        def make(rows, local, src):
            return pltpu.make_async_copy(yb_ref.at[pl.ds(src, rows)],
                                         ybuf.at[buf, pl.ds(local, rows)], sems.at[buf])
        return make

    @pl.when(i == 0)
    def _():
        ybuf[...] = jnp.zeros(ybuf.shape, ybuf.dtype)
        _segment_copies(i, seg_local_ref, seg_dst_ref, seg_n_ref, seg_copy(slot), wait=False)

    @pl.when(i + 1 < n_tiles)
    def _():
        _segment_copies(i + 1, seg_local_ref, seg_dst_ref, seg_n_ref, seg_copy(1 - slot), wait=False)

    dl = mc_ref[:, 0:TOP_K]
    gate = mc_ref[:, TOP_K:2 * TOP_K]
    rb, cb = SEL_CHUNK
    col = lax.broadcasted_iota(I32, (rb, cb), 1).astype(F32)
    for r0 in range(0, tm, rb):
        rows_k = [jnp.broadcast_to(dl[r0:r0 + rb, k:k + 1], (rb, cb)) for k in range(TOP_K)]
        gate_k = [jnp.broadcast_to(gate[r0:r0 + rb, k:k + 1], (rb, cb)) for k in range(TOP_K)]
        for c0 in range(0, rt, cb):
            rr = col + float(c0)
            w = jnp.where(rr == rows_k[0], gate_k[0], 0.0)
            for k in range(1, TOP_K):
                w = jnp.where(rr == rows_k[k], gate_k[k], w)
            wsel[r0:r0 + rb, c0:c0 + cb] = w.astype(BF16)
    _segment_copies(i, seg_local_ref, seg_dst_ref, seg_n_ref, seg_copy(slot), wait=True)
    acc = x1_ref[...] + jnp.dot(wsel[...], ybuf[slot].astype(BF16), preferred_element_type=F32)
    y = acc * lax.rsqrt(jnp.mean(acc * acc, axis=-1, keepdims=True) + RMS_EPS) * gf_ref[...]

    @pl.when(i < tiles_a)
    def _():
        ya_ref[...] = y

    @pl.when(i >= tiles_a)
    def _():
        yb_out_ref[...] = y


def _combine(tables, x1, slots, yb, gfin, n_a):
    t = x1.shape[0]
    tm = TOKEN_TILE
    tiles_a = n_a // tm
    return pl.pallas_call(
        functools.partial(_combine_kernel, tiles_a),
        grid_spec=pltpu.PrefetchScalarGridSpec(
            num_scalar_prefetch=3,
            grid=(t // tm,),
            in_specs=[
                pl.BlockSpec((tm, D_MODEL), lambda i, *_: (i, 0)),
                pl.BlockSpec((tm, LANES), lambda i, *_: (i, 0)),
                pl.BlockSpec(memory_space=pl.ANY),
                pl.BlockSpec((1, D_MODEL), lambda i, *_: (0, 0)),
            ],
            out_specs=[
                pl.BlockSpec((tm, D_MODEL), lambda i, *_: (jnp.minimum(i, tiles_a - 1), 0)),
                pl.BlockSpec((tm, D_MODEL), lambda i, *_: (jnp.maximum(i - tiles_a, 0), 0)),
            ],
            scratch_shapes=[pltpu.VMEM((2, TILE_ROWS, D_MODEL), F32),
                            pltpu.VMEM((tm, TILE_ROWS), BF16),
                            pltpu.SemaphoreType.DMA((2,))],
        ),
        out_shape=[jax.ShapeDtypeStruct((n_a, D_MODEL), F32),
                   jax.ShapeDtypeStruct((t - n_a, D_MODEL), F32)],
        compiler_params=pltpu.CompilerParams(dimension_semantics=("arbitrary",),
                                             vmem_limit_bytes=EXPERT_VMEM_LIMIT),
        name="combine",
    )(*tables, x1, slots, yb, gfin)


def kernel(x_prompt, x_sample, state_gdn, state_qkv_conv, state_dwconv, norm_mix_g, w_in, w_conv_qkv,
           a_log, dt_bias, w_onorm, b_glu, w_dw, b_dw, ln_g, ln_b, w_out, norm_ffn_g, w_router,
           b_router, w_gate_up, b_gate_up, w_down, b_down, norm_final_g):
    bp, tp, _ = x_prompt.shape
    bs, ts, _ = x_sample.shape
    n_p, n_s = bp * tp, bs * ts

    wi = w_in[0]
    c_z = QKV_COLS
    c_bd = c_z + GDN_WIDTH
    c_glu = c_bd + 2 * GDN_HEADS
    wqkv = wi[:, :c_z].astype(BF16)
    wz = wi[:, c_z:c_bd].astype(BF16)
    wbd = jnp.pad(wi[:, c_bd:c_glu], ((0, 0), (0, LANES - 2 * GDN_HEADS))).astype(BF16)
    wglu = wi[:, c_glu:].astype(BF16)
    bglu = b_glu[0][None, :]
    lane_pad = (GDN_HEADS, LANES - 2 * GDN_HEADS)
    pa = jnp.pad(-jnp.exp(a_log[0].astype(F32)), lane_pad)[None, :]
    pb = jnp.pad(dt_bias[0].astype(F32), lane_pad)[None, :]
    gmix = norm_mix_g[0][None, :]
    wconv = w_conv_qkv[0]
    wonorm = w_onorm[0][None, :]
    wdw, bdw = w_dw[0], b_dw[0][None, :]
    lng, lnb = ln_g[0][None, :], ln_b[0][None, :]
    wo = w_out[0][:GDN_WIDTH].astype(BF16)
    wc = w_out[0][GDN_WIDTH:].astype(BF16)
    gffn = norm_ffn_g[0][None, :]
    wr, br = w_router[0].T.astype(BF16), b_router[0].astype(F32)[:, None]
    bgu = b_gate_up[0][:, None, :]
    bdn = b_down[0][:, None, :]
    gfin = norm_final_g[None, :]

    xp = x_prompt.reshape(n_p, D_MODEL)
    xs_tok = x_sample.reshape(n_s, D_MODEL)

    def mixers(x2d, bsz, t, hist_qkv, s0, hist_glu, chunk, gdn_tile, gdn_rows, conv_tile):
        qkv, z, bd, bdt, uu = _inproj(x2d, gmix, wqkv, wz, wbd, wglu, bglu, pa, pb, chunk, t)
        qkv3 = qkv.reshape(bsz, t, QKV_COLS)
        uu3 = uu.reshape(bsz, t, CONF_WIDTH)
        if bdt.ndim == 2:
            bdt = bdt.reshape(2 * GDN_HEADS, bsz, t).transpose(1, 0, 2)
        o, s_new = _gdn(qkv3, z.reshape(bsz, t, GDN_WIDTH), bd.reshape(bsz, t, 2 * GDN_HEADS),
                        bdt, hist_qkv, s0, wconv, wonorm, chunk, gdn_tile, gdn_rows)
        cc = _cconv(uu3, hist_glu, wdw, bdw, lng, lnb, conv_tile)
        return o.reshape(bsz * t, GDN_WIDTH), cc.reshape(bsz * t, CONF_WIDTH), s_new, qkv3, uu3

    zero_qkv = jnp.zeros((bp, QKV_TAPS - 1, QKV_COLS), F32)
    zero_s = jnp.zeros((bp, GDN_HEADS, HEAD_DIM, HEAD_DIM), F32)
    zero_glu = jnp.zeros((bp, DW_TAPS - 1, CONF_WIDTH), F32)
    o_p, c_p, s_p, qkv_p, uu_p = mixers(xp, bp, tp, zero_qkv, zero_s, zero_glu,
                                        PROMPT_CHUNK, GDN_TILE, 1, CONV_TILE)
    o_s, c_s, s_s, qkv_s, uu_s = mixers(xs_tok, bs, ts, state_qkv_conv[0], state_gdn[0],
                                        state_dwconv[0], ts, ts, GDN_SAMPLE_ROWS, ts)

    n_tok = n_p + n_s
    x1, h2, metar, cnt_tile = _outproj((o_p, c_p, xp), (o_s, c_s, xs_tok), wo, wc, gffn, wr, br)

    bm = EXPERT_ROWS
    n_tiles = n_tok // TOKEN_TILE
    counts = cnt_tile[:, :, 0].astype(I32)
    seg_rows = (counts + GRANULE - 1) // GRANULE * GRANULE
    ids = jnp.arange(N_EXPERTS, dtype=I32)
    tile_ids = jnp.arange(n_tiles, dtype=I32)
    earlier_e = (ids[:, None] < ids[None, :]).astype(I32)
    earlier_t = (tile_ids[None, :] < tile_ids[:, None]).astype(I32)
    seg_local = jnp.sum(seg_rows[:, :, None] * earlier_e[None], axis=1)
    seg_before = jnp.sum(earlier_t[:, :, None] * seg_rows[None], axis=1)
    rows_e = jnp.sum(seg_rows, axis=0)
    padded = (rows_e + bm - 1) // bm * bm
    pstart = jnp.sum(padded[:, None] * earlier_e, axis=0)
    pend = pstart + padded
    seg_dst = pstart[None, :] + seg_before
    n_used = jnp.maximum(pend[-1] // bm, 1).astype(I32)
    max_rows = n_tok * TOP_K + n_tiles * N_EXPERTS * (GRANULE - 1) + N_EXPERTS * (bm - 1)
    n_blocks = -(-max_rows // bm)
    blk = jnp.minimum(jnp.arange(n_blocks, dtype=I32), n_used - 1)
    block_expert = jnp.minimum(
        jnp.sum((pend[None, :] <= (blk * bm)[:, None]).astype(I32), axis=1), N_EXPERTS - 1)
    seg_tables = (seg_local.reshape(-1).astype(I32), seg_dst.reshape(-1).astype(I32),
                  (seg_rows // GRANULE).reshape(-1).astype(I32))
    gap_tables = ((pstart + rows_e).astype(I32), ((padded - rows_e) // GRANULE).astype(I32))
    zero_rows = GRANULE << (GAP_BITS - 1)
    tail_table = jnp.stack([pend[-1], (n_blocks * bm - pend[-1]) // zero_rows]).astype(I32)

    xs, slots = _dispatch(seg_tables + gap_tables + (tail_table,), h2, metar, n_blocks * bm)
    active = padded > 0
    later = active[None, :] & (ids[None, :] > ids[:, None])
    next_active = jnp.min(jnp.where(later, ids[None, :], N_EXPERTS), axis=1)
    next_active = jnp.where(next_active == N_EXPERTS, -1, next_active)
    order = jnp.sum(active.astype(I32)[:, None] * earlier_e, axis=0)
    steps = jnp.arange(n_blocks, dtype=I32)
    of_block = (block_expert[:, None] == ids[None, :]).astype(I32)
    per_block = lambda table: jnp.sum(of_block * table[None, :], axis=1)
    first = ((steps * bm == per_block(pstart)) & (steps < n_used)).astype(I32)
    used = jnp.clip(per_block(pstart + rows_e) - steps * bm, 0, bm)
    used = jnp.where(steps < n_used, used, 0).astype(I32)
    expert_tables = (block_expert, blk, used, first,
                     (per_block(order) % 2).astype(I32), per_block(next_active).astype(I32))
    yb = _experts(expert_tables, xs, w_gate_up[0], bgu, w_down[0], bdn)
    y_p, y_s = _combine(seg_tables, x1, slots, yb, gfin, n_p)

    hist_glu_s = jnp.concatenate([state_dwconv[0], uu_s], axis=1)[:, -(DW_TAPS - 1):]
    return (y_p.reshape(bp, tp, D_MODEL),
            y_s.reshape(bs, ts, D_MODEL),
            s_p[None],
            qkv_p[:, -(QKV_TAPS - 1):][None],
            uu_p[:, -(DW_TAPS - 1):][None],
            s_s[None],
            qkv_s[:, -(QKV_TAPS - 1):][None],
            hist_glu_s[None])
```

```python
import functools
import math

import jax
import jax.numpy as jnp
from jax import lax
from jax.experimental import pallas as pl
from jax.experimental.pallas import tpu as pltpu

F32 = jnp.float32
BF16 = jnp.bfloat16
I32 = jnp.int32
HIGHEST = lax.Precision.HIGHEST

D_MODEL = 1024
GDN_HEADS = 4
HEAD_DIM = 128
GDN_WIDTH = GDN_HEADS * HEAD_DIM
QKV_COLS = 3 * GDN_WIDTH
CONF_WIDTH = 512
QKV_TAPS = 4
DW_TAPS = 31
N_EXPERTS = 32
TOP_K = 4
D_EXPERT = 1024
SWIGLU_LIMIT = 7.0
SWIGLU_ALPHA = 1.702
RMS_EPS = 1e-6
LN_EPS = 1e-5
L2_EPS = 1e-6
PROMPT_CHUNK = 64

LANES = 128
SUBLANES = 8
TOKEN_TILE = 512
GDN_TILE = 512
GDN_SAMPLE_ROWS = 8
CONV_TILE = 512
CONV_ROWS = 32
EXPERT_ROWS = 512
GRANULE = SUBLANES
GAP_BITS = 6
TILE_ROWS = -(-(TOKEN_TILE * TOP_K + N_EXPERTS * (GRANULE - 1)) // 256) * 256
META_EXPERT, META_RANK, META_GATE, META_ROWS = 0, TOP_K, 2 * TOP_K, 16
assert EXPERT_ROWS // GRANULE <= 1 << GAP_BITS
SEL_CHUNK = (64, 256)
VMEM_LIMIT = 48 * 1024 * 1024
EXPERT_VMEM_LIMIT = 56 * 1024 * 1024


def _dotb(a, b):
    return jnp.dot(a.astype(BF16), b.astype(BF16), preferred_element_type=F32)


def _dotf(a, b):
    return jnp.dot(a, b, precision=HIGHEST, preferred_element_type=F32)


def _sigmoid(x):
    return jax.nn.sigmoid(x)


def _params(sem):
    return pltpu.CompilerParams(dimension_semantics=sem, vmem_limit_bytes=VMEM_LIMIT)


def _inproj_kernel(x_ref, g_ref, wqkv_ref, wz_ref, wbd_ref, wglu_ref, bglu_ref, pa_ref, pb_ref,
                   tri_ref, qkv_ref, z_ref, bd_ref, bdt_ref, uu_ref):
    x = x_ref[...]
    tm = x.shape[0]
    ms = jnp.mean(x * x, axis=-1, keepdims=True)
    h = (x * lax.rsqrt(ms + RMS_EPS) * g_ref[...]).astype(BF16)
    qkv_ref[...] = jnp.dot(h, wqkv_ref[...], preferred_element_type=F32)
    z_ref[...] = jnp.dot(h, wz_ref[...], preferred_element_type=F32)
    glu = jnp.dot(h, wglu_ref[...], preferred_element_type=F32) + bglu_ref[...]
    uu_ref[...] = glu[:, :CONF_WIDTH] * _sigmoid(glu[:, CONF_WIDTH:])
    bd = jnp.dot(h, wbd_ref[...], preferred_element_type=F32)
    beta = _sigmoid(bd)
    v = bd + pb_ref[...]
    softplus = jnp.maximum(v, 0.0) + jnp.log1p(jnp.exp(-jnp.abs(v)))
    g = pa_ref[...] * softplus
    g_hi, g_rest = _split_bf16(g)
    g_mid = g - g_hi.astype(F32) - g_rest.astype(F32)
    pieces = jnp.concatenate([g_hi, g_rest, g_mid.astype(BF16)], axis=1)
    sums = jnp.dot(tri_ref[...], pieces, preferred_element_type=F32)
    gc = sums[:, :LANES] + sums[:, LANES:2 * LANES] + sums[:, 2 * LANES:]
    lane = lax.broadcasted_iota(I32, (tm, LANES), 1)
    res = jnp.where(lane < GDN_HEADS, beta, gc)
    bd_ref[...] = res[:, :2 * GDN_HEADS]
    bdt_ref[...] = res.T[:2 * GDN_HEADS, :]


def _inproj(x, g, wqkv, wz, wbd, wglu, bglu, pa, pb, chunk, t_batch):
    t = x.shape[0]
    tm = TOKEN_TILE
    tiles_per_batch = t_batch // tm if t_batch % tm == 0 else 0
    pos = jnp.arange(tm)
    tri = ((pos[:, None] >= pos[None, :])
           & (pos[:, None] // chunk == pos[None, :] // chunk)).astype(BF16)
    full = lambda shape: pl.BlockSpec(shape, lambda i: (0, 0))
    if tiles_per_batch:
        bdt_spec = pl.BlockSpec((None, 2 * GDN_HEADS, tm),
                                lambda i: (i // tiles_per_batch, 0, i % tiles_per_batch))
        bdt_shape = jax.ShapeDtypeStruct((t // t_batch, 2 * GDN_HEADS, t_batch), F32)
    else:
        bdt_spec = pl.BlockSpec((2 * GDN_HEADS, tm), lambda i: (0, i))
        bdt_shape = jax.ShapeDtypeStruct((2 * GDN_HEADS, t), F32)
    return pl.pallas_call(
        _inproj_kernel,
        grid=(t // tm,),
        in_specs=[
            pl.BlockSpec((tm, D_MODEL), lambda i: (i, 0)),
            full((1, D_MODEL)),
            full((D_MODEL, QKV_COLS)),
            full((D_MODEL, GDN_WIDTH)),
            full((D_MODEL, LANES)),
            full((D_MODEL, 2 * CONF_WIDTH)),
            full((1, 2 * CONF_WIDTH)),
            full((1, LANES)),
            full((1, LANES)),
            full((tm, tm)),
        ],
        out_specs=[
            pl.BlockSpec((tm, QKV_COLS), lambda i: (i, 0)),
            pl.BlockSpec((tm, GDN_WIDTH), lambda i: (i, 0)),
            pl.BlockSpec((tm, 2 * GDN_HEADS), lambda i: (i, 0)),
            bdt_spec,
            pl.BlockSpec((tm, CONF_WIDTH), lambda i: (i, 0)),
        ],
        out_shape=[
            jax.ShapeDtypeStruct((t, QKV_COLS), F32),
            jax.ShapeDtypeStruct((t, GDN_WIDTH), F32),
            jax.ShapeDtypeStruct((t, 2 * GDN_HEADS), F32),
            bdt_shape,
            jax.ShapeDtypeStruct((t, CONF_WIDTH), F32),
        ],
        compiler_params=_params(("arbitrary",)),
        name="inproj",
    )(x, g, wqkv, wz, wbd, wglu, bglu, pa, pb, tri)


def _split_bf16(a):
    hi = a.astype(BF16)
    lo = (a - hi.astype(F32)).astype(BF16)
    return hi, lo


def _unit_lower_inverses(ms, eye, n):
    levels = int(math.log2(n)) - 1
    ps = [eye - m for m in ms]
    pw = [(-m).astype(BF16) for m in ms]
    pw = [jnp.dot(p, p, preferred_element_type=F32).astype(BF16) for p in pw]
    for level in range(1, levels + 1):
        last = level == levels
        nxt = []
        for u, p in enumerate(pw):
            if last:
                ps[u] = ps[u] + jnp.dot(ps[u].astype(BF16), p, preferred_element_type=F32)
            else:
                out = jnp.dot(jnp.concatenate([p, ps[u].astype(BF16)], axis=0), p,
                              preferred_element_type=F32)
                nxt.append(out[:n].astype(BF16))
                ps[u] = ps[u] + out[n:]
        pw = nxt
    return ps


def _gdn_kernel(chunk, x_ref, z_ref, bd_ref, bdt_ref, hist_ref, s0_ref, wc_ref, won_ref,
                o_ref, s_ref, xbuf):
    i = pl.program_id(1)
    nb, tt = x_ref.shape[0], x_ref.shape[1]
    nchunk = tt // chunk
    pad = SUBLANES
    hist_rows = QKV_TAPS - 1

    @pl.when(i == 0)
    def _():
        xbuf[:, pad - hist_rows:pad, :] = hist_ref[...]
        s_ref[...] = s0_ref[...]

    xbuf[:, pad:pad + tt, :] = x_ref[...]

    ii = lax.broadcasted_iota(I32, (chunk, chunk), 0)
    jj = lax.broadcasted_iota(I32, (chunk, chunk), 1)
    lower_incl = ii >= jj
    lower_strict = ii > jj
    eye = (ii == jj).astype(F32)
    won = won_ref[...]

    def conv_silu(b, col0):
        cols = slice(col0, col0 + HEAD_DIM)
        acc = wc_ref[hist_rows:hist_rows + 1, cols] * xbuf[b, pad:pad + tt, cols]
        for j in range(hist_rows):
            r0 = pad - hist_rows + j
            acc = acc + wc_ref[j:j + 1, cols] * xbuf[b, r0:r0 + tt, cols]
        return acc * _sigmoid(acc)

    units = []
    for b in range(nb):
        bd = bd_ref[b]
        bdt = bdt_ref[b]
        for h in range(GDN_HEADS):
            q = conv_silu(b, h * HEAD_DIM)
            k = conv_silu(b, GDN_WIDTH + h * HEAD_DIM)
            v = conv_silu(b, 2 * GDN_WIDTH + h * HEAD_DIM)
            q = q * lax.rsqrt(jnp.sum(q * q, axis=-1, keepdims=True) + L2_EPS) * (HEAD_DIM ** -0.5)
            k = k * lax.rsqrt(jnp.sum(k * k, axis=-1, keepdims=True) + L2_EPS)
            kt = k.T
            beta = bd[:, h:h + 1]
            gcol = bd[:, GDN_HEADS + h:GDN_HEADS + h + 1]
            grow = bdt[GDN_HEADS + h:GDN_HEADS + h + 1, :]
            for c in range(nchunk):
                rows = slice(c * chunk, (c + 1) * chunk)
                qc, kc, vc, ktc = q[rows], k[rows], v[rows], kt[:, rows]
                bc, gc, gr = beta[rows], gcol[rows], grow[:, rows]
                dec = jnp.exp(jnp.where(lower_incl, gc - gr, -jnp.inf))
                kb = kc * bc
                eg = jnp.exp(gc)
                both = _dotb(jnp.concatenate([kb, qc], axis=0), ktc)
                units.append(dict(
                    b=b, h=h, c=c,
                    m=both[:chunk] * jnp.where(lower_strict, dec, 0.0),
                    qk=both[chunk:] * dec,
                    rhs=jnp.concatenate([vc * bc, kb * eg], axis=1),
                    qg=qc * eg,
                    kdt=ktc * jnp.exp(gr[:, chunk - 1:chunk] - gr),
                    d_last=jnp.exp(gc[chunk - 1:chunk, :]),
                ))

    tinvs = _unit_lower_inverses([u["m"] for u in units], eye, chunk)
    for u, tinv in zip(units, tinvs):
        sol = _dotb(tinv, u["rhs"])
        u["u"], u["w"] = sol[:, :HEAD_DIM], sol[:, HEAD_DIM:]

    state = {(b, h): s_ref[b, h] for b in range(nb) for h in range(GDN_HEADS)}
    by_key = {(u["b"], u["h"], u["c"]): u for u in units}
    for c in range(nchunk):
        rows = slice(c * chunk, (c + 1) * chunk)
        for b in range(nb):
            for h in range(GDN_HEADS):
                u = by_key[(b, h, c)]
                s = state[(b, h)]
                ws = _dotb(jnp.concatenate([u["w"], u["qg"]], axis=0), s)
                v_new = u["u"] - ws[:chunk]
                upd = _dotb(jnp.concatenate([u["qk"], u["kdt"]], axis=0), v_new)
                o_c = ws[chunk:] + upd[:chunk]
                state[(b, h)] = s * u["d_last"] + upd[chunk:]
                cols = slice(h * HEAD_DIM, (h + 1) * HEAD_DIM)
                on = o_c * lax.rsqrt(jnp.mean(o_c * o_c, axis=-1, keepdims=True) + RMS_EPS) * won
                zc = z_ref[b, rows, cols]
                o_ref[b, rows, cols] = (on * (zc * _sigmoid(zc))).astype(o_ref.dtype)
    for (b, h), s in state.items():
        s_ref[b, h] = s

    xbuf[:, pad - hist_rows:pad, :] = x_ref[:, tt - hist_rows:tt, :]


def _gdn(qkv, z, bd, bdt, hist, s0, wconv, wonorm, chunk, tt, nb):
    b, t, _ = qkv.shape
    nt = t // tt
    return pl.pallas_call(
        functools.partial(_gdn_kernel, chunk),
        grid=(b // nb, nt),
        in_specs=[
            pl.BlockSpec((nb, tt, QKV_COLS), lambda bi, i: (bi, i, 0)),
            pl.BlockSpec((nb, tt, GDN_WIDTH), lambda bi, i: (bi, i, 0)),
            pl.BlockSpec((nb, tt, 2 * GDN_HEADS), lambda bi, i: (bi, i, 0)),
            pl.BlockSpec((nb, 2 * GDN_HEADS, tt), lambda bi, i: (bi, 0, i)),
            pl.BlockSpec((nb, QKV_TAPS - 1, QKV_COLS), lambda bi, i: (bi, 0, 0)),
            pl.BlockSpec((nb, GDN_HEADS, HEAD_DIM, HEAD_DIM), lambda bi, i: (bi, 0, 0, 0)),
            pl.BlockSpec((QKV_TAPS, QKV_COLS), lambda bi, i: (0, 0)),
            pl.BlockSpec((1, HEAD_DIM), lambda bi, i: (0, 0)),
        ],
        out_specs=[
            pl.BlockSpec((nb, tt, GDN_WIDTH), lambda bi, i: (bi, i, 0)),
            pl.BlockSpec((nb, GDN_HEADS, HEAD_DIM, HEAD_DIM), lambda bi, i: (bi, 0, 0, 0)),
        ],
        out_shape=[
            jax.ShapeDtypeStruct((b, t, GDN_WIDTH), BF16),
            jax.ShapeDtypeStruct((b, GDN_HEADS, HEAD_DIM, HEAD_DIM), F32),
        ],
        scratch_shapes=[pltpu.VMEM((nb, tt + SUBLANES, QKV_COLS), F32)],
        compiler_params=_params(("arbitrary", "arbitrary")),
        name="gdn",
    )(qkv, z, bd, bdt, hist, s0, wconv, wonorm)


def _cconv_kernel(carry, u_ref, hist_ref, w_ref, b_ref, lg_ref, lb_ref, c_ref, ubuf, shifted):
    i = pl.program_id(1)
    tt = u_ref.shape[0]
    hist_rows = DW_TAPS - 1
    pad = 4 * SUBLANES
    base = pad - hist_rows

    @pl.when(i == 0)
    def _():
        ubuf[base:pad, :] = hist_ref[...]

    ubuf[pad:pad + tt, :] = u_ref[...]
    span = shifted.shape[1]
    for s in range(1, SUBLANES):
        shifted[s - 1] = ubuf[s:s + span, :]

    def tap_rows(j, r0, rows):
        q, s = divmod(base + j, SUBLANES)
        start = q * SUBLANES + r0
        if s == 0:
            return ubuf[start:start + rows, :]
        return shifted[s - 1, start:start + rows, :]

    rows = min(CONV_ROWS, tt)
    for r0 in range(0, tt, rows):
        acc = w_ref[0:1, :] * tap_rows(0, r0, rows)
        for j in range(1, DW_TAPS):
            acc = acc + w_ref[j:j + 1, :] * tap_rows(j, r0, rows)
        cv = acc + b_ref[...]
        mu = jnp.mean(cv, axis=-1, keepdims=True)
        xc = cv - mu
        var = jnp.mean(xc * xc, axis=-1, keepdims=True)
        y = xc * lax.rsqrt(var + LN_EPS) * lg_ref[...] + lb_ref[...]
        c_ref[r0:r0 + rows, :] = (y * _sigmoid(y)).astype(c_ref.dtype)
    if carry:
        ubuf[base:pad, :] = u_ref[tt - hist_rows:tt, :]


def _cconv(uu, hist, w, b, lg, lb, tt):
    bsz, t, _ = uu.shape
    nt = t // tt
    vec = pl.BlockSpec((1, CONF_WIDTH), lambda bi, i: (0, 0))
    return pl.pallas_call(
        functools.partial(_cconv_kernel, nt > 1),
        grid=(bsz, nt),
        in_specs=[
            pl.BlockSpec((None, tt, CONF_WIDTH), lambda bi, i: (bi, i, 0)),
            pl.BlockSpec((None, DW_TAPS - 1, CONF_WIDTH), lambda bi, i: (bi, 0, 0)),
            pl.BlockSpec((DW_TAPS, CONF_WIDTH), lambda bi, i: (0, 0)),
            vec, vec, vec,
        ],
        out_specs=pl.BlockSpec((None, tt, CONF_WIDTH), lambda bi, i: (bi, i, 0)),
        out_shape=jax.ShapeDtypeStruct((bsz, t, CONF_WIDTH), BF16),
        scratch_shapes=[pltpu.VMEM((tt + 4 * SUBLANES, CONF_WIDTH), F32),
                        pltpu.VMEM((SUBLANES - 1, tt + 3 * SUBLANES, CONF_WIDTH), F32)],
        compiler_params=_params(("arbitrary", "arbitrary")),
        name="cconv",
    )(uu, hist, w, b, lg, lb)


def _outproj_kernel(tiles_a, oa_ref, ca_ref, xa_ref, ob_ref, cb_ref, xb_ref, *rest):
    i = pl.program_id(0)

    @pl.when(i < tiles_a)
    def _():
        _outproj_tile(oa_ref, ca_ref, xa_ref, *rest)

    @pl.when(i >= tiles_a)
    def _():
        _outproj_tile(ob_ref, cb_ref, xb_ref, *rest)


def _outproj_tile(o_ref, c_ref, x_ref, wo_ref, wc_ref, g_ref, wr_ref, br_ref, upper_ref,
                  x1_ref, h2_ref, metar_ref, cnt_ref):
    tm = x_ref.shape[0]

    mix = _dotb(o_ref[...], wo_ref[...]) + _dotb(c_ref[...], wc_ref[...])
    x1 = x_ref[...] + mix
    x1_ref[...] = x1
    h2 = x1 * lax.rsqrt(jnp.mean(x1 * x1, axis=-1, keepdims=True) + RMS_EPS) * g_ref[...]
    h2_ref[...] = h2.astype(BF16)
    logits = lax.dot_general(wr_ref[...], h2.astype(BF16), (((1,), (1,)), ((), ())),
                             preferred_element_type=F32) + br_ref[...]
    row = lax.broadcasted_iota(I32, (N_EXPERTS, tm), 0).astype(F32)
    vals = logits
    idxs, tops = [], []
    for _ in range(TOP_K):
        m = jnp.max(vals, axis=0, keepdims=True)
        idx = jnp.min(jnp.where(vals == m, row, float(N_EXPERTS)), axis=0, keepdims=True)
        idxs.append(idx)
        tops.append(m)
        vals = jnp.where(row == idx, -jnp.inf, vals)
    exps = [jnp.exp(v - tops[0]) for v in tops]
    den = exps[0] + exps[1] + exps[2] + exps[3]
    onehots = [(row == idx).astype(F32) for idx in idxs]
    chosen = onehots[0] + onehots[1] + onehots[2] + onehots[3]
    before = jnp.dot(chosen.astype(BF16), upper_ref[...], preferred_element_type=F32)
    ranks = [jnp.sum(oh * before, axis=0, keepdims=True) for oh in onehots]
    gates = [e / den for e in exps]
    metar_ref[...] = jnp.concatenate(
        idxs + ranks + gates + [jnp.zeros((META_ROWS - 3 * TOP_K, tm), F32)], axis=0)
    cnt_ref[...] = jnp.broadcast_to(jnp.sum(chosen, axis=1, keepdims=True), (N_EXPERTS, LANES))[None]


def _outproj(first, second, wo, wc, g, wr, br):
    t_a, t_b = first[2].shape[0], second[2].shape[0]
    tm = TOKEN_TILE
    upper = (jnp.arange(tm)[:, None] < jnp.arange(tm)[None, :]).astype(BF16)
    tiles_a = t_a // tm
    t_total = t_a + t_b
    full = lambda shape: pl.BlockSpec(shape, lambda i: (0, 0))
    rows_a = lambda width: pl.BlockSpec((tm, width), lambda i: (jnp.minimum(i, tiles_a - 1), 0))
    rows_b = lambda width: pl.BlockSpec((tm, width), lambda i: (jnp.maximum(i - tiles_a, 0), 0))
    widths = (GDN_WIDTH, CONF_WIDTH, D_MODEL)
    return pl.pallas_call(
        functools.partial(_outproj_kernel, tiles_a),
        grid=(t_total // tm,),
        in_specs=[rows_a(w) for w in widths] + [rows_b(w) for w in widths] + [
            full((GDN_WIDTH, D_MODEL)),
            full((CONF_WIDTH, D_MODEL)),
            full((1, D_MODEL)),
            full((N_EXPERTS, D_MODEL)),
            full((N_EXPERTS, 1)),
            full((tm, tm)),
        ],
        out_specs=[
            pl.BlockSpec((tm, D_MODEL), lambda i: (i, 0)),
            pl.BlockSpec((tm, D_MODEL), lambda i: (i, 0)),
            pl.BlockSpec((META_ROWS, tm), lambda i: (0, i)),
            pl.BlockSpec((1, N_EXPERTS, LANES), lambda i: (i, 0, 0)),
        ],
        out_shape=[
            jax.ShapeDtypeStruct((t_total, D_MODEL), F32),
            jax.ShapeDtypeStruct((t_total, D_MODEL), BF16),
            jax.ShapeDtypeStruct((META_ROWS, t_total), F32),
            jax.ShapeDtypeStruct((t_total // tm, N_EXPERTS, LANES), F32),
        ],
        compiler_params=_params(("arbitrary",)),
        name="outproj_router",
    )(*first, *second, wo, wc, g, wr, br, upper)


def _segment_copies(i, seg_local_ref, seg_dst_ref, seg_n_ref, make_copy, wait):
    def copy(rows, local, dst):
        cp = make_copy(rows, pl.multiple_of(local, GRANULE), pl.multiple_of(dst, GRANULE))
        cp.wait() if wait else cp.start()

    def per_expert(ex, carry):
        j = i * N_EXPERTS + ex
        local, dst, n = seg_local_ref[j], seg_dst_ref[j], seg_n_ref[j]
        big = 8 * GRANULE

        def body(g, c):
            copy(big, local + g * big, dst + g * big)
            return c

        lax.fori_loop(0, n >> 3, body, 0)
        off = (n >> 3) * big
        for bit in (2, 1, 0):
            rows = GRANULE << bit
            has = (n >> bit) & 1

            @pl.when(has == 1)
            def _(rows=rows, off=off):
                copy(rows, local + off, dst + off)

            off = off + has * rows
        return carry

    lax.fori_loop(0, N_EXPERTS, per_expert, 0)


def _gap_copies(gap_start_ref, gap_n_ref, make_copy, wait):
    def per_expert(ex, carry):
        n = gap_n_ref[ex]
        start = pl.multiple_of(gap_start_ref[ex], GRANULE)
        off = 0 * n
        for bit in reversed(range(GAP_BITS)):
            rows = GRANULE << bit

            @pl.when(((n >> bit) & 1) == 1)
            def _(rows=rows, off=off):
                cp = make_copy(rows, pl.multiple_of(start + off, GRANULE))
                cp.wait() if wait else cp.start()

            off = off + ((n >> bit) & 1) * rows
        return carry

    lax.fori_loop(0, N_EXPERTS, per_expert, 0)


def _local_rows(meta_e, seg_local_ref, i):
    base = jnp.zeros(meta_e.shape, F32)
    for ex in range(N_EXPERTS):
        base = jnp.where(meta_e == float(ex), seg_local_ref[i * N_EXPERTS + ex].astype(F32), base)
    return base


def _dispatch_kernel(seg_local_ref, seg_dst_ref, seg_n_ref, gap_start_ref, gap_n_ref, tail_ref,
                     h_ref, mr_ref, xs_ref, slots_ref, xloc, zbuf, sems, zsem):
    i = pl.program_id(0)
    last = pl.num_programs(0) - 1
    tm = h_ref.shape[0]
    rt = xloc.shape[1]
    zrows = zbuf.shape[0]
    slot = i % 2

    def gap_copy(rows, dst):
        return pltpu.make_async_copy(zbuf.at[pl.ds(0, rows)], xs_ref.at[pl.ds(dst, rows)], zsem)

    def tail_copies(wait):
        def body(j, c):
            cp = gap_copy(zrows, pl.multiple_of(tail_ref[0] + j * zrows, zrows))
            cp.wait() if wait else cp.start()
            return c
        lax.fori_loop(0, tail_ref[1], body, 0)

    @pl.when(i == 0)
    def _():
        zbuf[...] = jnp.zeros(zbuf.shape, zbuf.dtype)
        _gap_copies(gap_start_ref, gap_n_ref, gap_copy, wait=False)
        tail_copies(wait=False)

    mr = mr_ref[...]
    dl = (_local_rows(mr[META_EXPERT:META_EXPERT + TOP_K], seg_local_ref, i)
          + mr[META_RANK:META_RANK + TOP_K])
    slots_ref[...] = jnp.concatenate(
        [dl, mr[META_GATE:META_GATE + TOP_K], jnp.zeros((LANES - 2 * TOP_K, tm), F32)], axis=0).T
    rr = lax.broadcasted_iota(I32, (rt, tm), 0).astype(F32)
    hit = rr == dl[0:1]
    for k in range(1, TOP_K):
        hit = hit | (rr == dl[k:k + 1])
    xloc[slot] = jnp.dot(hit.astype(BF16), h_ref[...], preferred_element_type=F32)

    def seg_copy(buf):
        def make(rows, local, dst):
            return pltpu.make_async_copy(xloc.at[buf, pl.ds(local, rows)],
                                         xs_ref.at[pl.ds(dst, rows)], sems.at[buf])
        return make

    _segment_copies(i, seg_local_ref, seg_dst_ref, seg_n_ref, seg_copy(slot), wait=False)

    @pl.when(i > 0)
    def _():
        _segment_copies(i - 1, seg_local_ref, seg_dst_ref, seg_n_ref, seg_copy(1 - slot), wait=True)

    @pl.when(i == last)
    def _():
        _segment_copies(i, seg_local_ref, seg_dst_ref, seg_n_ref, seg_copy(slot), wait=True)
        _gap_copies(gap_start_ref, gap_n_ref, gap_copy, wait=True)
        tail_copies(wait=True)


def _dispatch(tables, h2, metar, n_rows):
    t = h2.shape[0]
    tm = TOKEN_TILE
    return pl.pallas_call(
        _dispatch_kernel,
        grid_spec=pltpu.PrefetchScalarGridSpec(
            num_scalar_prefetch=6,
            grid=(t // tm,),
            in_specs=[
                pl.BlockSpec((tm, D_MODEL), lambda i, *_: (i, 0)),
                pl.BlockSpec((META_ROWS, tm), lambda i, *_: (0, i)),
            ],
            out_specs=[pl.BlockSpec(memory_space=pl.ANY),
                       pl.BlockSpec((tm, LANES), lambda i, *_: (i, 0))],
            scratch_shapes=[
                pltpu.VMEM((2, TILE_ROWS, D_MODEL), F32),
                pltpu.VMEM((GRANULE << (GAP_BITS - 1), D_MODEL), F32),
                pltpu.SemaphoreType.DMA((2,)),
                pltpu.SemaphoreType.DMA,
            ],
        ),
        out_shape=[jax.ShapeDtypeStruct((n_rows, D_MODEL), F32),
                   jax.ShapeDtypeStruct((t, LANES), F32)],
        compiler_params=_params(("arbitrary",)),
        name="dispatch",
    )(*tables, h2, metar)


def _expert_kernel(be_ref, rb_ref, used_ref, first_ref, slot_ref, next_ref,
                   x_ref, wgu_hbm, bgu_ref, wd_hbm, bdn_ref, y_ref,
                   wgu_f32, wd_f32, wgu_bf, wd_bf, sems):
    i = pl.program_id(0)

    def fetch(expert, slot):
        return (pltpu.make_async_copy(wgu_hbm.at[expert], wgu_f32.at[slot], sems.at[slot, 0]),
                pltpu.make_async_copy(wd_hbm.at[expert], wd_f32.at[slot], sems.at[slot, 1]))

    @pl.when(i == 0)
    def _():
        for cp in fetch(be_ref[0], 0):
            cp.start()

    @pl.when(first_ref[i] == 1)
    def _():
        slot = slot_ref[i]
        for cp in fetch(be_ref[i], slot):
            cp.wait()
        wgu_bf[...] = wgu_f32[slot].astype(BF16)
        wd_bf[...] = wd_f32[slot].astype(BF16)

        @pl.when(next_ref[i] >= 0)
        def _():
            for cp in fetch(next_ref[i], 1 - slot):
                cp.start()

    def ffn(x):
        gu = _dotb(x, wgu_bf[...]) + bgu_ref[...]
        x_glu = jnp.minimum(gu[:, :D_EXPERT], SWIGLU_LIMIT)
        x_lin = jnp.clip(gu[:, D_EXPERT:], -SWIGLU_LIMIT, SWIGLU_LIMIT)
        act = x_glu * _sigmoid(SWIGLU_ALPHA * x_glu) * (x_lin + 1.0)
        return _dotb(act, wd_bf[...]) + bdn_ref[...]

    used = used_ref[i]
    half = x_ref.shape[0] // 2

    @pl.when(used > half)
    def _():
        y_ref[...] = ffn(x_ref[...])

    @pl.when((used > 0) & (used <= half))
    def _():
        y_ref[:half, :] = ffn(x_ref[:half, :])
        y_ref[half:, :] = jnp.zeros((half, y_ref.shape[1]), y_ref.dtype)


def _experts(tables, xs, wgu, bgu, wd, bdn):
    rows = xs.shape[0]
    bm = EXPERT_ROWS
    nb = rows // bm
    return pl.pallas_call(
        _expert_kernel,
        grid_spec=pltpu.PrefetchScalarGridSpec(
            num_scalar_prefetch=6,
            grid=(nb,),
            in_specs=[
                pl.BlockSpec((bm, D_MODEL), lambda i, be, rb, *_: (rb[i], 0)),
                pl.BlockSpec(memory_space=pl.ANY),
                pl.BlockSpec((None, 1, 2 * D_EXPERT), lambda i, be, *_: (be[i], 0, 0)),
                pl.BlockSpec(memory_space=pl.ANY),
                pl.BlockSpec((None, 1, D_MODEL), lambda i, be, *_: (be[i], 0, 0)),
            ],
            out_specs=pl.BlockSpec((bm, D_MODEL), lambda i, be, rb, *_: (rb[i], 0)),
            scratch_shapes=[
                pltpu.VMEM((2, D_MODEL, 2 * D_EXPERT), F32),
                pltpu.VMEM((2, D_EXPERT, D_MODEL), F32),
                pltpu.VMEM((D_MODEL, 2 * D_EXPERT), BF16),
                pltpu.VMEM((D_EXPERT, D_MODEL), BF16),
                pltpu.SemaphoreType.DMA((2, 2)),
            ],
        ),
        out_shape=jax.ShapeDtypeStruct((rows, D_MODEL), F32),
        input_output_aliases={6: 0},
        compiler_params=pltpu.CompilerParams(dimension_semantics=("arbitrary",),
                                             vmem_limit_bytes=EXPERT_VMEM_LIMIT),
        name="experts",
    )(*tables, xs, wgu, bgu, wd, bdn)


def _combine_kernel(tiles_a, seg_local_ref, seg_dst_ref, seg_n_ref, x1_ref, mc_ref, yb_ref, gf_ref,
                    ya_ref, yb_out_ref, ybuf, wsel, sems):
    i = pl.program_id(0)
    n_tiles = pl.num_programs(0)
    tm = x1_ref.shape[0]
    rt = ybuf.shape[1]
    slot = i % 2

    def seg_copy(buf):
        def make(rows, local, src):
            return pltpu.make_async_copy(yb_ref.at[pl.ds(src, rows)],
                                         ybuf.at[buf, pl.ds(local, rows)], sems.at[buf])
        return make

    @pl.when(i == 0)
    def _():
        ybuf[...] = jnp.zeros(ybuf.shape, ybuf.dtype)
        _segment_copies(i, seg_local_ref, seg_dst_ref, seg_n_ref, seg_copy(slot), wait=False)

    @pl.when(i + 1 < n_tiles)
    def _():
        _segment_copies(i + 1, seg_local_ref, seg_dst_ref, seg_n_ref, seg_copy(1 - slot), wait=False)

    dl = mc_ref[:, 0:TOP_K]
    gate = mc_ref[:, TOP_K:2 * TOP_K]
    rb, cb = SEL_CHUNK
    col = lax.broadcasted_iota(I32, (rb, cb), 1).astype(F32)
    for r0 in range(0, tm, rb):
        rows_k = [jnp.broadcast_to(dl[r0:r0 + rb, k:k + 1], (rb, cb)) for k in range(TOP_K)]
        gate_k = [jnp.broadcast_to(gate[r0:r0 + rb, k:k + 1], (rb, cb)) for k in range(TOP_K)]
        for c0 in range(0, rt, cb):
            rr = col + float(c0)
            w = jnp.where(rr == rows_k[0], gate_k[0], 0.0)
            for k in range(1, TOP_K):
                w = jnp.where(rr == rows_k[k], gate_k[k], w)
            wsel[r0:r0 + rb, c0:c0 + cb] = w.astype(BF16)
    _segment_copies(i, seg_local_ref, seg_dst_ref, seg_n_ref, seg_copy(slot), wait=True)
    acc = x1_ref[...] + jnp.dot(wsel[...], ybuf[slot].astype(BF16), preferred_element_type=F32)
    y = acc * lax.rsqrt(jnp.mean(acc * acc, axis=-1, keepdims=True) + RMS_EPS) * gf_ref[...]

    @pl.when(i < tiles_a)
    def _():
        ya_ref[...] = y

    @pl.when(i >= tiles_a)
    def _():
        yb_out_ref[...] = y


def _combine(tables, x1, slots, yb, gfin, n_a):
    t = x1.shape[0]
    tm = TOKEN_TILE
    tiles_a = n_a // tm
    return pl.pallas_call(
        functools.partial(_combine_kernel, tiles_a),
        grid_spec=pltpu.PrefetchScalarGridSpec(
            num_scalar_prefetch=3,
            grid=(t // tm,),
            in_specs=[
                pl.BlockSpec((tm, D_MODEL), lambda i, *_: (i, 0)),
                pl.BlockSpec((tm, LANES), lambda i, *_: (i, 0)),
                pl.BlockSpec(memory_space=pl.ANY),
                pl.BlockSpec((1, D_MODEL), lambda i, *_: (0, 0)),
            ],
            out_specs=[
                pl.BlockSpec((tm, D_MODEL), lambda i, *_: (jnp.minimum(i, tiles_a - 1), 0)),
                pl.BlockSpec((tm, D_MODEL), lambda i, *_: (jnp.maximum(i - tiles_a, 0), 0)),
            ],
            scratch_shapes=[pltpu.VMEM((2, TILE_ROWS, D_MODEL), F32),
                            pltpu.VMEM((tm, TILE_ROWS), BF16),
                            pltpu.SemaphoreType.DMA((2,))],
        ),
        out_shape=[jax.ShapeDtypeStruct((n_a, D_MODEL), F32),
                   jax.ShapeDtypeStruct((t - n_a, D_MODEL), F32)],
        compiler_params=pltpu.CompilerParams(dimension_semantics=("arbitrary",),
                                             vmem_limit_bytes=EXPERT_VMEM_LIMIT),
        name="combine",
    )(*tables, x1, slots, yb, gfin)


def kernel(x_prompt, x_sample, state_gdn, state_qkv_conv, state_dwconv, norm_mix_g, w_in, w_conv_qkv,
           a_log, dt_bias, w_onorm, b_glu, w_dw, b_dw, ln_g, ln_b, w_out, norm_ffn_g, w_router,
           b_router, w_gate_up, b_gate_up, w_down, b_down, norm_final_g):
    bp, tp, _ = x_prompt.shape
    bs, ts, _ = x_sample.shape
    n_p, n_s = bp * tp, bs * ts

    wi = w_in[0]
    c_z = QKV_COLS
    c_bd = c_z + GDN_WIDTH
    c_glu = c_bd + 2 * GDN_HEADS
    wqkv = wi[:, :c_z].astype(BF16)
    wz = wi[:, c_z:c_bd].astype(BF16)
    wbd = jnp.pad(wi[:, c_bd:c_glu], ((0, 0), (0, LANES - 2 * GDN_HEADS))).astype(BF16)
    wglu = wi[:, c_glu:].astype(BF16)
    bglu = b_glu[0][None, :]
    lane_pad = (GDN_HEADS, LANES - 2 * GDN_HEADS)
    pa = jnp.pad(-jnp.exp(a_log[0].astype(F32)), lane_pad)[None, :]
    pb = jnp.pad(dt_bias[0].astype(F32), lane_pad)[None, :]
    gmix = norm_mix_g[0][None, :]
    wconv = w_conv_qkv[0]
    wonorm = w_onorm[0][None, :]
    wdw, bdw = w_dw[0], b_dw[0][None, :]
    lng, lnb = ln_g[0][None, :], ln_b[0][None, :]
    wo = w_out[0][:GDN_WIDTH].astype(BF16)
    wc = w_out[0][GDN_WIDTH:].astype(BF16)
    gffn = norm_ffn_g[0][None, :]
    wr, br = w_router[0].T.astype(BF16), b_router[0].astype(F32)[:, None]
    bgu = b_gate_up[0][:, None, :]
    bdn = b_down[0][:, None, :]
    gfin = norm_final_g[None, :]

    xp = x_prompt.reshape(n_p, D_MODEL)
    xs_tok = x_sample.reshape(n_s, D_MODEL)

    def mixers(x2d, bsz, t, hist_qkv, s0, hist_glu, chunk, gdn_tile, gdn_rows, conv_tile):
        qkv, z, bd, bdt, uu = _inproj(x2d, gmix, wqkv, wz, wbd, wglu, bglu, pa, pb, chunk, t)
        qkv3 = qkv.reshape(bsz, t, QKV_COLS)
        uu3 = uu.reshape(bsz, t, CONF_WIDTH)
        if bdt.ndim == 2:
            bdt = bdt.reshape(2 * GDN_HEADS, bsz, t).transpose(1, 0, 2)
        o, s_new = _gdn(qkv3, z.reshape(bsz, t, GDN_WIDTH), bd.reshape(bsz, t, 2 * GDN_HEADS),
                        bdt, hist_qkv, s0, wconv, wonorm, chunk, gdn_tile, gdn_rows)
        cc = _cconv(uu3, hist_glu, wdw, bdw, lng, lnb, conv_tile)
        return o.reshape(bsz * t, GDN_WIDTH), cc.reshape(bsz * t, CONF_WIDTH), s_new, qkv3, uu3

    zero_qkv = jnp.zeros((bp, QKV_TAPS - 1, QKV_COLS), F32)
    zero_s = jnp.zeros((bp, GDN_HEADS, HEAD_DIM, HEAD_DIM), F32)
    zero_glu = jnp.zeros((bp, DW_TAPS - 1, CONF_WIDTH), F32)
    o_p, c_p, s_p, qkv_p, uu_p = mixers(xp, bp, tp, zero_qkv, zero_s, zero_glu,
                                        PROMPT_CHUNK, GDN_TILE, 1, CONV_TILE)
    o_s, c_s, s_s, qkv_s, uu_s = mixers(xs_tok, bs, ts, state_qkv_conv[0], state_gdn[0],
                                        state_dwconv[0], ts, ts, GDN_SAMPLE_ROWS, ts)

    n_tok = n_p + n_s
    x1, h2, metar, cnt_tile = _outproj((o_p, c_p, xp), (o_s, c_s, xs_tok), wo, wc, gffn, wr, br)

    bm = EXPERT_ROWS
    n_tiles = n_tok // TOKEN_TILE
    counts = cnt_tile[:, :, 0].astype(I32)
    seg_rows = (counts + GRANULE - 1) // GRANULE * GRANULE
    ids = jnp.arange(N_EXPERTS, dtype=I32)
    tile_ids = jnp.arange(n_tiles, dtype=I32)
    earlier_e = (ids[:, None] < ids[None, :]).astype(I32)
    earlier_t = (tile_ids[None, :] < tile_ids[:, None]).astype(I32)
    seg_local = jnp.sum(seg_rows[:, :, None] * earlier_e[None], axis=1)
    seg_before = jnp.sum(earlier_t[:, :, None] * seg_rows[None], axis=1)
    rows_e = jnp.sum(seg_rows, axis=0)
    padded = (rows_e + bm - 1) // bm * bm
    pstart = jnp.sum(padded[:, None] * earlier_e, axis=0)
    pend = pstart + padded
    seg_dst = pstart[None, :] + seg_before
    n_used = jnp.maximum(pend[-1] // bm, 1).astype(I32)
    max_rows = n_tok * TOP_K + n_tiles * N_EXPERTS * (GRANULE - 1) + N_EXPERTS * (bm - 1)
    n_blocks = -(-max_rows // bm)
    blk = jnp.minimum(jnp.arange(n_blocks, dtype=I32), n_used - 1)
    block_expert = jnp.minimum(
        jnp.sum((pend[None, :] <= (blk * bm)[:, None]).astype(I32), axis=1), N_EXPERTS - 1)
    seg_tables = (seg_local.reshape(-1).astype(I32), seg_dst.reshape(-1).astype(I32),
                  (seg_rows // GRANULE).reshape(-1).astype(I32))
    gap_tables = ((pstart + rows_e).astype(I32), ((padded - rows_e) // GRANULE).astype(I32))
    zero_rows = GRANULE << (GAP_BITS - 1)
    tail_table = jnp.stack([pend[-1], (n_blocks * bm - pend[-1]) // zero_rows]).astype(I32)

    xs, slots = _dispatch(seg_tables + gap_tables + (tail_table,), h2, metar, n_blocks * bm)
    active = padded > 0
    later = active[None, :] & (ids[None, :] > ids[:, None])
    next_active = jnp.min(jnp.where(later, ids[None, :], N_EXPERTS), axis=1)
    next_active = jnp.where(next_active == N_EXPERTS, -1, next_active)
    order = jnp.sum(active.astype(I32)[:, None] * earlier_e, axis=0)
    steps = jnp.arange(n_blocks, dtype=I32)
    of_block = (block_expert[:, None] == ids[None, :]).astype(I32)
    per_block = lambda table: jnp.sum(of_block * table[None, :], axis=1)
    first = ((steps * bm == per_block(pstart)) & (steps < n_used)).astype(I32)
    used = jnp.clip(per_block(pstart + rows_e) - steps * bm, 0, bm)
    used = jnp.where(steps < n_used, used, 0).astype(I32)
    expert_tables = (block_expert, blk, used, first,
                     (per_block(order) % 2).astype(I32), per_block(next_active).astype(I32))
    yb = _experts(expert_tables, xs, w_gate_up[0], bgu, w_down[0], bdn)
    y_p, y_s = _combine(seg_tables, x1, slots, yb, gfin, n_p)

    hist_glu_s = jnp.concatenate([state_dwconv[0], uu_s], axis=1)[:, -(DW_TAPS - 1):]
    return (y_p.reshape(bp, tp, D_MODEL),
            y_s.reshape(bs, ts, D_MODEL),
            s_p[None],
            qkv_p[:, -(QKV_TAPS - 1):][None],
            uu_p[:, -(DW_TAPS - 1):][None],
            s_s[None],
            qkv_s[:, -(QKV_TAPS - 1):][None],
            hist_glu_s[None])
```

```python
import functools
import math

import jax
import jax.numpy as jnp
from jax import lax
from jax.experimental import pallas as pl
from jax.experimental.pallas import tpu as pltpu

F32 = jnp.float32
BF16 = jnp.bfloat16
I32 = jnp.int32
HIGHEST = lax.Precision.HIGHEST

D_MODEL = 1024
GDN_HEADS = 4
HEAD_DIM = 128
GDN_WIDTH = GDN_HEADS * HEAD_DIM
QKV_COLS = 3 * GDN_WIDTH
CONF_WIDTH = 512
QKV_TAPS = 4
DW_TAPS = 31
N_EXPERTS = 32
TOP_K = 4
D_EXPERT = 1024
SWIGLU_LIMIT = 7.0
SWIGLU_ALPHA = 1.702
RMS_EPS = 1e-6
LN_EPS = 1e-5
L2_EPS = 1e-6
PROMPT_CHUNK = 64

LANES = 128
SUBLANES = 8
TOKEN_TILE = 512
GDN_TILE = 512
GDN_SAMPLE_ROWS = 8
CONV_TILE = 512
CONV_ROWS = 32
EXPERT_ROWS = 1024
EXPERT_PART_ROWS = 256
GRANULE = SUBLANES
GAP_BITS = 7
TILE_ROWS = -(-(TOKEN_TILE * TOP_K + N_EXPERTS * (GRANULE - 1)) // 256) * 256
META_EXPERT, META_RANK, META_GATE, META_ROWS = 0, TOP_K, 2 * TOP_K, 16
assert EXPERT_ROWS // GRANULE <= 1 << GAP_BITS
SEL_CHUNK = (64, 256)
VMEM_LIMIT = 48 * 1024 * 1024
EXPERT_VMEM_LIMIT = 56 * 1024 * 1024


def _dotb(a, b):
    return jnp.dot(a.astype(BF16), b.astype(BF16), preferred_element_type=F32)


def _dotf(a, b):
    return jnp.dot(a, b, precision=HIGHEST, preferred_element_type=F32)


def _sigmoid(x):
    return jax.nn.sigmoid(x)


def _params(sem):
    return pltpu.CompilerParams(dimension_semantics=sem, vmem_limit_bytes=VMEM_LIMIT)


def _inproj_kernel(x_ref, g_ref, wqkv_ref, wz_ref, wbd_ref, wglu_ref, bglu_ref, pa_ref, pb_ref,
                   tri_ref, qkv_ref, z_ref, bd_ref, bdt_ref, uu_ref):
    x = x_ref[...]
    tm = x.shape[0]
    ms = jnp.mean(x * x, axis=-1, keepdims=True)
    h = (x * lax.rsqrt(ms + RMS_EPS) * g_ref[...]).astype(BF16)
    qkv_ref[...] = jnp.dot(h, wqkv_ref[...], preferred_element_type=F32)
    z_ref[...] = jnp.dot(h, wz_ref[...], preferred_element_type=F32)
    glu = jnp.dot(h, wglu_ref[...], preferred_element_type=F32) + bglu_ref[...]
    uu_ref[...] = glu[:, :CONF_WIDTH] * _sigmoid(glu[:, CONF_WIDTH:])
    bd = jnp.dot(h, wbd_ref[...], preferred_element_type=F32)
    beta = _sigmoid(bd)
    v = bd + pb_ref[...]
    softplus = jnp.maximum(v, 0.0) + jnp.log1p(jnp.exp(-jnp.abs(v)))
    g = pa_ref[...] * softplus
    g_hi, g_rest = _split_bf16(g)
    g_mid = g - g_hi.astype(F32) - g_rest.astype(F32)
    pieces = jnp.concatenate([g_hi, g_rest, g_mid.astype(BF16)], axis=1)
    sums = jnp.dot(tri_ref[...], pieces, preferred_element_type=F32)
    gc = sums[:, :LANES] + sums[:, LANES:2 * LANES] + sums[:, 2 * LANES:]
    lane = lax.broadcasted_iota(I32, (tm, LANES), 1)
    res = jnp.where(lane < GDN_HEADS, beta, gc)
    bd_ref[...] = res[:, :2 * GDN_HEADS]
    bdt_ref[...] = res.T[:2 * GDN_HEADS, :]


def _inproj(x, g, wqkv, wz, wbd, wglu, bglu, pa, pb, chunk, t_batch):
    t = x.shape[0]
    tm = TOKEN_TILE
    tiles_per_batch = t_batch // tm if t_batch % tm == 0 else 0
    pos = jnp.arange(tm)
    tri = ((pos[:, None] >= pos[None, :])
           & (pos[:, None] // chunk == pos[None, :] // chunk)).astype(BF16)
    full = lambda shape: pl.BlockSpec(shape, lambda i: (0, 0))
    if tiles_per_batch:
        bdt_spec = pl.BlockSpec((None, 2 * GDN_HEADS, tm),
                                lambda i: (i // tiles_per_batch, 0, i % tiles_per_batch))
        bdt_shape = jax.ShapeDtypeStruct((t // t_batch, 2 * GDN_HEADS, t_batch), F32)
    else:
        bdt_spec = pl.BlockSpec((2 * GDN_HEADS, tm), lambda i: (0, i))
        bdt_shape = jax.ShapeDtypeStruct((2 * GDN_HEADS, t), F32)
    return pl.pallas_call(
        _inproj_kernel,
        grid=(t // tm,),
        in_specs=[
            pl.BlockSpec((tm, D_MODEL), lambda i: (i, 0)),
            full((1, D_MODEL)),
            full((D_MODEL, QKV_COLS)),
            full((D_MODEL, GDN_WIDTH)),
            full((D_MODEL, LANES)),
            full((D_MODEL, 2 * CONF_WIDTH)),
            full((1, 2 * CONF_WIDTH)),
            full((1, LANES)),
            full((1, LANES)),
            full((tm, tm)),
        ],
        out_specs=[
            pl.BlockSpec((tm, QKV_COLS), lambda i: (i, 0)),
            pl.BlockSpec((tm, GDN_WIDTH), lambda i: (i, 0)),
            pl.BlockSpec((tm, 2 * GDN_HEADS), lambda i: (i, 0)),
            bdt_spec,
            pl.BlockSpec((tm, CONF_WIDTH), lambda i: (i, 0)),
        ],
        out_shape=[
            jax.ShapeDtypeStruct((t, QKV_COLS), F32),
            jax.ShapeDtypeStruct((t, GDN_WIDTH), F32),
            jax.ShapeDtypeStruct((t, 2 * GDN_HEADS), F32),
            bdt_shape,
            jax.ShapeDtypeStruct((t, CONF_WIDTH), F32),
        ],
        compiler_params=_params(("arbitrary",)),
        name="inproj",
    )(x, g, wqkv, wz, wbd, wglu, bglu, pa, pb, tri)


def _split_bf16(a):
    hi = a.astype(BF16)
    lo = (a - hi.astype(F32)).astype(BF16)
    return hi, lo


def _unit_lower_inverses(ms, eye, n):
    levels = int(math.log2(n)) - 1
    ps = [eye - m for m in ms]
    pw = [(-m).astype(BF16) for m in ms]
    pw = [jnp.dot(p, p, preferred_element_type=F32).astype(BF16) for p in pw]
    for level in range(1, levels + 1):
        last = level == levels
        nxt = []
        for u, p in enumerate(pw):
            if last:
                ps[u] = ps[u] + jnp.dot(ps[u].astype(BF16), p, preferred_element_type=F32)
            else:
                out = jnp.dot(jnp.concatenate([p, ps[u].astype(BF16)], axis=0), p,
                              preferred_element_type=F32)
                nxt.append(out[:n].astype(BF16))
                ps[u] = ps[u] + out[n:]
        pw = nxt
    return ps


def _gdn_kernel(chunk, x_ref, z_ref, bd_ref, bdt_ref, hist_ref, s0_ref, wc_ref, won_ref,
                o_ref, s_ref, xbuf):
    i = pl.program_id(1)
    nb, tt = x_ref.shape[0], x_ref.shape[1]
    nchunk = tt // chunk
    pad = SUBLANES
    hist_rows = QKV_TAPS - 1

    @pl.when(i == 0)
    def _():
        xbuf[:, pad - hist_rows:pad, :] = hist_ref[...]
        s_ref[...] = s0_ref[...]

    xbuf[:, pad:pad + tt, :] = x_ref[...]

    ii = lax.broadcasted_iota(I32, (chunk, chunk), 0)
    jj = lax.broadcasted_iota(I32, (chunk, chunk), 1)
    lower_incl = ii >= jj
    lower_strict = ii > jj
    eye = (ii == jj).astype(F32)
    won = won_ref[...]

    def conv_silu(b, col0):
        cols = slice(col0, col0 + HEAD_DIM)
        acc = wc_ref[hist_rows:hist_rows + 1, cols] * xbuf[b, pad:pad + tt, cols]
        for j in range(hist_rows):
            r0 = pad - hist_rows + j
            acc = acc + wc_ref[j:j + 1, cols] * xbuf[b, r0:r0 + tt, cols]
        return acc * _sigmoid(acc)

    units = []
    for b in range(nb):
        bd = bd_ref[b]
        bdt = bdt_ref[b]
        for h in range(GDN_HEADS):
            q = conv_silu(b, h * HEAD_DIM)
            k = conv_silu(b, GDN_WIDTH + h * HEAD_DIM)
            v = conv_silu(b, 2 * GDN_WIDTH + h * HEAD_DIM)
            q = q * lax.rsqrt(jnp.sum(q * q, axis=-1, keepdims=True) + L2_EPS) * (HEAD_DIM ** -0.5)
            k = k * lax.rsqrt(jnp.sum(k * k, axis=-1, keepdims=True) + L2_EPS)
            kt = k.T
            beta = bd[:, h:h + 1]
            gcol = bd[:, GDN_HEADS + h:GDN_HEADS + h + 1]
            grow = bdt[GDN_HEADS + h:GDN_HEADS + h + 1, :]
            for c in range(nchunk):
                rows = slice(c * chunk, (c + 1) * chunk)
                qc, kc, vc, ktc = q[rows], k[rows], v[rows], kt[:, rows]
                bc, gc, gr = beta[rows], gcol[rows], grow[:, rows]
                dec = jnp.exp(jnp.where(lower_incl, gc - gr, -jnp.inf))
                kb = kc * bc
                eg = jnp.exp(gc)
                both = _dotb(jnp.concatenate([kb, qc], axis=0), ktc)
                units.append(dict(
                    b=b, h=h, c=c,
                    m=both[:chunk] * jnp.where(lower_strict, dec, 0.0),
                    qk=both[chunk:] * dec,
                    rhs=jnp.concatenate([vc * bc, kb * eg], axis=1),
                    qg=qc * eg,
                    kdt=ktc * jnp.exp(gr[:, chunk - 1:chunk] - gr),
                    d_last=jnp.exp(gc[chunk - 1:chunk, :]),
                ))

    tinvs = _unit_lower_inverses([u["m"] for u in units], eye, chunk)
    for u, tinv in zip(units, tinvs):
        sol = _dotb(tinv, u["rhs"])
        u["u"], u["w"] = sol[:, :HEAD_DIM], sol[:, HEAD_DIM:]

    state = {(b, h): s_ref[b, h] for b in range(nb) for h in range(GDN_HEADS)}
    by_key = {(u["b"], u["h"], u["c"]): u for u in units}
    for c in range(nchunk):
        rows = slice(c * chunk, (c + 1) * chunk)
        for b in range(nb):
            for h in range(GDN_HEADS):
                u = by_key[(b, h, c)]
                s = state[(b, h)]
                ws = _dotb(jnp.concatenate([u["w"], u["qg"]], axis=0), s)
                v_new = u["u"] - ws[:chunk]
                upd = _dotb(jnp.concatenate([u["qk"], u["kdt"]], axis=0), v_new)
                o_c = ws[chunk:] + upd[:chunk]
                state[(b, h)] = s * u["d_last"] + upd[chunk:]
                cols = slice(h * HEAD_DIM, (h + 1) * HEAD_DIM)
                on = o_c * lax.rsqrt(jnp.mean(o_c * o_c, axis=-1, keepdims=True) + RMS_EPS) * won
                zc = z_ref[b, rows, cols]
                o_ref[b, rows, cols] = (on * (zc * _sigmoid(zc))).astype(o_ref.dtype)
    for (b, h), s in state.items():
        s_ref[b, h] = s

    xbuf[:, pad - hist_rows:pad, :] = x_ref[:, tt - hist_rows:tt, :]


def _gdn(qkv, z, bd, bdt, hist, s0, wconv, wonorm, chunk, tt, nb):
    b, t, _ = qkv.shape
    nt = t // tt
    return pl.pallas_call(
        functools.partial(_gdn_kernel, chunk),
        grid=(b // nb, nt),
        in_specs=[
            pl.BlockSpec((nb, tt, QKV_COLS), lambda bi, i: (bi, i, 0)),
            pl.BlockSpec((nb, tt, GDN_WIDTH), lambda bi, i: (bi, i, 0)),
            pl.BlockSpec((nb, tt, 2 * GDN_HEADS), lambda bi, i: (bi, i, 0)),
            pl.BlockSpec((nb, 2 * GDN_HEADS, tt), lambda bi, i: (bi, 0, i)),
            pl.BlockSpec((nb, QKV_TAPS - 1, QKV_COLS), lambda bi, i: (bi, 0, 0)),
            pl.BlockSpec((nb, GDN_HEADS, HEAD_DIM, HEAD_DIM), lambda bi, i: (bi, 0, 0, 0)),
            pl.BlockSpec((QKV_TAPS, QKV_COLS), lambda bi, i: (0, 0)),
            pl.BlockSpec((1, HEAD_DIM), lambda bi, i: (0, 0)),
        ],
        out_specs=[
            pl.BlockSpec((nb, tt, GDN_WIDTH), lambda bi, i: (bi, i, 0)),
            pl.BlockSpec((nb, GDN_HEADS, HEAD_DIM, HEAD_DIM), lambda bi, i: (bi, 0, 0, 0)),
        ],
        out_shape=[
            jax.ShapeDtypeStruct((b, t, GDN_WIDTH), BF16),
            jax.ShapeDtypeStruct((b, GDN_HEADS, HEAD_DIM, HEAD_DIM), F32),
        ],
        scratch_shapes=[pltpu.VMEM((nb, tt + SUBLANES, QKV_COLS), F32)],
        compiler_params=_params(("arbitrary", "arbitrary")),
        name="gdn",
    )(qkv, z, bd, bdt, hist, s0, wconv, wonorm)


def _cconv_kernel(carry, u_ref, hist_ref, w_ref, b_ref, lg_ref, lb_ref, c_ref, ubuf, shifted):
    i = pl.program_id(1)
    tt = u_ref.shape[0]
    hist_rows = DW_TAPS - 1
    pad = 4 * SUBLANES
    base = pad - hist_rows

    @pl.when(i == 0)
    def _():
        ubuf[base:pad, :] = hist_ref[...]

    ubuf[pad:pad + tt, :] = u_ref[...]
    span = shifted.shape[1]
    for s in range(1, SUBLANES):
        shifted[s - 1] = ubuf[s:s + span, :]

    def tap_rows(j, r0, rows):
        q, s = divmod(base + j, SUBLANES)
        start = q * SUBLANES + r0
        if s == 0:
            return ubuf[start:start + rows, :]
        return shifted[s - 1, start:start + rows, :]

    rows = min(CONV_ROWS, tt)
    for r0 in range(0, tt, rows):
        acc = w_ref[0:1, :] * tap_rows(0, r0, rows)
        for j in range(1, DW_TAPS):
            acc = acc + w_ref[j:j + 1, :] * tap_rows(j, r0, rows)
        cv = acc + b_ref[...]
        mu = jnp.mean(cv, axis=-1, keepdims=True)
        xc = cv - mu
        var = jnp.mean(xc * xc, axis=-1, keepdims=True)
        y = xc * lax.rsqrt(var + LN_EPS) * lg_ref[...] + lb_ref[...]
        c_ref[r0:r0 + rows, :] = (y * _sigmoid(y)).astype(c_ref.dtype)
    if carry:
        ubuf[base:pad, :] = u_ref[tt - hist_rows:tt, :]


def _cconv(uu, hist, w, b, lg, lb, tt):
    bsz, t, _ = uu.shape
    nt = t // tt
    vec = pl.BlockSpec((1, CONF_WIDTH), lambda bi, i: (0, 0))
    return pl.pallas_call(
        functools.partial(_cconv_kernel, nt > 1),
        grid=(bsz, nt),
        in_specs=[
            pl.BlockSpec((None, tt, CONF_WIDTH), lambda bi, i: (bi, i, 0)),
            pl.BlockSpec((None, DW_TAPS - 1, CONF_WIDTH), lambda bi, i: (bi, 0, 0)),
            pl.BlockSpec((DW_TAPS, CONF_WIDTH), lambda bi, i: (0, 0)),
            vec, vec, vec,
        ],
        out_specs=pl.BlockSpec((None, tt, CONF_WIDTH), lambda bi, i: (bi, i, 0)),
        out_shape=jax.ShapeDtypeStruct((bsz, t, CONF_WIDTH), BF16),
        scratch_shapes=[pltpu.VMEM((tt + 4 * SUBLANES, CONF_WIDTH), F32),
                        pltpu.VMEM((SUBLANES - 1, tt + 3 * SUBLANES, CONF_WIDTH), F32)],
        compiler_params=_params(("arbitrary", "arbitrary")),
        name="cconv",
    )(uu, hist, w, b, lg, lb)


def _outproj_kernel(tiles_a, oa_ref, ca_ref, xa_ref, ob_ref, cb_ref, xb_ref, *rest):
    i = pl.program_id(0)

    @pl.when(i < tiles_a)
    def _():
        _outproj_tile(oa_ref, ca_ref, xa_ref, *rest)

    @pl.when(i >= tiles_a)
    def _():
        _outproj_tile(ob_ref, cb_ref, xb_ref, *rest)


def _outproj_tile(o_ref, c_ref, x_ref, wo_ref, wc_ref, g_ref, wr_ref, br_ref, upper_ref,
                  x1_ref, h2_ref, metar_ref, cnt_ref):
    tm = x_ref.shape[0]

    mix = _dotb(o_ref[...], wo_ref[...]) + _dotb(c_ref[...], wc_ref[...])
    x1 = x_ref[...] + mix
    x1_ref[...] = x1
    h2 = x1 * lax.rsqrt(jnp.mean(x1 * x1, axis=-1, keepdims=True) + RMS_EPS) * g_ref[...]
    h2_ref[...] = h2.astype(BF16)
    logits = lax.dot_general(wr_ref[...], h2.astype(BF16), (((1,), (1,)), ((), ())),
                             preferred_element_type=F32) + br_ref[...]
    row = lax.broadcasted_iota(I32, (N_EXPERTS, tm), 0).astype(F32)
    vals = logits
    idxs, tops = [], []
    for _ in range(TOP_K):
        m = jnp.max(vals, axis=0, keepdims=True)
        idx = jnp.min(jnp.where(vals == m, row, float(N_EXPERTS)), axis=0, keepdims=True)
        idxs.append(idx)
        tops.append(m)
        vals = jnp.where(row == idx, -jnp.inf, vals)
    exps = [jnp.exp(v - tops[0]) for v in tops]
    den = exps[0] + exps[1] + exps[2] + exps[3]
    onehots = [(row == idx).astype(F32) for idx in idxs]
    chosen = onehots[0] + onehots[1] + onehots[2] + onehots[3]
    before = jnp.dot(chosen.astype(BF16), upper_ref[...], preferred_element_type=F32)
    ranks = [jnp.sum(oh * before, axis=0, keepdims=True) for oh in onehots]
    gates = [e / den for e in exps]
    metar_ref[...] = jnp.concatenate(
        idxs + ranks + gates + [jnp.zeros((META_ROWS - 3 * TOP_K, tm), F32)], axis=0)
    cnt_ref[...] = jnp.broadcast_to(jnp.sum(chosen, axis=1, keepdims=True), (N_EXPERTS, LANES))[None]


def _outproj(first, second, wo, wc, g, wr, br):
    t_a, t_b = first[2].shape[0], second[2].shape[0]
    tm = TOKEN_TILE
    upper = (jnp.arange(tm)[:, None] < jnp.arange(tm)[None, :]).astype(BF16)
    tiles_a = t_a // tm
    t_total = t_a + t_b
    full = lambda shape: pl.BlockSpec(shape, lambda i: (0, 0))
    rows_a = lambda width: pl.BlockSpec((tm, width), lambda i: (jnp.minimum(i, tiles_a - 1), 0))
    rows_b = lambda width: pl.BlockSpec((tm, width), lambda i: (jnp.maximum(i - tiles_a, 0), 0))
    widths = (GDN_WIDTH, CONF_WIDTH, D_MODEL)
    return pl.pallas_call(
        functools.partial(_outproj_kernel, tiles_a),
        grid=(t_total // tm,),
        in_specs=[rows_a(w) for w in widths] + [rows_b(w) for w in widths] + [
            full((GDN_WIDTH, D_MODEL)),
            full((CONF_WIDTH, D_MODEL)),
            full((1, D_MODEL)),
            full((N_EXPERTS, D_MODEL)),
            full((N_EXPERTS, 1)),
            full((tm, tm)),
        ],
        out_specs=[
            pl.BlockSpec((tm, D_MODEL), lambda i: (i, 0)),
            pl.BlockSpec((tm, D_MODEL), lambda i: (i, 0)),
            pl.BlockSpec((META_ROWS, tm), lambda i: (0, i)),
            pl.BlockSpec((1, N_EXPERTS, LANES), lambda i: (i, 0, 0)),
        ],
        out_shape=[
            jax.ShapeDtypeStruct((t_total, D_MODEL), F32),
            jax.ShapeDtypeStruct((t_total, D_MODEL), BF16),
            jax.ShapeDtypeStruct((META_ROWS, t_total), F32),
            jax.ShapeDtypeStruct((t_total // tm, N_EXPERTS, LANES), F32),
        ],
        compiler_params=_params(("arbitrary",)),
        name="outproj_router",
    )(*first, *second, wo, wc, g, wr, br, upper)


def _segment_copies(i, seg_local_ref, seg_dst_ref, seg_n_ref, make_copy, wait):
    def copy(rows, local, dst):
        cp = make_copy(rows, pl.multiple_of(local, GRANULE), pl.multiple_of(dst, GRANULE))
        cp.wait() if wait else cp.start()

    def per_expert(ex, carry):
        j = i * N_EXPERTS + ex
        local, dst, n = seg_local_ref[j], seg_dst_ref[j], seg_n_ref[j]
        big = 8 * GRANULE

        def body(g, c):
            copy(big, local + g * big, dst + g * big)
            return c

        lax.fori_loop(0, n >> 3, body, 0)
        off = (n >> 3) * big
        for bit in (2, 1, 0):
            rows = GRANULE << bit
            has = (n >> bit) & 1

            @pl.when(has == 1)
            def _(rows=rows, off=off):
                copy(rows, local + off, dst + off)

            off = off + has * rows
        return carry

    lax.fori_loop(0, N_EXPERTS, per_expert, 0)


def _gap_copies(gap_start_ref, gap_n_ref, make_copy, wait):
    def per_expert(ex, carry):
        n = gap_n_ref[ex]
        start = pl.multiple_of(gap_start_ref[ex], GRANULE)
        off = 0 * n
        for bit in reversed(range(GAP_BITS)):
            rows = GRANULE << bit

            @pl.when(((n >> bit) & 1) == 1)
            def _(rows=rows, off=off):
                cp = make_copy(rows, pl.multiple_of(start + off, GRANULE))
                cp.wait() if wait else cp.start()

            off = off + ((n >> bit) & 1) * rows
        return carry

    lax.fori_loop(0, N_EXPERTS, per_expert, 0)


def _local_rows(meta_e, seg_local_ref, i):
    base = jnp.zeros(meta_e.shape, F32)
    for ex in range(N_EXPERTS):
        base = jnp.where(meta_e == float(ex), seg_local_ref[i * N_EXPERTS + ex].astype(F32), base)
    return base


def _dispatch_kernel(seg_local_ref, seg_dst_ref, seg_n_ref, gap_start_ref, gap_n_ref, tail_ref,
                     h_ref, mr_ref, xs_ref, slots_ref, xloc, zbuf, sems, zsem):
    i = pl.program_id(0)
    last = pl.num_programs(0) - 1
    tm = h_ref.shape[0]
    rt = xloc.shape[1]
    zrows = zbuf.shape[0]
    slot = i % 2

    def gap_copy(rows, dst):
        return pltpu.make_async_copy(zbuf.at[pl.ds(0, rows)], xs_ref.at[pl.ds(dst, rows)], zsem)

    def tail_copies(wait):
        def body(j, c):
            cp = gap_copy(zrows, pl.multiple_of(tail_ref[0] + j * zrows, zrows))
            cp.wait() if wait else cp.start()
            return c
        lax.fori_loop(0, tail_ref[1], body, 0)

    @pl.when(i == 0)
    def _():
        zbuf[...] = jnp.zeros(zbuf.shape, zbuf.dtype)
        _gap_copies(gap_start_ref, gap_n_ref, gap_copy, wait=False)
        tail_copies(wait=False)

    mr = mr_ref[...]
    dl = (_local_rows(mr[META_EXPERT:META_EXPERT + TOP_K], seg_local_ref, i)
          + mr[META_RANK:META_RANK + TOP_K])
    slots_ref[...] = jnp.concatenate(
        [dl, mr[META_GATE:META_GATE + TOP_K], jnp.zeros((LANES - 2 * TOP_K, tm), F32)], axis=0).T
    rr = lax.broadcasted_iota(I32, (rt, tm), 0).astype(F32)
    hit = rr == dl[0:1]
    for k in range(1, TOP_K):
        hit = hit | (rr == dl[k:k + 1])
    xloc[slot] = jnp.dot(hit.astype(BF16), h_ref[...], preferred_element_type=F32)

    def seg_copy(buf):
        def make(rows, local, dst):
            return pltpu.make_async_copy(xloc.at[buf, pl.ds(local, rows)],
                                         xs_ref.at[pl.ds(dst, rows)], sems.at[buf])
        return make

    _segment_copies(i, seg_local_ref, seg_dst_ref, seg_n_ref, seg_copy(slot), wait=False)

    @pl.when(i > 0)
    def _():
        _segment_copies(i - 1, seg_local_ref, seg_dst_ref, seg_n_ref, seg_copy(1 - slot), wait=True)

    @pl.when(i == last)
    def _():
        _segment_copies(i, seg_local_ref, seg_dst_ref, seg_n_ref, seg_copy(slot), wait=True)
        _gap_copies(gap_start_ref, gap_n_ref, gap_copy, wait=True)
        tail_copies(wait=True)


def _dispatch(tables, h2, metar, n_rows):
    t = h2.shape[0]
    tm = TOKEN_TILE
    return pl.pallas_call(
        _dispatch_kernel,
        grid_spec=pltpu.PrefetchScalarGridSpec(
            num_scalar_prefetch=6,
            grid=(t // tm,),
            in_specs=[
                pl.BlockSpec((tm, D_MODEL), lambda i, *_: (i, 0)),
                pl.BlockSpec((META_ROWS, tm), lambda i, *_: (0, i)),
            ],
            out_specs=[pl.BlockSpec(memory_space=pl.ANY),
                       pl.BlockSpec((tm, LANES), lambda i, *_: (i, 0))],
            scratch_shapes=[
                pltpu.VMEM((2, TILE_ROWS, D_MODEL), F32),
                pltpu.VMEM((GRANULE << (GAP_BITS - 1), D_MODEL), F32),
                pltpu.SemaphoreType.DMA((2,)),
                pltpu.SemaphoreType.DMA,
            ],
        ),
        out_shape=[jax.ShapeDtypeStruct((n_rows, D_MODEL), F32),
                   jax.ShapeDtypeStruct((t, LANES), F32)],
        compiler_params=_params(("arbitrary",)),
        name="dispatch",
    )(*tables, h2, metar)


def _expert_kernel(be_ref, rb_ref, used_ref, first_ref, next_ref,
                   x_ref, wgu_hbm, bgu_ref, wd_hbm, bdn_ref, y_ref,
                   wgu_f32, wd_f32, wgu_bf, wd_bf, sems):
    i = pl.program_id(0)

    def fetch(expert):
        return (pltpu.make_async_copy(wgu_hbm.at[expert], wgu_f32, sems.at[0]),
                pltpu.make_async_copy(wd_hbm.at[expert], wd_f32, sems.at[1]))

    @pl.when(i == 0)
    def _():
        for cp in fetch(be_ref[0]):
            cp.start()

    @pl.when(first_ref[i] == 1)
    def _():
        for cp in fetch(be_ref[i]):
            cp.wait()
        wgu_bf[...] = wgu_f32[...].astype(BF16)
        wd_bf[...] = wd_f32[...].astype(BF16)

        @pl.when(next_ref[i] >= 0)
        def _():
            for cp in fetch(next_ref[i]):
                cp.start()

    def ffn(x):
        gu = _dotb(x, wgu_bf[...]) + bgu_ref[...]
        x_glu = jnp.minimum(gu[:, :D_EXPERT], SWIGLU_LIMIT)
        x_lin = jnp.clip(gu[:, D_EXPERT:], -SWIGLU_LIMIT, SWIGLU_LIMIT)
        act = x_glu * _sigmoid(SWIGLU_ALPHA * x_glu) * (x_lin + 1.0)
        return _dotb(act, wd_bf[...]) + bdn_ref[...]

    bm = x_ref.shape[0]
    parts = (used_ref[i] + EXPERT_PART_ROWS - 1) // EXPERT_PART_ROWS
    for p in range(1, bm // EXPERT_PART_ROWS + 1):
        @pl.when(parts == p)
        def _(rows=p * EXPERT_PART_ROWS):
            y_ref[:rows, :] = ffn(x_ref[:rows, :])
            if rows < bm:
                y_ref[rows:, :] = jnp.zeros((bm - rows, y_ref.shape[1]), y_ref.dtype)


def _experts(tables, xs, wgu, bgu, wd, bdn):
    rows = xs.shape[0]
    bm = EXPERT_ROWS
    nb = rows // bm
    return pl.pallas_call(
        _expert_kernel,
        grid_spec=pltpu.PrefetchScalarGridSpec(
            num_scalar_prefetch=5,
            grid=(nb,),
            in_specs=[
                pl.BlockSpec((bm, D_MODEL), lambda i, be, rb, *_: (rb[i], 0)),
                pl.BlockSpec(memory_space=pl.ANY),
                pl.BlockSpec((None, 1, 2 * D_EXPERT), lambda i, be, *_: (be[i], 0, 0)),
                pl.BlockSpec(memory_space=pl.ANY),
                pl.BlockSpec((None, 1, D_MODEL), lambda i, be, *_: (be[i], 0, 0)),
            ],
            out_specs=pl.BlockSpec((bm, D_MODEL), lambda i, be, rb, *_: (rb[i], 0)),
            scratch_shapes=[
                pltpu.VMEM((D_MODEL, 2 * D_EXPERT), F32),
                pltpu.VMEM((D_EXPERT, D_MODEL), F32),
                pltpu.VMEM((D_MODEL, 2 * D_EXPERT), BF16),
                pltpu.VMEM((D_EXPERT, D_MODEL), BF16),
                pltpu.SemaphoreType.DMA((2,)),
            ],
        ),
        out_shape=jax.ShapeDtypeStruct((rows, D_MODEL), F32),
        input_output_aliases={5: 0},
        compiler_params=pltpu.CompilerParams(dimension_semantics=("arbitrary",),
                                             vmem_limit_bytes=EXPERT_VMEM_LIMIT),
        name="experts",
    )(*tables, xs, wgu, bgu, wd, bdn)


def _combine_kernel(tiles_a, seg_local_ref, seg_dst_ref, seg_n_ref, x1_ref, mc_ref, yb_ref, gf_ref,
                    ya_ref, yb_out_ref, ybuf, wsel, sems):
    i = pl.program_id(0)
    n_tiles = pl.num_programs(0)
    tm = x1_ref.shape[0]
    rt = ybuf.shape[1]
    slot = i % 2

    def seg_copy(buf):
        def make(rows, local, src):
            return pltpu.make_async_copy(yb_ref.at[pl.ds(src, rows)],
                                         ybuf.at[buf, pl.ds(local, rows)], sems.at[buf])
        return make

    @pl.when(i == 0)
    def _():
        ybuf[...] = jnp.zeros(ybuf.shape, ybuf.dtype)
        _segment_copies(i, seg_local_ref, seg_dst_ref, seg_n_ref, seg_copy(slot), wait=False)

    @pl.when(i + 1 < n_tiles)
    def _():
        _segment_copies(i + 1, seg_local_ref, seg_dst_ref, seg_n_ref, seg_copy(1 - slot), wait=False)

    dl = mc_ref[:, 0:TOP_K]
    gate = mc_ref[:, TOP_K:2 * TOP_K]
    rb, cb = SEL_CHUNK
    col = lax.broadcasted_iota(I32, (rb, cb), 1).astype(F32)
    for r0 in range(0, tm, rb):
        rows_k = [jnp.broadcast_to(dl[r0:r0 + rb, k:k + 1], (rb, cb)) for k in range(TOP_K)]
        gate_k = [jnp.broadcast_to(gate[r0:r0 + rb, k:k + 1], (rb, cb)) for k in range(TOP_K)]
        for c0 in range(0, rt, cb):
            rr = col + float(c0)
            w = jnp.where(rr == rows_k[0], gate_k[0], 0.0)
            for k in range(1, TOP_K):
                w = jnp.where(rr == rows_k[k], gate_k[k], w)
            wsel[r0:r0 + rb, c0:c0 + cb] = w.astype(BF16)
    _segment_copies(i, seg_local_ref, seg_dst_ref, seg_n_ref, seg_copy(slot), wait=True)
    acc = x1_ref[...] + jnp.dot(wsel[...], ybuf[slot].astype(BF16), preferred_element_type=F32)
    y = acc * lax.rsqrt(jnp.mean(acc * acc, axis=-1, keepdims=True) + RMS_EPS) * gf_ref[...]

    @pl.when(i < tiles_a)
    def _():
        ya_ref[...] = y

    @pl.when(i >= tiles_a)
    def _():
        yb_out_ref[...] = y


def _combine(tables, x1, slots, yb, gfin, n_a):
    t = x1.shape[0]
    tm = TOKEN_TILE
    tiles_a = n_a // tm
    return pl.pallas_call(
        functools.partial(_combine_kernel, tiles_a),
        grid_spec=pltpu.PrefetchScalarGridSpec(
            num_scalar_prefetch=3,
            grid=(t // tm,),
            in_specs=[
                pl.BlockSpec((tm, D_MODEL), lambda i, *_: (i, 0)),
                pl.BlockSpec((tm, LANES), lambda i, *_: (i, 0)),
                pl.BlockSpec(memory_space=pl.ANY),
                pl.BlockSpec((1, D_MODEL), lambda i, *_: (0, 0)),
            ],
            out_specs=[
                pl.BlockSpec((tm, D_MODEL), lambda i, *_: (jnp.minimum(i, tiles_a - 1), 0)),
                pl.BlockSpec((tm, D_MODEL), lambda i, *_: (jnp.maximum(i - tiles_a, 0), 0)),
            ],
            scratch_shapes=[pltpu.VMEM((2, TILE_ROWS, D_MODEL), F32),
                            pltpu.VMEM((tm, TILE_ROWS), BF16),
                            pltpu.SemaphoreType.DMA((2,))],
        ),
        out_shape=[jax.ShapeDtypeStruct((n_a, D_MODEL), F32),
                   jax.ShapeDtypeStruct((t - n_a, D_MODEL), F32)],
        compiler_params=pltpu.CompilerParams(dimension_semantics=("arbitrary",),
                                             vmem_limit_bytes=EXPERT_VMEM_LIMIT),
        name="combine",
    )(*tables, x1, slots, yb, gfin)


def kernel(x_prompt, x_sample, state_gdn, state_qkv_conv, state_dwconv, norm_mix_g, w_in, w_conv_qkv,
           a_log, dt_bias, w_onorm, b_glu, w_dw, b_dw, ln_g, ln_b, w_out, norm_ffn_g, w_router,
           b_router, w_gate_up, b_gate_up, w_down, b_down, norm_final_g):
    bp, tp, _ = x_prompt.shape
    bs, ts, _ = x_sample.shape
    n_p, n_s = bp * tp, bs * ts

    wi = w_in[0]
    c_z = QKV_COLS
    c_bd = c_z + GDN_WIDTH
    c_glu = c_bd + 2 * GDN_HEADS
    wqkv = wi[:, :c_z].astype(BF16)
    wz = wi[:, c_z:c_bd].astype(BF16)
    wbd = jnp.pad(wi[:, c_bd:c_glu], ((0, 0), (0, LANES - 2 * GDN_HEADS))).astype(BF16)
    wglu = wi[:, c_glu:].astype(BF16)
    bglu = b_glu[0][None, :]
    lane_pad = (GDN_HEADS, LANES - 2 * GDN_HEADS)
    pa = jnp.pad(-jnp.exp(a_log[0].astype(F32)), lane_pad)[None, :]
    pb = jnp.pad(dt_bias[0].astype(F32), lane_pad)[None, :]
    gmix = norm_mix_g[0][None, :]
    wconv = w_conv_qkv[0]
    wonorm = w_onorm[0][None, :]
    wdw, bdw = w_dw[0], b_dw[0][None, :]
    lng, lnb = ln_g[0][None, :], ln_b[0][None, :]
    wo = w_out[0][:GDN_WIDTH].astype(BF16)
    wc = w_out[0][GDN_WIDTH:].astype(BF16)
    gffn = norm_ffn_g[0][None, :]
    wr, br = w_router[0].T.astype(BF16), b_router[0].astype(F32)[:, None]
    bgu = b_gate_up[0][:, None, :]
    bdn = b_down[0][:, None, :]
    gfin = norm_final_g[None, :]

    xp = x_prompt.reshape(n_p, D_MODEL)
    xs_tok = x_sample.reshape(n_s, D_MODEL)

    def mixers(x2d, bsz, t, hist_qkv, s0, hist_glu, chunk, gdn_tile, gdn_rows, conv_tile):
        qkv, z, bd, bdt, uu = _inproj(x2d, gmix, wqkv, wz, wbd, wglu, bglu, pa, pb, chunk, t)
        qkv3 = qkv.reshape(bsz, t, QKV_COLS)
        uu3 = uu.reshape(bsz, t, CONF_WIDTH)
        if bdt.ndim == 2:
            bdt = bdt.reshape(2 * GDN_HEADS, bsz, t).transpose(1, 0, 2)
        o, s_new = _gdn(qkv3, z.reshape(bsz, t, GDN_WIDTH), bd.reshape(bsz, t, 2 * GDN_HEADS),
                        bdt, hist_qkv, s0, wconv, wonorm, chunk, gdn_tile, gdn_rows)
        cc = _cconv(uu3, hist_glu, wdw, bdw, lng, lnb, conv_tile)
        return o.reshape(bsz * t, GDN_WIDTH), cc.reshape(bsz * t, CONF_WIDTH), s_new, qkv3, uu3

    zero_qkv = jnp.zeros((bp, QKV_TAPS - 1, QKV_COLS), F32)
    zero_s = jnp.zeros((bp, GDN_HEADS, HEAD_DIM, HEAD_DIM), F32)
    zero_glu = jnp.zeros((bp, DW_TAPS - 1, CONF_WIDTH), F32)
    o_p, c_p, s_p, qkv_p, uu_p = mixers(xp, bp, tp, zero_qkv, zero_s, zero_glu,
                                        PROMPT_CHUNK, GDN_TILE, 1, CONV_TILE)
    o_s, c_s, s_s, qkv_s, uu_s = mixers(xs_tok, bs, ts, state_qkv_conv[0], state_gdn[0],
                                        state_dwconv[0], ts, ts, GDN_SAMPLE_ROWS, ts)

    n_tok = n_p + n_s
    x1, h2, metar, cnt_tile = _outproj((o_p, c_p, xp), (o_s, c_s, xs_tok), wo, wc, gffn, wr, br)

    bm = EXPERT_ROWS
    n_tiles = n_tok // TOKEN_TILE
    counts = cnt_tile[:, :, 0].astype(I32)
    seg_rows = (counts + GRANULE - 1) // GRANULE * GRANULE
    ids = jnp.arange(N_EXPERTS, dtype=I32)
    tile_ids = jnp.arange(n_tiles, dtype=I32)
    earlier_e = (ids[:, None] < ids[None, :]).astype(I32)
    earlier_t = (tile_ids[None, :] < tile_ids[:, None]).astype(I32)
    seg_local = jnp.sum(seg_rows[:, :, None] * earlier_e[None], axis=1)
    seg_before = jnp.sum(earlier_t[:, :, None] * seg_rows[None], axis=1)
    rows_e = jnp.sum(seg_rows, axis=0)
    padded = (rows_e + bm - 1) // bm * bm
    pstart = jnp.sum(padded[:, None] * earlier_e, axis=0)
    pend = pstart + padded
    seg_dst = pstart[None, :] + seg_before
    n_used = jnp.maximum(pend[-1] // bm, 1).astype(I32)
    max_rows = n_tok * TOP_K + n_tiles * N_EXPERTS * (GRANULE - 1) + N_EXPERTS * (bm - 1)
    n_blocks = -(-max_rows // bm)
    blk = jnp.minimum(jnp.arange(n_blocks, dtype=I32), n_used - 1)
    block_expert = jnp.minimum(
        jnp.sum((pend[None, :] <= (blk * bm)[:, None]).astype(I32), axis=1), N_EXPERTS - 1)
    seg_tables = (seg_local.reshape(-1).astype(I32), seg_dst.reshape(-1).astype(I32),
                  (seg_rows // GRANULE).reshape(-1).astype(I32))
    gap_tables = ((pstart + rows_e).astype(I32), ((padded - rows_e) // GRANULE).astype(I32))
    zero_rows = GRANULE << (GAP_BITS - 1)
    tail_table = jnp.stack([pend[-1], (n_blocks * bm - pend[-1]) // zero_rows]).astype(I32)

    xs, slots = _dispatch(seg_tables + gap_tables + (tail_table,), h2, metar, n_blocks * bm)
    active = padded > 0
    later = active[None, :] & (ids[None, :] > ids[:, None])
    next_active = jnp.min(jnp.where(later, ids[None, :], N_EXPERTS), axis=1)
    next_active = jnp.where(next_active == N_EXPERTS, -1, next_active)
    steps = jnp.arange(n_blocks, dtype=I32)
    of_block = (block_expert[:, None] == ids[None, :]).astype(I32)
    per_block = lambda table: jnp.sum(of_block * table[None, :], axis=1)
    first = ((steps * bm == per_block(pstart)) & (steps < n_used)).astype(I32)
    used = jnp.clip(per_block(pstart + rows_e) - steps * bm, 0, bm)
    used = jnp.where(steps < n_used, used, 0).astype(I32)
    expert_tables = (block_expert, blk, used, first, per_block(next_active).astype(I32))
    yb = _experts(expert_tables, xs, w_gate_up[0], bgu, w_down[0], bdn)
    y_p, y_s = _combine(seg_tables, x1, slots, yb, gfin, n_p)

    hist_glu_s = jnp.concatenate([state_dwconv[0], uu_s], axis=1)[:, -(DW_TAPS - 1):]
    return (y_p.reshape(bp, tp, D_MODEL),
            y_s.reshape(bs, ts, D_MODEL),
            s_p[None],
            qkv_p[:, -(QKV_TAPS - 1):][None],
            uu_p[:, -(DW_TAPS - 1):][None],
            s_s[None],
            qkv_s[:, -(QKV_TAPS - 1):][None],
            hist_glu_s[None])
```

```python
import functools
import math

import jax
import jax.numpy as jnp
from jax import lax
from jax.experimental import pallas as pl
from jax.experimental.pallas import tpu as pltpu

F32 = jnp.float32
BF16 = jnp.bfloat16
I32 = jnp.int32
HIGHEST = lax.Precision.HIGHEST

D_MODEL = 1024
GDN_HEADS = 4
HEAD_DIM = 128
GDN_WIDTH = GDN_HEADS * HEAD_DIM
QKV_COLS = 3 * GDN_WIDTH
CONF_WIDTH = 512
QKV_TAPS = 4
DW_TAPS = 31
N_EXPERTS = 32
TOP_K = 4
D_EXPERT = 1024
SWIGLU_LIMIT = 7.0
SWIGLU_ALPHA = 1.702
RMS_EPS = 1e-6
LN_EPS = 1e-5
L2_EPS = 1e-6
PROMPT_CHUNK = 64

LANES = 128
SUBLANES = 8
TOKEN_TILE = 512
GDN_TILE = 512
GDN_SAMPLE_ROWS = 8
CONV_TILE = 512
CONV_ROWS = 32
EXPERT_ROWS = 512
EXPERT_PART_ROWS = 256
GRANULE = SUBLANES
GAP_BITS = 6
TILE_ROWS = -(-(TOKEN_TILE * TOP_K + N_EXPERTS * (GRANULE - 1)) // 256) * 256
META_EXPERT, META_RANK, META_GATE, META_ROWS = 0, TOP_K, 2 * TOP_K, 16
assert EXPERT_ROWS // GRANULE <= 1 << GAP_BITS
SEL_CHUNK = (64, 256)
VMEM_LIMIT = 48 * 1024 * 1024
EXPERT_VMEM_LIMIT = 56 * 1024 * 1024


def _dotb(a, b):
    return jnp.dot(a.astype(BF16), b.astype(BF16), preferred_element_type=F32)


def _dotf(a, b):
    return jnp.dot(a, b, precision=HIGHEST, preferred_element_type=F32)


def _sigmoid(x):
    return jax.nn.sigmoid(x)


def _params(sem):
    return pltpu.CompilerParams(dimension_semantics=sem, vmem_limit_bytes=VMEM_LIMIT)


def _inproj_kernel(x_ref, g_ref, wqkv_ref, wz_ref, wbd_ref, wglu_ref, bglu_ref, pa_ref, pb_ref,
                   tri_ref, qkv_ref, z_ref, bd_ref, bdt_ref, uu_ref):
    x = x_ref[...]
    tm = x.shape[0]
    ms = jnp.mean(x * x, axis=-1, keepdims=True)
    h = (x * lax.rsqrt(ms + RMS_EPS) * g_ref[...]).astype(BF16)
    qkv_ref[...] = jnp.dot(h, wqkv_ref[...], preferred_element_type=F32)
    z_ref[...] = jnp.dot(h, wz_ref[...], preferred_element_type=F32)
    glu = jnp.dot(h, wglu_ref[...], preferred_element_type=F32) + bglu_ref[...]
    uu_ref[...] = glu[:, :CONF_WIDTH] * _sigmoid(glu[:, CONF_WIDTH:])
    bd = jnp.dot(h, wbd_ref[...], preferred_element_type=F32)
    beta = _sigmoid(bd)
    v = bd + pb_ref[...]
    softplus = jnp.maximum(v, 0.0) + jnp.log1p(jnp.exp(-jnp.abs(v)))
    g = pa_ref[...] * softplus
    g_hi, g_rest = _split_bf16(g)
    g_mid = g - g_hi.astype(F32) - g_rest.astype(F32)
    pieces = jnp.concatenate([g_hi, g_rest, g_mid.astype(BF16)], axis=1)
    sums = jnp.dot(tri_ref[...], pieces, preferred_element_type=F32)
    gc = sums[:, :LANES] + sums[:, LANES:2 * LANES] + sums[:, 2 * LANES:]
    lane = lax.broadcasted_iota(I32, (tm, LANES), 1)
    res = jnp.where(lane < GDN_HEADS, beta, gc)
    bd_ref[...] = res[:, :2 * GDN_HEADS]
    bdt_ref[...] = res.T[:2 * GDN_HEADS, :]


def _inproj(x, g, wqkv, wz, wbd, wglu, bglu, pa, pb, chunk, t_batch):
    t = x.shape[0]
    tm = TOKEN_TILE
    tiles_per_batch = t_batch // tm if t_batch % tm == 0 else 0
    pos = jnp.arange(tm)
    tri = ((pos[:, None] >= pos[None, :])
           & (pos[:, None] // chunk == pos[None, :] // chunk)).astype(BF16)
    full = lambda shape: pl.BlockSpec(shape, lambda i: (0, 0))
    if tiles_per_batch:
        bdt_spec = pl.BlockSpec((None, 2 * GDN_HEADS, tm),
                                lambda i: (i // tiles_per_batch, 0, i % tiles_per_batch))
        bdt_shape = jax.ShapeDtypeStruct((t // t_batch, 2 * GDN_HEADS, t_batch), F32)
    else:
        bdt_spec = pl.BlockSpec((2 * GDN_HEADS, tm), lambda i: (0, i))
        bdt_shape = jax.ShapeDtypeStruct((2 * GDN_HEADS, t), F32)
    return pl.pallas_call(
        _inproj_kernel,
        grid=(t // tm,),
        in_specs=[
            pl.BlockSpec((tm, D_MODEL), lambda i: (i, 0)),
            full((1, D_MODEL)),
            full((D_MODEL, QKV_COLS)),
            full((D_MODEL, GDN_WIDTH)),
            full((D_MODEL, LANES)),
            full((D_MODEL, 2 * CONF_WIDTH)),
            full((1, 2 * CONF_WIDTH)),
            full((1, LANES)),
            full((1, LANES)),
            full((tm, tm)),
        ],
        out_specs=[
            pl.BlockSpec((tm, QKV_COLS), lambda i: (i, 0)),
            pl.BlockSpec((tm, GDN_WIDTH), lambda i: (i, 0)),
            pl.BlockSpec((tm, 2 * GDN_HEADS), lambda i: (i, 0)),
            bdt_spec,
            pl.BlockSpec((tm, CONF_WIDTH), lambda i: (i, 0)),
        ],
        out_shape=[
            jax.ShapeDtypeStruct((t, QKV_COLS), F32),
            jax.ShapeDtypeStruct((t, GDN_WIDTH), F32),
            jax.ShapeDtypeStruct((t, 2 * GDN_HEADS), F32),
            bdt_shape,
            jax.ShapeDtypeStruct((t, CONF_WIDTH), F32),
        ],
        compiler_params=_params(("arbitrary",)),
        name="inproj",
    )(x, g, wqkv, wz, wbd, wglu, bglu, pa, pb, tri)


def _split_bf16(a):
    hi = a.astype(BF16)
    lo = (a - hi.astype(F32)).astype(BF16)
    return hi, lo


def _unit_lower_inverses(ms, eye, n):
    levels = int(math.log2(n)) - 1
    ps = [eye - m for m in ms]
    pw = [(-m).astype(BF16) for m in ms]
    pw = [jnp.dot(p, p, preferred_element_type=F32).astype(BF16) for p in pw]
    for level in range(1, levels + 1):
        last = level == levels
        nxt = []
        for u, p in enumerate(pw):
            if last:
                ps[u] = ps[u] + jnp.dot(ps[u].astype(BF16), p, preferred_element_type=F32)
            else:
                out = jnp.dot(jnp.concatenate([p, ps[u].astype(BF16)], axis=0), p,
                              preferred_element_type=F32)
                nxt.append(out[:n].astype(BF16))
                ps[u] = ps[u] + out[n:]
        pw = nxt
    return ps


def _gdn_kernel(chunk, x_ref, z_ref, bd_ref, bdt_ref, hist_ref, s0_ref, wc_ref, won_ref,
                o_ref, s_ref, xbuf):
    i = pl.program_id(1)
    nb, tt = x_ref.shape[0], x_ref.shape[1]
    nchunk = tt // chunk
    pad = SUBLANES
    hist_rows = QKV_TAPS - 1

    @pl.when(i == 0)
    def _():
        xbuf[:, pad - hist_rows:pad, :] = hist_ref[...]
        s_ref[...] = s0_ref[...]

    xbuf[:, pad:pad + tt, :] = x_ref[...]

    ii = lax.broadcasted_iota(I32, (chunk, chunk), 0)
    jj = lax.broadcasted_iota(I32, (chunk, chunk), 1)
    lower_incl = ii >= jj
    lower_strict = ii > jj
    eye = (ii == jj).astype(F32)
    won = won_ref[...]

    def conv_silu(b, col0):
        cols = slice(col0, col0 + HEAD_DIM)
        acc = wc_ref[hist_rows:hist_rows + 1, cols] * xbuf[b, pad:pad + tt, cols]
        for j in range(hist_rows):
            r0 = pad - hist_rows + j
            acc = acc + wc_ref[j:j + 1, cols] * xbuf[b, r0:r0 + tt, cols]
        return acc * _sigmoid(acc)

    units = []
    for b in range(nb):
        bd = bd_ref[b]
        bdt = bdt_ref[b]
        for h in range(GDN_HEADS):
            q = conv_silu(b, h * HEAD_DIM)
            k = conv_silu(b, GDN_WIDTH + h * HEAD_DIM)
            v = conv_silu(b, 2 * GDN_WIDTH + h * HEAD_DIM)
            q = q * lax.rsqrt(jnp.sum(q * q, axis=-1, keepdims=True) + L2_EPS) * (HEAD_DIM ** -0.5)
            k = k * lax.rsqrt(jnp.sum(k * k, axis=-1, keepdims=True) + L2_EPS)
            kt = k.T
            beta = bd[:, h:h + 1]
            gcol = bd[:, GDN_HEADS + h:GDN_HEADS + h + 1]
            grow = bdt[GDN_HEADS + h:GDN_HEADS + h + 1, :]
            for c in range(nchunk):
                rows = slice(c * chunk, (c + 1) * chunk)
                qc, kc, vc, ktc = q[rows], k[rows], v[rows], kt[:, rows]
                bc, gc, gr = beta[rows], gcol[rows], grow[:, rows]
                dec = jnp.exp(jnp.where(lower_incl, gc - gr, -jnp.inf))
                kb = kc * bc
                eg = jnp.exp(gc)
                both = _dotb(jnp.concatenate([kb, qc], axis=0), ktc)
                units.append(dict(
                    b=b, h=h, c=c,
                    m=both[:chunk] * jnp.where(lower_strict, dec, 0.0),
                    qk=both[chunk:] * dec,
                    rhs=jnp.concatenate([vc * bc, kb * eg], axis=1),
                    qg=qc * eg,
                    kdt=ktc * jnp.exp(gr[:, chunk - 1:chunk] - gr),
                    d_last=jnp.exp(gc[chunk - 1:chunk, :]),
                ))

    tinvs = _unit_lower_inverses([u["m"] for u in units], eye, chunk)
    for u, tinv in zip(units, tinvs):
        sol = _dotb(tinv, u["rhs"])
        u["u"], u["w"] = sol[:, :HEAD_DIM], sol[:, HEAD_DIM:]

    state = {(b, h): s_ref[b, h] for b in range(nb) for h in range(GDN_HEADS)}
    by_key = {(u["b"], u["h"], u["c"]): u for u in units}
    for c in range(nchunk):
        rows = slice(c * chunk, (c + 1) * chunk)
        for b in range(nb):
            for h in range(GDN_HEADS):
                u = by_key[(b, h, c)]
                s = state[(b, h)]
                ws = _dotb(jnp.concatenate([u["w"], u["qg"]], axis=0), s)
                v_new = u["u"] - ws[:chunk]
                upd = _dotb(jnp.concatenate([u["qk"], u["kdt"]], axis=0), v_new)
                o_c = ws[chunk:] + upd[:chunk]
                state[(b, h)] = s * u["d_last"] + upd[chunk:]
                cols = slice(h * HEAD_DIM, (h + 1) * HEAD_DIM)
                on = o_c * lax.rsqrt(jnp.mean(o_c * o_c, axis=-1, keepdims=True) + RMS_EPS) * won
                zc = z_ref[b, rows, cols]
                o_ref[b, rows, cols] = (on * (zc * _sigmoid(zc))).astype(o_ref.dtype)
    for (b, h), s in state.items():
        s_ref[b, h] = s

    xbuf[:, pad - hist_rows:pad, :] = x_ref[:, tt - hist_rows:tt, :]


def _gdn(qkv, z, bd, bdt, hist, s0, wconv, wonorm, chunk, tt, nb):
    b, t, _ = qkv.shape
    nt = t // tt
    return pl.pallas_call(
        functools.partial(_gdn_kernel, chunk),
        grid=(b // nb, nt),
        in_specs=[
            pl.BlockSpec((nb, tt, QKV_COLS), lambda bi, i: (bi, i, 0)),
            pl.BlockSpec((nb, tt, GDN_WIDTH), lambda bi, i: (bi, i, 0)),
            pl.BlockSpec((nb, tt, 2 * GDN_HEADS), lambda bi, i: (bi, i, 0)),
            pl.BlockSpec((nb, 2 * GDN_HEADS, tt), lambda bi, i: (bi, 0, i)),
            pl.BlockSpec((nb, QKV_TAPS - 1, QKV_COLS), lambda bi, i: (bi, 0, 0)),
            pl.BlockSpec((nb, GDN_HEADS, HEAD_DIM, HEAD_DIM), lambda bi, i: (bi, 0, 0, 0)),
            pl.BlockSpec((QKV_TAPS, QKV_COLS), lambda bi, i: (0, 0)),
            pl.BlockSpec((1, HEAD_DIM), lambda bi, i: (0, 0)),
        ],
        out_specs=[
            pl.BlockSpec((nb, tt, GDN_WIDTH), lambda bi, i: (bi, i, 0)),
            pl.BlockSpec((nb, GDN_HEADS, HEAD_DIM, HEAD_DIM), lambda bi, i: (bi, 0, 0, 0)),
        ],
        out_shape=[
            jax.ShapeDtypeStruct((b, t, GDN_WIDTH), BF16),
            jax.ShapeDtypeStruct((b, GDN_HEADS, HEAD_DIM, HEAD_DIM), F32),
        ],
        scratch_shapes=[pltpu.VMEM((nb, tt + SUBLANES, QKV_COLS), F32)],
        compiler_params=_params(("arbitrary", "arbitrary")),
        name="gdn",
    )(qkv, z, bd, bdt, hist, s0, wconv, wonorm)


def _cconv_kernel(carry, u_ref, hist_ref, w_ref, b_ref, lg_ref, lb_ref, c_ref, ubuf, shifted):
    i = pl.program_id(1)
    tt = u_ref.shape[0]
    hist_rows = DW_TAPS - 1
    pad = 4 * SUBLANES
    base = pad - hist_rows

    @pl.when(i == 0)
    def _():
        ubuf[base:pad, :] = hist_ref[...]

    ubuf[pad:pad + tt, :] = u_ref[...]
    span = shifted.shape[1]
    for s in range(1, SUBLANES):
        shifted[s - 1] = ubuf[s:s + span, :]

    def tap_rows(j, r0, rows):
        q, s = divmod(base + j, SUBLANES)
        start = q * SUBLANES + r0
        if s == 0:
            return ubuf[start:start + rows, :]
        return shifted[s - 1, start:start + rows, :]

    rows = min(CONV_ROWS, tt)
    for r0 in range(0, tt, rows):
        acc = w_ref[0:1, :] * tap_rows(0, r0, rows)
        for j in range(1, DW_TAPS):
            acc = acc + w_ref[j:j + 1, :] * tap_rows(j, r0, rows)
        cv = acc + b_ref[...]
        mu = jnp.mean(cv, axis=-1, keepdims=True)
        xc = cv - mu
        var = jnp.mean(xc * xc, axis=-1, keepdims=True)
        y = xc * lax.rsqrt(var + LN_EPS) * lg_ref[...] + lb_ref[...]
        c_ref[r0:r0 + rows, :] = (y * _sigmoid(y)).astype(c_ref.dtype)
    if carry:
        ubuf[base:pad, :] = u_ref[tt - hist_rows:tt, :]


def _cconv(uu, hist, w, b, lg, lb, tt):
    bsz, t, _ = uu.shape
    nt = t // tt
    vec = pl.BlockSpec((1, CONF_WIDTH), lambda bi, i: (0, 0))
    return pl.pallas_call(
        functools.partial(_cconv_kernel, nt > 1),
        grid=(bsz, nt),
        in_specs=[
            pl.BlockSpec((None, tt, CONF_WIDTH), lambda bi, i: (bi, i, 0)),
            pl.BlockSpec((None, DW_TAPS - 1, CONF_WIDTH), lambda bi, i: (bi, 0, 0)),
            pl.BlockSpec((DW_TAPS, CONF_WIDTH), lambda bi, i: (0, 0)),
            vec, vec, vec,
        ],
        out_specs=pl.BlockSpec((None, tt, CONF_WIDTH), lambda bi, i: (bi, i, 0)),
        out_shape=jax.ShapeDtypeStruct((bsz, t, CONF_WIDTH), BF16),
        scratch_shapes=[pltpu.VMEM((tt + 4 * SUBLANES, CONF_WIDTH), F32),
                        pltpu.VMEM((SUBLANES - 1, tt + 3 * SUBLANES, CONF_WIDTH), F32)],
        compiler_params=_params(("arbitrary", "arbitrary")),
        name="cconv",
    )(uu, hist, w, b, lg, lb)


def _outproj_kernel(tiles_a, oa_ref, ca_ref, xa_ref, ob_ref, cb_ref, xb_ref, *rest):
    i = pl.program_id(0)

    @pl.when(i < tiles_a)
    def _():
        _outproj_tile(oa_ref, ca_ref, xa_ref, *rest)

    @pl.when(i >= tiles_a)
    def _():
        _outproj_tile(ob_ref, cb_ref, xb_ref, *rest)


def _outproj_tile(o_ref, c_ref, x_ref, wo_ref, wc_ref, g_ref, wr_ref, br_ref, upper_ref,
                  x1_ref, h2_ref, metar_ref, cnt_ref):
    tm = x_ref.shape[0]

    mix = _dotb(o_ref[...], wo_ref[...]) + _dotb(c_ref[...], wc_ref[...])
    x1 = x_ref[...] + mix
    x1_ref[...] = x1
    h2 = x1 * lax.rsqrt(jnp.mean(x1 * x1, axis=-1, keepdims=True) + RMS_EPS) * g_ref[...]
    h2_ref[...] = h2.astype(BF16)
    logits = lax.dot_general(wr_ref[...], h2.astype(BF16), (((1,), (1,)), ((), ())),
                             preferred_element_type=F32) + br_ref[...]
    row = lax.broadcasted_iota(I32, (N_EXPERTS, tm), 0).astype(F32)
    vals = logits
    idxs, tops = [], []
    for _ in range(TOP_K):
        m = jnp.max(vals, axis=0, keepdims=True)
        idx = jnp.min(jnp.where(vals == m, row, float(N_EXPERTS)), axis=0, keepdims=True)
        idxs.append(idx)
        tops.append(m)
        vals = jnp.where(row == idx, -jnp.inf, vals)
    exps = [jnp.exp(v - tops[0]) for v in tops]
    den = exps[0] + exps[1] + exps[2] + exps[3]
    onehots = [(row == idx).astype(F32) for idx in idxs]
    chosen = onehots[0] + onehots[1] + onehots[2] + onehots[3]
    before = jnp.dot(chosen.astype(BF16), upper_ref[...], preferred_element_type=F32)
    ranks = [jnp.sum(oh * before, axis=0, keepdims=True) for oh in onehots]
    gates = [e / den for e in exps]
    metar_ref[...] = jnp.concatenate(
        idxs + ranks + gates + [jnp.zeros((META_ROWS - 3 * TOP_K, tm), F32)], axis=0)
    cnt_ref[...] = jnp.broadcast_to(jnp.sum(chosen, axis=1, keepdims=True), (N_EXPERTS, LANES))[None]


def _outproj(first, second, wo, wc, g, wr, br):
    t_a, t_b = first[2].shape[0], second[2].shape[0]
    tm = TOKEN_TILE
    upper = (jnp.arange(tm)[:, None] < jnp.arange(tm)[None, :]).astype(BF16)
    tiles_a = t_a // tm
    t_total = t_a + t_b
    full = lambda shape: pl.BlockSpec(shape, lambda i: (0, 0))
    rows_a = lambda width: pl.BlockSpec((tm, width), lambda i: (jnp.minimum(i, tiles_a - 1), 0))
    rows_b = lambda width: pl.BlockSpec((tm, width), lambda i: (jnp.maximum(i - tiles_a, 0), 0))
    widths = (GDN_WIDTH, CONF_WIDTH, D_MODEL)
    return pl.pallas_call(
        functools.partial(_outproj_kernel, tiles_a),
        grid=(t_total // tm,),
        in_specs=[rows_a(w) for w in widths] + [rows_b(w) for w in widths] + [
            full((GDN_WIDTH, D_MODEL)),
            full((CONF_WIDTH, D_MODEL)),
            full((1, D_MODEL)),
            full((N_EXPERTS, D_MODEL)),
            full((N_EXPERTS, 1)),
            full((tm, tm)),
        ],
        out_specs=[
            pl.BlockSpec((tm, D_MODEL), lambda i: (i, 0)),
            pl.BlockSpec((tm, D_MODEL), lambda i: (i, 0)),
            pl.BlockSpec((META_ROWS, tm), lambda i: (0, i)),
            pl.BlockSpec((1, N_EXPERTS, LANES), lambda i: (i, 0, 0)),
        ],
        out_shape=[
            jax.ShapeDtypeStruct((t_total, D_MODEL), F32),
            jax.ShapeDtypeStruct((t_total, D_MODEL), BF16),
            jax.ShapeDtypeStruct((META_ROWS, t_total), F32),
            jax.ShapeDtypeStruct((t_total // tm, N_EXPERTS, LANES), F32),
        ],
        compiler_params=_params(("arbitrary",)),
        name="outproj_router",
    )(*first, *second, wo, wc, g, wr, br, upper)


def _segment_copies(i, seg_local_ref, seg_dst_ref, seg_n_ref, make_copy, wait):
    def copy(rows, local, dst):
        cp = make_copy(rows, pl.multiple_of(local, GRANULE), pl.multiple_of(dst, GRANULE))
        cp.wait() if wait else cp.start()

    def per_expert(ex, carry):
        j = i * N_EXPERTS + ex
        local, dst, n = seg_local_ref[j], seg_dst_ref[j], seg_n_ref[j]
        big = 8 * GRANULE

        def body(g, c):
            copy(big, local + g * big, dst + g * big)
            return c

        lax.fori_loop(0, n >> 3, body, 0)
        off = (n >> 3) * big
        for bit in (2, 1, 0):
            rows = GRANULE << bit
            has = (n >> bit) & 1

            @pl.when(has == 1)
            def _(rows=rows, off=off):
                copy(rows, local + off, dst + off)

            off = off + has * rows
        return carry

    lax.fori_loop(0, N_EXPERTS, per_expert, 0)


def _gap_copies(gap_start_ref, gap_n_ref, make_copy, wait):
    def per_expert(ex, carry):
        n = gap_n_ref[ex]
        start = pl.multiple_of(gap_start_ref[ex], GRANULE)
        off = 0 * n
        for bit in reversed(range(GAP_BITS)):
            rows = GRANULE << bit

            @pl.when(((n >> bit) & 1) == 1)
            def _(rows=rows, off=off):
                cp = make_copy(rows, pl.multiple_of(start + off, GRANULE))
                cp.wait() if wait else cp.start()

            off = off + ((n >> bit) & 1) * rows
        return carry

    lax.fori_loop(0, N_EXPERTS, per_expert, 0)


def _local_rows(meta_e, seg_local_ref, i):
    base = jnp.zeros(meta_e.shape, F32)
    for ex in range(N_EXPERTS):
        base = jnp.where(meta_e == float(ex), seg_local_ref[i * N_EXPERTS + ex].astype(F32), base)
    return base


def _dispatch_kernel(seg_local_ref, seg_dst_ref, seg_n_ref, gap_start_ref, gap_n_ref, tail_ref,
                     h_ref, mr_ref, xs_ref, slots_ref, xloc, zbuf, sems, zsem):
    i = pl.program_id(0)
    last = pl.num_programs(0) - 1
    tm = h_ref.shape[0]
    rt = xloc.shape[1]
    zrows = zbuf.shape[0]
    slot = i % 2

    def gap_copy(rows, dst):
        return pltpu.make_async_copy(zbuf.at[pl.ds(0, rows)], xs_ref.at[pl.ds(dst, rows)], zsem)

    def tail_copies(wait):
        def body(j, c):
            cp = gap_copy(zrows, pl.multiple_of(tail_ref[0] + j * zrows, zrows))
            cp.wait() if wait else cp.start()
            return c
        lax.fori_loop(0, tail_ref[1], body, 0)

    @pl.when(i == 0)
    def _():
        zbuf[...] = jnp.zeros(zbuf.shape, zbuf.dtype)
        _gap_copies(gap_start_ref, gap_n_ref, gap_copy, wait=False)
        tail_copies(wait=False)

    mr = mr_ref[...]
    dl = (_local_rows(mr[META_EXPERT:META_EXPERT + TOP_K], seg_local_ref, i)
          + mr[META_RANK:META_RANK + TOP_K])
    slots_ref[...] = jnp.concatenate(
        [dl, mr[META_GATE:META_GATE + TOP_K], jnp.zeros((LANES - 2 * TOP_K, tm), F32)], axis=0).T
    rr = lax.broadcasted_iota(I32, (rt, tm), 0).astype(F32)
    hit = rr == dl[0:1]
    for k in range(1, TOP_K):
        hit = hit | (rr == dl[k:k + 1])
    xloc[slot] = jnp.dot(hit.astype(BF16), h_ref[...], preferred_element_type=F32)

    def seg_copy(buf):
        def make(rows, local, dst):
            return pltpu.make_async_copy(xloc.at[buf, pl.ds(local, rows)],
                                         xs_ref.at[pl.ds(dst, rows)], sems.at[buf])
        return make

    _segment_copies(i, seg_local_ref, seg_dst_ref, seg_n_ref, seg_copy(slot), wait=False)

    @pl.when(i > 0)
    def _():
        _segment_copies(i - 1, seg_local_ref, seg_dst_ref, seg_n_ref, seg_copy(1 - slot), wait=True)

    @pl.when(i == last)
    def _():
        _segment_copies(i, seg_local_ref, seg_dst_ref, seg_n_ref, seg_copy(slot), wait=True)
        _gap_copies(gap_start_ref, gap_n_ref, gap_copy, wait=True)
        tail_copies(wait=True)


def _dispatch(tables, h2, metar, n_rows):
    t = h2.shape[0]
    tm = TOKEN_TILE
    return pl.pallas_call(
        _dispatch_kernel,
        grid_spec=pltpu.PrefetchScalarGridSpec(
            num_scalar_prefetch=6,
            grid=(t // tm,),
            in_specs=[
                pl.BlockSpec((tm, D_MODEL), lambda i, *_: (i, 0)),
                pl.BlockSpec((META_ROWS, tm), lambda i, *_: (0, i)),
            ],
            out_specs=[pl.BlockSpec(memory_space=pl.ANY),
                       pl.BlockSpec((tm, LANES), lambda i, *_: (i, 0))],
            scratch_shapes=[
                pltpu.VMEM((2, TILE_ROWS, D_MODEL), F32),
                pltpu.VMEM((GRANULE << (GAP_BITS - 1), D_MODEL), F32),
                pltpu.SemaphoreType.DMA((2,)),
                pltpu.SemaphoreType.DMA,
            ],
        ),
        out_shape=[jax.ShapeDtypeStruct((n_rows, D_MODEL), F32),
                   jax.ShapeDtypeStruct((t, LANES), F32)],
        compiler_params=_params(("arbitrary",)),
        name="dispatch",
    )(*tables, h2, metar)


def _expert_kernel(be_ref, rb_ref, used_ref, first_ref, next_ref,
                   x_ref, wgu_hbm, bgu_ref, wd_hbm, bdn_ref, y_ref,
                   wgu_f32, wd_f32, wgu_bf, wd_bf, sems):
    i = pl.program_id(0)

    def fetch(expert):
        return (pltpu.make_async_copy(wgu_hbm.at[expert], wgu_f32, sems.at[0]),
                pltpu.make_async_copy(wd_hbm.at[expert], wd_f32, sems.at[1]))

    @pl.when(i == 0)
    def _():
        for cp in fetch(be_ref[0]):
            cp.start()

    @pl.when(first_ref[i] == 1)
    def _():
        for cp in fetch(be_ref[i]):
            cp.wait()
        wgu_bf[...] = wgu_f32[...].astype(BF16)
        wd_bf[...] = wd_f32[...].astype(BF16)

        @pl.when(next_ref[i] >= 0)
        def _():
            for cp in fetch(next_ref[i]):
                cp.start()

    def ffn(x):
        gu = _dotb(x, wgu_bf[...]) + bgu_ref[...]
        x_glu = jnp.minimum(gu[:, :D_EXPERT], SWIGLU_LIMIT)
        x_lin = jnp.clip(gu[:, D_EXPERT:], -SWIGLU_LIMIT, SWIGLU_LIMIT)
        act = x_glu * _sigmoid(SWIGLU_ALPHA * x_glu) * (x_lin + 1.0)
        return _dotb(act, wd_bf[...]) + bdn_ref[...]

    bm = x_ref.shape[0]
    parts = (used_ref[i] + EXPERT_PART_ROWS - 1) // EXPERT_PART_ROWS
    for p in range(1, bm // EXPERT_PART_ROWS + 1):
        @pl.when(parts == p)
        def _(rows=p * EXPERT_PART_ROWS):
            y_ref[:rows, :] = ffn(x_ref[:rows, :])
            if rows < bm:
                y_ref[rows:, :] = jnp.zeros((bm - rows, y_ref.shape[1]), y_ref.dtype)


def _experts(tables, xs, wgu, bgu, wd, bdn):
    rows = xs.shape[0]
    bm = EXPERT_ROWS
    nb = rows // bm
    return pl.pallas_call(
        _expert_kernel,
        grid_spec=pltpu.PrefetchScalarGridSpec(
            num_scalar_prefetch=5,
            grid=(nb,),
            in_specs=[
                pl.BlockSpec((bm, D_MODEL), lambda i, be, rb, *_: (rb[i], 0)),
                pl.BlockSpec(memory_space=pl.ANY),
                pl.BlockSpec((None, 1, 2 * D_EXPERT), lambda i, be, *_: (be[i], 0, 0)),
                pl.BlockSpec(memory_space=pl.ANY),
                pl.BlockSpec((None, 1, D_MODEL), lambda i, be, *_: (be[i], 0, 0)),
            ],
            out_specs=pl.BlockSpec((bm, D_MODEL), lambda i, be, rb, *_: (rb[i], 0)),
            scratch_shapes=[
                pltpu.VMEM((D_MODEL, 2 * D_EXPERT), F32),
                pltpu.VMEM((D_EXPERT, D_MODEL), F32),
                pltpu.VMEM((D_MODEL, 2 * D_EXPERT), BF16),
                pltpu.VMEM((D_EXPERT, D_MODEL), BF16),
                pltpu.SemaphoreType.DMA((2,)),
            ],
        ),
        out_shape=jax.ShapeDtypeStruct((rows, D_MODEL), F32),
        input_output_aliases={5: 0},
        compiler_params=pltpu.CompilerParams(dimension_semantics=("arbitrary",),
                                             vmem_limit_bytes=EXPERT_VMEM_LIMIT),
        name="experts",
    )(*tables, xs, wgu, bgu, wd, bdn)


def _combine_kernel(tiles_a, seg_local_ref, seg_dst_ref, seg_n_ref, x1_ref, mc_ref, yb_ref, gf_ref,
                    ya_ref, yb_out_ref, ybuf, wsel, sems):
    i = pl.program_id(0)
    n_tiles = pl.num_programs(0)
    tm = x1_ref.shape[0]
    rt = ybuf.shape[1]
    slot = i % 2

    def seg_copy(buf):
        def make(rows, local, src):
            return pltpu.make_async_copy(yb_ref.at[pl.ds(src, rows)],
                                         ybuf.at[buf, pl.ds(local, rows)], sems.at[buf])
        return make

    @pl.when(i == 0)
    def _():
        ybuf[...] = jnp.zeros(ybuf.shape, ybuf.dtype)
        _segment_copies(i, seg_local_ref, seg_dst_ref, seg_n_ref, seg_copy(slot), wait=False)

    @pl.when(i + 1 < n_tiles)
    def _():
        _segment_copies(i + 1, seg_local_ref, seg_dst_ref, seg_n_ref, seg_copy(1 - slot), wait=False)

    dl = mc_ref[:, 0:TOP_K]
    gate = mc_ref[:, TOP_K:2 * TOP_K]
    rb, cb = SEL_CHUNK
    col = lax.broadcasted_iota(I32, (rb, cb), 1).astype(F32)
    for r0 in range(0, tm, rb):
        rows_k = [jnp.broadcast_to(dl[r0:r0 + rb, k:k + 1], (rb, cb)) for k in range(TOP_K)]
        gate_k = [jnp.broadcast_to(gate[r0:r0 + rb, k:k + 1], (rb, cb)) for k in range(TOP_K)]
        for c0 in range(0, rt, cb):
            rr = col + float(c0)
            w = jnp.where(rr == rows_k[0], gate_k[0], 0.0)
            for k in range(1, TOP_K):
                w = jnp.where(rr == rows_k[k], gate_k[k], w)
            wsel[r0:r0 + rb, c0:c0 + cb] = w.astype(BF16)
    _segment_copies(i, seg_local_ref, seg_dst_ref, seg_n_ref, seg_copy(slot), wait=True)
    acc = x1_ref[...] + jnp.dot(wsel[...], ybuf[slot].astype(BF16), preferred_element_type=F32)
    y = acc * lax.rsqrt(jnp.mean(acc * acc, axis=-1, keepdims=True) + RMS_EPS) * gf_ref[...]

    @pl.when(i < tiles_a)
    def _():
        ya_ref[...] = y

    @pl.when(i >= tiles_a)
    def _():
        yb_out_ref[...] = y


def _combine(tables, x1, slots, yb, gfin, n_a):
    t = x1.shape[0]
    tm = TOKEN_TILE
    tiles_a = n_a // tm
    return pl.pallas_call(
        functools.partial(_combine_kernel, tiles_a),
        grid_spec=pltpu.PrefetchScalarGridSpec(
            num_scalar_prefetch=3,
            grid=(t // tm,),
            in_specs=[
                pl.BlockSpec((tm, D_MODEL), lambda i, *_: (i, 0)),
                pl.BlockSpec((tm, LANES), lambda i, *_: (i, 0)),
                pl.BlockSpec(memory_space=pl.ANY),
                pl.BlockSpec((1, D_MODEL), lambda i, *_: (0, 0)),
            ],
            out_specs=[
                pl.BlockSpec((tm, D_MODEL), lambda i, *_: (jnp.minimum(i, tiles_a - 1), 0)),
                pl.BlockSpec((tm, D_MODEL), lambda i, *_: (jnp.maximum(i - tiles_a, 0), 0)),
            ],
            scratch_shapes=[pltpu.VMEM((2, TILE_ROWS, D_MODEL), F32),
                            pltpu.VMEM((tm, TILE_ROWS), BF16),
                            pltpu.SemaphoreType.DMA((2,))],
        ),
        out_shape=[jax.ShapeDtypeStruct((n_a, D_MODEL), F32),
                   jax.ShapeDtypeStruct((t - n_a, D_MODEL), F32)],
        compiler_params=pltpu.CompilerParams(dimension_semantics=("arbitrary",),
                                             vmem_limit_bytes=EXPERT_VMEM_LIMIT),
        name="combine",
    )(*tables, x1, slots, yb, gfin)


def kernel(x_prompt, x_sample, state_gdn, state_qkv_conv, state_dwconv, norm_mix_g, w_in, w_conv_qkv,
           a_log, dt_bias, w_onorm, b_glu, w_dw, b_dw, ln_g, ln_b, w_out, norm_ffn_g, w_router,
           b_router, w_gate_up, b_gate_up, w_down, b_down, norm_final_g):
    bp, tp, _ = x_prompt.shape
    bs, ts, _ = x_sample.shape
    n_p, n_s = bp * tp, bs * ts

    wi = w_in[0]
    c_z = QKV_COLS
    c_bd = c_z + GDN_WIDTH
    c_glu = c_bd + 2 * GDN_HEADS
    wqkv = wi[:, :c_z].astype(BF16)
    wz = wi[:, c_z:c_bd].astype(BF16)
    wbd = jnp.pad(wi[:, c_bd:c_glu], ((0, 0), (0, LANES - 2 * GDN_HEADS))).astype(BF16)
    wglu = wi[:, c_glu:].astype(BF16)
    bglu = b_glu[0][None, :]
    lane_pad = (GDN_HEADS, LANES - 2 * GDN_HEADS)
    pa = jnp.pad(-jnp.exp(a_log[0].astype(F32)), lane_pad)[None, :]
    pb = jnp.pad(dt_bias[0].astype(F32), lane_pad)[None, :]
    gmix = norm_mix_g[0][None, :]
    wconv = w_conv_qkv[0]
    wonorm = w_onorm[0][None, :]
    wdw, bdw = w_dw[0], b_dw[0][None, :]
    lng, lnb = ln_g[0][None, :], ln_b[0][None, :]
    wo = w_out[0][:GDN_WIDTH].astype(BF16)
    wc = w_out[0][GDN_WIDTH:].astype(BF16)
    gffn = norm_ffn_g[0][None, :]
    wr, br = w_router[0].T.astype(BF16), b_router[0].astype(F32)[:, None]
    bgu = b_gate_up[0][:, None, :]
    bdn = b_down[0][:, None, :]
    gfin = norm_final_g[None, :]

    xp = x_prompt.reshape(n_p, D_MODEL)
    xs_tok = x_sample.reshape(n_s, D_MODEL)

    def mixers(x2d, bsz, t, hist_qkv, s0, hist_glu, chunk, gdn_tile, gdn_rows, conv_tile):
        qkv, z, bd, bdt, uu = _inproj(x2d, gmix, wqkv, wz, wbd, wglu, bglu, pa, pb, chunk, t)
        qkv3 = qkv.reshape(bsz, t, QKV_COLS)
        uu3 = uu.reshape(bsz, t, CONF_WIDTH)
        if bdt.ndim == 2:
            bdt = bdt.reshape(2 * GDN_HEADS, bsz, t).transpose(1, 0, 2)
        o, s_new = _gdn(qkv3, z.reshape(bsz, t, GDN_WIDTH), bd.reshape(bsz, t, 2 * GDN_HEADS),
                        bdt, hist_qkv, s0, wconv, wonorm, chunk, gdn_tile, gdn_rows)
        cc = _cconv(uu3, hist_glu, wdw, bdw, lng, lnb, conv_tile)
        return o.reshape(bsz * t, GDN_WIDTH), cc.reshape(bsz * t, CONF_WIDTH), s_new, qkv3, uu3

    zero_qkv = jnp.zeros((bp, QKV_TAPS - 1, QKV_COLS), F32)
    zero_s = jnp.zeros((bp, GDN_HEADS, HEAD_DIM, HEAD_DIM), F32)
    zero_glu = jnp.zeros((bp, DW_TAPS - 1, CONF_WIDTH), F32)
    o_p, c_p, s_p, qkv_p, uu_p = mixers(xp, bp, tp, zero_qkv, zero_s, zero_glu,
                                        PROMPT_CHUNK, GDN_TILE, 1, CONV_TILE)
    o_s, c_s, s_s, qkv_s, uu_s = mixers(xs_tok, bs, ts, state_qkv_conv[0], state_gdn[0],
                                        state_dwconv[0], ts, ts, GDN_SAMPLE_ROWS, ts)

    n_tok = n_p + n_s
    x1, h2, metar, cnt_tile = _outproj((o_p, c_p, xp), (o_s, c_s, xs_tok), wo, wc, gffn, wr, br)

    bm = EXPERT_ROWS
    n_tiles = n_tok // TOKEN_TILE
    counts = cnt_tile[:, :, 0].astype(I32)
    seg_rows = (counts + GRANULE - 1) // GRANULE * GRANULE
    ids = jnp.arange(N_EXPERTS, dtype=I32)
    tile_ids = jnp.arange(n_tiles, dtype=I32)
    earlier_e = (ids[:, None] < ids[None, :]).astype(I32)
    earlier_t = (tile_ids[None, :] < tile_ids[:, None]).astype(I32)
    seg_local = jnp.sum(seg_rows[:, :, None] * earlier_e[None], axis=1)
    seg_before = jnp.sum(earlier_t[:, :, None] * seg_rows[None], axis=1)
    rows_e = jnp.sum(seg_rows, axis=0)
    padded = (rows_e + bm - 1) // bm * bm
    pstart = jnp.sum(padded[:, None] * earlier_e, axis=0)
    pend = pstart + padded
    seg_dst = pstart[None, :] + seg_before
    n_used = jnp.maximum(pend[-1] // bm, 1).astype(I32)
    max_rows = n_tok * TOP_K + n_tiles * N_EXPERTS * (GRANULE - 1) + N_EXPERTS * (bm - 1)
    n_blocks = -(-max_rows // bm)
    blk = jnp.minimum(jnp.arange(n_blocks, dtype=I32), n_used - 1)
    block_expert = jnp.minimum(
        jnp.sum((pend[None, :] <= (blk * bm)[:, None]).astype(I32), axis=1), N_EXPERTS - 1)
    seg_tables = (seg_local.reshape(-1).astype(I32), seg_dst.reshape(-1).astype(I32),
                  (seg_rows // GRANULE).reshape(-1).astype(I32))
    gap_tables = ((pstart + rows_e).astype(I32), ((padded - rows_e) // GRANULE).astype(I32))
    zero_rows = GRANULE << (GAP_BITS - 1)
    tail_table = jnp.stack([pend[-1], (n_blocks * bm - pend[-1]) // zero_rows]).astype(I32)

    xs, slots = _dispatch(seg_tables + gap_tables + (tail_table,), h2, metar, n_blocks * bm)
    active = padded > 0
    later = active[None, :] & (ids[None, :] > ids[:, None])
    next_active = jnp.min(jnp.where(later, ids[None, :], N_EXPERTS), axis=1)
    next_active = jnp.where(next_active == N_EXPERTS, -1, next_active)
    steps = jnp.arange(n_blocks, dtype=I32)
    of_block = (block_expert[:, None] == ids[None, :]).astype(I32)
    per_block = lambda table: jnp.sum(of_block * table[None, :], axis=1)
    first = ((steps * bm == per_block(pstart)) & (steps < n_used)).astype(I32)
    used = jnp.clip(per_block(pstart + rows_e) - steps * bm, 0, bm)
    used = jnp.where(steps < n_used, used, 0).astype(I32)
    expert_tables = (block_expert, blk, used, first, per_block(next_active).astype(I32))
    yb = _experts(expert_tables, xs, w_gate_up[0], bgu, w_down[0], bdn)
    y_p, y_s = _combine(seg_tables, x1, slots, yb, gfin, n_p)

    hist_glu_s = jnp.concatenate([state_dwconv[0], uu_s], axis=1)[:, -(DW_TAPS - 1):]
    return (y_p.reshape(bp, tp, D_MODEL),
            y_s.reshape(bs, ts, D_MODEL),
            s_p[None],
            qkv_p[:, -(QKV_TAPS - 1):][None],
            uu_p[:, -(DW_TAPS - 1):][None],
            s_s[None],
            qkv_s[:, -(QKV_TAPS - 1):][None],
            hist_glu_s[None])
```

```python
import functools
import math

import jax
import jax.numpy as jnp
from jax import lax
from jax.experimental import pallas as pl
from jax.experimental.pallas import tpu as pltpu

F32 = jnp.float32
BF16 = jnp.bfloat16
I32 = jnp.int32
HIGHEST = lax.Precision.HIGHEST

D_MODEL = 1024
GDN_HEADS = 4
HEAD_DIM = 128
GDN_WIDTH = GDN_HEADS * HEAD_DIM
QKV_COLS = 3 * GDN_WIDTH
CONF_WIDTH = 512
QKV_TAPS = 4
DW_TAPS = 31
N_EXPERTS = 32
TOP_K = 4
D_EXPERT = 1024
SWIGLU_LIMIT = 7.0
SWIGLU_ALPHA = 1.702
RMS_EPS = 1e-6
LN_EPS = 1e-5
L2_EPS = 1e-6
PROMPT_CHUNK = 64

LANES = 128
SUBLANES = 8
TOKEN_TILE = 512
GDN_TILE = 512
GDN_SAMPLE_ROWS = 8
CONV_TILE = 512
CONV_ROWS = 32
EXPERT_ROWS = 512
EXPERT_PART_ROWS = 256
GRANULE = SUBLANES
GAP_BITS = 6
TILE_ROWS = -(-(TOKEN_TILE * TOP_K + N_EXPERTS * (GRANULE - 1)) // 256) * 256
META_EXPERT, META_RANK, META_GATE, META_ROWS = 0, TOP_K, 2 * TOP_K, 16
assert EXPERT_ROWS // GRANULE <= 1 << GAP_BITS
SEL_CHUNK = (64, 256)
VMEM_LIMIT = 48 * 1024 * 1024
EXPERT_VMEM_LIMIT = 56 * 1024 * 1024


def _dotb(a, b):
    return jnp.dot(a.astype(BF16), b.astype(BF16), preferred_element_type=F32)


def _dotf(a, b):
    return jnp.dot(a, b, precision=HIGHEST, preferred_element_type=F32)


def _sigmoid(x):
    return jax.nn.sigmoid(x)


def _params(sem):
    return pltpu.CompilerParams(dimension_semantics=sem, vmem_limit_bytes=VMEM_LIMIT)


def _inproj_kernel(x_ref, g_ref, wqkv_ref, wz_ref, wbd_ref, wglu_ref, bglu_ref, pa_ref, pb_ref,
                   tri_ref, qkv_ref, z_ref, bd_ref, bdt_ref, uu_ref):
    x = x_ref[...]
    tm = x.shape[0]
    ms = jnp.mean(x * x, axis=-1, keepdims=True)
    h = (x * lax.rsqrt(ms + RMS_EPS) * g_ref[...]).astype(BF16)
    qkv_ref[...] = jnp.dot(h, wqkv_ref[...], preferred_element_type=F32)
    z_ref[...] = jnp.dot(h, wz_ref[...], preferred_element_type=F32)
    glu = jnp.dot(h, wglu_ref[...], preferred_element_type=F32) + bglu_ref[...]
    uu_ref[...] = glu[:, :CONF_WIDTH] * _sigmoid(glu[:, CONF_WIDTH:])
    bd = jnp.dot(h, wbd_ref[...], preferred_element_type=F32)
    bdt = bd.T[:2 * GDN_HEADS, :]
    beta = _sigmoid(bdt)
    v = bdt + pb_ref[...]
    softplus = jnp.maximum(v, 0.0) + jnp.log1p(jnp.exp(-jnp.abs(v)))
    g = pa_ref[...] * softplus
    g_hi, g_rest = _split_bf16(g)
    g_mid = g - g_hi.astype(F32) - g_rest.astype(F32)
    pieces = jnp.concatenate([g_hi, g_rest, g_mid.astype(BF16)], axis=0)
    sums = jnp.dot(pieces, tri_ref[...], preferred_element_type=F32)
    rows = 2 * GDN_HEADS
    gc = sums[:rows] + sums[rows:2 * rows] + sums[2 * rows:]
    row = lax.broadcasted_iota(I32, (rows, tm), 0)
    res = jnp.where(row < GDN_HEADS, beta, gc)
    bdt_ref[...] = res
    bd_ref[...] = jnp.concatenate([res, jnp.zeros((LANES - rows, tm), F32)], axis=0).T[:, :rows]


def _inproj(x, g, wqkv, wz, wbd, wglu, bglu, pa, pb, chunk, t_batch):
    t = x.shape[0]
    tm = TOKEN_TILE
    tiles_per_batch = t_batch // tm if t_batch % tm == 0 else 0
    pos = jnp.arange(tm)
    tri = ((pos[:, None] <= pos[None, :])
           & (pos[:, None] // chunk == pos[None, :] // chunk)).astype(BF16)
    full = lambda shape: pl.BlockSpec(shape, lambda i: (0, 0))
    if tiles_per_batch:
        bdt_spec = pl.BlockSpec((None, 2 * GDN_HEADS, tm),
                                lambda i: (i // tiles_per_batch, 0, i % tiles_per_batch))
        bdt_shape = jax.ShapeDtypeStruct((t // t_batch, 2 * GDN_HEADS, t_batch), F32)
    else:
        bdt_spec = pl.BlockSpec((2 * GDN_HEADS, tm), lambda i: (0, i))
        bdt_shape = jax.ShapeDtypeStruct((2 * GDN_HEADS, t), F32)
    return pl.pallas_call(
        _inproj_kernel,
        grid=(t // tm,),
        in_specs=[
            pl.BlockSpec((tm, D_MODEL), lambda i: (i, 0)),
            full((1, D_MODEL)),
            full((D_MODEL, QKV_COLS)),
            full((D_MODEL, GDN_WIDTH)),
            full((D_MODEL, LANES)),
            full((D_MODEL, 2 * CONF_WIDTH)),
            full((1, 2 * CONF_WIDTH)),
            full((2 * GDN_HEADS, 1)),
            full((2 * GDN_HEADS, 1)),
            full((tm, tm)),
        ],
        out_specs=[
            pl.BlockSpec((tm, QKV_COLS), lambda i: (i, 0)),
            pl.BlockSpec((tm, GDN_WIDTH), lambda i: (i, 0)),
            pl.BlockSpec((tm, 2 * GDN_HEADS), lambda i: (i, 0)),
            bdt_spec,
            pl.BlockSpec((tm, CONF_WIDTH), lambda i: (i, 0)),
        ],
        out_shape=[
            jax.ShapeDtypeStruct((t, QKV_COLS), F32),
            jax.ShapeDtypeStruct((t, GDN_WIDTH), F32),
            jax.ShapeDtypeStruct((t, 2 * GDN_HEADS), F32),
            bdt_shape,
            jax.ShapeDtypeStruct((t, CONF_WIDTH), F32),
        ],
        compiler_params=_params(("arbitrary",)),
        name="inproj",
    )(x, g, wqkv, wz, wbd, wglu, bglu, pa, pb, tri)


def _split_bf16(a):
    hi = a.astype(BF16)
    lo = (a - hi.astype(F32)).astype(BF16)
    return hi, lo


def _unit_lower_inverses(ms, eye, n):
    levels = int(math.log2(n)) - 1
    ps = [eye - m for m in ms]
    pw = [(-m).astype(BF16) for m in ms]
    pw = [jnp.dot(p, p, preferred_element_type=F32).astype(BF16) for p in pw]
    for level in range(1, levels + 1):
        last = level == levels
        nxt = []
        for u, p in enumerate(pw):
            if last:
                ps[u] = ps[u] + jnp.dot(ps[u].astype(BF16), p, preferred_element_type=F32)
            else:
                out = jnp.dot(jnp.concatenate([p, ps[u].astype(BF16)], axis=0), p,
                              preferred_element_type=F32)
                nxt.append(out[:n].astype(BF16))
                ps[u] = ps[u] + out[n:]
        pw = nxt
    return ps


def _gdn_kernel(chunk, x_ref, z_ref, bd_ref, bdt_ref, hist_ref, s0_ref, wc_ref, won_ref,
                o_ref, s_ref, xbuf):
    i = pl.program_id(1)
    nb, tt = x_ref.shape[0], x_ref.shape[1]
    nchunk = tt // chunk
    pad = SUBLANES
    hist_rows = QKV_TAPS - 1

    @pl.when(i == 0)
    def _():
        xbuf[:, pad - hist_rows:pad, :] = hist_ref[...]
        s_ref[...] = s0_ref[...]

    xbuf[:, pad:pad + tt, :] = x_ref[...]

    ii = lax.broadcasted_iota(I32, (chunk, chunk), 0)
    jj = lax.broadcasted_iota(I32, (chunk, chunk), 1)
    lower_incl = ii >= jj
    lower_strict = ii > jj
    eye = (ii == jj).astype(F32)
    won = won_ref[...]

    def conv_silu(b, col0):
        cols = slice(col0, col0 + HEAD_DIM)
        acc = wc_ref[hist_rows:hist_rows + 1, cols] * xbuf[b, pad:pad + tt, cols]
        for j in range(hist_rows):
            r0 = pad - hist_rows + j
            acc = acc + wc_ref[j:j + 1, cols] * xbuf[b, r0:r0 + tt, cols]
        return acc * _sigmoid(acc)

    units = []
    for b in range(nb):
        bd = bd_ref[b]
        bdt = bdt_ref[b]
        for h in range(GDN_HEADS):
            q = conv_silu(b, h * HEAD_DIM)
            k = conv_silu(b, GDN_WIDTH + h * HEAD_DIM)
            v = conv_silu(b, 2 * GDN_WIDTH + h * HEAD_DIM)
            q = q * lax.rsqrt(jnp.sum(q * q, axis=-1, keepdims=True) + L2_EPS) * (HEAD_DIM ** -0.5)
            k = k * lax.rsqrt(jnp.sum(k * k, axis=-1, keepdims=True) + L2_EPS)
            kt = k.T
            beta = bd[:, h:h + 1]
            gcol = bd[:, GDN_HEADS + h:GDN_HEADS + h + 1]
            grow = bdt[GDN_HEADS + h:GDN_HEADS + h + 1, :]
            for c in range(nchunk):
                rows = slice(c * chunk, (c + 1) * chunk)
                qc, kc, vc, ktc = q[rows], k[rows], v[rows], kt[:, rows]
                bc, gc, gr = beta[rows], gcol[rows], grow[:, rows]
                dec = jnp.exp(jnp.where(lower_incl, gc - gr, -jnp.inf))
                kb = kc * bc
                eg = jnp.exp(gc)
                both = _dotb(jnp.concatenate([kb, qc], axis=0), ktc)
                units.append(dict(
                    b=b, h=h, c=c,
                    m=both[:chunk] * jnp.where(lower_strict, dec, 0.0),
                    qk=both[chunk:] * dec,
                    rhs=jnp.concatenate([vc * bc, kb * eg], axis=1),
                    qg=qc * eg,
                    kdt=ktc * jnp.exp(gr[:, chunk - 1:chunk] - gr),
                    d_last=jnp.exp(gc[chunk - 1:chunk, :]),
                ))

    tinvs = _unit_lower_inverses([u["m"] for u in units], eye, chunk)
    for u, tinv in zip(units, tinvs):
        sol = _dotb(tinv, u["rhs"])
        u["u"], u["w"] = sol[:, :HEAD_DIM], sol[:, HEAD_DIM:]

    state = {(b, h): s_ref[b, h] for b in range(nb) for h in range(GDN_HEADS)}
    by_key = {(u["b"], u["h"], u["c"]): u for u in units}
    for c in range(nchunk):
        rows = slice(c * chunk, (c + 1) * chunk)
        for b in range(nb):
            for h in range(GDN_HEADS):
                u = by_key[(b, h, c)]
                s = state[(b, h)]
                ws = _dotb(jnp.concatenate([u["w"], u["qg"]], axis=0), s)
                v_new = u["u"] - ws[:chunk]
                upd = _dotb(jnp.concatenate([u["qk"], u["kdt"]], axis=0), v_new)
                o_c = ws[chunk:] + upd[:chunk]
                state[(b, h)] = s * u["d_last"] + upd[chunk:]
                cols = slice(h * HEAD_DIM, (h + 1) * HEAD_DIM)
                on = o_c * lax.rsqrt(jnp.mean(o_c * o_c, axis=-1, keepdims=True) + RMS_EPS) * won
                zc = z_ref[b, rows, cols]
                o_ref[b, rows, cols] = (on * (zc * _sigmoid(zc))).astype(o_ref.dtype)
    for (b, h), s in state.items():
        s_ref[b, h] = s

    xbuf[:, pad - hist_rows:pad, :] = x_ref[:, tt - hist_rows:tt, :]


def _gdn(qkv, z, bd, bdt, hist, s0, wconv, wonorm, chunk, tt, nb):
    b, t, _ = qkv.shape
    nt = t // tt
    return pl.pallas_call(
        functools.partial(_gdn_kernel, chunk),
        grid=(b // nb, nt),
        in_specs=[
            pl.BlockSpec((nb, tt, QKV_COLS), lambda bi, i: (bi, i, 0)),
            pl.BlockSpec((nb, tt, GDN_WIDTH), lambda bi, i: (bi, i, 0)),
            pl.BlockSpec((nb, tt, 2 * GDN_HEADS), lambda bi, i: (bi, i, 0)),
            pl.BlockSpec((nb, 2 * GDN_HEADS, tt), lambda bi, i: (bi, 0, i)),
            pl.BlockSpec((nb, QKV_TAPS - 1, QKV_COLS), lambda bi, i: (bi, 0, 0)),
            pl.BlockSpec((nb, GDN_HEADS, HEAD_DIM, HEAD_DIM), lambda bi, i: (bi, 0, 0, 0)),
            pl.BlockSpec((QKV_TAPS, QKV_COLS), lambda bi, i: (0, 0)),
            pl.BlockSpec((1, HEAD_DIM), lambda bi, i: (0, 0)),
        ],
        out_specs=[
            pl.BlockSpec((nb, tt, GDN_WIDTH), lambda bi, i: (bi, i, 0)),
            pl.BlockSpec((nb, GDN_HEADS, HEAD_DIM, HEAD_DIM), lambda bi, i: (bi, 0, 0, 0)),
        ],
        out_shape=[
            jax.ShapeDtypeStruct((b, t, GDN_WIDTH), BF16),
            jax.ShapeDtypeStruct((b, GDN_HEADS, HEAD_DIM, HEAD_DIM), F32),
        ],
        scratch_shapes=[pltpu.VMEM((nb, tt + SUBLANES, QKV_COLS), F32)],
        compiler_params=_params(("arbitrary", "arbitrary")),
        name="gdn",
    )(qkv, z, bd, bdt, hist, s0, wconv, wonorm)


def _cconv_kernel(carry, u_ref, hist_ref, w_ref, b_ref, lg_ref, lb_ref, c_ref, ubuf, shifted):
    i = pl.program_id(1)
    nb, tt = u_ref.shape[0], u_ref.shape[1]
    hist_rows = DW_TAPS - 1
    pad = 4 * SUBLANES
    base = pad - hist_rows
    span = shifted.shape[1]

    def tap_rows(j, r0, rows):
        q, s = divmod(base + j, SUBLANES)
        start = q * SUBLANES + r0
        if s == 0:
            return ubuf[start:start + rows, :]
        return shifted[s - 1, start:start + rows, :]

    for b in range(nb):
        @pl.when(i == 0)
        def _():
            ubuf[base:pad, :] = hist_ref[b]

        ubuf[pad:pad + tt, :] = u_ref[b]
        for s in range(1, SUBLANES):
            shifted[s - 1] = ubuf[s:s + span, :]

        rows = min(CONV_ROWS, tt)
        for r0 in range(0, tt, rows):
            acc = w_ref[0:1, :] * tap_rows(0, r0, rows)
            for j in range(1, DW_TAPS):
                acc = acc + w_ref[j:j + 1, :] * tap_rows(j, r0, rows)
            cv = acc + b_ref[...]
            mu = jnp.mean(cv, axis=-1, keepdims=True)
            xc = cv - mu
            var = jnp.mean(xc * xc, axis=-1, keepdims=True)
            y = xc * lax.rsqrt(var + LN_EPS) * lg_ref[...] + lb_ref[...]
            c_ref[b, r0:r0 + rows, :] = (y * _sigmoid(y)).astype(c_ref.dtype)
    if carry:
        ubuf[base:pad, :] = u_ref[0, tt - hist_rows:tt, :]


def _cconv(uu, hist, w, b, lg, lb, tt, nb):
    bsz, t, _ = uu.shape
    nt = t // tt
    assert nb == 1 or nt == 1
    vec = pl.BlockSpec((1, CONF_WIDTH), lambda bi, i: (0, 0))
    return pl.pallas_call(
        functools.partial(_cconv_kernel, nt > 1),
        grid=(bsz // nb, nt),
        in_specs=[
            pl.BlockSpec((nb, tt, CONF_WIDTH), lambda bi, i: (bi, i, 0)),
            pl.BlockSpec((nb, DW_TAPS - 1, CONF_WIDTH), lambda bi, i: (bi, 0, 0)),
            pl.BlockSpec((DW_TAPS, CONF_WIDTH), lambda bi, i: (0, 0)),
            vec, vec, vec,
        ],
        out_specs=pl.BlockSpec((nb, tt, CONF_WIDTH), lambda bi, i: (bi, i, 0)),
        out_shape=jax.ShapeDtypeStruct((bsz, t, CONF_WIDTH), BF16),
        scratch_shapes=[pltpu.VMEM((tt + 4 * SUBLANES, CONF_WIDTH), F32),
                        pltpu.VMEM((SUBLANES - 1, tt + 3 * SUBLANES, CONF_WIDTH), F32)],
        compiler_params=_params(("arbitrary", "arbitrary")),
        name="cconv",
    )(uu, hist, w, b, lg, lb)


def _outproj_kernel(tiles_a, oa_ref, ca_ref, xa_ref, ob_ref, cb_ref, xb_ref, *rest):
    i = pl.program_id(0)

    @pl.when(i < tiles_a)
    def _():
        _outproj_tile(oa_ref, ca_ref, xa_ref, *rest)

    @pl.when(i >= tiles_a)
    def _():
        _outproj_tile(ob_ref, cb_ref, xb_ref, *rest)


def _outproj_tile(o_ref, c_ref, x_ref, wo_ref, wc_ref, g_ref, wr_ref, br_ref, upper_ref,
                  x1_ref, h2_ref, metar_ref, cnt_ref):
    tm = x_ref.shape[0]

    mix = _dotb(o_ref[...], wo_ref[...]) + _dotb(c_ref[...], wc_ref[...])
    x1 = x_ref[...] + mix
    x1_ref[...] = x1
    h2 = x1 * lax.rsqrt(jnp.mean(x1 * x1, axis=-1, keepdims=True) + RMS_EPS) * g_ref[...]
    h2_ref[...] = h2.astype(BF16)
    logits = lax.dot_general(wr_ref[...], h2.astype(BF16), (((1,), (1,)), ((), ())),
                             preferred_element_type=F32) + br_ref[...]
    row = lax.broadcasted_iota(I32, (N_EXPERTS, tm), 0).astype(F32)
    vals = logits
    idxs, tops = [], []
    for _ in range(TOP_K):
        m = jnp.max(vals, axis=0, keepdims=True)
        idx = jnp.min(jnp.where(vals == m, row, float(N_EXPERTS)), axis=0, keepdims=True)
        idxs.append(idx)
        tops.append(m)
        vals = jnp.where(row == idx, -jnp.inf, vals)
    exps = [jnp.exp(v - tops[0]) for v in tops]
    den = exps[0] + exps[1] + exps[2] + exps[3]
    onehots = [(row == idx).astype(F32) for idx in idxs]
    chosen = onehots[0] + onehots[1] + onehots[2] + onehots[3]
    before = jnp.dot(chosen.astype(BF16), upper_ref[...], preferred_element_type=F32)
    ranks = [jnp.sum(oh * before, axis=0, keepdims=True) for oh in onehots]
    gates = [e / den for e in exps]
    metar_ref[...] = jnp.concatenate(
        idxs + ranks + gates + [jnp.zeros((META_ROWS - 3 * TOP_K, tm), F32)], axis=0)
    cnt_ref[...] = jnp.broadcast_to(jnp.sum(chosen, axis=1, keepdims=True), (N_EXPERTS, LANES))[None]


def _outproj(first, second, wo, wc, g, wr, br):
    t_a, t_b = first[2].shape[0], second[2].shape[0]
    tm = TOKEN_TILE
    upper = (jnp.arange(tm)[:, None] < jnp.arange(tm)[None, :]).astype(BF16)
    tiles_a = t_a // tm
    t_total = t_a + t_b
    full = lambda shape: pl.BlockSpec(shape, lambda i: (0, 0))
    rows_a = lambda width: pl.BlockSpec((tm, width), lambda i: (jnp.minimum(i, tiles_a - 1), 0))
    rows_b = lambda width: pl.BlockSpec((tm, width), lambda i: (jnp.maximum(i - tiles_a, 0), 0))
    widths = (GDN_WIDTH, CONF_WIDTH, D_MODEL)
    return pl.pallas_call(
        functools.partial(_outproj_kernel, tiles_a),
        grid=(t_total // tm,),
        in_specs=[rows_a(w) for w in widths] + [rows_b(w) for w in widths] + [
            full((GDN_WIDTH, D_MODEL)),
            full((CONF_WIDTH, D_MODEL)),
            full((1, D_MODEL)),
            full((N_EXPERTS, D_MODEL)),
            full((N_EXPERTS, 1)),
            full((tm, tm)),
        ],
        out_specs=[
            pl.BlockSpec((tm, D_MODEL), lambda i: (i, 0)),
            pl.BlockSpec((tm, D_MODEL), lambda i: (i, 0)),
            pl.BlockSpec((META_ROWS, tm), lambda i: (0, i)),
            pl.BlockSpec((1, N_EXPERTS, LANES), lambda i: (i, 0, 0)),
        ],
        out_shape=[
            jax.ShapeDtypeStruct((t_total, D_MODEL), F32),
            jax.ShapeDtypeStruct((t_total, D_MODEL), BF16),
            jax.ShapeDtypeStruct((META_ROWS, t_total), F32),
            jax.ShapeDtypeStruct((t_total // tm, N_EXPERTS, LANES), F32),
        ],
        compiler_params=_params(("arbitrary",)),
        name="outproj_router",
    )(*first, *second, wo, wc, g, wr, br, upper)


def _segment_copies(i, seg_local_ref, seg_dst_ref, seg_n_ref, make_copy, wait):
    def copy(rows, local, dst):
        cp = make_copy(rows, pl.multiple_of(local, GRANULE), pl.multiple_of(dst, GRANULE))
        cp.wait() if wait else cp.start()

    def per_expert(ex, carry):
        j = i * N_EXPERTS + ex
        local, dst, n = seg_local_ref[j], seg_dst_ref[j], seg_n_ref[j]
        big = 8 * GRANULE

        def body(g, c):
            copy(big, local + g * big, dst + g * big)
            return c

        lax.fori_loop(0, n >> 3, body, 0)
        off = (n >> 3) * big
        for bit in (2, 1, 0):
            rows = GRANULE << bit
            has = (n >> bit) & 1

            @pl.when(has == 1)
            def _(rows=rows, off=off):
                copy(rows, local + off, dst + off)

            off = off + has * rows
        return carry

    lax.fori_loop(0, N_EXPERTS, per_expert, 0)


def _gap_copies(gap_start_ref, gap_n_ref, make_copy, wait):
    def per_expert(ex, carry):
        n = gap_n_ref[ex]
        start = pl.multiple_of(gap_start_ref[ex], GRANULE)
        off = 0 * n
        for bit in reversed(range(GAP_BITS)):
            rows = GRANULE << bit

            @pl.when(((n >> bit) & 1) == 1)
            def _(rows=rows, off=off):
                cp = make_copy(rows, pl.multiple_of(start + off, GRANULE))
                cp.wait() if wait else cp.start()

            off = off + ((n >> bit) & 1) * rows
        return carry

    lax.fori_loop(0, N_EXPERTS, per_expert, 0)


def _local_rows(meta_e, seg_local_ref, i):
    base = jnp.zeros(meta_e.shape, F32)
    for ex in range(N_EXPERTS):
        base = jnp.where(meta_e == float(ex), seg_local_ref[i * N_EXPERTS + ex].astype(F32), base)
    return base


def _dispatch_kernel(seg_local_ref, seg_dst_ref, seg_n_ref, gap_start_ref, gap_n_ref, tail_ref,
                     h_ref, mr_ref, xs_ref, slots_ref, xloc, zbuf, sems, zsem):
    i = pl.program_id(0)
    last = pl.num_programs(0) - 1
    tm = h_ref.shape[0]
    rt = xloc.shape[1]
    zrows = zbuf.shape[0]
    slot = i % 2

    def gap_copy(rows, dst):
        return pltpu.make_async_copy(zbuf.at[pl.ds(0, rows)], xs_ref.at[pl.ds(dst, rows)], zsem)

    def tail_copies(wait):
        def body(j, c):
            cp = gap_copy(zrows, pl.multiple_of(tail_ref[0] + j * zrows, zrows))
            cp.wait() if wait else cp.start()
            return c
        lax.fori_loop(0, tail_ref[1], body, 0)

    @pl.when(i == 0)
    def _():
        zbuf[...] = jnp.zeros(zbuf.shape, zbuf.dtype)
        _gap_copies(gap_start_ref, gap_n_ref, gap_copy, wait=False)
        tail_copies(wait=False)

    mr = mr_ref[...]
    dl = (_local_rows(mr[META_EXPERT:META_EXPERT + TOP_K], seg_local_ref, i)
          + mr[META_RANK:META_RANK + TOP_K])
    slots_ref[...] = jnp.concatenate(
        [dl, mr[META_GATE:META_GATE + TOP_K], jnp.zeros((LANES - 2 * TOP_K, tm), F32)], axis=0).T
    rr = lax.broadcasted_iota(I32, (rt, tm), 0).astype(F32)
    hit = rr == dl[0:1]
    for k in range(1, TOP_K):
        hit = hit | (rr == dl[k:k + 1])
    xloc[slot] = jnp.dot(hit.astype(BF16), h_ref[...], preferred_element_type=F32)

    def seg_copy(buf):
        def make(rows, local, dst):
            return pltpu.make_async_copy(xloc.at[buf, pl.ds(local, rows)],
                                         xs_ref.at[pl.ds(dst, rows)], sems.at[buf])
        return make

    _segment_copies(i, seg_local_ref, seg_dst_ref, seg_n_ref, seg_copy(slot), wait=False)

    @pl.when(i > 0)
    def _():
        _segment_copies(i - 1, seg_local_ref, seg_dst_ref, seg_n_ref, seg_copy(1 - slot), wait=True)

    @pl.when(i == last)
    def _():
        _segment_copies(i, seg_local_ref, seg_dst_ref, seg_n_ref, seg_copy(slot), wait=True)
        _gap_copies(gap_start_ref, gap_n_ref, gap_copy, wait=True)
        tail_copies(wait=True)


def _dispatch(tables, h2, metar, n_rows):
    t = h2.shape[0]
    tm = TOKEN_TILE
    return pl.pallas_call(
        _dispatch_kernel,
        grid_spec=pltpu.PrefetchScalarGridSpec(
            num_scalar_prefetch=6,
            grid=(t // tm,),
            in_specs=[
                pl.BlockSpec((tm, D_MODEL), lambda i, *_: (i, 0)),
                pl.BlockSpec((META_ROWS, tm), lambda i, *_: (0, i)),
            ],
            out_specs=[pl.BlockSpec(memory_space=pl.ANY),
                       pl.BlockSpec((tm, LANES), lambda i, *_: (i, 0))],
            scratch_shapes=[
                pltpu.VMEM((2, TILE_ROWS, D_MODEL), F32),
                pltpu.VMEM((GRANULE << (GAP_BITS - 1), D_MODEL), F32),
                pltpu.SemaphoreType.DMA((2,)),
                pltpu.SemaphoreType.DMA,
            ],
        ),
        out_shape=[jax.ShapeDtypeStruct((n_rows, D_MODEL), F32),
                   jax.ShapeDtypeStruct((t, LANES), F32)],
        compiler_params=_params(("arbitrary",)),
        name="dispatch",
    )(*tables, h2, metar)


def _expert_kernel(be_ref, rb_ref, used_ref, first_ref, next_ref,
                   x_ref, wgu_hbm, bgu_ref, wd_hbm, bdn_ref, y_ref,
                   wgu_f32, wd_f32, wgu_bf, wd_bf, sems):
    i = pl.program_id(0)

    def fetch(expert):
        return (pltpu.make_async_copy(wgu_hbm.at[expert], wgu_f32, sems.at[0]),
                pltpu.make_async_copy(wd_hbm.at[expert], wd_f32, sems.at[1]))

    @pl.when(i == 0)
    def _():
        for cp in fetch(be_ref[0]):
            cp.start()

    @pl.when(first_ref[i] == 1)
    def _():
        for cp in fetch(be_ref[i]):
            cp.wait()
        wgu_bf[...] = wgu_f32[...].astype(BF16)
        wd_bf[...] = wd_f32[...].astype(BF16)

        @pl.when(next_ref[i] >= 0)
        def _():
            for cp in fetch(next_ref[i]):
                cp.start()

    def ffn(x):
        gu = _dotb(x, wgu_bf[...]) + bgu_ref[...]
        x_glu = jnp.minimum(gu[:, :D_EXPERT], SWIGLU_LIMIT)
        x_lin = jnp.clip(gu[:, D_EXPERT:], -SWIGLU_LIMIT, SWIGLU_LIMIT)
        act = x_glu * _sigmoid(SWIGLU_ALPHA * x_glu) * (x_lin + 1.0)
        return _dotb(act, wd_bf[...]) + bdn_ref[...]

    bm = x_ref.shape[0]
    parts = (used_ref[i] + EXPERT_PART_ROWS - 1) // EXPERT_PART_ROWS
    for p in range(1, bm // EXPERT_PART_ROWS + 1):
        @pl.when(parts == p)
        def _(rows=p * EXPERT_PART_ROWS):
            y_ref[:rows, :] = ffn(x_ref[:rows, :])
            if rows < bm:
                y_ref[rows:, :] = jnp.zeros((bm - rows, y_ref.shape[1]), y_ref.dtype)


def _experts(tables, xs, wgu, bgu, wd, bdn):
    rows = xs.shape[0]
    bm = EXPERT_ROWS
    nb = rows // bm
    return pl.pallas_call(
        _expert_kernel,
        grid_spec=pltpu.PrefetchScalarGridSpec(
            num_scalar_prefetch=5,
            grid=(nb,),
            in_specs=[
                pl.BlockSpec((bm, D_MODEL), lambda i, be, rb, *_: (rb[i], 0)),
                pl.BlockSpec(memory_space=pl.ANY),
                pl.BlockSpec((None, 1, 2 * D_EXPERT), lambda i, be, *_: (be[i], 0, 0)),
                pl.BlockSpec(memory_space=pl.ANY),
                pl.BlockSpec((None, 1, D_MODEL), lambda i, be, *_: (be[i], 0, 0)),
            ],
            out_specs=pl.BlockSpec((bm, D_MODEL), lambda i, be, rb, *_: (rb[i], 0)),
            scratch_shapes=[
                pltpu.VMEM((D_MODEL, 2 * D_EXPERT), F32),
                pltpu.VMEM((D_EXPERT, D_MODEL), F32),
                pltpu.VMEM((D_MODEL, 2 * D_EXPERT), BF16),
                pltpu.VMEM((D_EXPERT, D_MODEL), BF16),
                pltpu.SemaphoreType.DMA((2,)),
            ],
        ),
        out_shape=jax.ShapeDtypeStruct((rows, D_MODEL), F32),
        input_output_aliases={5: 0},
        compiler_params=pltpu.CompilerParams(dimension_semantics=("arbitrary",),
                                             vmem_limit_bytes=EXPERT_VMEM_LIMIT),
        name="experts",
    )(*tables, xs, wgu, bgu, wd, bdn)


def _combine_kernel(tiles_a, seg_local_ref, seg_dst_ref, seg_n_ref, x1_ref, mc_ref, yb_ref, gf_ref,
                    ya_ref, yb_out_ref, ybuf, wsel, sems):
    i = pl.program_id(0)
    n_tiles = pl.num_programs(0)
    tm = x1_ref.shape[0]
    rt = ybuf.shape[1]
    slot = i % 2

    def seg_copy(buf):
        def make(rows, local, src):
            return pltpu.make_async_copy(yb_ref.at[pl.ds(src, rows)],
                                         ybuf.at[buf, pl.ds(local, rows)], sems.at[buf])
        return make

    @pl.when(i == 0)
    def _():
        ybuf[...] = jnp.zeros(ybuf.shape, ybuf.dtype)
        _segment_copies(i, seg_local_ref, seg_dst_ref, seg_n_ref, seg_copy(slot), wait=False)

    @pl.when(i + 1 < n_tiles)
    def _():
        _segment_copies(i + 1, seg_local_ref, seg_dst_ref, seg_n_ref, seg_copy(1 - slot), wait=False)

    dl = mc_ref[:, 0:TOP_K]
    gate = mc_ref[:, TOP_K:2 * TOP_K]
    rb, cb = SEL_CHUNK
    col = lax.broadcasted_iota(I32, (rb, cb), 1).astype(F32)
    for r0 in range(0, tm, rb):
        rows_k = [jnp.broadcast_to(dl[r0:r0 + rb, k:k + 1], (rb, cb)) for k in range(TOP_K)]
        gate_k = [jnp.broadcast_to(gate[r0:r0 + rb, k:k + 1], (rb, cb)) for k in range(TOP_K)]
        for c0 in range(0, rt, cb):
            rr = col + float(c0)
            w = jnp.where(rr == rows_k[0], gate_k[0], 0.0)
            for k in range(1, TOP_K):
                w = jnp.where(rr == rows_k[k], gate_k[k], w)
            wsel[r0:r0 + rb, c0:c0 + cb] = w.astype(BF16)
    _segment_copies(i, seg_local_ref, seg_dst_ref, seg_n_ref, seg_copy(slot), wait=True)
    acc = x1_ref[...] + jnp.dot(wsel[...], ybuf[slot].astype(BF16), preferred_element_type=F32)
    y = acc * lax.rsqrt(jnp.mean(acc * acc, axis=-1, keepdims=True) + RMS_EPS) * gf_ref[...]

    @pl.when(i < tiles_a)
    def _():
        ya_ref[...] = y

    @pl.when(i >= tiles_a)
    def _():
        yb_out_ref[...] = y


def _combine(tables, x1, slots, yb, gfin, n_a):
    t = x1.shape[0]
    tm = TOKEN_TILE
    tiles_a = n_a // tm
    return pl.pallas_call(
        functools.partial(_combine_kernel, tiles_a),
        grid_spec=pltpu.PrefetchScalarGridSpec(
            num_scalar_prefetch=3,
            grid=(t // tm,),
            in_specs=[
                pl.BlockSpec((tm, D_MODEL), lambda i, *_: (i, 0)),
                pl.BlockSpec((tm, LANES), lambda i, *_: (i, 0)),
                pl.BlockSpec(memory_space=pl.ANY),
                pl.BlockSpec((1, D_MODEL), lambda i, *_: (0, 0)),
            ],
            out_specs=[
                pl.BlockSpec((tm, D_MODEL), lambda i, *_: (jnp.minimum(i, tiles_a - 1), 0)),
                pl.BlockSpec((tm, D_MODEL), lambda i, *_: (jnp.maximum(i - tiles_a, 0), 0)),
            ],
            scratch_shapes=[pltpu.VMEM((2, TILE_ROWS, D_MODEL), F32),
                            pltpu.VMEM((tm, TILE_ROWS), BF16),
                            pltpu.SemaphoreType.DMA((2,))],
        ),
        out_shape=[jax.ShapeDtypeStruct((n_a, D_MODEL), F32),
                   jax.ShapeDtypeStruct((t - n_a, D_MODEL), F32)],
        compiler_params=pltpu.CompilerParams(dimension_semantics=("arbitrary",),
                                             vmem_limit_bytes=EXPERT_VMEM_LIMIT),
        name="combine",
    )(*tables, x1, slots, yb, gfin)


def kernel(x_prompt, x_sample, state_gdn, state_qkv_conv, state_dwconv, norm_mix_g, w_in, w_conv_qkv,
           a_log, dt_bias, w_onorm, b_glu, w_dw, b_dw, ln_g, ln_b, w_out, norm_ffn_g, w_router,
           b_router, w_gate_up, b_gate_up, w_down, b_down, norm_final_g):
    bp, tp, _ = x_prompt.shape
    bs, ts, _ = x_sample.shape
    n_p, n_s = bp * tp, bs * ts

    wi = w_in[0]
    c_z = QKV_COLS
    c_bd = c_z + GDN_WIDTH
    c_glu = c_bd + 2 * GDN_HEADS
    wqkv = wi[:, :c_z].astype(BF16)
    wz = wi[:, c_z:c_bd].astype(BF16)
    wbd = jnp.pad(wi[:, c_bd:c_glu], ((0, 0), (0, LANES - 2 * GDN_HEADS))).astype(BF16)
    wglu = wi[:, c_glu:].astype(BF16)
    bglu = b_glu[0][None, :]
    pa = jnp.pad(-jnp.exp(a_log[0].astype(F32)), (GDN_HEADS, 0))[:, None]
    pb = jnp.pad(dt_bias[0].astype(F32), (GDN_HEADS, 0))[:, None]
    gmix = norm_mix_g[0][None, :]
    wconv = w_conv_qkv[0]
    wonorm = w_onorm[0][None, :]
    wdw, bdw = w_dw[0], b_dw[0][None, :]
    lng, lnb = ln_g[0][None, :], ln_b[0][None, :]
    wo = w_out[0][:GDN_WIDTH].astype(BF16)
    wc = w_out[0][GDN_WIDTH:].astype(BF16)
    gffn = norm_ffn_g[0][None, :]
    wr, br = w_router[0].T.astype(BF16), b_router[0].astype(F32)[:, None]
    bgu = b_gate_up[0][:, None, :]
    bdn = b_down[0][:, None, :]
    gfin = norm_final_g[None, :]

    xp = x_prompt.reshape(n_p, D_MODEL)
    xs_tok = x_sample.reshape(n_s, D_MODEL)

    def mixers(x2d, bsz, t, hist_qkv, s0, hist_glu, chunk, gdn_tile, gdn_rows, conv_tile):
        qkv, z, bd, bdt, uu = _inproj(x2d, gmix, wqkv, wz, wbd, wglu, bglu, pa, pb, chunk, t)
        qkv3 = qkv.reshape(bsz, t, QKV_COLS)
        uu3 = uu.reshape(bsz, t, CONF_WIDTH)
        if bdt.ndim == 2:
            bdt = bdt.reshape(2 * GDN_HEADS, bsz, t).transpose(1, 0, 2)
        o, s_new = _gdn(qkv3, z.reshape(bsz, t, GDN_WIDTH), bd.reshape(bsz, t, 2 * GDN_HEADS),
                        bdt, hist_qkv, s0, wconv, wonorm, chunk, gdn_tile, gdn_rows)
        cc = _cconv(uu3, hist_glu, wdw, bdw, lng, lnb, conv_tile, gdn_rows)
        return o.reshape(bsz * t, GDN_WIDTH), cc.reshape(bsz * t, CONF_WIDTH), s_new, qkv3, uu3

    zero_qkv = jnp.zeros((bp, QKV_TAPS - 1, QKV_COLS), F32)
    zero_s = jnp.zeros((bp, GDN_HEADS, HEAD_DIM, HEAD_DIM), F32)
    zero_glu = jnp.zeros((bp, DW_TAPS - 1, CONF_WIDTH), F32)
    o_p, c_p, s_p, qkv_p, uu_p = mixers(xp, bp, tp, zero_qkv, zero_s, zero_glu,
                                        PROMPT_CHUNK, GDN_TILE, 1, CONV_TILE)
    o_s, c_s, s_s, qkv_s, uu_s = mixers(xs_tok, bs, ts, state_qkv_conv[0], state_gdn[0],
                                        state_dwconv[0], ts, ts, GDN_SAMPLE_ROWS, ts)

    n_tok = n_p + n_s
    x1, h2, metar, cnt_tile = _outproj((o_p, c_p, xp), (o_s, c_s, xs_tok), wo, wc, gffn, wr, br)

    bm = EXPERT_ROWS
    n_tiles = n_tok // TOKEN_TILE
    counts = cnt_tile[:, :, 0].astype(I32)
    seg_rows = (counts + GRANULE - 1) // GRANULE * GRANULE
    ids = jnp.arange(N_EXPERTS, dtype=I32)
    tile_ids = jnp.arange(n_tiles, dtype=I32)
    earlier_e = (ids[:, None] < ids[None, :]).astype(I32)
    earlier_t = (tile_ids[None, :] < tile_ids[:, None]).astype(I32)
    seg_local = jnp.sum(seg_rows[:, :, None] * earlier_e[None], axis=1)
    seg_before = jnp.sum(earlier_t[:, :, None] * seg_rows[None], axis=1)
    rows_e = jnp.sum(seg_rows, axis=0)
    padded = (rows_e + bm - 1) // bm * bm
    pstart = jnp.sum(padded[:, None] * earlier_e, axis=0)
    pend = pstart + padded
    seg_dst = pstart[None, :] + seg_before
    n_used = jnp.maximum(pend[-1] // bm, 1).astype(I32)
    max_rows = n_tok * TOP_K + n_tiles * N_EXPERTS * (GRANULE - 1) + N_EXPERTS * (bm - 1)
    n_blocks = -(-max_rows // bm)
    blk = jnp.minimum(jnp.arange(n_blocks, dtype=I32), n_used - 1)
    block_expert = jnp.minimum(
        jnp.sum((pend[None, :] <= (blk * bm)[:, None]).astype(I32), axis=1), N_EXPERTS - 1)
    seg_tables = (seg_local.reshape(-1).astype(I32), seg_dst.reshape(-1).astype(I32),
                  (seg_rows // GRANULE).reshape(-1).astype(I32))
    gap_tables = ((pstart + rows_e).astype(I32), ((padded - rows_e) // GRANULE).astype(I32))
    zero_rows = GRANULE << (GAP_BITS - 1)
    tail_table = jnp.stack([pend[-1], (n_blocks * bm - pend[-1]) // zero_rows]).astype(I32)

    xs, slots = _dispatch(seg_tables + gap_tables + (tail_table,), h2, metar, n_blocks * bm)
    active = padded > 0
    later = active[None, :] & (ids[None, :] > ids[:, None])
    next_active = jnp.min(jnp.where(later, ids[None, :], N_EXPERTS), axis=1)
    next_active = jnp.where(next_active == N_EXPERTS, -1, next_active)
    steps = jnp.arange(n_blocks, dtype=I32)
    of_block = (block_expert[:, None] == ids[None, :]).astype(I32)
    per_block = lambda table: jnp.sum(of_block * table[None, :], axis=1)
    first = ((steps * bm == per_block(pstart)) & (steps < n_used)).astype(I32)
    used = jnp.clip(per_block(pstart + rows_e) - steps * bm, 0, bm)
    used = jnp.where(steps < n_used, used, 0).astype(I32)
    expert_tables = (block_expert, blk, used, first, per_block(next_active).astype(I32))
    yb = _experts(expert_tables, xs, w_gate_up[0], bgu, w_down[0], bdn)
    y_p, y_s = _combine(seg_tables, x1, slots, yb, gfin, n_p)

    hist_glu_s = jnp.concatenate([state_dwconv[0], uu_s], axis=1)[:, -(DW_TAPS - 1):]
    return (y_p.reshape(bp, tp, D_MODEL),
            y_s.reshape(bs, ts, D_MODEL),
            s_p[None],
            qkv_p[:, -(QKV_TAPS - 1):][None],
            uu_p[:, -(DW_TAPS - 1):][None],
            s_s[None],
            qkv_s[:, -(QKV_TAPS - 1):][None],
            hist_glu_s[None])
```

```python
import functools
import math

import jax
import jax.numpy as jnp
from jax import lax
from jax.experimental import pallas as pl
from jax.experimental.pallas import tpu as pltpu

F32 = jnp.float32
BF16 = jnp.bfloat16
I32 = jnp.int32

D_MODEL = 1024
GDN_HEADS = 4
HEAD_DIM = 128
GDN_WIDTH = GDN_HEADS * HEAD_DIM
QKV_COLS = 3 * GDN_WIDTH
CONF_WIDTH = 512
QKV_TAPS = 4
DW_TAPS = 31
N_EXPERTS = 32
TOP_K = 4
D_EXPERT = 1024
SWIGLU_LIMIT = 7.0
SWIGLU_ALPHA = 1.702
RMS_EPS = 1e-6
LN_EPS = 1e-5
L2_EPS = 1e-6
PROMPT_CHUNK = 64

LANES = 128
SUBLANES = 8
TOKEN_TILE = 512
GDN_TILE = 512
GDN_SAMPLE_ROWS = 8
CONV_TILE = 512
CONV_ROWS = 512
EXPERT_ROWS = 512
EXPERT_PART_ROWS = 256
GRANULE = SUBLANES
GAP_BITS = 6
TILE_ROWS = -(-(TOKEN_TILE * TOP_K + N_EXPERTS * (GRANULE - 1)) // 256) * 256
META_EXPERT, META_RANK, META_GATE, META_ROWS = 0, TOP_K, 2 * TOP_K, 16
assert EXPERT_ROWS // GRANULE <= 1 << GAP_BITS
SEL_CHUNK = (64, 256)
VMEM_LIMIT = 48 * 1024 * 1024
EXPERT_VMEM_LIMIT = 56 * 1024 * 1024


def _dotb(a, b):
    return jnp.dot(a.astype(BF16), b.astype(BF16), preferred_element_type=F32)


def _sigmoid(x):
    return jax.nn.sigmoid(x)


def _params(sem):
    return pltpu.CompilerParams(dimension_semantics=sem, vmem_limit_bytes=VMEM_LIMIT)


def _inproj_kernel(x_ref, g_ref, wqkv_ref, wz_ref, wbd_ref, wglu_ref, bglu_ref, pa_ref, pb_ref,
                   tri_ref, qkv_ref, z_ref, bd_ref, bdt_ref, uu_ref):
    x = x_ref[...]
    tm = x.shape[0]
    ms = jnp.mean(x * x, axis=-1, keepdims=True)
    h = (x * lax.rsqrt(ms + RMS_EPS) * g_ref[...]).astype(BF16)
    qkv_ref[...] = jnp.dot(h, wqkv_ref[...], preferred_element_type=F32)
    z_ref[...] = jnp.dot(h, wz_ref[...], preferred_element_type=F32)
    glu = jnp.dot(h, wglu_ref[...], preferred_element_type=F32) + bglu_ref[...]
    uu_ref[...] = glu[:, :CONF_WIDTH] * _sigmoid(glu[:, CONF_WIDTH:])
    bd = jnp.dot(h, wbd_ref[...], preferred_element_type=F32)
    bdt = bd.T[:2 * GDN_HEADS, :]
    beta = _sigmoid(bdt)
    v = bdt + pb_ref[...]
    softplus = jnp.maximum(v, 0.0) + jnp.log1p(jnp.exp(-jnp.abs(v)))
    g = pa_ref[...] * softplus
    g_hi, g_rest = _split_bf16(g)
    g_mid = g - g_hi.astype(F32) - g_rest.astype(F32)
    pieces = jnp.concatenate([g_hi, g_rest, g_mid.astype(BF16)], axis=0)
    sums = jnp.dot(pieces, tri_ref[...], preferred_element_type=F32)
    rows = 2 * GDN_HEADS
    gc = sums[:rows] + sums[rows:2 * rows] + sums[2 * rows:]
    row = lax.broadcasted_iota(I32, (rows, tm), 0)
    res = jnp.where(row < GDN_HEADS, beta, gc)
    bdt_ref[...] = res
    bd_ref[...] = jnp.concatenate([res, jnp.zeros((LANES - rows, tm), F32)], axis=0).T[:, :rows]


def _inproj(x, g, wqkv, wz, wbd, wglu, bglu, pa, pb, chunk, t_batch):
    t = x.shape[0]
    tm = TOKEN_TILE
    tiles_per_batch = t_batch // tm if t_batch % tm == 0 else 0
    pos = jnp.arange(tm)
    tri = ((pos[:, None] <= pos[None, :])
           & (pos[:, None] // chunk == pos[None, :] // chunk)).astype(BF16)
    full = lambda shape: pl.BlockSpec(shape, lambda i: (0, 0))
    if tiles_per_batch:
        bdt_spec = pl.BlockSpec((None, 2 * GDN_HEADS, tm),
                                lambda i: (i // tiles_per_batch, 0, i % tiles_per_batch))
        bdt_shape = jax.ShapeDtypeStruct((t // t_batch, 2 * GDN_HEADS, t_batch), F32)
    else:
        bdt_spec = pl.BlockSpec((2 * GDN_HEADS, tm), lambda i: (0, i))
        bdt_shape = jax.ShapeDtypeStruct((2 * GDN_HEADS, t), F32)
    return pl.pallas_call(
        _inproj_kernel,
        grid=(t // tm,),
        in_specs=[
            pl.BlockSpec((tm, D_MODEL), lambda i: (i, 0)),
            full((1, D_MODEL)),
            full((D_MODEL, QKV_COLS)),
            full((D_MODEL, GDN_WIDTH)),
            full((D_MODEL, LANES)),
            full((D_MODEL, 2 * CONF_WIDTH)),
            full((1, 2 * CONF_WIDTH)),
            full((2 * GDN_HEADS, 1)),
            full((2 * GDN_HEADS, 1)),
            full((tm, tm)),
        ],
        out_specs=[
            pl.BlockSpec((tm, QKV_COLS), lambda i: (i, 0)),
            pl.BlockSpec((tm, GDN_WIDTH), lambda i: (i, 0)),
            pl.BlockSpec((tm, 2 * GDN_HEADS), lambda i: (i, 0)),
            bdt_spec,
            pl.BlockSpec((tm, CONF_WIDTH), lambda i: (i, 0)),
        ],
        out_shape=[
            jax.ShapeDtypeStruct((t, QKV_COLS), F32),
            jax.ShapeDtypeStruct((t, GDN_WIDTH), F32),
            jax.ShapeDtypeStruct((t, 2 * GDN_HEADS), F32),
            bdt_shape,
            jax.ShapeDtypeStruct((t, CONF_WIDTH), F32),
        ],
        compiler_params=_params(("arbitrary",)),
        name="inproj",
    )(x, g, wqkv, wz, wbd, wglu, bglu, pa, pb, tri)


def _split_bf16(a):
    hi = a.astype(BF16)
    lo = (a - hi.astype(F32)).astype(BF16)
    return hi, lo


def _unit_lower_inverses(ms, eye, n):
    levels = int(math.log2(n)) - 1
    ps = [eye - m for m in ms]
    pw = [(-m).astype(BF16) for m in ms]
    pw = [jnp.dot(p, p, preferred_element_type=F32).astype(BF16) for p in pw]
    for level in range(1, levels + 1):
        last = level == levels
        nxt = []
        for u, p in enumerate(pw):
            if last:
                ps[u] = ps[u] + jnp.dot(ps[u].astype(BF16), p, preferred_element_type=F32)
            else:
                out = jnp.dot(jnp.concatenate([p, ps[u].astype(BF16)], axis=0), p,
                              preferred_element_type=F32)
                nxt.append(out[:n].astype(BF16))
                ps[u] = ps[u] + out[n:]
        pw = nxt
    return ps


def _gdn_kernel(chunk, x_ref, z_ref, bd_ref, bdt_ref, hist_ref, s0_ref, wc_ref, won_ref,
                o_ref, s_ref, xbuf):
    i = pl.program_id(1)
    nb, tt = x_ref.shape[0], x_ref.shape[1]
    nchunk = tt // chunk
    pad = SUBLANES
    hist_rows = QKV_TAPS - 1

    @pl.when(i == 0)
    def _():
        xbuf[:, pad - hist_rows:pad, :] = hist_ref[...]
        s_ref[...] = s0_ref[...]

    xbuf[:, pad:pad + tt, :] = x_ref[...]

    ii = lax.broadcasted_iota(I32, (chunk, chunk), 0)
    jj = lax.broadcasted_iota(I32, (chunk, chunk), 1)
    lower_incl = ii >= jj
    lower_strict = ii > jj
    eye = (ii == jj).astype(F32)
    won = won_ref[...]

    def conv_silu(b, col0):
        cols = slice(col0, col0 + HEAD_DIM)
        acc = wc_ref[hist_rows:hist_rows + 1, cols] * xbuf[b, pad:pad + tt, cols]
        for j in range(hist_rows):
            r0 = pad - hist_rows + j
            acc = acc + wc_ref[j:j + 1, cols] * xbuf[b, r0:r0 + tt, cols]
        return acc * _sigmoid(acc)

    units = []
    for b in range(nb):
        bd = bd_ref[b]
        bdt = bdt_ref[b]
        for h in range(GDN_HEADS):
            q = conv_silu(b, h * HEAD_DIM)
            k = conv_silu(b, GDN_WIDTH + h * HEAD_DIM)
            v = conv_silu(b, 2 * GDN_WIDTH + h * HEAD_DIM)
            q = q * lax.rsqrt(jnp.sum(q * q, axis=-1, keepdims=True) + L2_EPS) * (HEAD_DIM ** -0.5)
            k = k * lax.rsqrt(jnp.sum(k * k, axis=-1, keepdims=True) + L2_EPS)
            kt = k.T
            beta = bd[:, h:h + 1]
            gcol = bd[:, GDN_HEADS + h:GDN_HEADS + h + 1]
            grow = bdt[GDN_HEADS + h:GDN_HEADS + h + 1, :]
            for c in range(nchunk):
                rows = slice(c * chunk, (c + 1) * chunk)
                qc, kc, vc, ktc = q[rows], k[rows], v[rows], kt[:, rows]
                bc, gc, gr = beta[rows], gcol[rows], grow[:, rows]
                dec = jnp.exp(jnp.where(lower_incl, gc - gr, -jnp.inf))
                kb = kc * bc
                eg = jnp.exp(gc)
                both = _dotb(jnp.concatenate([kb, qc], axis=0), ktc)
                units.append(dict(
                    b=b, h=h, c=c,
                    m=both[:chunk] * jnp.where(lower_strict, dec, 0.0),
                    qk=both[chunk:] * dec,
                    rhs=jnp.concatenate([vc * bc, kb * eg], axis=1),
                    qg=qc * eg,
                    kdt=ktc * jnp.exp(gr[:, chunk - 1:chunk] - gr),
                    d_last=jnp.exp(gc[chunk - 1:chunk, :]),
                ))

    tinvs = _unit_lower_inverses([u["m"] for u in units], eye, chunk)
    for u, tinv in zip(units, tinvs):
        sol = _dotb(tinv, u["rhs"])
        u["u"], u["w"] = sol[:, :HEAD_DIM], sol[:, HEAD_DIM:]

    state = {(b, h): s_ref[b, h] for b in range(nb) for h in range(GDN_HEADS)}
    by_key = {(u["b"], u["h"], u["c"]): u for u in units}
    for c in range(nchunk):
        rows = slice(c * chunk, (c + 1) * chunk)
        for b in range(nb):
            for h in range(GDN_HEADS):
                u = by_key[(b, h, c)]
                s = state[(b, h)]
                ws = _dotb(jnp.concatenate([u["w"], u["qg"]], axis=0), s)
                v_new = u["u"] - ws[:chunk]
                upd = _dotb(jnp.concatenate([u["qk"], u["kdt"]], axis=0), v_new)
                o_c = ws[chunk:] + upd[:chunk]
                state[(b, h)] = s * u["d_last"] + upd[chunk:]
                cols = slice(h * HEAD_DIM, (h + 1) * HEAD_DIM)
                on = o_c * lax.rsqrt(jnp.mean(o_c * o_c, axis=-1, keepdims=True) + RMS_EPS) * won
                zc = z_ref[b, rows, cols]
                o_ref[b, rows, cols] = (on * (zc * _sigmoid(zc))).astype(o_ref.dtype)
    for (b, h), s in state.items():
        s_ref[b, h] = s

    xbuf[:, pad - hist_rows:pad, :] = x_ref[:, tt - hist_rows:tt, :]


def _gdn(qkv, z, bd, bdt, hist, s0, wconv, wonorm, chunk, tt, nb):
    b, t, _ = qkv.shape
    nt = t // tt
    return pl.pallas_call(
        functools.partial(_gdn_kernel, chunk),
        grid=(b // nb, nt),
        in_specs=[
            pl.BlockSpec((nb, tt, QKV_COLS), lambda bi, i: (bi, i, 0)),
            pl.BlockSpec((nb, tt, GDN_WIDTH), lambda bi, i: (bi, i, 0)),
            pl.BlockSpec((nb, tt, 2 * GDN_HEADS), lambda bi, i: (bi, i, 0)),
            pl.BlockSpec((nb, 2 * GDN_HEADS, tt), lambda bi, i: (bi, 0, i)),
            pl.BlockSpec((nb, QKV_TAPS - 1, QKV_COLS), lambda bi, i: (bi, 0, 0)),
            pl.BlockSpec((nb, GDN_HEADS, HEAD_DIM, HEAD_DIM), lambda bi, i: (bi, 0, 0, 0)),
            pl.BlockSpec((QKV_TAPS, QKV_COLS), lambda bi, i: (0, 0)),
            pl.BlockSpec((1, HEAD_DIM), lambda bi, i: (0, 0)),
        ],
        out_specs=[
            pl.BlockSpec((nb, tt, GDN_WIDTH), lambda bi, i: (bi, i, 0)),
            pl.BlockSpec((nb, GDN_HEADS, HEAD_DIM, HEAD_DIM), lambda bi, i: (bi, 0, 0, 0)),
        ],
        out_shape=[
            jax.ShapeDtypeStruct((b, t, GDN_WIDTH), BF16),
            jax.ShapeDtypeStruct((b, GDN_HEADS, HEAD_DIM, HEAD_DIM), F32),
        ],
        scratch_shapes=[pltpu.VMEM((nb, tt + SUBLANES, QKV_COLS), F32)],
        compiler_params=_params(("arbitrary", "arbitrary")),
        name="gdn",
    )(qkv, z, bd, bdt, hist, s0, wconv, wonorm)


def _cconv_kernel(carry, u_ref, hist_ref, w_ref, b_ref, lg_ref, lb_ref, c_ref, ubuf, shifted):
    i = pl.program_id(1)
    nb, tt = u_ref.shape[0], u_ref.shape[1]
    hist_rows = DW_TAPS - 1
    pad = 4 * SUBLANES
    base = pad - hist_rows
    span = shifted.shape[1]

    def tap_rows(j, r0, rows):
        q, s = divmod(base + j, SUBLANES)
        start = q * SUBLANES + r0
        if s == 0:
            return ubuf[start:start + rows, :]
        return shifted[s - 1, start:start + rows, :]

    for b in range(nb):
        @pl.when(i == 0)
        def _():
            ubuf[base:pad, :] = hist_ref[b]

        ubuf[pad:pad + tt, :] = u_ref[b]
        for s in range(1, SUBLANES):
            shifted[s - 1] = ubuf[s:s + span, :]

        rows = min(CONV_ROWS, tt)
        for r0 in range(0, tt, rows):
            acc = w_ref[0:1, :] * tap_rows(0, r0, rows)
            for j in range(1, DW_TAPS):
                acc = acc + w_ref[j:j + 1, :] * tap_rows(j, r0, rows)
            cv = acc + b_ref[...]
            mu = jnp.mean(cv, axis=-1, keepdims=True)
            xc = cv - mu
            var = jnp.mean(xc * xc, axis=-1, keepdims=True)
            y = xc * lax.rsqrt(var + LN_EPS) * lg_ref[...] + lb_ref[...]
            c_ref[b, r0:r0 + rows, :] = (y * _sigmoid(y)).astype(c_ref.dtype)
    if carry:
        ubuf[base:pad, :] = u_ref[0, tt - hist_rows:tt, :]


def _cconv(uu, hist, w, b, lg, lb, tt, nb):
    bsz, t, _ = uu.shape
    nt = t // tt
    assert nb == 1 or nt == 1
    vec = pl.BlockSpec((1, CONF_WIDTH), lambda bi, i: (0, 0))
    return pl.pallas_call(
        functools.partial(_cconv_kernel, nt > 1),
        grid=(bsz // nb, nt),
        in_specs=[
            pl.BlockSpec((nb, tt, CONF_WIDTH), lambda bi, i: (bi, i, 0)),
            pl.BlockSpec((nb, DW_TAPS - 1, CONF_WIDTH), lambda bi, i: (bi, 0, 0)),
            pl.BlockSpec((DW_TAPS, CONF_WIDTH), lambda bi, i: (0, 0)),
            vec, vec, vec,
        ],
        out_specs=pl.BlockSpec((nb, tt, CONF_WIDTH), lambda bi, i: (bi, i, 0)),
        out_shape=jax.ShapeDtypeStruct((bsz, t, CONF_WIDTH), BF16),
        scratch_shapes=[pltpu.VMEM((tt + 4 * SUBLANES, CONF_WIDTH), F32),
                        pltpu.VMEM((SUBLANES - 1, tt + 3 * SUBLANES, CONF_WIDTH), F32)],
        compiler_params=_params(("arbitrary", "arbitrary")),
        name="cconv",
    )(uu, hist, w, b, lg, lb)


def _outproj_kernel(tiles_a, oa_ref, ca_ref, xa_ref, ob_ref, cb_ref, xb_ref, *rest):
    i = pl.program_id(0)

    @pl.when(i < tiles_a)
    def _():
        _outproj_tile(oa_ref, ca_ref, xa_ref, *rest)

    @pl.when(i >= tiles_a)
    def _():
        _outproj_tile(ob_ref, cb_ref, xb_ref, *rest)


def _outproj_tile(o_ref, c_ref, x_ref, wo_ref, wc_ref, g_ref, wr_ref, br_ref, upper_ref,
                  x1_ref, h2_ref, metar_ref, cnt_ref):
    tm = x_ref.shape[0]

    mix = _dotb(o_ref[...], wo_ref[...]) + _dotb(c_ref[...], wc_ref[...])
    x1 = x_ref[...] + mix
    x1_ref[...] = x1
    h2 = x1 * lax.rsqrt(jnp.mean(x1 * x1, axis=-1, keepdims=True) + RMS_EPS) * g_ref[...]
    h2_ref[...] = h2.astype(BF16)
    logits = lax.dot_general(wr_ref[...], h2.astype(BF16), (((1,), (1,)), ((), ())),
                             preferred_element_type=F32) + br_ref[...]
    row = lax.broadcasted_iota(I32, (N_EXPERTS, tm), 0).astype(F32)
    vals = logits
    idxs, tops = [], []
    for _ in range(TOP_K):
        m = jnp.max(vals, axis=0, keepdims=True)
        idx = jnp.min(jnp.where(vals == m, row, float(N_EXPERTS)), axis=0, keepdims=True)
        idxs.append(idx)
        tops.append(m)
        vals = jnp.where(row == idx, -jnp.inf, vals)
    exps = [jnp.exp(v - tops[0]) for v in tops]
    den = exps[0] + exps[1] + exps[2] + exps[3]
    onehots = [(row == idx).astype(F32) for idx in idxs]
    chosen = onehots[0] + onehots[1] + onehots[2] + onehots[3]
    before = jnp.dot(chosen.astype(BF16), upper_ref[...], preferred_element_type=F32)
    ranks = [jnp.sum(oh * before, axis=0, keepdims=True) for oh in onehots]
    gates = [e / den for e in exps]
    metar_ref[...] = jnp.concatenate(
        idxs + ranks + gates + [jnp.zeros((META_ROWS - 3 * TOP_K, tm), F32)], axis=0)
    cnt_ref[...] = jnp.broadcast_to(jnp.sum(chosen, axis=1, keepdims=True), (N_EXPERTS, LANES))[None]


def _outproj(first, second, wo, wc, g, wr, br):
    t_a, t_b = first[2].shape[0], second[2].shape[0]
    tm = TOKEN_TILE
    upper = (jnp.arange(tm)[:, None] < jnp.arange(tm)[None, :]).astype(BF16)
    tiles_a = t_a // tm
    t_total = t_a + t_b
    full = lambda shape: pl.BlockSpec(shape, lambda i: (0, 0))
    rows_a = lambda width: pl.BlockSpec((tm, width), lambda i: (jnp.minimum(i, tiles_a - 1), 0))
    rows_b = lambda width: pl.BlockSpec((tm, width), lambda i: (jnp.maximum(i - tiles_a, 0), 0))
    widths = (GDN_WIDTH, CONF_WIDTH, D_MODEL)
    return pl.pallas_call(
        functools.partial(_outproj_kernel, tiles_a),
        grid=(t_total // tm,),
        in_specs=[rows_a(w) for w in widths] + [rows_b(w) for w in widths] + [
            full((GDN_WIDTH, D_MODEL)),
            full((CONF_WIDTH, D_MODEL)),
            full((1, D_MODEL)),
            full((N_EXPERTS, D_MODEL)),
            full((N_EXPERTS, 1)),
            full((tm, tm)),
        ],
        out_specs=[
            pl.BlockSpec((tm, D_MODEL), lambda i: (i, 0)),
            pl.BlockSpec((tm, D_MODEL), lambda i: (i, 0)),
            pl.BlockSpec((META_ROWS, tm), lambda i: (0, i)),
            pl.BlockSpec((1, N_EXPERTS, LANES), lambda i: (i, 0, 0)),
        ],
        out_shape=[
            jax.ShapeDtypeStruct((t_total, D_MODEL), F32),
            jax.ShapeDtypeStruct((t_total, D_MODEL), BF16),
            jax.ShapeDtypeStruct((META_ROWS, t_total), F32),
            jax.ShapeDtypeStruct((t_total // tm, N_EXPERTS, LANES), F32),
        ],
        compiler_params=_params(("arbitrary",)),
        name="outproj_router",
    )(*first, *second, wo, wc, g, wr, br, upper)


def _segment_copies(i, seg_local_ref, seg_dst_ref, seg_n_ref, make_copy, wait):
    def copy(rows, local, dst):
        cp = make_copy(rows, pl.multiple_of(local, GRANULE), pl.multiple_of(dst, GRANULE))
        cp.wait() if wait else cp.start()

    def per_expert(ex, carry):
        j = i * N_EXPERTS + ex
        local, dst, n = seg_local_ref[j], seg_dst_ref[j], seg_n_ref[j]
        big = 8 * GRANULE

        def body(g, c):
            copy(big, local + g * big, dst + g * big)
            return c

        lax.fori_loop(0, n >> 3, body, 0)
        off = (n >> 3) * big
        for bit in (2, 1, 0):
            rows = GRANULE << bit
            has = (n >> bit) & 1

            @pl.when(has == 1)
            def _(rows=rows, off=off):
                copy(rows, local + off, dst + off)

            off = off + has * rows
        return carry

    lax.fori_loop(0, N_EXPERTS, per_expert, 0)


def _gap_copies(gap_start_ref, gap_n_ref, make_copy, wait):
    def per_expert(ex, carry):
        n = gap_n_ref[ex]
        start = pl.multiple_of(gap_start_ref[ex], GRANULE)
        off = 0 * n
        for bit in reversed(range(GAP_BITS)):
            rows = GRANULE << bit

            @pl.when(((n >> bit) & 1) == 1)
            def _(rows=rows, off=off):
                cp = make_copy(rows, pl.multiple_of(start + off, GRANULE))
                cp.wait() if wait else cp.start()

            off = off + ((n >> bit) & 1) * rows
        return carry

    lax.fori_loop(0, N_EXPERTS, per_expert, 0)


def _local_rows(meta_e, seg_local_ref, i):
    base = jnp.zeros(meta_e.shape, F32)
    for ex in range(N_EXPERTS):
        base = jnp.where(meta_e == float(ex), seg_local_ref[i * N_EXPERTS + ex].astype(F32), base)
    return base


def _dispatch_kernel(seg_local_ref, seg_dst_ref, seg_n_ref, gap_start_ref, gap_n_ref, tail_ref,
                     h_ref, mr_ref, xs_ref, slots_ref, xloc, zbuf, sems, zsem):
    i = pl.program_id(0)
    last = pl.num_programs(0) - 1
    tm = h_ref.shape[0]
    rt = xloc.shape[1]
    zrows = zbuf.shape[0]
    slot = i % 2

    def gap_copy(rows, dst):
        return pltpu.make_async_copy(zbuf.at[pl.ds(0, rows)], xs_ref.at[pl.ds(dst, rows)], zsem)

    def tail_copies(wait):
        def body(j, c):
            cp = gap_copy(zrows, pl.multiple_of(tail_ref[0] + j * zrows, zrows))
            cp.wait() if wait else cp.start()
            return c
        lax.fori_loop(0, tail_ref[1], body, 0)

    @pl.when(i == 0)
    def _():
        zbuf[...] = jnp.zeros(zbuf.shape, zbuf.dtype)
        _gap_copies(gap_start_ref, gap_n_ref, gap_copy, wait=False)
        tail_copies(wait=False)

    mr = mr_ref[...]
    dl = (_local_rows(mr[META_EXPERT:META_EXPERT + TOP_K], seg_local_ref, i)
          + mr[META_RANK:META_RANK + TOP_K])
    slots_ref[...] = jnp.concatenate(
        [dl, mr[META_GATE:META_GATE + TOP_K], jnp.zeros((LANES - 2 * TOP_K, tm), F32)], axis=0).T
    rr = lax.broadcasted_iota(I32, (rt, tm), 0).astype(F32)
    hit = rr == dl[0:1]
    for k in range(1, TOP_K):
        hit = hit | (rr == dl[k:k + 1])
    xloc[slot] = jnp.dot(hit.astype(BF16), h_ref[...], preferred_element_type=F32)

    def seg_copy(buf):
        def make(rows, local, dst):
            return pltpu.make_async_copy(xloc.at[buf, pl.ds(local, rows)],
                                         xs_ref.at[pl.ds(dst, rows)], sems.at[buf])
        return make

    _segment_copies(i, seg_local_ref, seg_dst_ref, seg_n_ref, seg_copy(slot), wait=False)

    @pl.when(i > 0)
    def _():
        _segment_copies(i - 1, seg_local_ref, seg_dst_ref, seg_n_ref, seg_copy(1 - slot), wait=True)

    @pl.when(i == last)
    def _():
        _segment_copies(i, seg_local_ref, seg_dst_ref, seg_n_ref, seg_copy(slot), wait=True)
        _gap_copies(gap_start_ref, gap_n_ref, gap_copy, wait=True)
        tail_copies(wait=True)


def _dispatch(tables, h2, metar, n_rows):
    t = h2.shape[0]
    tm = TOKEN_TILE
    return pl.pallas_call(
        _dispatch_kernel,
        grid_spec=pltpu.PrefetchScalarGridSpec(
            num_scalar_prefetch=6,
            grid=(t // tm,),
            in_specs=[
                pl.BlockSpec((tm, D_MODEL), lambda i, *_: (i, 0)),
                pl.BlockSpec((META_ROWS, tm), lambda i, *_: (0, i)),
            ],
            out_specs=[pl.BlockSpec(memory_space=pl.ANY),
                       pl.BlockSpec((tm, LANES), lambda i, *_: (i, 0))],
            scratch_shapes=[
                pltpu.VMEM((2, TILE_ROWS, D_MODEL), F32),
                pltpu.VMEM((GRANULE << (GAP_BITS - 1), D_MODEL), F32),
                pltpu.SemaphoreType.DMA((2,)),
                pltpu.SemaphoreType.DMA,
            ],
        ),
        out_shape=[jax.ShapeDtypeStruct((n_rows, D_MODEL), F32),
                   jax.ShapeDtypeStruct((t, LANES), F32)],
        compiler_params=_params(("arbitrary",)),
        name="dispatch",
    )(*tables, h2, metar)


def _expert_kernel(be_ref, rb_ref, used_ref, first_ref, next_ref,
                   x_ref, wgu_hbm, bgu_ref, wd_hbm, bdn_ref, y_ref,
                   wgu_f32, wd_f32, wgu_bf, wd_bf, sems):
    i = pl.program_id(0)

    def fetch(expert):
        return (pltpu.make_async_copy(wgu_hbm.at[expert], wgu_f32, sems.at[0]),
                pltpu.make_async_copy(wd_hbm.at[expert], wd_f32, sems.at[1]))

    @pl.when(i == 0)
    def _():
        for cp in fetch(be_ref[0]):
            cp.start()

    @pl.when(first_ref[i] == 1)
    def _():
        for cp in fetch(be_ref[i]):
            cp.wait()
        wgu_bf[...] = wgu_f32[...].astype(BF16)
        wd_bf[...] = wd_f32[...].astype(BF16)

        @pl.when(next_ref[i] >= 0)
        def _():
            for cp in fetch(next_ref[i]):
                cp.start()

    def ffn(x):
        gu = _dotb(x, wgu_bf[...]) + bgu_ref[...]
        x_glu = jnp.minimum(gu[:, :D_EXPERT], SWIGLU_LIMIT)
        x_lin = jnp.clip(gu[:, D_EXPERT:], -SWIGLU_LIMIT, SWIGLU_LIMIT)
        act = x_glu * _sigmoid(SWIGLU_ALPHA * x_glu) * (x_lin + 1.0)
        return _dotb(act, wd_bf[...]) + bdn_ref[...]

    bm = x_ref.shape[0]
    parts = (used_ref[i] + EXPERT_PART_ROWS - 1) // EXPERT_PART_ROWS
    for p in range(1, bm // EXPERT_PART_ROWS + 1):
        @pl.when(parts == p)
        def _(rows=p * EXPERT_PART_ROWS):
            y_ref[:rows, :] = ffn(x_ref[:rows, :])
            if rows < bm:
                y_ref[rows:, :] = jnp.zeros((bm - rows, y_ref.shape[1]), y_ref.dtype)


def _experts(tables, xs, wgu, bgu, wd, bdn):
    rows = xs.shape[0]
    bm = EXPERT_ROWS
    nb = rows // bm
    return pl.pallas_call(
        _expert_kernel,
        grid_spec=pltpu.PrefetchScalarGridSpec(
            num_scalar_prefetch=5,
            grid=(nb,),
            in_specs=[
                pl.BlockSpec((bm, D_MODEL), lambda i, be, rb, *_: (rb[i], 0)),
                pl.BlockSpec(memory_space=pl.ANY),
                pl.BlockSpec((None, 1, 2 * D_EXPERT), lambda i, be, *_: (be[i], 0, 0)),
                pl.BlockSpec(memory_space=pl.ANY),
                pl.BlockSpec((None, 1, D_MODEL), lambda i, be, *_: (be[i], 0, 0)),
            ],
            out_specs=pl.BlockSpec((bm, D_MODEL), lambda i, be, rb, *_: (rb[i], 0)),
            scratch_shapes=[
                pltpu.VMEM((D_MODEL, 2 * D_EXPERT), F32),
                pltpu.VMEM((D_EXPERT, D_MODEL), F32),
                pltpu.VMEM((D_MODEL, 2 * D_EXPERT), BF16),
                pltpu.VMEM((D_EXPERT, D_MODEL), BF16),
                pltpu.SemaphoreType.DMA((2,)),
            ],
        ),
        out_shape=jax.ShapeDtypeStruct((rows, D_MODEL), F32),
        input_output_aliases={5: 0},
        compiler_params=pltpu.CompilerParams(dimension_semantics=("arbitrary",),
                                             vmem_limit_bytes=EXPERT_VMEM_LIMIT),
        name="experts",
    )(*tables, xs, wgu, bgu, wd, bdn)


def _combine_kernel(tiles_a, seg_local_ref, seg_dst_ref, seg_n_ref, x1_ref, mc_ref, yb_ref, gf_ref,
                    ya_ref, yb_out_ref, ybuf, wsel, sems):
    i = pl.program_id(0)
    n_tiles = pl.num_programs(0)
    tm = x1_ref.shape[0]
    rt = ybuf.shape[1]
    slot = i % 2

    def seg_copy(buf):
        def make(rows, local, src):
            return pltpu.make_async_copy(yb_ref.at[pl.ds(src, rows)],
                                         ybuf.at[buf, pl.ds(local, rows)], sems.at[buf])
        return make

    @pl.when(i == 0)
    def _():
        ybuf[...] = jnp.zeros(ybuf.shape, ybuf.dtype)
        _segment_copies(i, seg_local_ref, seg_dst_ref, seg_n_ref, seg_copy(slot), wait=False)

    @pl.when(i + 1 < n_tiles)
    def _():
        _segment_copies(i + 1, seg_local_ref, seg_dst_ref, seg_n_ref, seg_copy(1 - slot), wait=False)

    dl = mc_ref[:, 0:TOP_K]
    gate = mc_ref[:, TOP_K:2 * TOP_K]
    rb, cb = SEL_CHUNK
    col = lax.broadcasted_iota(I32, (rb, cb), 1).astype(F32)
    for r0 in range(0, tm, rb):
        rows_k = [jnp.broadcast_to(dl[r0:r0 + rb, k:k + 1], (rb, cb)) for k in range(TOP_K)]
        gate_k = [jnp.broadcast_to(gate[r0:r0 + rb, k:k + 1], (rb, cb)) for k in range(TOP_K)]
        for c0 in range(0, rt, cb):
            rr = col + float(c0)
            w = jnp.where(rr == rows_k[0], gate_k[0], 0.0)
            for k in range(1, TOP_K):
                w = jnp.where(rr == rows_k[k], gate_k[k], w)
            wsel[r0:r0 + rb, c0:c0 + cb] = w.astype(BF16)
    _segment_copies(i, seg_local_ref, seg_dst_ref, seg_n_ref, seg_copy(slot), wait=True)
    acc = x1_ref[...] + jnp.dot(wsel[...], ybuf[slot].astype(BF16), preferred_element_type=F32)
    y = acc * lax.rsqrt(jnp.mean(acc * acc, axis=-1, keepdims=True) + RMS_EPS) * gf_ref[...]

    @pl.when(i < tiles_a)
    def _():
        ya_ref[...] = y

    @pl.when(i >= tiles_a)
    def _():
        yb_out_ref[...] = y


def _combine(tables, x1, slots, yb, gfin, n_a):
    t = x1.shape[0]
    tm = TOKEN_TILE
    tiles_a = n_a // tm
    return pl.pallas_call(
        functools.partial(_combine_kernel, tiles_a),
        grid_spec=pltpu.PrefetchScalarGridSpec(
            num_scalar_prefetch=3,
            grid=(t // tm,),
            in_specs=[
                pl.BlockSpec((tm, D_MODEL), lambda i, *_: (i, 0)),
                pl.BlockSpec((tm, LANES), lambda i, *_: (i, 0)),
                pl.BlockSpec(memory_space=pl.ANY),
                pl.BlockSpec((1, D_MODEL), lambda i, *_: (0, 0)),
            ],
            out_specs=[
                pl.BlockSpec((tm, D_MODEL), lambda i, *_: (jnp.minimum(i, tiles_a - 1), 0)),
                pl.BlockSpec((tm, D_MODEL), lambda i, *_: (jnp.maximum(i - tiles_a, 0), 0)),
            ],
            scratch_shapes=[pltpu.VMEM((2, TILE_ROWS, D_MODEL), F32),
                            pltpu.VMEM((tm, TILE_ROWS), BF16),
                            pltpu.SemaphoreType.DMA((2,))],
        ),
        out_shape=[jax.ShapeDtypeStruct((n_a, D_MODEL), F32),
                   jax.ShapeDtypeStruct((t - n_a, D_MODEL), F32)],
        compiler_params=pltpu.CompilerParams(dimension_semantics=("arbitrary",),
                                             vmem_limit_bytes=EXPERT_VMEM_LIMIT),
        name="combine",
    )(*tables, x1, slots, yb, gfin)


def kernel(x_prompt, x_sample, state_gdn, state_qkv_conv, state_dwconv, norm_mix_g, w_in, w_conv_qkv,
           a_log, dt_bias, w_onorm, b_glu, w_dw, b_dw, ln_g, ln_b, w_out, norm_ffn_g, w_router,
           b_router, w_gate_up, b_gate_up, w_down, b_down, norm_final_g):
    bp, tp, _ = x_prompt.shape
    bs, ts, _ = x_sample.shape
    n_p, n_s = bp * tp, bs * ts

    wi = w_in[0]
    c_z = QKV_COLS
    c_bd = c_z + GDN_WIDTH
    c_glu = c_bd + 2 * GDN_HEADS
    wqkv = wi[:, :c_z].astype(BF16)
    wz = wi[:, c_z:c_bd].astype(BF16)
    wbd = jnp.pad(wi[:, c_bd:c_glu], ((0, 0), (0, LANES - 2 * GDN_HEADS))).astype(BF16)
    wglu = wi[:, c_glu:].astype(BF16)
    bglu = b_glu[0][None, :]
    pa = jnp.pad(-jnp.exp(a_log[0].astype(F32)), (GDN_HEADS, 0))[:, None]
    pb = jnp.pad(dt_bias[0].astype(F32), (GDN_HEADS, 0))[:, None]
    gmix = norm_mix_g[0][None, :]
    wconv = w_conv_qkv[0]
    wonorm = w_onorm[0][None, :]
    wdw, bdw = w_dw[0], b_dw[0][None, :]
    lng, lnb = ln_g[0][None, :], ln_b[0][None, :]
    wo = w_out[0][:GDN_WIDTH].astype(BF16)
    wc = w_out[0][GDN_WIDTH:].astype(BF16)
    gffn = norm_ffn_g[0][None, :]
    wr, br = w_router[0].T.astype(BF16), b_router[0].astype(F32)[:, None]
    bgu = b_gate_up[0][:, None, :]
    bdn = b_down[0][:, None, :]
    gfin = norm_final_g[None, :]

    xp = x_prompt.reshape(n_p, D_MODEL)
    xs_tok = x_sample.reshape(n_s, D_MODEL)

    def mixers(x2d, bsz, t, hist_qkv, s0, hist_glu, chunk, gdn_tile, gdn_rows, conv_tile):
        qkv, z, bd, bdt, uu = _inproj(x2d, gmix, wqkv, wz, wbd, wglu, bglu, pa, pb, chunk, t)
        qkv3 = qkv.reshape(bsz, t, QKV_COLS)
        uu3 = uu.reshape(bsz, t, CONF_WIDTH)
        if bdt.ndim == 2:
            bdt = bdt.reshape(2 * GDN_HEADS, bsz, t).transpose(1, 0, 2)
        o, s_new = _gdn(qkv3, z.reshape(bsz, t, GDN_WIDTH), bd.reshape(bsz, t, 2 * GDN_HEADS),
                        bdt, hist_qkv, s0, wconv, wonorm, chunk, gdn_tile, gdn_rows)
        cc = _cconv(uu3, hist_glu, wdw, bdw, lng, lnb, conv_tile, gdn_rows)
        return o.reshape(bsz * t, GDN_WIDTH), cc.reshape(bsz * t, CONF_WIDTH), s_new, qkv3, uu3

    zero_qkv = jnp.zeros((bp, QKV_TAPS - 1, QKV_COLS), F32)
    zero_s = jnp.zeros((bp, GDN_HEADS, HEAD_DIM, HEAD_DIM), F32)
    zero_glu = jnp.zeros((bp, DW_TAPS - 1, CONF_WIDTH), F32)
    o_p, c_p, s_p, qkv_p, uu_p = mixers(xp, bp, tp, zero_qkv, zero_s, zero_glu,
                                        PROMPT_CHUNK, GDN_TILE, 1, CONV_TILE)
    o_s, c_s, s_s, qkv_s, uu_s = mixers(xs_tok, bs, ts, state_qkv_conv[0], state_gdn[0],
                                        state_dwconv[0], ts, ts, GDN_SAMPLE_ROWS, ts)

    n_tok = n_p + n_s
    x1, h2, metar, cnt_tile = _outproj((o_p, c_p, xp), (o_s, c_s, xs_tok), wo, wc, gffn, wr, br)

    bm = EXPERT_ROWS
    n_tiles = n_tok // TOKEN_TILE
    counts = cnt_tile[:, :, 0].astype(I32)
    seg_rows = (counts + GRANULE - 1) // GRANULE * GRANULE
    ids = jnp.arange(N_EXPERTS, dtype=I32)
    tile_ids = jnp.arange(n_tiles, dtype=I32)
    earlier_e = (ids[:, None] < ids[None, :]).astype(I32)
    earlier_t = (tile_ids[None, :] < tile_ids[:, None]).astype(I32)
    seg_local = jnp.sum(seg_rows[:, :, None] * earlier_e[None], axis=1)
    seg_before = jnp.sum(earlier_t[:, :, None] * seg_rows[None], axis=1)
    rows_e = jnp.sum(seg_rows, axis=0)
    padded = (rows_e + bm - 1) // bm * bm
    pstart = jnp.sum(padded[:, None] * earlier_e, axis=0)
    pend = pstart + padded
    seg_dst = pstart[None, :] + seg_before
    n_used = jnp.maximum(pend[-1] // bm, 1).astype(I32)
    max_rows = n_tok * TOP_K + n_tiles * N_EXPERTS * (GRANULE - 1) + N_EXPERTS * (bm - 1)
    n_blocks = -(-max_rows // bm)
    blk = jnp.minimum(jnp.arange(n_blocks, dtype=I32), n_used - 1)
    block_expert = jnp.minimum(
        jnp.sum((pend[None, :] <= (blk * bm)[:, None]).astype(I32), axis=1), N_EXPERTS - 1)
    seg_tables = (seg_local.reshape(-1).astype(I32), seg_dst.reshape(-1).astype(I32),
                  (seg_rows // GRANULE).reshape(-1).astype(I32))
    gap_tables = ((pstart + rows_e).astype(I32), ((padded - rows_e) // GRANULE).astype(I32))
    zero_rows = GRANULE << (GAP_BITS - 1)
    tail_table = jnp.stack([pend[-1], (n_blocks * bm - pend[-1]) // zero_rows]).astype(I32)

    xs, slots = _dispatch(seg_tables + gap_tables + (tail_table,), h2, metar, n_blocks * bm)
    active = padded > 0
    later = active[None, :] & (ids[None, :] > ids[:, None])
    next_active = jnp.min(jnp.where(later, ids[None, :], N_EXPERTS), axis=1)
    next_active = jnp.where(next_active == N_EXPERTS, -1, next_active)
    steps = jnp.arange(n_blocks, dtype=I32)
    of_block = (block_expert[:, None] == ids[None, :]).astype(I32)
    per_block = lambda table: jnp.sum(of_block * table[None, :], axis=1)
    first = ((steps * bm == per_block(pstart)) & (steps < n_used)).astype(I32)
    used = jnp.clip(per_block(pstart + rows_e) - steps * bm, 0, bm)
    used = jnp.where(steps < n_used, used, 0).astype(I32)
    expert_tables = (block_expert, blk, used, first, per_block(next_active).astype(I32))
    yb = _experts(expert_tables, xs, w_gate_up[0], bgu, w_down[0], bdn)
    y_p, y_s = _combine(seg_tables, x1, slots, yb, gfin, n_p)

    hist_glu_s = jnp.concatenate([state_dwconv[0], uu_s], axis=1)[:, -(DW_TAPS - 1):]
    return (y_p.reshape(bp, tp, D_MODEL),
            y_s.reshape(bs, ts, D_MODEL),
            s_p[None],
            qkv_p[:, -(QKV_TAPS - 1):][None],
            uu_p[:, -(DW_TAPS - 1):][None],
            s_s[None],
            qkv_s[:, -(QKV_TAPS - 1):][None],
            hist_glu_s[None])
```

```python
import functools
import math

import jax
import jax.numpy as jnp
from jax import lax
from jax.experimental import pallas as pl
from jax.experimental.pallas import tpu as pltpu

F32 = jnp.float32
BF16 = jnp.bfloat16
I32 = jnp.int32

D_MODEL = 1024
GDN_HEADS = 4
HEAD_DIM = 128
GDN_WIDTH = GDN_HEADS * HEAD_DIM
QKV_COLS = 3 * GDN_WIDTH
CONF_WIDTH = 512
QKV_TAPS = 4
DW_TAPS = 31
N_EXPERTS = 32
TOP_K = 4
D_EXPERT = 1024
SWIGLU_LIMIT = 7.0
SWIGLU_ALPHA = 1.702
RMS_EPS = 1e-6
LN_EPS = 1e-5
L2_EPS = 1e-6
PROMPT_CHUNK = 64

LANES = 128
SUBLANES = 8
TOKEN_TILE = 512
INPROJ_TILE = 1024
GDN_TILE = 1024
GDN_SAMPLE_ROWS = 8
CONV_TILE = 512
CONV_ROWS = 512
EXPERT_ROWS = 512
EXPERT_PART_ROWS = 256
GRANULE = SUBLANES
GAP_BITS = 6
TILE_ROWS = -(-(TOKEN_TILE * TOP_K + N_EXPERTS * (GRANULE - 1)) // 256) * 256
META_EXPERT, META_RANK, META_GATE, META_ROWS = 0, TOP_K, 2 * TOP_K, 16
assert EXPERT_ROWS // GRANULE <= 1 << GAP_BITS
SEL_CHUNK = (64, 256)
VMEM_LIMIT = 48 * 1024 * 1024
EXPERT_VMEM_LIMIT = 56 * 1024 * 1024


def _dotb(a, b):
    return jnp.dot(a.astype(BF16), b.astype(BF16), preferred_element_type=F32)


def _sigmoid(x):
    return jax.nn.sigmoid(x)


def _params(sem):
    return pltpu.CompilerParams(dimension_semantics=sem, vmem_limit_bytes=VMEM_LIMIT)


def _inproj_kernel(x_ref, g_ref, wqkv_ref, wz_ref, wbd_ref, wglu_ref, bglu_ref, pa_ref, pb_ref,
                   tri_ref, qkv_ref, z_ref, bd_ref, bdt_ref, uu_ref):
    x = x_ref[...]
    tm = x.shape[0]
    ms = jnp.mean(x * x, axis=-1, keepdims=True)
    h = (x * lax.rsqrt(ms + RMS_EPS) * g_ref[...]).astype(BF16)
    qkv_ref[...] = jnp.dot(h, wqkv_ref[...], preferred_element_type=F32)
    z_ref[...] = jnp.dot(h, wz_ref[...], preferred_element_type=F32)
    glu = jnp.dot(h, wglu_ref[...], preferred_element_type=F32) + bglu_ref[...]
    uu_ref[...] = glu[:, :CONF_WIDTH] * _sigmoid(glu[:, CONF_WIDTH:])
    bd = jnp.dot(h, wbd_ref[...], preferred_element_type=F32)
    bdt = bd.T[:2 * GDN_HEADS, :]
    beta = _sigmoid(bdt)
    v = bdt + pb_ref[...]
    softplus = jnp.maximum(v, 0.0) + jnp.log1p(jnp.exp(-jnp.abs(v)))
    g = pa_ref[...] * softplus
    g_hi, g_rest = _split_bf16(g)
    g_mid = g - g_hi.astype(F32) - g_rest.astype(F32)
    pieces = jnp.concatenate([g_hi, g_rest, g_mid.astype(BF16)], axis=0)
    sums = jnp.dot(pieces, tri_ref[...], preferred_element_type=F32)
    rows = 2 * GDN_HEADS
    gc = sums[:rows] + sums[rows:2 * rows] + sums[2 * rows:]
    row = lax.broadcasted_iota(I32, (rows, tm), 0)
    res = jnp.where(row < GDN_HEADS, beta, gc)
    bdt_ref[...] = res
    bd_ref[...] = jnp.concatenate([res, jnp.zeros((LANES - rows, tm), F32)], axis=0).T[:, :rows]


def _inproj(x, g, wqkv, wz, wbd, wglu, bglu, pa, pb, chunk, t_batch):
    t = x.shape[0]
    tm = min(INPROJ_TILE, t)
    tiles_per_batch = t_batch // tm if t_batch % tm == 0 else 0
    pos = jnp.arange(tm)
    tri = ((pos[:, None] <= pos[None, :])
           & (pos[:, None] // chunk == pos[None, :] // chunk)).astype(BF16)
    full = lambda shape: pl.BlockSpec(shape, lambda i: (0, 0))
    if tiles_per_batch:
        bdt_spec = pl.BlockSpec((None, 2 * GDN_HEADS, tm),
                                lambda i: (i // tiles_per_batch, 0, i % tiles_per_batch))
        bdt_shape = jax.ShapeDtypeStruct((t // t_batch, 2 * GDN_HEADS, t_batch), F32)
    else:
        bdt_spec = pl.BlockSpec((2 * GDN_HEADS, tm), lambda i: (0, i))
        bdt_shape = jax.ShapeDtypeStruct((2 * GDN_HEADS, t), F32)
    return pl.pallas_call(
        _inproj_kernel,
        grid=(t // tm,),
        in_specs=[
            pl.BlockSpec((tm, D_MODEL), lambda i: (i, 0)),
            full((1, D_MODEL)),
            full((D_MODEL, QKV_COLS)),
            full((D_MODEL, GDN_WIDTH)),
            full((D_MODEL, LANES)),
            full((D_MODEL, 2 * CONF_WIDTH)),
            full((1, 2 * CONF_WIDTH)),
            full((2 * GDN_HEADS, 1)),
            full((2 * GDN_HEADS, 1)),
            full((tm, tm)),
        ],
        out_specs=[
            pl.BlockSpec((tm, QKV_COLS), lambda i: (i, 0)),
            pl.BlockSpec((tm, GDN_WIDTH), lambda i: (i, 0)),
            pl.BlockSpec((tm, 2 * GDN_HEADS), lambda i: (i, 0)),
            bdt_spec,
            pl.BlockSpec((tm, CONF_WIDTH), lambda i: (i, 0)),
        ],
        out_shape=[
            jax.ShapeDtypeStruct((t, QKV_COLS), F32),
            jax.ShapeDtypeStruct((t, GDN_WIDTH), F32),
            jax.ShapeDtypeStruct((t, 2 * GDN_HEADS), F32),
            bdt_shape,
            jax.ShapeDtypeStruct((t, CONF_WIDTH), F32),
        ],
        compiler_params=_params(("arbitrary",)),
        name="inproj",
    )(x, g, wqkv, wz, wbd, wglu, bglu, pa, pb, tri)


def _split_bf16(a):
    hi = a.astype(BF16)
    lo = (a - hi.astype(F32)).astype(BF16)
    return hi, lo


def _unit_lower_inverses(ms, eye, n):
    levels = int(math.log2(n)) - 1
    ps = [eye - m for m in ms]
    pw = [(-m).astype(BF16) for m in ms]
    pw = [jnp.dot(p, p, preferred_element_type=F32).astype(BF16) for p in pw]
    for level in range(1, levels + 1):
        last = level == levels
        nxt = []
        for u, p in enumerate(pw):
            if last:
                ps[u] = ps[u] + jnp.dot(ps[u].astype(BF16), p, preferred_element_type=F32)
            else:
                out = jnp.dot(jnp.concatenate([p, ps[u].astype(BF16)], axis=0), p,
                              preferred_element_type=F32)
                nxt.append(out[:n].astype(BF16))
                ps[u] = ps[u] + out[n:]
        pw = nxt
    return ps


def _gdn_kernel(chunk, x_ref, z_ref, bd_ref, bdt_ref, hist_ref, s0_ref, wc_ref, won_ref,
                o_ref, s_ref, xbuf):
    i = pl.program_id(1)
    nb, tt = x_ref.shape[0], x_ref.shape[1]
    nchunk = tt // chunk
    pad = SUBLANES
    hist_rows = QKV_TAPS - 1

    @pl.when(i == 0)
    def _():
        xbuf[:, pad - hist_rows:pad, :] = hist_ref[...]
        s_ref[...] = s0_ref[...]

    xbuf[:, pad:pad + tt, :] = x_ref[...]

    ii = lax.broadcasted_iota(I32, (chunk, chunk), 0)
    jj = lax.broadcasted_iota(I32, (chunk, chunk), 1)
    lower_incl = ii >= jj
    lower_strict = ii > jj
    eye = (ii == jj).astype(F32)
    won = won_ref[...]

    def conv_silu(b, col0):
        cols = slice(col0, col0 + HEAD_DIM)
        acc = wc_ref[hist_rows:hist_rows + 1, cols] * xbuf[b, pad:pad + tt, cols]
        for j in range(hist_rows):
            r0 = pad - hist_rows + j
            acc = acc + wc_ref[j:j + 1, cols] * xbuf[b, r0:r0 + tt, cols]
        return acc * _sigmoid(acc)

    units = []
    for b in range(nb):
        bd = bd_ref[b]
        bdt = bdt_ref[b]
        for h in range(GDN_HEADS):
            q = conv_silu(b, h * HEAD_DIM)
            k = conv_silu(b, GDN_WIDTH + h * HEAD_DIM)
            v = conv_silu(b, 2 * GDN_WIDTH + h * HEAD_DIM)
            q = q * lax.rsqrt(jnp.sum(q * q, axis=-1, keepdims=True) + L2_EPS) * (HEAD_DIM ** -0.5)
            k = k * lax.rsqrt(jnp.sum(k * k, axis=-1, keepdims=True) + L2_EPS)
            kt = k.T
            beta = bd[:, h:h + 1]
            gcol = bd[:, GDN_HEADS + h:GDN_HEADS + h + 1]
            grow = bdt[GDN_HEADS + h:GDN_HEADS + h + 1, :]
            for c in range(nchunk):
                rows = slice(c * chunk, (c + 1) * chunk)
                qc, kc, vc, ktc = q[rows], k[rows], v[rows], kt[:, rows]
                bc, gc, gr = beta[rows], gcol[rows], grow[:, rows]
                dec = jnp.exp(jnp.where(lower_incl, gc - gr, -jnp.inf))
                kb = kc * bc
                eg = jnp.exp(gc)
                both = _dotb(jnp.concatenate([kb, qc], axis=0), ktc)
                units.append(dict(
                    b=b, h=h, c=c,
                    m=both[:chunk] * jnp.where(lower_strict, dec, 0.0),
                    qk=both[chunk:] * dec,
                    rhs=jnp.concatenate([vc * bc, kb * eg], axis=1),
                    qg=qc * eg,
                    kdt=ktc * jnp.exp(gr[:, chunk - 1:chunk] - gr),
                    d_last=jnp.exp(gc[chunk - 1:chunk, :]),
                ))

    tinvs = _unit_lower_inverses([u["m"] for u in units], eye, chunk)
    for u, tinv in zip(units, tinvs):
        sol = _dotb(tinv, u["rhs"])
        u["u"], u["w"] = sol[:, :HEAD_DIM], sol[:, HEAD_DIM:]

    state = {(b, h): s_ref[b, h] for b in range(nb) for h in range(GDN_HEADS)}
    by_key = {(u["b"], u["h"], u["c"]): u for u in units}
    for c in range(nchunk):
        rows = slice(c * chunk, (c + 1) * chunk)
        for b in range(nb):
            for h in range(GDN_HEADS):
                u = by_key[(b, h, c)]
                s = state[(b, h)]
                ws = _dotb(jnp.concatenate([u["w"], u["qg"]], axis=0), s)
                v_new = u["u"] - ws[:chunk]
                upd = _dotb(jnp.concatenate([u["qk"], u["kdt"]], axis=0), v_new)
                o_c = ws[chunk:] + upd[:chunk]
                state[(b, h)] = s * u["d_last"] + upd[chunk:]
                cols = slice(h * HEAD_DIM, (h + 1) * HEAD_DIM)
                on = o_c * lax.rsqrt(jnp.mean(o_c * o_c, axis=-1, keepdims=True) + RMS_EPS) * won
                zc = z_ref[b, rows, cols]
                o_ref[b, rows, cols] = (on * (zc * _sigmoid(zc))).astype(o_ref.dtype)
    for (b, h), s in state.items():
        s_ref[b, h] = s

    xbuf[:, pad - hist_rows:pad, :] = x_ref[:, tt - hist_rows:tt, :]


def _gdn(qkv, z, bd, bdt, hist, s0, wconv, wonorm, chunk, tt, nb):
    b, t, _ = qkv.shape
    nt = t // tt
    return pl.pallas_call(
        functools.partial(_gdn_kernel, chunk),
        grid=(b // nb, nt),
        in_specs=[
            pl.BlockSpec((nb, tt, QKV_COLS), lambda bi, i: (bi, i, 0)),
            pl.BlockSpec((nb, tt, GDN_WIDTH), lambda bi, i: (bi, i, 0)),
            pl.BlockSpec((nb, tt, 2 * GDN_HEADS), lambda bi, i: (bi, i, 0)),
            pl.BlockSpec((nb, 2 * GDN_HEADS, tt), lambda bi, i: (bi, 0, i)),
            pl.BlockSpec((nb, QKV_TAPS - 1, QKV_COLS), lambda bi, i: (bi, 0, 0)),
            pl.BlockSpec((nb, GDN_HEADS, HEAD_DIM, HEAD_DIM), lambda bi, i: (bi, 0, 0, 0)),
            pl.BlockSpec((QKV_TAPS, QKV_COLS), lambda bi, i: (0, 0)),
            pl.BlockSpec((1, HEAD_DIM), lambda bi, i: (0, 0)),
        ],
        out_specs=[
            pl.BlockSpec((nb, tt, GDN_WIDTH), lambda bi, i: (bi, i, 0)),
            pl.BlockSpec((nb, GDN_HEADS, HEAD_DIM, HEAD_DIM), lambda bi, i: (bi, 0, 0, 0)),
        ],
        out_shape=[
            jax.ShapeDtypeStruct((b, t, GDN_WIDTH), BF16),
            jax.ShapeDtypeStruct((b, GDN_HEADS, HEAD_DIM, HEAD_DIM), F32),
        ],
        scratch_shapes=[pltpu.VMEM((nb, tt + SUBLANES, QKV_COLS), F32)],
        compiler_params=_params(("arbitrary", "arbitrary")),
        name="gdn",
    )(qkv, z, bd, bdt, hist, s0, wconv, wonorm)


def _cconv_kernel(carry, u_ref, hist_ref, w_ref, b_ref, lg_ref, lb_ref, c_ref, ubuf, shifted):
    i = pl.program_id(1)
    nb, tt = u_ref.shape[0], u_ref.shape[1]
    hist_rows = DW_TAPS - 1
    pad = 4 * SUBLANES
    base = pad - hist_rows
    span = shifted.shape[1]

    def tap_rows(j, r0, rows):
        q, s = divmod(base + j, SUBLANES)
        start = q * SUBLANES + r0
        if s == 0:
            return ubuf[start:start + rows, :]
        return shifted[s - 1, start:start + rows, :]

    for b in range(nb):
        @pl.when(i == 0)
        def _():
            ubuf[base:pad, :] = hist_ref[b]

        ubuf[pad:pad + tt, :] = u_ref[b]
        for s in range(1, SUBLANES):
            shifted[s - 1] = ubuf[s:s + span, :]

        rows = min(CONV_ROWS, tt)
        for r0 in range(0, tt, rows):
            acc = w_ref[0:1, :] * tap_rows(0, r0, rows)
            for j in range(1, DW_TAPS):
                acc = acc + w_ref[j:j + 1, :] * tap_rows(j, r0, rows)
            cv = acc + b_ref[...]
            mu = jnp.mean(cv, axis=-1, keepdims=True)
            xc = cv - mu
            var = jnp.mean(xc * xc, axis=-1, keepdims=True)
            y = xc * lax.rsqrt(var + LN_EPS) * lg_ref[...] + lb_ref[...]
            c_ref[b, r0:r0 + rows, :] = (y * _sigmoid(y)).astype(c_ref.dtype)
    if carry:
        ubuf[base:pad, :] = u_ref[0, tt - hist_rows:tt, :]


def _cconv(uu, hist, w, b, lg, lb, tt, nb):
    bsz, t, _ = uu.shape
    nt = t // tt
    assert nb == 1 or nt == 1
    vec = pl.BlockSpec((1, CONF_WIDTH), lambda bi, i: (0, 0))
    return pl.pallas_call(
        functools.partial(_cconv_kernel, nt > 1),
        grid=(bsz // nb, nt),
        in_specs=[
            pl.BlockSpec((nb, tt, CONF_WIDTH), lambda bi, i: (bi, i, 0)),
            pl.BlockSpec((nb, DW_TAPS - 1, CONF_WIDTH), lambda bi, i: (bi, 0, 0)),
            pl.BlockSpec((DW_TAPS, CONF_WIDTH), lambda bi, i: (0, 0)),
            vec, vec, vec,
        ],
        out_specs=pl.BlockSpec((nb, tt, CONF_WIDTH), lambda bi, i: (bi, i, 0)),
        out_shape=jax.ShapeDtypeStruct((bsz, t, CONF_WIDTH), BF16),
        scratch_shapes=[pltpu.VMEM((tt + 4 * SUBLANES, CONF_WIDTH), F32),
                        pltpu.VMEM((SUBLANES - 1, tt + 3 * SUBLANES, CONF_WIDTH), F32)],
        compiler_params=_params(("arbitrary", "arbitrary")),
        name="cconv",
    )(uu, hist, w, b, lg, lb)


def _outproj_kernel(tiles_a, oa_ref, ca_ref, xa_ref, ob_ref, cb_ref, xb_ref, *rest):
    i = pl.program_id(0)

    @pl.when(i < tiles_a)
    def _():
        _outproj_tile(oa_ref, ca_ref, xa_ref, *rest)

    @pl.when(i >= tiles_a)
    def _():
        _outproj_tile(ob_ref, cb_ref, xb_ref, *rest)


def _outproj_tile(o_ref, c_ref, x_ref, wo_ref, wc_ref, g_ref, wr_ref, br_ref, upper_ref,
                  x1_ref, h2_ref, metar_ref, cnt_ref):
    tm = x_ref.shape[0]

    mix = _dotb(o_ref[...], wo_ref[...]) + _dotb(c_ref[...], wc_ref[...])
    x1 = x_ref[...] + mix
    x1_ref[...] = x1
    h2 = x1 * lax.rsqrt(jnp.mean(x1 * x1, axis=-1, keepdims=True) + RMS_EPS) * g_ref[...]
    h2_ref[...] = h2.astype(BF16)
    logits = lax.dot_general(wr_ref[...], h2.astype(BF16), (((1,), (1,)), ((), ())),
                             preferred_element_type=F32) + br_ref[...]
    row = lax.broadcasted_iota(I32, (N_EXPERTS, tm), 0).astype(F32)
    vals = logits
    idxs, tops = [], []
    for _ in range(TOP_K):
        m = jnp.max(vals, axis=0, keepdims=True)
        idx = jnp.min(jnp.where(vals == m, row, float(N_EXPERTS)), axis=0, keepdims=True)
        idxs.append(idx)
        tops.append(m)
        vals = jnp.where(row == idx, -jnp.inf, vals)
    exps = [jnp.exp(v - tops[0]) for v in tops]
    den = exps[0] + exps[1] + exps[2] + exps[3]
    onehots = [(row == idx).astype(F32) for idx in idxs]
    chosen = onehots[0] + onehots[1] + onehots[2] + onehots[3]
    before = jnp.dot(chosen.astype(BF16), upper_ref[...], preferred_element_type=F32)
    ranks = [jnp.sum(oh * before, axis=0, keepdims=True) for oh in onehots]
    gates = [e / den for e in exps]
    metar_ref[...] = jnp.concatenate(
        idxs + ranks + gates + [jnp.zeros((META_ROWS - 3 * TOP_K, tm), F32)], axis=0)
    cnt_ref[...] = jnp.broadcast_to(jnp.sum(chosen, axis=1, keepdims=True), (N_EXPERTS, LANES))[None]


def _outproj(first, second, wo, wc, g, wr, br):
    t_a, t_b = first[2].shape[0], second[2].shape[0]
    tm = TOKEN_TILE
    upper = (jnp.arange(tm)[:, None] < jnp.arange(tm)[None, :]).astype(BF16)
    tiles_a = t_a // tm
    t_total = t_a + t_b
    full = lambda shape: pl.BlockSpec(shape, lambda i: (0, 0))
    rows_a = lambda width: pl.BlockSpec((tm, width), lambda i: (jnp.minimum(i, tiles_a - 1), 0))
    rows_b = lambda width: pl.BlockSpec((tm, width), lambda i: (jnp.maximum(i - tiles_a, 0), 0))
    widths = (GDN_WIDTH, CONF_WIDTH, D_MODEL)
    return pl.pallas_call(
        functools.partial(_outproj_kernel, tiles_a),
        grid=(t_total // tm,),
        in_specs=[rows_a(w) for w in widths] + [rows_b(w) for w in widths] + [
            full((GDN_WIDTH, D_MODEL)),
            full((CONF_WIDTH, D_MODEL)),
            full((1, D_MODEL)),
            full((N_EXPERTS, D_MODEL)),
            full((N_EXPERTS, 1)),
            full((tm, tm)),
        ],
        out_specs=[
            pl.BlockSpec((tm, D_MODEL), lambda i: (i, 0)),
            pl.BlockSpec((tm, D_MODEL), lambda i: (i, 0)),
            pl.BlockSpec((META_ROWS, tm), lambda i: (0, i)),
            pl.BlockSpec((1, N_EXPERTS, LANES), lambda i: (i, 0, 0)),
        ],
        out_shape=[
            jax.ShapeDtypeStruct((t_total, D_MODEL), F32),
            jax.ShapeDtypeStruct((t_total, D_MODEL), BF16),
            jax.ShapeDtypeStruct((META_ROWS, t_total), F32),
            jax.ShapeDtypeStruct((t_total // tm, N_EXPERTS, LANES), F32),
        ],
        compiler_params=_params(("arbitrary",)),
        name="outproj_router",
    )(*first, *second, wo, wc, g, wr, br, upper)


def _segment_copies(i, seg_local_ref, seg_dst_ref, seg_n_ref, make_copy, wait):
    def copy(rows, local, dst):
        cp = make_copy(rows, pl.multiple_of(local, GRANULE), pl.multiple_of(dst, GRANULE))
        cp.wait() if wait else cp.start()

    def per_expert(ex, carry):
        j = i * N_EXPERTS + ex
        local, dst, n = seg_local_ref[j], seg_dst_ref[j], seg_n_ref[j]
        big = 8 * GRANULE

        def body(g, c):
            copy(big, local + g * big, dst + g * big)
            return c

        lax.fori_loop(0, n >> 3, body, 0)
        off = (n >> 3) * big
        for bit in (2, 1, 0):
            rows = GRANULE << bit
            has = (n >> bit) & 1

            @pl.when(has == 1)
            def _(rows=rows, off=off):
                copy(rows, local + off, dst + off)

            off = off + has * rows
        return carry

    lax.fori_loop(0, N_EXPERTS, per_expert, 0)


def _gap_copies(gap_start_ref, gap_n_ref, make_copy, wait):
    def per_expert(ex, carry):
        n = gap_n_ref[ex]
        start = pl.multiple_of(gap_start_ref[ex], GRANULE)
        off = 0 * n
        for bit in reversed(range(GAP_BITS)):
            rows = GRANULE << bit

            @pl.when(((n >> bit) & 1) == 1)
            def _(rows=rows, off=off):
                cp = make_copy(rows, pl.multiple_of(start + off, GRANULE))
                cp.wait() if wait else cp.start()

            off = off + ((n >> bit) & 1) * rows
        return carry

    lax.fori_loop(0, N_EXPERTS, per_expert, 0)


def _local_rows(meta_e, seg_local_ref, i):
    base = jnp.zeros(meta_e.shape, F32)
    for ex in range(N_EXPERTS):
        base = jnp.where(meta_e == float(ex), seg_local_ref[i * N_EXPERTS + ex].astype(F32), base)
    return base


def _dispatch_kernel(seg_local_ref, seg_dst_ref, seg_n_ref, gap_start_ref, gap_n_ref, tail_ref,
                     h_ref, mr_ref, xs_ref, slots_ref, xloc, zbuf, sems, zsem):
    i = pl.program_id(0)
    last = pl.num_programs(0) - 1
    tm = h_ref.shape[0]
    rt = xloc.shape[1]
    zrows = zbuf.shape[0]
    slot = i % 2

    def gap_copy(rows, dst):
        return pltpu.make_async_copy(zbuf.at[pl.ds(0, rows)], xs_ref.at[pl.ds(dst, rows)], zsem)

    def tail_copies(wait):
        def body(j, c):
            cp = gap_copy(zrows, pl.multiple_of(tail_ref[0] + j * zrows, zrows))
            cp.wait() if wait else cp.start()
            return c
        lax.fori_loop(0, tail_ref[1], body, 0)

    @pl.when(i == 0)
    def _():
        zbuf[...] = jnp.zeros(zbuf.shape, zbuf.dtype)
        _gap_copies(gap_start_ref, gap_n_ref, gap_copy, wait=False)
        tail_copies(wait=False)

    mr = mr_ref[...]
    dl = (_local_rows(mr[META_EXPERT:META_EXPERT + TOP_K], seg_local_ref, i)
          + mr[META_RANK:META_RANK + TOP_K])
    slots_ref[...] = jnp.concatenate(
        [dl, mr[META_GATE:META_GATE + TOP_K], jnp.zeros((LANES - 2 * TOP_K, tm), F32)], axis=0).T
    rr = lax.broadcasted_iota(I32, (rt, tm), 0).astype(F32)
    hit = rr == dl[0:1]
    for k in range(1, TOP_K):
        hit = hit | (rr == dl[k:k + 1])
    xloc[slot] = jnp.dot(hit.astype(BF16), h_ref[...], preferred_element_type=F32)

    def seg_copy(buf):
        def make(rows, local, dst):
            return pltpu.make_async_copy(xloc.at[buf, pl.ds(local, rows)],
                                         xs_ref.at[pl.ds(dst, rows)], sems.at[buf])
        return make

    _segment_copies(i, seg_local_ref, seg_dst_ref, seg_n_ref, seg_copy(slot), wait=False)

    @pl.when(i > 0)
    def _():
        _segment_copies(i - 1, seg_local_ref, seg_dst_ref, seg_n_ref, seg_copy(1 - slot), wait=True)

    @pl.when(i == last)
    def _():
        _segment_copies(i, seg_local_ref, seg_dst_ref, seg_n_ref, seg_copy(slot), wait=True)
        _gap_copies(gap_start_ref, gap_n_ref, gap_copy, wait=True)
        tail_copies(wait=True)


def _dispatch(tables, h2, metar, n_rows):
    t = h2.shape[0]
    tm = TOKEN_TILE
    return pl.pallas_call(
        _dispatch_kernel,
        grid_spec=pltpu.PrefetchScalarGridSpec(
            num_scalar_prefetch=6,
            grid=(t // tm,),
            in_specs=[
                pl.BlockSpec((tm, D_MODEL), lambda i, *_: (i, 0)),
                pl.BlockSpec((META_ROWS, tm), lambda i, *_: (0, i)),
            ],
            out_specs=[pl.BlockSpec(memory_space=pl.ANY),
                       pl.BlockSpec((tm, LANES), lambda i, *_: (i, 0))],
            scratch_shapes=[
                pltpu.VMEM((2, TILE_ROWS, D_MODEL), F32),
                pltpu.VMEM((GRANULE << (GAP_BITS - 1), D_MODEL), F32),
                pltpu.SemaphoreType.DMA((2,)),
                pltpu.SemaphoreType.DMA,
            ],
        ),
        out_shape=[jax.ShapeDtypeStruct((n_rows, D_MODEL), F32),
                   jax.ShapeDtypeStruct((t, LANES), F32)],
        compiler_params=_params(("arbitrary",)),
        name="dispatch",
    )(*tables, h2, metar)


def _expert_kernel(be_ref, rb_ref, used_ref, first_ref, next_ref,
                   x_ref, wgu_hbm, bgu_ref, wd_hbm, bdn_ref, y_ref,
                   wgu_f32, wd_f32, wgu_bf, wd_bf, sems):
    i = pl.program_id(0)

    def fetch(expert):
        return (pltpu.make_async_copy(wgu_hbm.at[expert], wgu_f32, sems.at[0]),
                pltpu.make_async_copy(wd_hbm.at[expert], wd_f32, sems.at[1]))

    @pl.when(i == 0)
    def _():
        for cp in fetch(be_ref[0]):
            cp.start()

    @pl.when(first_ref[i] == 1)
    def _():
        for cp in fetch(be_ref[i]):
            cp.wait()
        wgu_bf[...] = wgu_f32[...].astype(BF16)
        wd_bf[...] = wd_f32[...].astype(BF16)

        @pl.when(next_ref[i] >= 0)
        def _():
            for cp in fetch(next_ref[i]):
                cp.start()

    def ffn(x):
        gu = _dotb(x, wgu_bf[...]) + bgu_ref[...]
        x_glu = jnp.minimum(gu[:, :D_EXPERT], SWIGLU_LIMIT)
        x_lin = jnp.clip(gu[:, D_EXPERT:], -SWIGLU_LIMIT, SWIGLU_LIMIT)
        act = x_glu * _sigmoid(SWIGLU_ALPHA * x_glu) * (x_lin + 1.0)
        return _dotb(act, wd_bf[...]) + bdn_ref[...]

    bm = x_ref.shape[0]
    parts = (used_ref[i] + EXPERT_PART_ROWS - 1) // EXPERT_PART_ROWS
    for p in range(1, bm // EXPERT_PART_ROWS + 1):
        @pl.when(parts == p)
        def _(rows=p * EXPERT_PART_ROWS):
            y_ref[:rows, :] = ffn(x_ref[:rows, :])
            if rows < bm:
                y_ref[rows:, :] = jnp.zeros((bm - rows, y_ref.shape[1]), y_ref.dtype)


def _experts(tables, xs, wgu, bgu, wd, bdn):
    rows = xs.shape[0]
    bm = EXPERT_ROWS
    nb = rows // bm
    return pl.pallas_call(
        _expert_kernel,
        grid_spec=pltpu.PrefetchScalarGridSpec(
            num_scalar_prefetch=5,
            grid=(nb,),
            in_specs=[
                pl.BlockSpec((bm, D_MODEL), lambda i, be, rb, *_: (rb[i], 0)),
                pl.BlockSpec(memory_space=pl.ANY),
                pl.BlockSpec((None, 1, 2 * D_EXPERT), lambda i, be, *_: (be[i], 0, 0)),
                pl.BlockSpec(memory_space=pl.ANY),
                pl.BlockSpec((None, 1, D_MODEL), lambda i, be, *_: (be[i], 0, 0)),
            ],
            out_specs=pl.BlockSpec((bm, D_MODEL), lambda i, be, rb, *_: (rb[i], 0)),
            scratch_shapes=[
                pltpu.VMEM((D_MODEL, 2 * D_EXPERT), F32),
                pltpu.VMEM((D_EXPERT, D_MODEL), F32),
                pltpu.VMEM((D_MODEL, 2 * D_EXPERT), BF16),
                pltpu.VMEM((D_EXPERT, D_MODEL), BF16),
                pltpu.SemaphoreType.DMA((2,)),
            ],
        ),
        out_shape=jax.ShapeDtypeStruct((rows, D_MODEL), F32),
        input_output_aliases={5: 0},
        compiler_params=pltpu.CompilerParams(dimension_semantics=("arbitrary",),
                                             vmem_limit_bytes=EXPERT_VMEM_LIMIT),
        name="experts",
    )(*tables, xs, wgu, bgu, wd, bdn)


def _combine_kernel(tiles_a, seg_local_ref, seg_dst_ref, seg_n_ref, x1_ref, mc_ref, yb_ref, gf_ref,
                    ya_ref, yb_out_ref, ybuf, wsel, sems):
    i = pl.program_id(0)
    n_tiles = pl.num_programs(0)
    tm = x1_ref.shape[0]
    rt = ybuf.shape[1]
    slot = i % 2

    def seg_copy(buf):
        def make(rows, local, src):
            return pltpu.make_async_copy(yb_ref.at[pl.ds(src, rows)],
                                         ybuf.at[buf, pl.ds(local, rows)], sems.at[buf])
        return make

    @pl.when(i == 0)
    def _():
        ybuf[...] = jnp.zeros(ybuf.shape, ybuf.dtype)
        _segment_copies(i, seg_local_ref, seg_dst_ref, seg_n_ref, seg_copy(slot), wait=False)

    @pl.when(i + 1 < n_tiles)
    def _():
        _segment_copies(i + 1, seg_local_ref, seg_dst_ref, seg_n_ref, seg_copy(1 - slot), wait=False)

    dl = mc_ref[:, 0:TOP_K]
    gate = mc_ref[:, TOP_K:2 * TOP_K]
    rb, cb = SEL_CHUNK
    col = lax.broadcasted_iota(I32, (rb, cb), 1).astype(F32)
    for r0 in range(0, tm, rb):
        rows_k = [jnp.broadcast_to(dl[r0:r0 + rb, k:k + 1], (rb, cb)) for k in range(TOP_K)]
        gate_k = [jnp.broadcast_to(gate[r0:r0 + rb, k:k + 1], (rb, cb)) for k in range(TOP_K)]
        for c0 in range(0, rt, cb):
            rr = col + float(c0)
            w = jnp.where(rr == rows_k[0], gate_k[0], 0.0)
            for k in range(1, TOP_K):
                w = jnp.where(rr == rows_k[k], gate_k[k], w)
            wsel[r0:r0 + rb, c0:c0 + cb] = w.astype(BF16)
    _segment_copies(i, seg_local_ref, seg_dst_ref, seg_n_ref, seg_copy(slot), wait=True)
    acc = x1_ref[...] + jnp.dot(wsel[...], ybuf[slot].astype(BF16), preferred_element_type=F32)
    y = acc * lax.rsqrt(jnp.mean(acc * acc, axis=-1, keepdims=True) + RMS_EPS) * gf_ref[...]

    @pl.when(i < tiles_a)
    def _():
        ya_ref[...] = y

    @pl.when(i >= tiles_a)
    def _():
        yb_out_ref[...] = y


def _combine(tables, x1, slots, yb, gfin, n_a):
    t = x1.shape[0]
    tm = TOKEN_TILE
    tiles_a = n_a // tm
    return pl.pallas_call(
        functools.partial(_combine_kernel, tiles_a),
        grid_spec=pltpu.PrefetchScalarGridSpec(
            num_scalar_prefetch=3,
            grid=(t // tm,),
            in_specs=[
                pl.BlockSpec((tm, D_MODEL), lambda i, *_: (i, 0)),
                pl.BlockSpec((tm, LANES), lambda i, *_: (i, 0)),
                pl.BlockSpec(memory_space=pl.ANY),
                pl.BlockSpec((1, D_MODEL), lambda i, *_: (0, 0)),
            ],
            out_specs=[
                pl.BlockSpec((tm, D_MODEL), lambda i, *_: (jnp.minimum(i, tiles_a - 1), 0)),
                pl.BlockSpec((tm, D_MODEL), lambda i, *_: (jnp.maximum(i - tiles_a, 0), 0)),
            ],
            scratch_shapes=[pltpu.VMEM((2, TILE_ROWS, D_MODEL), F32),
                            pltpu.VMEM((tm, TILE_ROWS), BF16),
                            pltpu.SemaphoreType.DMA((2,))],
        ),
        out_shape=[jax.ShapeDtypeStruct((n_a, D_MODEL), F32),
                   jax.ShapeDtypeStruct((t - n_a, D_MODEL), F32)],
        compiler_params=pltpu.CompilerParams(dimension_semantics=("arbitrary",),
                                             vmem_limit_bytes=EXPERT_VMEM_LIMIT),
        name="combine",
    )(*tables, x1, slots, yb, gfin)


def kernel(x_prompt, x_sample, state_gdn, state_qkv_conv, state_dwconv, norm_mix_g, w_in, w_conv_qkv,
           a_log, dt_bias, w_onorm, b_glu, w_dw, b_dw, ln_g, ln_b, w_out, norm_ffn_g, w_router,
           b_router, w_gate_up, b_gate_up, w_down, b_down, norm_final_g):
    bp, tp, _ = x_prompt.shape
    bs, ts, _ = x_sample.shape
    n_p, n_s = bp * tp, bs * ts

    wi = w_in[0]
    c_z = QKV_COLS
    c_bd = c_z + GDN_WIDTH
    c_glu = c_bd + 2 * GDN_HEADS
    wqkv = wi[:, :c_z].astype(BF16)
    wz = wi[:, c_z:c_bd].astype(BF16)
    wbd = jnp.pad(wi[:, c_bd:c_glu], ((0, 0), (0, LANES - 2 * GDN_HEADS))).astype(BF16)
    wglu = wi[:, c_glu:].astype(BF16)
    bglu = b_glu[0][None, :]
    pa = jnp.pad(-jnp.exp(a_log[0].astype(F32)), (GDN_HEADS, 0))[:, None]
    pb = jnp.pad(dt_bias[0].astype(F32), (GDN_HEADS, 0))[:, None]
    gmix = norm_mix_g[0][None, :]
    wconv = w_conv_qkv[0]
    wonorm = w_onorm[0][None, :]
    wdw, bdw = w_dw[0], b_dw[0][None, :]
    lng, lnb = ln_g[0][None, :], ln_b[0][None, :]
    wo = w_out[0][:GDN_WIDTH].astype(BF16)
    wc = w_out[0][GDN_WIDTH:].astype(BF16)
    gffn = norm_ffn_g[0][None, :]
    wr, br = w_router[0].T.astype(BF16), b_router[0].astype(F32)[:, None]
    bgu = b_gate_up[0][:, None, :]
    bdn = b_down[0][:, None, :]
    gfin = norm_final_g[None, :]

    xp = x_prompt.reshape(n_p, D_MODEL)
    xs_tok = x_sample.reshape(n_s, D_MODEL)

    def mixers(x2d, bsz, t, hist_qkv, s0, hist_glu, chunk, gdn_tile, gdn_rows, conv_tile):
        qkv, z, bd, bdt, uu = _inproj(x2d, gmix, wqkv, wz, wbd, wglu, bglu, pa, pb, chunk, t)
        qkv3 = qkv.reshape(bsz, t, QKV_COLS)
        uu3 = uu.reshape(bsz, t, CONF_WIDTH)
        if bdt.ndim == 2:
            bdt = bdt.reshape(2 * GDN_HEADS, bsz, t).transpose(1, 0, 2)
        o, s_new = _gdn(qkv3, z.reshape(bsz, t, GDN_WIDTH), bd.reshape(bsz, t, 2 * GDN_HEADS),
                        bdt, hist_qkv, s0, wconv, wonorm, chunk, gdn_tile, gdn_rows)
        cc = _cconv(uu3, hist_glu, wdw, bdw, lng, lnb, conv_tile, gdn_rows)
        return o.reshape(bsz * t, GDN_WIDTH), cc.reshape(bsz * t, CONF_WIDTH), s_new, qkv3, uu3

    zero_qkv = jnp.zeros((bp, QKV_TAPS - 1, QKV_COLS), F32)
    zero_s = jnp.zeros((bp, GDN_HEADS, HEAD_DIM, HEAD_DIM), F32)
    zero_glu = jnp.zeros((bp, DW_TAPS - 1, CONF_WIDTH), F32)
    o_p, c_p, s_p, qkv_p, uu_p = mixers(xp, bp, tp, zero_qkv, zero_s, zero_glu,
                                        PROMPT_CHUNK, GDN_TILE, 1, CONV_TILE)
    o_s, c_s, s_s, qkv_s, uu_s = mixers(xs_tok, bs, ts, state_qkv_conv[0], state_gdn[0],
                                        state_dwconv[0], ts, ts, GDN_SAMPLE_ROWS, ts)

    n_tok = n_p + n_s
    x1, h2, metar, cnt_tile = _outproj((o_p, c_p, xp), (o_s, c_s, xs_tok), wo, wc, gffn, wr, br)

    bm = EXPERT_ROWS
    n_tiles = n_tok // TOKEN_TILE
    counts = cnt_tile[:, :, 0].astype(I32)
    seg_rows = (counts + GRANULE - 1) // GRANULE * GRANULE
    ids = jnp.arange(N_EXPERTS, dtype=I32)
    tile_ids = jnp.arange(n_tiles, dtype=I32)
    earlier_e = (ids[:, None] < ids[None, :]).astype(I32)
    earlier_t = (tile_ids[None, :] < tile_ids[:, None]).astype(I32)
    seg_local = jnp.sum(seg_rows[:, :, None] * earlier_e[None], axis=1)
    seg_before = jnp.sum(earlier_t[:, :, None] * seg_rows[None], axis=1)
    rows_e = jnp.sum(seg_rows, axis=0)
    padded = (rows_e + bm - 1) // bm * bm
    pstart = jnp.sum(padded[:, None] * earlier_e, axis=0)
    pend = pstart + padded
    seg_dst = pstart[None, :] + seg_before
    n_used = jnp.maximum(pend[-1] // bm, 1).astype(I32)
    max_rows = n_tok * TOP_K + n_tiles * N_EXPERTS * (GRANULE - 1) + N_EXPERTS * (bm - 1)
    n_blocks = -(-max_rows // bm)
    blk = jnp.minimum(jnp.arange(n_blocks, dtype=I32), n_used - 1)
    block_expert = jnp.minimum(
        jnp.sum((pend[None, :] <= (blk * bm)[:, None]).astype(I32), axis=1), N_EXPERTS - 1)
    seg_tables = (seg_local.reshape(-1).astype(I32), seg_dst.reshape(-1).astype(I32),
                  (seg_rows // GRANULE).reshape(-1).astype(I32))
    gap_tables = ((pstart + rows_e).astype(I32), ((padded - rows_e) // GRANULE).astype(I32))
    zero_rows = GRANULE << (GAP_BITS - 1)
    tail_table = jnp.stack([pend[-1], (n_blocks * bm - pend[-1]) // zero_rows]).astype(I32)

    xs, slots = _dispatch(seg_tables + gap_tables + (tail_table,), h2, metar, n_blocks * bm)
    active = padded > 0
    later = active[None, :] & (ids[None, :] > ids[:, None])
    next_active = jnp.min(jnp.where(later, ids[None, :], N_EXPERTS), axis=1)
    next_active = jnp.where(next_active == N_EXPERTS, -1, next_active)
    steps = jnp.arange(n_blocks, dtype=I32)
    of_block = (block_expert[:, None] == ids[None, :]).astype(I32)
    per_block = lambda table: jnp.sum(of_block * table[None, :], axis=1)
    first = ((steps * bm == per_block(pstart)) & (steps < n_used)).astype(I32)
    used = jnp.clip(per_block(pstart + rows_e) - steps * bm, 0, bm)
    used = jnp.where(steps < n_used, used, 0).astype(I32)
    expert_tables = (block_expert, blk, used, first, per_block(next_active).astype(I32))
    yb = _experts(expert_tables, xs, w_gate_up[0], bgu, w_down[0], bdn)
    y_p, y_s = _combine(seg_tables, x1, slots, yb, gfin, n_p)

    hist_glu_s = jnp.concatenate([state_dwconv[0], uu_s], axis=1)[:, -(DW_TAPS - 1):]
    return (y_p.reshape(bp, tp, D_MODEL),
            y_s.reshape(bs, ts, D_MODEL),
            s_p[None],
            qkv_p[:, -(QKV_TAPS - 1):][None],
            uu_p[:, -(DW_TAPS - 1):][None],
            s_s[None],
            qkv_s[:, -(QKV_TAPS - 1):][None],
            hist_glu_s[None])
```

```python
import functools
import math

import jax
import jax.numpy as jnp
from jax import lax
from jax.experimental import pallas as pl
from jax.experimental.pallas import tpu as pltpu

F32 = jnp.float32
BF16 = jnp.bfloat16
I32 = jnp.int32

D_MODEL = 1024
GDN_HEADS = 4
HEAD_DIM = 128
GDN_WIDTH = GDN_HEADS * HEAD_DIM
QKV_COLS = 3 * GDN_WIDTH
CONF_WIDTH = 512
QKV_TAPS = 4
DW_TAPS = 31
N_EXPERTS = 32
TOP_K = 4
D_EXPERT = 1024
SWIGLU_LIMIT = 7.0
SWIGLU_ALPHA = 1.702
RMS_EPS = 1e-6
LN_EPS = 1e-5
L2_EPS = 1e-6
PROMPT_CHUNK = 128
INVERSE_BLOCK = 64

LANES = 128
SUBLANES = 8
TOKEN_TILE = 512
GDN_TILE = 512
GDN_SAMPLE_ROWS = 8
CONV_TILE = 512
CONV_ROWS = 512
EXPERT_ROWS = 512
EXPERT_PART_ROWS = 256
GRANULE = SUBLANES
GAP_BITS = 6
TILE_ROWS = -(-(TOKEN_TILE * TOP_K + N_EXPERTS * (GRANULE - 1)) // 256) * 256
META_EXPERT, META_RANK, META_GATE, META_ROWS = 0, TOP_K, 2 * TOP_K, 16
assert EXPERT_ROWS // GRANULE <= 1 << GAP_BITS
SEL_CHUNK = (64, 256)
VMEM_LIMIT = 48 * 1024 * 1024
EXPERT_VMEM_LIMIT = 56 * 1024 * 1024


def _dotb(a, b):
    return jnp.dot(a.astype(BF16), b.astype(BF16), preferred_element_type=F32)


def _sigmoid(x):
    return jax.nn.sigmoid(x)


def _params(sem):
    return pltpu.CompilerParams(dimension_semantics=sem, vmem_limit_bytes=VMEM_LIMIT)


def _inproj_kernel(x_ref, g_ref, wqkv_ref, wz_ref, wbd_ref, wglu_ref, bglu_ref, pa_ref, pb_ref,
                   tri_ref, qkv_ref, z_ref, bd_ref, bdt_ref, uu_ref):
    x = x_ref[...]
    tm = x.shape[0]
    ms = jnp.mean(x * x, axis=-1, keepdims=True)
    h = (x * lax.rsqrt(ms + RMS_EPS) * g_ref[...]).astype(BF16)
    qkv_ref[...] = jnp.dot(h, wqkv_ref[...], preferred_element_type=F32)
    z_ref[...] = jnp.dot(h, wz_ref[...], preferred_element_type=F32)
    glu = jnp.dot(h, wglu_ref[...], preferred_element_type=F32) + bglu_ref[...]
    uu_ref[...] = glu[:, :CONF_WIDTH] * _sigmoid(glu[:, CONF_WIDTH:])
    bd = jnp.dot(h, wbd_ref[...], preferred_element_type=F32)
    bdt = bd.T[:2 * GDN_HEADS, :]
    beta = _sigmoid(bdt)
    v = bdt + pb_ref[...]
    softplus = jnp.maximum(v, 0.0) + jnp.log1p(jnp.exp(-jnp.abs(v)))
    g = pa_ref[...] * softplus
    g_hi, g_rest = _split_bf16(g)
    g_mid = g - g_hi.astype(F32) - g_rest.astype(F32)
    pieces = jnp.concatenate([g_hi, g_rest, g_mid.astype(BF16)], axis=0)
    sums = jnp.dot(pieces, tri_ref[...], preferred_element_type=F32)
    rows = 2 * GDN_HEADS
    gc = sums[:rows] + sums[rows:2 * rows] + sums[2 * rows:]
    row = lax.broadcasted_iota(I32, (rows, tm), 0)
    res = jnp.where(row < GDN_HEADS, beta, gc)
    bdt_ref[...] = res
    bd_ref[...] = jnp.concatenate([res, jnp.zeros((LANES - rows, tm), F32)], axis=0).T[:, :rows]


def _inproj(x, g, wqkv, wz, wbd, wglu, bglu, pa, pb, chunk, t_batch):
    t = x.shape[0]
    tm = TOKEN_TILE
    tiles_per_batch = t_batch // tm if t_batch % tm == 0 else 0
    pos = jnp.arange(tm)
    tri = ((pos[:, None] <= pos[None, :])
           & (pos[:, None] // chunk == pos[None, :] // chunk)).astype(BF16)
    full = lambda shape: pl.BlockSpec(shape, lambda i: (0, 0))
    if tiles_per_batch:
        bdt_spec = pl.BlockSpec((None, 2 * GDN_HEADS, tm),
                                lambda i: (i // tiles_per_batch, 0, i % tiles_per_batch))
        bdt_shape = jax.ShapeDtypeStruct((t // t_batch, 2 * GDN_HEADS, t_batch), F32)
    else:
        bdt_spec = pl.BlockSpec((2 * GDN_HEADS, tm), lambda i: (0, i))
        bdt_shape = jax.ShapeDtypeStruct((2 * GDN_HEADS, t), F32)
    return pl.pallas_call(
        _inproj_kernel,
        grid=(t // tm,),
        in_specs=[
            pl.BlockSpec((tm, D_MODEL), lambda i: (i, 0)),
            full((1, D_MODEL)),
            full((D_MODEL, QKV_COLS)),
            full((D_MODEL, GDN_WIDTH)),
            full((D_MODEL, LANES)),
            full((D_MODEL, 2 * CONF_WIDTH)),
            full((1, 2 * CONF_WIDTH)),
            full((2 * GDN_HEADS, 1)),
            full((2 * GDN_HEADS, 1)),
            full((tm, tm)),
        ],
        out_specs=[
            pl.BlockSpec((tm, QKV_COLS), lambda i: (i, 0)),
            pl.BlockSpec((tm, GDN_WIDTH), lambda i: (i, 0)),
            pl.BlockSpec((tm, 2 * GDN_HEADS), lambda i: (i, 0)),
            bdt_spec,
            pl.BlockSpec((tm, CONF_WIDTH), lambda i: (i, 0)),
        ],
        out_shape=[
            jax.ShapeDtypeStruct((t, QKV_COLS), F32),
            jax.ShapeDtypeStruct((t, GDN_WIDTH), F32),
            jax.ShapeDtypeStruct((t, 2 * GDN_HEADS), F32),
            bdt_shape,
            jax.ShapeDtypeStruct((t, CONF_WIDTH), F32),
        ],
        compiler_params=_params(("arbitrary",)),
        name="inproj",
    )(x, g, wqkv, wz, wbd, wglu, bglu, pa, pb, tri)


def _split_bf16(a):
    hi = a.astype(BF16)
    lo = (a - hi.astype(F32)).astype(BF16)
    return hi, lo


def _unit_lower_inverses(ms, eye, n):
    if n <= INVERSE_BLOCK:
        return _neumann_inverses(ms, eye, n)
    h = n // 2
    assert h <= INVERSE_BLOCK
    eye_h = eye[:h, :h]
    diag = _neumann_inverses([m[:h, :h] for m in ms] + [m[h:, h:] for m in ms], eye_h, h)
    out = []
    for u, m in enumerate(ms):
        a_inv, b_inv = diag[u], diag[len(ms) + u]
        low = -_dotb(b_inv, _dotb(m[h:, :h], a_inv))
        top = jnp.concatenate([a_inv, jnp.zeros((h, h), F32)], axis=1)
        out.append(jnp.concatenate([top, jnp.concatenate([low, b_inv], axis=1)], axis=0))
    return out


def _neumann_inverses(ms, eye, n):
    levels = int(math.log2(n)) - 1
    ps = [eye - m for m in ms]
    pw = [(-m).astype(BF16) for m in ms]
    pw = [jnp.dot(p, p, preferred_element_type=F32).astype(BF16) for p in pw]
    for level in range(1, levels + 1):
        last = level == levels
        nxt = []
        for u, p in enumerate(pw):
            if last:
                ps[u] = ps[u] + jnp.dot(ps[u].astype(BF16), p, preferred_element_type=F32)
            else:
                out = jnp.dot(jnp.concatenate([p, ps[u].astype(BF16)], axis=0), p,
                              preferred_element_type=F32)
                nxt.append(out[:n].astype(BF16))
                ps[u] = ps[u] + out[n:]
        pw = nxt
    return ps


def _gdn_kernel(chunk, x_ref, z_ref, bd_ref, bdt_ref, hist_ref, s0_ref, wc_ref, won_ref,
                o_ref, s_ref, xbuf):
    i = pl.program_id(1)
    nb, tt = x_ref.shape[0], x_ref.shape[1]
    nchunk = tt // chunk
    pad = SUBLANES
    hist_rows = QKV_TAPS - 1

    @pl.when(i == 0)
    def _():
        xbuf[:, pad - hist_rows:pad, :] = hist_ref[...]
        s_ref[...] = s0_ref[...]

    xbuf[:, pad:pad + tt, :] = x_ref[...]

    ii = lax.broadcasted_iota(I32, (chunk, chunk), 0)
    jj = lax.broadcasted_iota(I32, (chunk, chunk), 1)
    lower_incl = ii >= jj
    lower_strict = ii > jj
    eye = (ii == jj).astype(F32)
    won = won_ref[...]

    def conv_silu(b, col0):
        cols = slice(col0, col0 + HEAD_DIM)
        acc = wc_ref[hist_rows:hist_rows + 1, cols] * xbuf[b, pad:pad + tt, cols]
        for j in range(hist_rows):
            r0 = pad - hist_rows + j
            acc = acc + wc_ref[j:j + 1, cols] * xbuf[b, r0:r0 + tt, cols]
        return acc * _sigmoid(acc)

    units = []
    for b in range(nb):
        bd = bd_ref[b]
        bdt = bdt_ref[b]
        for h in range(GDN_HEADS):
            q = conv_silu(b, h * HEAD_DIM)
            k = conv_silu(b, GDN_WIDTH + h * HEAD_DIM)
            v = conv_silu(b, 2 * GDN_WIDTH + h * HEAD_DIM)
            q = q * lax.rsqrt(jnp.sum(q * q, axis=-1, keepdims=True) + L2_EPS) * (HEAD_DIM ** -0.5)
            k = k * lax.rsqrt(jnp.sum(k * k, axis=-1, keepdims=True) + L2_EPS)
            kt = k.T
            beta = bd[:, h:h + 1]
            gcol = bd[:, GDN_HEADS + h:GDN_HEADS + h + 1]
            grow = bdt[GDN_HEADS + h:GDN_HEADS + h + 1, :]
            for c in range(nchunk):
                rows = slice(c * chunk, (c + 1) * chunk)
                qc, kc, vc, ktc = q[rows], k[rows], v[rows], kt[:, rows]
                bc, gc, gr = beta[rows], gcol[rows], grow[:, rows]
                dec = jnp.exp(jnp.where(lower_incl, gc - gr, -jnp.inf))
                kb = kc * bc
                eg = jnp.exp(gc)
                both = _dotb(jnp.concatenate([kb, qc], axis=0), ktc)
                units.append(dict(
                    b=b, h=h, c=c,
                    m=both[:chunk] * jnp.where(lower_strict, dec, 0.0),
                    qk=both[chunk:] * dec,
                    rhs=jnp.concatenate([vc * bc, kb * eg], axis=1),
                    qg=qc * eg,
                    kdt=ktc * jnp.exp(gr[:, chunk - 1:chunk] - gr),
                    d_last=jnp.exp(gc[chunk - 1:chunk, :]),
                ))

    tinvs = _unit_lower_inverses([u["m"] for u in units], eye, chunk)
    for u, tinv in zip(units, tinvs):
        sol = _dotb(tinv, u["rhs"])
        u["u"], u["w"] = sol[:, :HEAD_DIM], sol[:, HEAD_DIM:]

    state = {(b, h): s_ref[b, h] for b in range(nb) for h in range(GDN_HEADS)}
    by_key = {(u["b"], u["h"], u["c"]): u for u in units}
    for c in range(nchunk):
        rows = slice(c * chunk, (c + 1) * chunk)
        for b in range(nb):
            for h in range(GDN_HEADS):
                u = by_key[(b, h, c)]
                s = state[(b, h)]
                ws = _dotb(jnp.concatenate([u["w"], u["qg"]], axis=0), s)
                v_new = u["u"] - ws[:chunk]
                upd = _dotb(jnp.concatenate([u["qk"], u["kdt"]], axis=0), v_new)
                o_c = ws[chunk:] + upd[:chunk]
                state[(b, h)] = s * u["d_last"] + upd[chunk:]
                cols = slice(h * HEAD_DIM, (h + 1) * HEAD_DIM)
                on = o_c * lax.rsqrt(jnp.mean(o_c * o_c, axis=-1, keepdims=True) + RMS_EPS) * won
                zc = z_ref[b, rows, cols]
                o_ref[b, rows, cols] = (on * (zc * _sigmoid(zc))).astype(o_ref.dtype)
    for (b, h), s in state.items():
        s_ref[b, h] = s

    xbuf[:, pad - hist_rows:pad, :] = x_ref[:, tt - hist_rows:tt, :]


def _gdn(qkv, z, bd, bdt, hist, s0, wconv, wonorm, chunk, tt, nb):
    b, t, _ = qkv.shape
    nt = t // tt
    return pl.pallas_call(
        functools.partial(_gdn_kernel, chunk),
        grid=(b // nb, nt),
        in_specs=[
            pl.BlockSpec((nb, tt, QKV_COLS), lambda bi, i: (bi, i, 0)),
            pl.BlockSpec((nb, tt, GDN_WIDTH), lambda bi, i: (bi, i, 0)),
            pl.BlockSpec((nb, tt, 2 * GDN_HEADS), lambda bi, i: (bi, i, 0)),
            pl.BlockSpec((nb, 2 * GDN_HEADS, tt), lambda bi, i: (bi, 0, i)),
            pl.BlockSpec((nb, QKV_TAPS - 1, QKV_COLS), lambda bi, i: (bi, 0, 0)),
            pl.BlockSpec((nb, GDN_HEADS, HEAD_DIM, HEAD_DIM), lambda bi, i: (bi, 0, 0, 0)),
            pl.BlockSpec((QKV_TAPS, QKV_COLS), lambda bi, i: (0, 0)),
            pl.BlockSpec((1, HEAD_DIM), lambda bi, i: (0, 0)),
        ],
        out_specs=[
            pl.BlockSpec((nb, tt, GDN_WIDTH), lambda bi, i: (bi, i, 0)),
            pl.BlockSpec((nb, GDN_HEADS, HEAD_DIM, HEAD_DIM), lambda bi, i: (bi, 0, 0, 0)),
        ],
        out_shape=[
            jax.ShapeDtypeStruct((b, t, GDN_WIDTH), BF16),
            jax.ShapeDtypeStruct((b, GDN_HEADS, HEAD_DIM, HEAD_DIM), F32),
        ],
        scratch_shapes=[pltpu.VMEM((nb, tt + SUBLANES, QKV_COLS), F32)],
        compiler_params=_params(("arbitrary", "arbitrary")),
        name="gdn",
    )(qkv, z, bd, bdt, hist, s0, wconv, wonorm)


def _cconv_kernel(carry, u_ref, hist_ref, w_ref, b_ref, lg_ref, lb_ref, c_ref, ubuf, shifted):
    i = pl.program_id(1)
    nb, tt = u_ref.shape[0], u_ref.shape[1]
    hist_rows = DW_TAPS - 1
    pad = 4 * SUBLANES
    base = pad - hist_rows
    span = shifted.shape[1]

    def tap_rows(j, r0, rows):
        q, s = divmod(base + j, SUBLANES)
        start = q * SUBLANES + r0
        if s == 0:
            return ubuf[start:start + rows, :]
        return shifted[s - 1, start:start + rows, :]

    for b in range(nb):
        @pl.when(i == 0)
        def _():
            ubuf[base:pad, :] = hist_ref[b]

        ubuf[pad:pad + tt, :] = u_ref[b]
        for s in range(1, SUBLANES):
            shifted[s - 1] = ubuf[s:s + span, :]

        rows = min(CONV_ROWS, tt)
        for r0 in range(0, tt, rows):
            acc = w_ref[0:1, :] * tap_rows(0, r0, rows)
            for j in range(1, DW_TAPS):
                acc = acc + w_ref[j:j + 1, :] * tap_rows(j, r0, rows)
            cv = acc + b_ref[...]
            mu = jnp.mean(cv, axis=-1, keepdims=True)
            xc = cv - mu
            var = jnp.mean(xc * xc, axis=-1, keepdims=True)
            y = xc * lax.rsqrt(var + LN_EPS) * lg_ref[...] + lb_ref[...]
            c_ref[b, r0:r0 + rows, :] = (y * _sigmoid(y)).astype(c_ref.dtype)
    if carry:
        ubuf[base:pad, :] = u_ref[0, tt - hist_rows:tt, :]


def _cconv(uu, hist, w, b, lg, lb, tt, nb):
    bsz, t, _ = uu.shape
    nt = t // tt
    assert nb == 1 or nt == 1
    vec = pl.BlockSpec((1, CONF_WIDTH), lambda bi, i: (0, 0))
    return pl.pallas_call(
        functools.partial(_cconv_kernel, nt > 1),
        grid=(bsz // nb, nt),
        in_specs=[
            pl.BlockSpec((nb, tt, CONF_WIDTH), lambda bi, i: (bi, i, 0)),
            pl.BlockSpec((nb, DW_TAPS - 1, CONF_WIDTH), lambda bi, i: (bi, 0, 0)),
            pl.BlockSpec((DW_TAPS, CONF_WIDTH), lambda bi, i: (0, 0)),
            vec, vec, vec,
        ],
        out_specs=pl.BlockSpec((nb, tt, CONF_WIDTH), lambda bi, i: (bi, i, 0)),
        out_shape=jax.ShapeDtypeStruct((bsz, t, CONF_WIDTH), BF16),
        scratch_shapes=[pltpu.VMEM((tt + 4 * SUBLANES, CONF_WIDTH), F32),
                        pltpu.VMEM((SUBLANES - 1, tt + 3 * SUBLANES, CONF_WIDTH), F32)],
        compiler_params=_params(("arbitrary", "arbitrary")),
        name="cconv",
    )(uu, hist, w, b, lg, lb)


def _outproj_kernel(tiles_a, oa_ref, ca_ref, xa_ref, ob_ref, cb_ref, xb_ref, *rest):
    i = pl.program_id(0)

    @pl.when(i < tiles_a)
    def _():
        _outproj_tile(oa_ref, ca_ref, xa_ref, *rest)

    @pl.when(i >= tiles_a)
    def _():
        _outproj_tile(ob_ref, cb_ref, xb_ref, *rest)


def _outproj_tile(o_ref, c_ref, x_ref, wo_ref, wc_ref, g_ref, wr_ref, br_ref, upper_ref,
                  x1_ref, h2_ref, metar_ref, cnt_ref):
    tm = x_ref.shape[0]

    mix = _dotb(o_ref[...], wo_ref[...]) + _dotb(c_ref[...], wc_ref[...])
    x1 = x_ref[...] + mix
    x1_ref[...] = x1
    h2 = x1 * lax.rsqrt(jnp.mean(x1 * x1, axis=-1, keepdims=True) + RMS_EPS) * g_ref[...]
    h2_ref[...] = h2.astype(BF16)
    logits = lax.dot_general(wr_ref[...], h2.astype(BF16), (((1,), (1,)), ((), ())),
                             preferred_element_type=F32) + br_ref[...]
    row = lax.broadcasted_iota(I32, (N_EXPERTS, tm), 0).astype(F32)
    vals = logits
    idxs, tops = [], []
    for _ in range(TOP_K):
        m = jnp.max(vals, axis=0, keepdims=True)
        idx = jnp.min(jnp.where(vals == m, row, float(N_EXPERTS)), axis=0, keepdims=True)
        idxs.append(idx)
        tops.append(m)
        vals = jnp.where(row == idx, -jnp.inf, vals)
    exps = [jnp.exp(v - tops[0]) for v in tops]
    den = exps[0] + exps[1] + exps[2] + exps[3]
    onehots = [(row == idx).astype(F32) for idx in idxs]
    chosen = onehots[0] + onehots[1] + onehots[2] + onehots[3]
    before = jnp.dot(chosen.astype(BF16), upper_ref[...], preferred_element_type=F32)
    ranks = [jnp.sum(oh * before, axis=0, keepdims=True) for oh in onehots]
    gates = [e / den for e in exps]
    metar_ref[...] = jnp.concatenate(
        idxs + ranks + gates + [jnp.zeros((META_ROWS - 3 * TOP_K, tm), F32)], axis=0)
    cnt_ref[...] = jnp.broadcast_to(jnp.sum(chosen, axis=1, keepdims=True), (N_EXPERTS, LANES))[None]


def _outproj(first, second, wo, wc, g, wr, br):
    t_a, t_b = first[2].shape[0], second[2].shape[0]
    tm = TOKEN_TILE
    upper = (jnp.arange(tm)[:, None] < jnp.arange(tm)[None, :]).astype(BF16)
    tiles_a = t_a // tm
    t_total = t_a + t_b
    full = lambda shape: pl.BlockSpec(shape, lambda i: (0, 0))
    rows_a = lambda width: pl.BlockSpec((tm, width), lambda i: (jnp.minimum(i, tiles_a - 1), 0))
    rows_b = lambda width: pl.BlockSpec((tm, width), lambda i: (jnp.maximum(i - tiles_a, 0), 0))
    widths = (GDN_WIDTH, CONF_WIDTH, D_MODEL)
    return pl.pallas_call(
        functools.partial(_outproj_kernel, tiles_a),
        grid=(t_total // tm,),
        in_specs=[rows_a(w) for w in widths] + [rows_b(w) for w in widths] + [
            full((GDN_WIDTH, D_MODEL)),
            full((CONF_WIDTH, D_MODEL)),
            full((1, D_MODEL)),
            full((N_EXPERTS, D_MODEL)),
            full((N_EXPERTS, 1)),
            full((tm, tm)),
        ],
        out_specs=[
            pl.BlockSpec((tm, D_MODEL), lambda i: (i, 0)),
            pl.BlockSpec((tm, D_MODEL), lambda i: (i, 0)),
            pl.BlockSpec((META_ROWS, tm), lambda i: (0, i)),
            pl.BlockSpec((1, N_EXPERTS, LANES), lambda i: (i, 0, 0)),
        ],
        out_shape=[
            jax.ShapeDtypeStruct((t_total, D_MODEL), F32),
            jax.ShapeDtypeStruct((t_total, D_MODEL), BF16),
            jax.ShapeDtypeStruct((META_ROWS, t_total), F32),
            jax.ShapeDtypeStruct((t_total // tm, N_EXPERTS, LANES), F32),
        ],
        compiler_params=_params(("arbitrary",)),
        name="outproj_router",
    )(*first, *second, wo, wc, g, wr, br, upper)


def _segment_copies(i, seg_local_ref, seg_dst_ref, seg_n_ref, make_copy, wait):
    def copy(rows, local, dst):
        cp = make_copy(rows, pl.multiple_of(local, GRANULE), pl.multiple_of(dst, GRANULE))
        cp.wait() if wait else cp.start()

    def per_expert(ex, carry):
        j = i * N_EXPERTS + ex
        local, dst, n = seg_local_ref[j], seg_dst_ref[j], seg_n_ref[j]
        big = 8 * GRANULE

        def body(g, c):
            copy(big, local + g * big, dst + g * big)
            return c

        lax.fori_loop(0, n >> 3, body, 0)
        off = (n >> 3) * big
        for bit in (2, 1, 0):
            rows = GRANULE << bit
            has = (n >> bit) & 1

            @pl.when(has == 1)
            def _(rows=rows, off=off):
                copy(rows, local + off, dst + off)

            off = off + has * rows
        return carry

    lax.fori_loop(0, N_EXPERTS, per_expert, 0)


def _gap_copies(gap_start_ref, gap_n_ref, make_copy, wait):
    def per_expert(ex, carry):
        n = gap_n_ref[ex]
        start = pl.multiple_of(gap_start_ref[ex], GRANULE)
        off = 0 * n
        for bit in reversed(range(GAP_BITS)):
            rows = GRANULE << bit

            @pl.when(((n >> bit) & 1) == 1)
            def _(rows=rows, off=off):
                cp = make_copy(rows, pl.multiple_of(start + off, GRANULE))
                cp.wait() if wait else cp.start()

            off = off + ((n >> bit) & 1) * rows
        return carry

    lax.fori_loop(0, N_EXPERTS, per_expert, 0)


def _local_rows(meta_e, seg_local_ref, i):
    base = jnp.zeros(meta_e.shape, F32)
    for ex in range(N_EXPERTS):
        base = jnp.where(meta_e == float(ex), seg_local_ref[i * N_EXPERTS + ex].astype(F32), base)
    return base


def _dispatch_kernel(seg_local_ref, seg_dst_ref, seg_n_ref, gap_start_ref, gap_n_ref, tail_ref,
                     h_ref, mr_ref, xs_ref, slots_ref, xloc, zbuf, sems, zsem):
    i = pl.program_id(0)
    last = pl.num_programs(0) - 1
    tm = h_ref.shape[0]
    rt = xloc.shape[1]
    zrows = zbuf.shape[0]
    slot = i % 2

    def gap_copy(rows, dst):
        return pltpu.make_async_copy(zbuf.at[pl.ds(0, rows)], xs_ref.at[pl.ds(dst, rows)], zsem)

    def tail_copies(wait):
        def body(j, c):
            cp = gap_copy(zrows, pl.multiple_of(tail_ref[0] + j * zrows, zrows))
            cp.wait() if wait else cp.start()
            return c
        lax.fori_loop(0, tail_ref[1], body, 0)

    @pl.when(i == 0)
    def _():
        zbuf[...] = jnp.zeros(zbuf.shape, zbuf.dtype)
        _gap_copies(gap_start_ref, gap_n_ref, gap_copy, wait=False)
        tail_copies(wait=False)

    mr = mr_ref[...]
    dl = (_local_rows(mr[META_EXPERT:META_EXPERT + TOP_K], seg_local_ref, i)
          + mr[META_RANK:META_RANK + TOP_K])
    slots_ref[...] = jnp.concatenate(
        [dl, mr[META_GATE:META_GATE + TOP_K], jnp.zeros((LANES - 2 * TOP_K, tm), F32)], axis=0).T
    rr = lax.broadcasted_iota(I32, (rt, tm), 0).astype(F32)
    hit = rr == dl[0:1]
    for k in range(1, TOP_K):
        hit = hit | (rr == dl[k:k + 1])
    xloc[slot] = jnp.dot(hit.astype(BF16), h_ref[...], preferred_element_type=F32)

    def seg_copy(buf):
        def make(rows, local, dst):
            return pltpu.make_async_copy(xloc.at[buf, pl.ds(local, rows)],
                                         xs_ref.at[pl.ds(dst, rows)], sems.at[buf])
        return make

    _segment_copies(i, seg_local_ref, seg_dst_ref, seg_n_ref, seg_copy(slot), wait=False)

    @pl.when(i > 0)
    def _():
        _segment_copies(i - 1, seg_local_ref, seg_dst_ref, seg_n_ref, seg_copy(1 - slot), wait=True)

    @pl.when(i == last)
    def _():
        _segment_copies(i, seg_local_ref, seg_dst_ref, seg_n_ref, seg_copy(slot), wait=True)
        _gap_copies(gap_start_ref, gap_n_ref, gap_copy, wait=True)
        tail_copies(wait=True)


def _dispatch(tables, h2, metar, n_rows):
    t = h2.shape[0]
    tm = TOKEN_TILE
    return pl.pallas_call(
        _dispatch_kernel,
        grid_spec=pltpu.PrefetchScalarGridSpec(
            num_scalar_prefetch=6,
            grid=(t // tm,),
            in_specs=[
                pl.BlockSpec((tm, D_MODEL), lambda i, *_: (i, 0)),
                pl.BlockSpec((META_ROWS, tm), lambda i, *_: (0, i)),
            ],
            out_specs=[pl.BlockSpec(memory_space=pl.ANY),
                       pl.BlockSpec((tm, LANES), lambda i, *_: (i, 0))],
            scratch_shapes=[
                pltpu.VMEM((2, TILE_ROWS, D_MODEL), F32),
                pltpu.VMEM((GRANULE << (GAP_BITS - 1), D_MODEL), F32),
                pltpu.SemaphoreType.DMA((2,)),
                pltpu.SemaphoreType.DMA,
            ],
        ),
        out_shape=[jax.ShapeDtypeStruct((n_rows, D_MODEL), F32),
                   jax.ShapeDtypeStruct((t, LANES), F32)],
        compiler_params=_params(("arbitrary",)),
        name="dispatch",
    )(*tables, h2, metar)


def _expert_kernel(be_ref, rb_ref, used_ref, first_ref, next_ref,
                   x_ref, wgu_hbm, bgu_ref, wd_hbm, bdn_ref, y_ref,
                   wgu_f32, wd_f32, wgu_bf, wd_bf, sems):
    i = pl.program_id(0)

    def fetch(expert):
        return (pltpu.make_async_copy(wgu_hbm.at[expert], wgu_f32, sems.at[0]),
                pltpu.make_async_copy(wd_hbm.at[expert], wd_f32, sems.at[1]))

    @pl.when(i == 0)
    def _():
        for cp in fetch(be_ref[0]):
            cp.start()

    @pl.when(first_ref[i] == 1)
    def _():
        for cp in fetch(be_ref[i]):
            cp.wait()
        wgu_bf[...] = wgu_f32[...].astype(BF16)
        wd_bf[...] = wd_f32[...].astype(BF16)

        @pl.when(next_ref[i] >= 0)
        def _():
            for cp in fetch(next_ref[i]):
                cp.start()

    def ffn(x):
        gu = _dotb(x, wgu_bf[...]) + bgu_ref[...]
        x_glu = jnp.minimum(gu[:, :D_EXPERT], SWIGLU_LIMIT)
        x_lin = jnp.clip(gu[:, D_EXPERT:], -SWIGLU_LIMIT, SWIGLU_LIMIT)
        act = x_glu * _sigmoid(SWIGLU_ALPHA * x_glu) * (x_lin + 1.0)
        return _dotb(act, wd_bf[...]) + bdn_ref[...]

    bm = x_ref.shape[0]
    parts = (used_ref[i] + EXPERT_PART_ROWS - 1) // EXPERT_PART_ROWS
    for p in range(1, bm // EXPERT_PART_ROWS + 1):
        @pl.when(parts == p)
        def _(rows=p * EXPERT_PART_ROWS):
            y_ref[:rows, :] = ffn(x_ref[:rows, :])
            if rows < bm:
                y_ref[rows:, :] = jnp.zeros((bm - rows, y_ref.shape[1]), y_ref.dtype)


def _experts(tables, xs, wgu, bgu, wd, bdn):
    rows = xs.shape[0]
    bm = EXPERT_ROWS
    nb = rows // bm
    return pl.pallas_call(
        _expert_kernel,
        grid_spec=pltpu.PrefetchScalarGridSpec(
            num_scalar_prefetch=5,
            grid=(nb,),
            in_specs=[
                pl.BlockSpec((bm, D_MODEL), lambda i, be, rb, *_: (rb[i], 0)),
                pl.BlockSpec(memory_space=pl.ANY),
                pl.BlockSpec((None, 1, 2 * D_EXPERT), lambda i, be, *_: (be[i], 0, 0)),
                pl.BlockSpec(memory_space=pl.ANY),
                pl.BlockSpec((None, 1, D_MODEL), lambda i, be, *_: (be[i], 0, 0)),
            ],
            out_specs=pl.BlockSpec((bm, D_MODEL), lambda i, be, rb, *_: (rb[i], 0)),
            scratch_shapes=[
                pltpu.VMEM((D_MODEL, 2 * D_EXPERT), F32),
                pltpu.VMEM((D_EXPERT, D_MODEL), F32),
                pltpu.VMEM((D_MODEL, 2 * D_EXPERT), BF16),
                pltpu.VMEM((D_EXPERT, D_MODEL), BF16),
                pltpu.SemaphoreType.DMA((2,)),
            ],
        ),
        out_shape=jax.ShapeDtypeStruct((rows, D_MODEL), F32),
        input_output_aliases={5: 0},
        compiler_params=pltpu.CompilerParams(dimension_semantics=("arbitrary",),
                                             vmem_limit_bytes=EXPERT_VMEM_LIMIT),
        name="experts",
    )(*tables, xs, wgu, bgu, wd, bdn)


def _combine_kernel(tiles_a, seg_local_ref, seg_dst_ref, seg_n_ref, x1_ref, mc_ref, yb_ref, gf_ref,
                    ya_ref, yb_out_ref, ybuf, wsel, sems):
    i = pl.program_id(0)
    n_tiles = pl.num_programs(0)
    tm = x1_ref.shape[0]
    rt = ybuf.shape[1]
    slot = i % 2

    def seg_copy(buf):
        def make(rows, local, src):
            return pltpu.make_async_copy(yb_ref.at[pl.ds(src, rows)],
                                         ybuf.at[buf, pl.ds(local, rows)], sems.at[buf])
        return make

    @pl.when(i == 0)
    def _():
        ybuf[...] = jnp.zeros(ybuf.shape, ybuf.dtype)
        _segment_copies(i, seg_local_ref, seg_dst_ref, seg_n_ref, seg_copy(slot), wait=False)

    @pl.when(i + 1 < n_tiles)
    def _():
        _segment_copies(i + 1, seg_local_ref, seg_dst_ref, seg_n_ref, seg_copy(1 - slot), wait=False)

    dl = mc_ref[:, 0:TOP_K]
    gate = mc_ref[:, TOP_K:2 * TOP_K]
    rb, cb = SEL_CHUNK
    col = lax.broadcasted_iota(I32, (rb, cb), 1).astype(F32)
    for r0 in range(0, tm, rb):
        rows_k = [jnp.broadcast_to(dl[r0:r0 + rb, k:k + 1], (rb, cb)) for k in range(TOP_K)]
        gate_k = [jnp.broadcast_to(gate[r0:r0 + rb, k:k + 1], (rb, cb)) for k in range(TOP_K)]
        for c0 in range(0, rt, cb):
            rr = col + float(c0)
            w = jnp.where(rr == rows_k[0], gate_k[0], 0.0)
            for k in range(1, TOP_K):
                w = jnp.where(rr == rows_k[k], gate_k[k], w)
            wsel[r0:r0 + rb, c0:c0 + cb] = w.astype(BF16)
    _segment_copies(i, seg_local_ref, seg_dst_ref, seg_n_ref, seg_copy(slot), wait=True)
    acc = x1_ref[...] + jnp.dot(wsel[...], ybuf[slot].astype(BF16), preferred_element_type=F32)
    y = acc * lax.rsqrt(jnp.mean(acc * acc, axis=-1, keepdims=True) + RMS_EPS) * gf_ref[...]

    @pl.when(i < tiles_a)
    def _():
        ya_ref[...] = y

    @pl.when(i >= tiles_a)
    def _():
        yb_out_ref[...] = y


def _combine(tables, x1, slots, yb, gfin, n_a):
    t = x1.shape[0]
    tm = TOKEN_TILE
    tiles_a = n_a // tm
    return pl.pallas_call(
        functools.partial(_combine_kernel, tiles_a),
        grid_spec=pltpu.PrefetchScalarGridSpec(
            num_scalar_prefetch=3,
            grid=(t // tm,),
            in_specs=[
                pl.BlockSpec((tm, D_MODEL), lambda i, *_: (i, 0)),
                pl.BlockSpec((tm, LANES), lambda i, *_: (i, 0)),
                pl.BlockSpec(memory_space=pl.ANY),
                pl.BlockSpec((1, D_MODEL), lambda i, *_: (0, 0)),
            ],
            out_specs=[
                pl.BlockSpec((tm, D_MODEL), lambda i, *_: (jnp.minimum(i, tiles_a - 1), 0)),
                pl.BlockSpec((tm, D_MODEL), lambda i, *_: (jnp.maximum(i - tiles_a, 0), 0)),
            ],
            scratch_shapes=[pltpu.VMEM((2, TILE_ROWS, D_MODEL), F32),
                            pltpu.VMEM((tm, TILE_ROWS), BF16),
                            pltpu.SemaphoreType.DMA((2,))],
        ),
        out_shape=[jax.ShapeDtypeStruct((n_a, D_MODEL), F32),
                   jax.ShapeDtypeStruct((t - n_a, D_MODEL), F32)],
        compiler_params=pltpu.CompilerParams(dimension_semantics=("arbitrary",),
                                             vmem_limit_bytes=EXPERT_VMEM_LIMIT),
        name="combine",
    )(*tables, x1, slots, yb, gfin)


def kernel(x_prompt, x_sample, state_gdn, state_qkv_conv, state_dwconv, norm_mix_g, w_in, w_conv_qkv,
           a_log, dt_bias, w_onorm, b_glu, w_dw, b_dw, ln_g, ln_b, w_out, norm_ffn_g, w_router,
           b_router, w_gate_up, b_gate_up, w_down, b_down, norm_final_g):
    bp, tp, _ = x_prompt.shape
    bs, ts, _ = x_sample.shape
    n_p, n_s = bp * tp, bs * ts

    wi = w_in[0]
    c_z = QKV_COLS
    c_bd = c_z + GDN_WIDTH
    c_glu = c_bd + 2 * GDN_HEADS
    wqkv = wi[:, :c_z].astype(BF16)
    wz = wi[:, c_z:c_bd].astype(BF16)
    wbd = jnp.pad(wi[:, c_bd:c_glu], ((0, 0), (0, LANES - 2 * GDN_HEADS))).astype(BF16)
    wglu = wi[:, c_glu:].astype(BF16)
    bglu = b_glu[0][None, :]
    pa = jnp.pad(-jnp.exp(a_log[0].astype(F32)), (GDN_HEADS, 0))[:, None]
    pb = jnp.pad(dt_bias[0].astype(F32), (GDN_HEADS, 0))[:, None]
    gmix = norm_mix_g[0][None, :]
    wconv = w_conv_qkv[0]
    wonorm = w_onorm[0][None, :]
    wdw, bdw = w_dw[0], b_dw[0][None, :]
    lng, lnb = ln_g[0][None, :], ln_b[0][None, :]
    wo = w_out[0][:GDN_WIDTH].astype(BF16)
    wc = w_out[0][GDN_WIDTH:].astype(BF16)
    gffn = norm_ffn_g[0][None, :]
    wr, br = w_router[0].T.astype(BF16), b_router[0].astype(F32)[:, None]
    bgu = b_gate_up[0][:, None, :]
    bdn = b_down[0][:, None, :]
    gfin = norm_final_g[None, :]

    xp = x_prompt.reshape(n_p, D_MODEL)
    xs_tok = x_sample.reshape(n_s, D_MODEL)

    def mixers(x2d, bsz, t, hist_qkv, s0, hist_glu, chunk, gdn_tile, gdn_rows, conv_tile):
        qkv, z, bd, bdt, uu = _inproj(x2d, gmix, wqkv, wz, wbd, wglu, bglu, pa, pb, chunk, t)
        qkv3 = qkv.reshape(bsz, t, QKV_COLS)
        uu3 = uu.reshape(bsz, t, CONF_WIDTH)
        if bdt.ndim == 2:
            bdt = bdt.reshape(2 * GDN_HEADS, bsz, t).transpose(1, 0, 2)
        o, s_new = _gdn(qkv3, z.reshape(bsz, t, GDN_WIDTH), bd.reshape(bsz, t, 2 * GDN_HEADS),
                        bdt, hist_qkv, s0, wconv, wonorm, chunk, gdn_tile, gdn_rows)
        cc = _cconv(uu3, hist_glu, wdw, bdw, lng, lnb, conv_tile, gdn_rows)
        return o.reshape(bsz * t, GDN_WIDTH), cc.reshape(bsz * t, CONF_WIDTH), s_new, qkv3, uu3

    zero_qkv = jnp.zeros((bp, QKV_TAPS - 1, QKV_COLS), F32)
    zero_s = jnp.zeros((bp, GDN_HEADS, HEAD_DIM, HEAD_DIM), F32)
    zero_glu = jnp.zeros((bp, DW_TAPS - 1, CONF_WIDTH), F32)
    o_p, c_p, s_p, qkv_p, uu_p = mixers(xp, bp, tp, zero_qkv, zero_s, zero_glu,
                                        PROMPT_CHUNK, GDN_TILE, 1, CONV_TILE)
    o_s, c_s, s_s, qkv_s, uu_s = mixers(xs_tok, bs, ts, state_qkv_conv[0], state_gdn[0],
                                        state_dwconv[0], ts, ts, GDN_SAMPLE_ROWS, ts)

    n_tok = n_p + n_s
    x1, h2, metar, cnt_tile = _outproj((o_p, c_p, xp), (o_s, c_s, xs_tok), wo, wc, gffn, wr, br)

    bm = EXPERT_ROWS
    n_tiles = n_tok // TOKEN_TILE
    counts = cnt_tile[:, :, 0].astype(I32)
    seg_rows = (counts + GRANULE - 1) // GRANULE * GRANULE
    ids = jnp.arange(N_EXPERTS, dtype=I32)
    tile_ids = jnp.arange(n_tiles, dtype=I32)
    earlier_e = (ids[:, None] < ids[None, :]).astype(I32)
    earlier_t = (tile_ids[None, :] < tile_ids[:, None]).astype(I32)
    seg_local = jnp.sum(seg_rows[:, :, None] * earlier_e[None], axis=1)
    seg_before = jnp.sum(earlier_t[:, :, None] * seg_rows[None], axis=1)
    rows_e = jnp.sum(seg_rows, axis=0)
    padded = (rows_e + bm - 1) // bm * bm
    pstart = jnp.sum(padded[:, None] * earlier_e, axis=0)
    pend = pstart + padded
    seg_dst = pstart[None, :] + seg_before
    n_used = jnp.maximum(pend[-1] // bm, 1).astype(I32)
    max_rows = n_tok * TOP_K + n_tiles * N_EXPERTS * (GRANULE - 1) + N_EXPERTS * (bm - 1)
    n_blocks = -(-max_rows // bm)
    blk = jnp.minimum(jnp.arange(n_blocks, dtype=I32), n_used - 1)
    block_expert = jnp.minimum(
        jnp.sum((pend[None, :] <= (blk * bm)[:, None]).astype(I32), axis=1), N_EXPERTS - 1)
    seg_tables = (seg_local.reshape(-1).astype(I32), seg_dst.reshape(-1).astype(I32),
                  (seg_rows // GRANULE).reshape(-1).astype(I32))
    gap_tables = ((pstart + rows_e).astype(I32), ((padded - rows_e) // GRANULE).astype(I32))
    zero_rows = GRANULE << (GAP_BITS - 1)
    tail_table = jnp.stack([pend[-1], (n_blocks * bm - pend[-1]) // zero_rows]).astype(I32)

    xs, slots = _dispatch(seg_tables + gap_tables + (tail_table,), h2, metar, n_blocks * bm)
    active = padded > 0
    later = active[None, :] & (ids[None, :] > ids[:, None])
    next_active = jnp.min(jnp.where(later, ids[None, :], N_EXPERTS), axis=1)
    next_active = jnp.where(next_active == N_EXPERTS, -1, next_active)
    steps = jnp.arange(n_blocks, dtype=I32)
    of_block = (block_expert[:, None] == ids[None, :]).astype(I32)
    per_block = lambda table: jnp.sum(of_block * table[None, :], axis=1)
    first = ((steps * bm == per_block(pstart)) & (steps < n_used)).astype(I32)
    used = jnp.clip(per_block(pstart + rows_e) - steps * bm, 0, bm)
    used = jnp.where(steps < n_used, used, 0).astype(I32)
    expert_tables = (block_expert, blk, used, first, per_block(next_active).astype(I32))
    yb = _experts(expert_tables, xs, w_gate_up[0], bgu, w_down[0], bdn)
    y_p, y_s = _combine(seg_tables, x1, slots, yb, gfin, n_p)

    hist_glu_s = jnp.concatenate([state_dwconv[0], uu_s], axis=1)[:, -(DW_TAPS - 1):]
    return (y_p.reshape(bp, tp, D_MODEL),
            y_s.reshape(bs, ts, D_MODEL),
            s_p[None],
            qkv_p[:, -(QKV_TAPS - 1):][None],
            uu_p[:, -(DW_TAPS - 1):][None],
            s_s[None],
            qkv_s[:, -(QKV_TAPS - 1):][None],
            hist_glu_s[None])
```

```python
import functools
import math

import jax
import jax.numpy as jnp
from jax import lax
from jax.experimental import pallas as pl
from jax.experimental.pallas import tpu as pltpu

F32 = jnp.float32
BF16 = jnp.bfloat16
I32 = jnp.int32

D_MODEL = 1024
GDN_HEADS = 4
HEAD_DIM = 128
GDN_WIDTH = GDN_HEADS * HEAD_DIM
QKV_COLS = 3 * GDN_WIDTH
CONF_WIDTH = 512
QKV_TAPS = 4
DW_TAPS = 31
N_EXPERTS = 32
TOP_K = 4
D_EXPERT = 1024
SWIGLU_LIMIT = 7.0
SWIGLU_ALPHA = 1.702
RMS_EPS = 1e-6
LN_EPS = 1e-5
L2_EPS = 1e-6
PROMPT_CHUNK = 128
INVERSE_BLOCK = 64

LANES = 128
SUBLANES = 8
TOKEN_TILE = 512
GDN_TILE = 512
GDN_SAMPLE_ROWS = 8
CONV_TILE = 512
CONV_ROWS = 512
EXPERT_ROWS = 512
EXPERT_PART_ROWS = 256
GRANULE = SUBLANES
BULK_DMA_PRIORITY = 0
SIDE_DMA_PRIORITY = 1
GAP_BITS = 6
TILE_ROWS = -(-(TOKEN_TILE * TOP_K + N_EXPERTS * (GRANULE - 1)) // 256) * 256
META_EXPERT, META_RANK, META_GATE, META_ROWS = 0, TOP_K, 2 * TOP_K, 16
assert EXPERT_ROWS // GRANULE <= 1 << GAP_BITS
SEL_CHUNK = (64, 256)
VMEM_LIMIT = 48 * 1024 * 1024
EXPERT_VMEM_LIMIT = 56 * 1024 * 1024


def _dotb(a, b):
    return jnp.dot(a.astype(BF16), b.astype(BF16), preferred_element_type=F32)


def _sigmoid(x):
    return jax.nn.sigmoid(x)


def _params(sem):
    return pltpu.CompilerParams(dimension_semantics=sem, vmem_limit_bytes=VMEM_LIMIT)


def _inproj_kernel(x_ref, g_ref, wqkv_ref, wz_ref, wbd_ref, wglu_ref, bglu_ref, pa_ref, pb_ref,
                   tri_ref, qkv_ref, z_ref, bd_ref, bdt_ref, uu_ref):
    x = x_ref[...]
    tm = x.shape[0]
    ms = jnp.mean(x * x, axis=-1, keepdims=True)
    h = (x * lax.rsqrt(ms + RMS_EPS) * g_ref[...]).astype(BF16)
    qkv_ref[...] = jnp.dot(h, wqkv_ref[...], preferred_element_type=F32)
    z_ref[...] = jnp.dot(h, wz_ref[...], preferred_element_type=F32)
    glu = jnp.dot(h, wglu_ref[...], preferred_element_type=F32) + bglu_ref[...]
    uu_ref[...] = glu[:, :CONF_WIDTH] * _sigmoid(glu[:, CONF_WIDTH:])
    bd = jnp.dot(h, wbd_ref[...], preferred_element_type=F32)
    bdt = bd.T[:2 * GDN_HEADS, :]
    beta = _sigmoid(bdt)
    v = bdt + pb_ref[...]
    softplus = jnp.maximum(v, 0.0) + jnp.log1p(jnp.exp(-jnp.abs(v)))
    g = pa_ref[...] * softplus
    g_hi, g_rest = _split_bf16(g)
    g_mid = g - g_hi.astype(F32) - g_rest.astype(F32)
    pieces = jnp.concatenate([g_hi, g_rest, g_mid.astype(BF16)], axis=0)
    sums = jnp.dot(pieces, tri_ref[...], preferred_element_type=F32)
    rows = 2 * GDN_HEADS
    gc = sums[:rows] + sums[rows:2 * rows] + sums[2 * rows:]
    row = lax.broadcasted_iota(I32, (rows, tm), 0)
    res = jnp.where(row < GDN_HEADS, beta, gc)
    bdt_ref[...] = res
    bd_ref[...] = jnp.concatenate([res, jnp.zeros((LANES - rows, tm), F32)], axis=0).T[:, :rows]


def _inproj(x, g, wqkv, wz, wbd, wglu, bglu, pa, pb, chunk, t_batch):
    t = x.shape[0]
    tm = TOKEN_TILE
    tiles_per_batch = t_batch // tm if t_batch % tm == 0 else 0
    pos = jnp.arange(tm)
    tri = ((pos[:, None] <= pos[None, :])
           & (pos[:, None] // chunk == pos[None, :] // chunk)).astype(BF16)
    full = lambda shape: pl.BlockSpec(shape, lambda i: (0, 0))
    if tiles_per_batch:
        bdt_spec = pl.BlockSpec((None, 2 * GDN_HEADS, tm),
                                lambda i: (i // tiles_per_batch, 0, i % tiles_per_batch))
        bdt_shape = jax.ShapeDtypeStruct((t // t_batch, 2 * GDN_HEADS, t_batch), F32)
    else:
        bdt_spec = pl.BlockSpec((2 * GDN_HEADS, tm), lambda i: (0, i))
        bdt_shape = jax.ShapeDtypeStruct((2 * GDN_HEADS, t), F32)
    return pl.pallas_call(
        _inproj_kernel,
        grid=(t // tm,),
        in_specs=[
            pl.BlockSpec((tm, D_MODEL), lambda i: (i, 0)),
            full((1, D_MODEL)),
            full((D_MODEL, QKV_COLS)),
            full((D_MODEL, GDN_WIDTH)),
            full((D_MODEL, LANES)),
            full((D_MODEL, 2 * CONF_WIDTH)),
            full((1, 2 * CONF_WIDTH)),
            full((2 * GDN_HEADS, 1)),
            full((2 * GDN_HEADS, 1)),
            full((tm, tm)),
        ],
        out_specs=[
            pl.BlockSpec((tm, QKV_COLS), lambda i: (i, 0)),
            pl.BlockSpec((tm, GDN_WIDTH), lambda i: (i, 0)),
            pl.BlockSpec((tm, 2 * GDN_HEADS), lambda i: (i, 0)),
            bdt_spec,
            pl.BlockSpec((tm, CONF_WIDTH), lambda i: (i, 0)),
        ],
        out_shape=[
            jax.ShapeDtypeStruct((t, QKV_COLS), F32),
            jax.ShapeDtypeStruct((t, GDN_WIDTH), F32),
            jax.ShapeDtypeStruct((t, 2 * GDN_HEADS), F32),
            bdt_shape,
            jax.ShapeDtypeStruct((t, CONF_WIDTH), F32),
        ],
        compiler_params=_params(("arbitrary",)),
        name="inproj",
    )(x, g, wqkv, wz, wbd, wglu, bglu, pa, pb, tri)


def _split_bf16(a):
    hi = a.astype(BF16)
    lo = (a - hi.astype(F32)).astype(BF16)
    return hi, lo


def _unit_lower_inverses(ms, eye, n):
    if n <= INVERSE_BLOCK:
        return _neumann_inverses(ms, eye, n)
    h = n // 2
    assert h <= INVERSE_BLOCK
    eye_h = eye[:h, :h]
    diag = _neumann_inverses([m[:h, :h] for m in ms] + [m[h:, h:] for m in ms], eye_h, h)
    out = []
    for u, m in enumerate(ms):
        a_inv, b_inv = diag[u], diag[len(ms) + u]
        low = -_dotb(b_inv, _dotb(m[h:, :h], a_inv))
        top = jnp.concatenate([a_inv, jnp.zeros((h, h), F32)], axis=1)
        out.append(jnp.concatenate([top, jnp.concatenate([low, b_inv], axis=1)], axis=0))
    return out


def _neumann_inverses(ms, eye, n):
    levels = int(math.log2(n)) - 1
    ps = [eye - m for m in ms]
    pw = [(-m).astype(BF16) for m in ms]
    pw = [jnp.dot(p, p, preferred_element_type=F32).astype(BF16) for p in pw]
    for level in range(1, levels + 1):
        last = level == levels
        nxt = []
        for u, p in enumerate(pw):
            if last:
                ps[u] = ps[u] + jnp.dot(ps[u].astype(BF16), p, preferred_element_type=F32)
            else:
                out = jnp.dot(jnp.concatenate([p, ps[u].astype(BF16)], axis=0), p,
                              preferred_element_type=F32)
                nxt.append(out[:n].astype(BF16))
                ps[u] = ps[u] + out[n:]
        pw = nxt
    return ps


def _gdn_kernel(chunk, x_ref, z_ref, bd_ref, bdt_ref, hist_ref, s0_ref, wc_ref, won_ref,
                o_ref, s_ref, xbuf):
    i = pl.program_id(1)
    nb, tt = x_ref.shape[0], x_ref.shape[1]
    nchunk = tt // chunk
    pad = SUBLANES
    hist_rows = QKV_TAPS - 1

    @pl.when(i == 0)
    def _():
        xbuf[:, pad - hist_rows:pad, :] = hist_ref[...]
        s_ref[...] = s0_ref[...]

    xbuf[:, pad:pad + tt, :] = x_ref[...]

    ii = lax.broadcasted_iota(I32, (chunk, chunk), 0)
    jj = lax.broadcasted_iota(I32, (chunk, chunk), 1)
    lower_incl = ii >= jj
    lower_strict = ii > jj
    eye = (ii == jj).astype(F32)
    won = won_ref[...]

    def conv_silu(b, col0):
        cols = slice(col0, col0 + HEAD_DIM)
        acc = wc_ref[hist_rows:hist_rows + 1, cols] * xbuf[b, pad:pad + tt, cols]
        for j in range(hist_rows):
            r0 = pad - hist_rows + j
            acc = acc + wc_ref[j:j + 1, cols] * xbuf[b, r0:r0 + tt, cols]
        return acc * _sigmoid(acc)

    units = []
    for b in range(nb):
        bd = bd_ref[b]
        bdt = bdt_ref[b]
        for h in range(GDN_HEADS):
            q = conv_silu(b, h * HEAD_DIM)
            k = conv_silu(b, GDN_WIDTH + h * HEAD_DIM)
            v = conv_silu(b, 2 * GDN_WIDTH + h * HEAD_DIM)
            q = q * lax.rsqrt(jnp.sum(q * q, axis=-1, keepdims=True) + L2_EPS) * (HEAD_DIM ** -0.5)
            k = k * lax.rsqrt(jnp.sum(k * k, axis=-1, keepdims=True) + L2_EPS)
            kt = k.T
            beta = bd[:, h:h + 1]
            gcol = bd[:, GDN_HEADS + h:GDN_HEADS + h + 1]
            grow = bdt[GDN_HEADS + h:GDN_HEADS + h + 1, :]
            for c in range(nchunk):
                rows = slice(c * chunk, (c + 1) * chunk)
                qc, kc, vc, ktc = q[rows], k[rows], v[rows], kt[:, rows]
                bc, gc, gr = beta[rows], gcol[rows], grow[:, rows]
                dec = jnp.exp(jnp.where(lower_incl, gc - gr, -jnp.inf))
                kb = kc * bc
                eg = jnp.exp(gc)
                both = _dotb(jnp.concatenate([kb, qc], axis=0), ktc)
                units.append(dict(
                    b=b, h=h, c=c,
                    m=both[:chunk] * jnp.where(lower_strict, dec, 0.0),
                    qk=both[chunk:] * dec,
                    rhs=jnp.concatenate([vc * bc, kb * eg], axis=1),
                    qg=qc * eg,
                    kdt=ktc * jnp.exp(gr[:, chunk - 1:chunk] - gr),
                    d_last=jnp.exp(gc[chunk - 1:chunk, :]),
                ))

    tinvs = _unit_lower_inverses([u["m"] for u in units], eye, chunk)
    for u, tinv in zip(units, tinvs):
        sol = _dotb(tinv, u["rhs"])
        u["u"], u["w"] = sol[:, :HEAD_DIM], sol[:, HEAD_DIM:]

    state = {(b, h): s_ref[b, h] for b in range(nb) for h in range(GDN_HEADS)}
    by_key = {(u["b"], u["h"], u["c"]): u for u in units}
    for c in range(nchunk):
        rows = slice(c * chunk, (c + 1) * chunk)
        for b in range(nb):
            for h in range(GDN_HEADS):
                u = by_key[(b, h, c)]
                s = state[(b, h)]
                ws = _dotb(jnp.concatenate([u["w"], u["qg"]], axis=0), s)
                v_new = u["u"] - ws[:chunk]
                upd = _dotb(jnp.concatenate([u["qk"], u["kdt"]], axis=0), v_new)
                o_c = ws[chunk:] + upd[:chunk]
                state[(b, h)] = s * u["d_last"] + upd[chunk:]
                cols = slice(h * HEAD_DIM, (h + 1) * HEAD_DIM)
                on = o_c * lax.rsqrt(jnp.mean(o_c * o_c, axis=-1, keepdims=True) + RMS_EPS) * won
                zc = z_ref[b, rows, cols]
                o_ref[b, rows, cols] = (on * (zc * _sigmoid(zc))).astype(o_ref.dtype)
    for (b, h), s in state.items():
        s_ref[b, h] = s

    xbuf[:, pad - hist_rows:pad, :] = x_ref[:, tt - hist_rows:tt, :]


def _gdn(qkv, z, bd, bdt, hist, s0, wconv, wonorm, chunk, tt, nb):
    b, t, _ = qkv.shape
    nt = t // tt
    return pl.pallas_call(
        functools.partial(_gdn_kernel, chunk),
        grid=(b // nb, nt),
        in_specs=[
            pl.BlockSpec((nb, tt, QKV_COLS), lambda bi, i: (bi, i, 0)),
            pl.BlockSpec((nb, tt, GDN_WIDTH), lambda bi, i: (bi, i, 0)),
            pl.BlockSpec((nb, tt, 2 * GDN_HEADS), lambda bi, i: (bi, i, 0)),
            pl.BlockSpec((nb, 2 * GDN_HEADS, tt), lambda bi, i: (bi, 0, i)),
            pl.BlockSpec((nb, QKV_TAPS - 1, QKV_COLS), lambda bi, i: (bi, 0, 0)),
            pl.BlockSpec((nb, GDN_HEADS, HEAD_DIM, HEAD_DIM), lambda bi, i: (bi, 0, 0, 0)),
            pl.BlockSpec((QKV_TAPS, QKV_COLS), lambda bi, i: (0, 0)),
            pl.BlockSpec((1, HEAD_DIM), lambda bi, i: (0, 0)),
        ],
        out_specs=[
            pl.BlockSpec((nb, tt, GDN_WIDTH), lambda bi, i: (bi, i, 0)),
            pl.BlockSpec((nb, GDN_HEADS, HEAD_DIM, HEAD_DIM), lambda bi, i: (bi, 0, 0, 0)),
        ],
        out_shape=[
            jax.ShapeDtypeStruct((b, t, GDN_WIDTH), BF16),
            jax.ShapeDtypeStruct((b, GDN_HEADS, HEAD_DIM, HEAD_DIM), F32),
        ],
        scratch_shapes=[pltpu.VMEM((nb, tt + SUBLANES, QKV_COLS), F32)],
        compiler_params=_params(("arbitrary", "arbitrary")),
        name="gdn",
    )(qkv, z, bd, bdt, hist, s0, wconv, wonorm)


def _cconv_kernel(carry, u_ref, hist_ref, w_ref, b_ref, lg_ref, lb_ref, c_ref, ubuf, shifted):
    i = pl.program_id(1)
    nb, tt = u_ref.shape[0], u_ref.shape[1]
    hist_rows = DW_TAPS - 1
    pad = 4 * SUBLANES
    base = pad - hist_rows
    span = shifted.shape[1]

    def tap_rows(j, r0, rows):
        q, s = divmod(base + j, SUBLANES)
        start = q * SUBLANES + r0
        if s == 0:
            return ubuf[start:start + rows, :]
        return shifted[s - 1, start:start + rows, :]

    for b in range(nb):
        @pl.when(i == 0)
        def _():
            ubuf[base:pad, :] = hist_ref[b]

        ubuf[pad:pad + tt, :] = u_ref[b]
        for s in range(1, SUBLANES):
            shifted[s - 1] = ubuf[s:s + span, :]

        rows = min(CONV_ROWS, tt)
        for r0 in range(0, tt, rows):
            acc = w_ref[0:1, :] * tap_rows(0, r0, rows)
            for j in range(1, DW_TAPS):
                acc = acc + w_ref[j:j + 1, :] * tap_rows(j, r0, rows)
            cv = acc + b_ref[...]
            mu = jnp.mean(cv, axis=-1, keepdims=True)
            xc = cv - mu
            var = jnp.mean(xc * xc, axis=-1, keepdims=True)
            y = xc * lax.rsqrt(var + LN_EPS) * lg_ref[...] + lb_ref[...]
            c_ref[b, r0:r0 + rows, :] = (y * _sigmoid(y)).astype(c_ref.dtype)
    if carry:
        ubuf[base:pad, :] = u_ref[0, tt - hist_rows:tt, :]


def _cconv(uu, hist, w, b, lg, lb, tt, nb):
    bsz, t, _ = uu.shape
    nt = t // tt
    assert nb == 1 or nt == 1
    vec = pl.BlockSpec((1, CONF_WIDTH), lambda bi, i: (0, 0))
    return pl.pallas_call(
        functools.partial(_cconv_kernel, nt > 1),
        grid=(bsz // nb, nt),
        in_specs=[
            pl.BlockSpec((nb, tt, CONF_WIDTH), lambda bi, i: (bi, i, 0)),
            pl.BlockSpec((nb, DW_TAPS - 1, CONF_WIDTH), lambda bi, i: (bi, 0, 0)),
            pl.BlockSpec((DW_TAPS, CONF_WIDTH), lambda bi, i: (0, 0)),
            vec, vec, vec,
        ],
        out_specs=pl.BlockSpec((nb, tt, CONF_WIDTH), lambda bi, i: (bi, i, 0)),
        out_shape=jax.ShapeDtypeStruct((bsz, t, CONF_WIDTH), BF16),
        scratch_shapes=[pltpu.VMEM((tt + 4 * SUBLANES, CONF_WIDTH), F32),
                        pltpu.VMEM((SUBLANES - 1, tt + 3 * SUBLANES, CONF_WIDTH), F32)],
        compiler_params=_params(("arbitrary", "arbitrary")),
        name="cconv",
    )(uu, hist, w, b, lg, lb)


def _outproj_kernel(tiles_a, oa_ref, ca_ref, xa_ref, ob_ref, cb_ref, xb_ref, *rest):
    i = pl.program_id(0)

    @pl.when(i < tiles_a)
    def _():
        _outproj_tile(oa_ref, ca_ref, xa_ref, *rest)

    @pl.when(i >= tiles_a)
    def _():
        _outproj_tile(ob_ref, cb_ref, xb_ref, *rest)


def _outproj_tile(o_ref, c_ref, x_ref, wo_ref, wc_ref, g_ref, wr_ref, br_ref, upper_ref,
                  x1_ref, h2_ref, metar_ref, cnt_ref):
    tm = x_ref.shape[0]

    mix = _dotb(o_ref[...], wo_ref[...]) + _dotb(c_ref[...], wc_ref[...])
    x1 = x_ref[...] + mix
    x1_ref[...] = x1
    h2 = x1 * lax.rsqrt(jnp.mean(x1 * x1, axis=-1, keepdims=True) + RMS_EPS) * g_ref[...]
    h2_ref[...] = h2.astype(BF16)
    logits = lax.dot_general(wr_ref[...], h2.astype(BF16), (((1,), (1,)), ((), ())),
                             preferred_element_type=F32) + br_ref[...]
    row = lax.broadcasted_iota(I32, (N_EXPERTS, tm), 0).astype(F32)
    vals = logits
    idxs, tops = [], []
    for _ in range(TOP_K):
        m = jnp.max(vals, axis=0, keepdims=True)
        idx = jnp.min(jnp.where(vals == m, row, float(N_EXPERTS)), axis=0, keepdims=True)
        idxs.append(idx)
        tops.append(m)
        vals = jnp.where(row == idx, -jnp.inf, vals)
    exps = [jnp.exp(v - tops[0]) for v in tops]
    den = exps[0] + exps[1] + exps[2] + exps[3]
    onehots = [(row == idx).astype(F32) for idx in idxs]
    chosen = onehots[0] + onehots[1] + onehots[2] + onehots[3]
    before = jnp.dot(chosen.astype(BF16), upper_ref[...], preferred_element_type=F32)
    ranks = [jnp.sum(oh * before, axis=0, keepdims=True) for oh in onehots]
    gates = [e / den for e in exps]
    metar_ref[...] = jnp.concatenate(
        idxs + ranks + gates + [jnp.zeros((META_ROWS - 3 * TOP_K, tm), F32)], axis=0)
    cnt_ref[...] = jnp.broadcast_to(jnp.sum(chosen, axis=1, keepdims=True), (N_EXPERTS, LANES))[None]


def _outproj(first, second, wo, wc, g, wr, br):
    t_a, t_b = first[2].shape[0], second[2].shape[0]
    tm = TOKEN_TILE
    upper = (jnp.arange(tm)[:, None] < jnp.arange(tm)[None, :]).astype(BF16)
    tiles_a = t_a // tm
    t_total = t_a + t_b
    full = lambda shape: pl.BlockSpec(shape, lambda i: (0, 0))
    rows_a = lambda width: pl.BlockSpec((tm, width), lambda i: (jnp.minimum(i, tiles_a - 1), 0))
    rows_b = lambda width: pl.BlockSpec((tm, width), lambda i: (jnp.maximum(i - tiles_a, 0), 0))
    widths = (GDN_WIDTH, CONF_WIDTH, D_MODEL)
    return pl.pallas_call(
        functools.partial(_outproj_kernel, tiles_a),
        grid=(t_total // tm,),
        in_specs=[rows_a(w) for w in widths] + [rows_b(w) for w in widths] + [
            full((GDN_WIDTH, D_MODEL)),
            full((CONF_WIDTH, D_MODEL)),
            full((1, D_MODEL)),
            full((N_EXPERTS, D_MODEL)),
            full((N_EXPERTS, 1)),
            full((tm, tm)),
        ],
        out_specs=[
            pl.BlockSpec((tm, D_MODEL), lambda i: (i, 0)),
            pl.BlockSpec((tm, D_MODEL), lambda i: (i, 0)),
            pl.BlockSpec((META_ROWS, tm), lambda i: (0, i)),
            pl.BlockSpec((1, N_EXPERTS, LANES), lambda i: (i, 0, 0)),
        ],
        out_shape=[
            jax.ShapeDtypeStruct((t_total, D_MODEL), F32),
            jax.ShapeDtypeStruct((t_total, D_MODEL), BF16),
            jax.ShapeDtypeStruct((META_ROWS, t_total), F32),
            jax.ShapeDtypeStruct((t_total // tm, N_EXPERTS, LANES), F32),
        ],
        compiler_params=_params(("arbitrary",)),
        name="outproj_router",
    )(*first, *second, wo, wc, g, wr, br, upper)


def _segment_copies(i, seg_local_ref, seg_dst_ref, seg_n_ref, make_copy, wait):
    def copy(rows, local, dst, priority):
        cp = make_copy(rows, pl.multiple_of(local, GRANULE), pl.multiple_of(dst, GRANULE))
        cp.wait() if wait else cp.start(priority)

    def per_expert(ex, carry):
        j = i * N_EXPERTS + ex
        local, dst, n = seg_local_ref[j], seg_dst_ref[j], seg_n_ref[j]
        big = 8 * GRANULE

        def body(g, c):
            copy(big, local + g * big, dst + g * big, BULK_DMA_PRIORITY)
            return c

        lax.fori_loop(0, n >> 3, body, 0)
        off = (n >> 3) * big
        for bit in (2, 1, 0):
            rows = GRANULE << bit
            has = (n >> bit) & 1

            @pl.when(has == 1)
            def _(rows=rows, off=off):
                copy(rows, local + off, dst + off, SIDE_DMA_PRIORITY)

            off = off + has * rows
        return carry

    lax.fori_loop(0, N_EXPERTS, per_expert, 0)


def _gap_copies(gap_start_ref, gap_n_ref, make_copy, wait):
    def per_expert(ex, carry):
        n = gap_n_ref[ex]
        start = pl.multiple_of(gap_start_ref[ex], GRANULE)
        off = 0 * n
        for bit in reversed(range(GAP_BITS)):
            rows = GRANULE << bit

            @pl.when(((n >> bit) & 1) == 1)
            def _(rows=rows, off=off):
                cp = make_copy(rows, pl.multiple_of(start + off, GRANULE))
                cp.wait() if wait else cp.start(SIDE_DMA_PRIORITY)

            off = off + ((n >> bit) & 1) * rows
        return carry

    lax.fori_loop(0, N_EXPERTS, per_expert, 0)


def _local_rows(meta_e, seg_local_ref, i):
    base = jnp.zeros(meta_e.shape, F32)
    for ex in range(N_EXPERTS):
        base = jnp.where(meta_e == float(ex), seg_local_ref[i * N_EXPERTS + ex].astype(F32), base)
    return base


def _dispatch_kernel(seg_local_ref, seg_dst_ref, seg_n_ref, gap_start_ref, gap_n_ref, tail_ref,
                     h_ref, mr_ref, xs_ref, slots_ref, xloc, zbuf, sems, zsem):
    i = pl.program_id(0)
    last = pl.num_programs(0) - 1
    tm = h_ref.shape[0]
    rt = xloc.shape[1]
    zrows = zbuf.shape[0]
    slot = i % 2

    def gap_copy(rows, dst):
        return pltpu.make_async_copy(zbuf.at[pl.ds(0, rows)], xs_ref.at[pl.ds(dst, rows)], zsem)

    def tail_copies(wait):
        def body(j, c):
            cp = gap_copy(zrows, pl.multiple_of(tail_ref[0] + j * zrows, zrows))
            cp.wait() if wait else cp.start(SIDE_DMA_PRIORITY)
            return c
        lax.fori_loop(0, tail_ref[1], body, 0)

    @pl.when(i == 0)
    def _():
        zbuf[...] = jnp.zeros(zbuf.shape, zbuf.dtype)
        _gap_copies(gap_start_ref, gap_n_ref, gap_copy, wait=False)
        tail_copies(wait=False)

    mr = mr_ref[...]
    dl = (_local_rows(mr[META_EXPERT:META_EXPERT + TOP_K], seg_local_ref, i)
          + mr[META_RANK:META_RANK + TOP_K])
    slots_ref[...] = jnp.concatenate(
        [dl, mr[META_GATE:META_GATE + TOP_K], jnp.zeros((LANES - 2 * TOP_K, tm), F32)], axis=0).T
    rr = lax.broadcasted_iota(I32, (rt, tm), 0).astype(F32)
    hit = rr == dl[0:1]
    for k in range(1, TOP_K):
        hit = hit | (rr == dl[k:k + 1])
    xloc[slot] = jnp.dot(hit.astype(BF16), h_ref[...], preferred_element_type=F32)

    def seg_copy(buf):
        def make(rows, local, dst):
            return pltpu.make_async_copy(xloc.at[buf, pl.ds(local, rows)],
                                         xs_ref.at[pl.ds(dst, rows)], sems.at[buf])
        return make

    _segment_copies(i, seg_local_ref, seg_dst_ref, seg_n_ref, seg_copy(slot), wait=False)

    @pl.when(i > 0)
    def _():
        _segment_copies(i - 1, seg_local_ref, seg_dst_ref, seg_n_ref, seg_copy(1 - slot), wait=True)

    @pl.when(i == last)
    def _():
        _segment_copies(i, seg_local_ref, seg_dst_ref, seg_n_ref, seg_copy(slot), wait=True)
        _gap_copies(gap_start_ref, gap_n_ref, gap_copy, wait=True)
        tail_copies(wait=True)


def _dispatch(tables, h2, metar, n_rows):
    t = h2.shape[0]
    tm = TOKEN_TILE
    return pl.pallas_call(
        _dispatch_kernel,
        grid_spec=pltpu.PrefetchScalarGridSpec(
            num_scalar_prefetch=6,
            grid=(t // tm,),
            in_specs=[
                pl.BlockSpec((tm, D_MODEL), lambda i, *_: (i, 0)),
                pl.BlockSpec((META_ROWS, tm), lambda i, *_: (0, i)),
            ],
            out_specs=[pl.BlockSpec(memory_space=pl.ANY),
                       pl.BlockSpec((tm, LANES), lambda i, *_: (i, 0))],
            scratch_shapes=[
                pltpu.VMEM((2, TILE_ROWS, D_MODEL), F32),
                pltpu.VMEM((GRANULE << (GAP_BITS - 1), D_MODEL), F32),
                pltpu.SemaphoreType.DMA((2,)),
                pltpu.SemaphoreType.DMA,
            ],
        ),
        out_shape=[jax.ShapeDtypeStruct((n_rows, D_MODEL), F32),
                   jax.ShapeDtypeStruct((t, LANES), F32)],
        compiler_params=_params(("arbitrary",)),
        name="dispatch",
    )(*tables, h2, metar)


def _expert_kernel(be_ref, rb_ref, used_ref, first_ref, next_ref,
                   x_ref, wgu_hbm, bgu_ref, wd_hbm, bdn_ref, y_ref,
                   wgu_f32, wd_f32, wgu_bf, wd_bf, sems):
    i = pl.program_id(0)

    def fetch(expert):
        return (pltpu.make_async_copy(wgu_hbm.at[expert], wgu_f32, sems.at[0]),
                pltpu.make_async_copy(wd_hbm.at[expert], wd_f32, sems.at[1]))

    @pl.when(i == 0)
    def _():
        for cp in fetch(be_ref[0]):
            cp.start(SIDE_DMA_PRIORITY)

    @pl.when(first_ref[i] == 1)
    def _():
        for cp in fetch(be_ref[i]):
            cp.wait()
        wgu_bf[...] = wgu_f32[...].astype(BF16)
        wd_bf[...] = wd_f32[...].astype(BF16)

        @pl.when(next_ref[i] >= 0)
        def _():
            for cp in fetch(next_ref[i]):
                cp.start(SIDE_DMA_PRIORITY)

    def ffn(x):
        gu = _dotb(x, wgu_bf[...]) + bgu_ref[...]
        x_glu = jnp.minimum(gu[:, :D_EXPERT], SWIGLU_LIMIT)
        x_lin = jnp.clip(gu[:, D_EXPERT:], -SWIGLU_LIMIT, SWIGLU_LIMIT)
        act = x_glu * _sigmoid(SWIGLU_ALPHA * x_glu) * (x_lin + 1.0)
        return _dotb(act, wd_bf[...]) + bdn_ref[...]

    bm = x_ref.shape[0]
    parts = (used_ref[i] + EXPERT_PART_ROWS - 1) // EXPERT_PART_ROWS
    for p in range(1, bm // EXPERT_PART_ROWS + 1):
        @pl.when(parts == p)
        def _(rows=p * EXPERT_PART_ROWS):
            y_ref[:rows, :] = ffn(x_ref[:rows, :])
            if rows < bm:
                y_ref[rows:, :] = jnp.zeros((bm - rows, y_ref.shape[1]), y_ref.dtype)


def _experts(tables, xs, wgu, bgu, wd, bdn):
    rows = xs.shape[0]
    bm = EXPERT_ROWS
    nb = rows // bm
    return pl.pallas_call(
        _expert_kernel,
        grid_spec=pltpu.PrefetchScalarGridSpec(
            num_scalar_prefetch=5,
            grid=(nb,),
            in_specs=[
                pl.BlockSpec((bm, D_MODEL), lambda i, be, rb, *_: (rb[i], 0)),
                pl.BlockSpec(memory_space=pl.ANY),
                pl.BlockSpec((None, 1, 2 * D_EXPERT), lambda i, be, *_: (be[i], 0, 0)),
                pl.BlockSpec(memory_space=pl.ANY),
                pl.BlockSpec((None, 1, D_MODEL), lambda i, be, *_: (be[i], 0, 0)),
            ],
            out_specs=pl.BlockSpec((bm, D_MODEL), lambda i, be, rb, *_: (rb[i], 0)),
            scratch_shapes=[
                pltpu.VMEM((D_MODEL, 2 * D_EXPERT), F32),
                pltpu.VMEM((D_EXPERT, D_MODEL), F32),
                pltpu.VMEM((D_MODEL, 2 * D_EXPERT), BF16),
                pltpu.VMEM((D_EXPERT, D_MODEL), BF16),
                pltpu.SemaphoreType.DMA((2,)),
            ],
        ),
        out_shape=jax.ShapeDtypeStruct((rows, D_MODEL), F32),
        input_output_aliases={5: 0},
        compiler_params=pltpu.CompilerParams(dimension_semantics=("arbitrary",),
                                             vmem_limit_bytes=EXPERT_VMEM_LIMIT),
        name="experts",
    )(*tables, xs, wgu, bgu, wd, bdn)


def _combine_kernel(tiles_a, seg_local_ref, seg_dst_ref, seg_n_ref, x1_ref, mc_ref, yb_ref, gf_ref,
                    ya_ref, yb_out_ref, ybuf, wsel, sems):
    i = pl.program_id(0)
    n_tiles = pl.num_programs(0)
    tm = x1_ref.shape[0]
    rt = ybuf.shape[1]
    slot = i % 2

    def seg_copy(buf):
        def make(rows, local, src):
            return pltpu.make_async_copy(yb_ref.at[pl.ds(src, rows)],
                                         ybuf.at[buf, pl.ds(local, rows)], sems.at[buf])
        return make

    @pl.when(i == 0)
    def _():
        ybuf[...] = jnp.zeros(ybuf.shape, ybuf.dtype)
        _segment_copies(i, seg_local_ref, seg_dst_ref, seg_n_ref, seg_copy(slot), wait=False)

    @pl.when(i + 1 < n_tiles)
    def _():
        _segment_copies(i + 1, seg_local_ref, seg_dst_ref, seg_n_ref, seg_copy(1 - slot), wait=False)

    dl = mc_ref[:, 0:TOP_K]
    gate = mc_ref[:, TOP_K:2 * TOP_K]
    rb, cb = SEL_CHUNK
    col = lax.broadcasted_iota(I32, (rb, cb), 1).astype(F32)
    for r0 in range(0, tm, rb):
        rows_k = [jnp.broadcast_to(dl[r0:r0 + rb, k:k + 1], (rb, cb)) for k in range(TOP_K)]
        gate_k = [jnp.broadcast_to(gate[r0:r0 + rb, k:k + 1], (rb, cb)) for k in range(TOP_K)]
        for c0 in range(0, rt, cb):
            rr = col + float(c0)
            w = jnp.where(rr == rows_k[0], gate_k[0], 0.0)
            for k in range(1, TOP_K):
                w = jnp.where(rr == rows_k[k], gate_k[k], w)
            wsel[r0:r0 + rb, c0:c0 + cb] = w.astype(BF16)
    _segment_copies(i, seg_local_ref, seg_dst_ref, seg_n_ref, seg_copy(slot), wait=True)
    acc = x1_ref[...] + jnp.dot(wsel[...], ybuf[slot].astype(BF16), preferred_element_type=F32)
    y = acc * lax.rsqrt(jnp.mean(acc * acc, axis=-1, keepdims=True) + RMS_EPS) * gf_ref[...]

    @pl.when(i < tiles_a)
    def _():
        ya_ref[...] = y

    @pl.when(i >= tiles_a)
    def _():
        yb_out_ref[...] = y


def _combine(tables, x1, slots, yb, gfin, n_a):
    t = x1.shape[0]
    tm = TOKEN_TILE
    tiles_a = n_a // tm
    return pl.pallas_call(
        functools.partial(_combine_kernel, tiles_a),
        grid_spec=pltpu.PrefetchScalarGridSpec(
            num_scalar_prefetch=3,
            grid=(t // tm,),
            in_specs=[
                pl.BlockSpec((tm, D_MODEL), lambda i, *_: (i, 0)),
                pl.BlockSpec((tm, LANES), lambda i, *_: (i, 0)),
                pl.BlockSpec(memory_space=pl.ANY),
                pl.BlockSpec((1, D_MODEL), lambda i, *_: (0, 0)),
            ],
            out_specs=[
                pl.BlockSpec((tm, D_MODEL), lambda i, *_: (jnp.minimum(i, tiles_a - 1), 0)),
                pl.BlockSpec((tm, D_MODEL), lambda i, *_: (jnp.maximum(i - tiles_a, 0), 0)),
            ],
            scratch_shapes=[pltpu.VMEM((2, TILE_ROWS, D_MODEL), F32),
                            pltpu.VMEM((tm, TILE_ROWS), BF16),
                            pltpu.SemaphoreType.DMA((2,))],
        ),
        out_shape=[jax.ShapeDtypeStruct((n_a, D_MODEL), F32),
                   jax.ShapeDtypeStruct((t - n_a, D_MODEL), F32)],
        compiler_params=pltpu.CompilerParams(dimension_semantics=("arbitrary",),
                                             vmem_limit_bytes=EXPERT_VMEM_LIMIT),
        name="combine",
    )(*tables, x1, slots, yb, gfin)


def kernel(x_prompt, x_sample, state_gdn, state_qkv_conv, state_dwconv, norm_mix_g, w_in, w_conv_qkv,
           a_log, dt_bias, w_onorm, b_glu, w_dw, b_dw, ln_g, ln_b, w_out, norm_ffn_g, w_router,
           b_router, w_gate_up, b_gate_up, w_down, b_down, norm_final_g):
    bp, tp, _ = x_prompt.shape
    bs, ts, _ = x_sample.shape
    n_p, n_s = bp * tp, bs * ts

    wi = w_in[0]
    c_z = QKV_COLS
    c_bd = c_z + GDN_WIDTH
    c_glu = c_bd + 2 * GDN_HEADS
    wqkv = wi[:, :c_z].astype(BF16)
    wz = wi[:, c_z:c_bd].astype(BF16)
    wbd = jnp.pad(wi[:, c_bd:c_glu], ((0, 0), (0, LANES - 2 * GDN_HEADS))).astype(BF16)
    wglu = wi[:, c_glu:].astype(BF16)
    bglu = b_glu[0][None, :]
    pa = jnp.pad(-jnp.exp(a_log[0].astype(F32)), (GDN_HEADS, 0))[:, None]
    pb = jnp.pad(dt_bias[0].astype(F32), (GDN_HEADS, 0))[:, None]
    gmix = norm_mix_g[0][None, :]
    wconv = w_conv_qkv[0]
    wonorm = w_onorm[0][None, :]
    wdw, bdw = w_dw[0], b_dw[0][None, :]
    lng, lnb = ln_g[0][None, :], ln_b[0][None, :]
    wo = w_out[0][:GDN_WIDTH].astype(BF16)
    wc = w_out[0][GDN_WIDTH:].astype(BF16)
    gffn = norm_ffn_g[0][None, :]
    wr, br = w_router[0].T.astype(BF16), b_router[0].astype(F32)[:, None]
    bgu = b_gate_up[0][:, None, :]
    bdn = b_down[0][:, None, :]
    gfin = norm_final_g[None, :]

    xp = x_prompt.reshape(n_p, D_MODEL)
    xs_tok = x_sample.reshape(n_s, D_MODEL)

    def mixers(x2d, bsz, t, hist_qkv, s0, hist_glu, chunk, gdn_tile, gdn_rows, conv_tile):
        qkv, z, bd, bdt, uu = _inproj(x2d, gmix, wqkv, wz, wbd, wglu, bglu, pa, pb, chunk, t)
        qkv3 = qkv.reshape(bsz, t, QKV_COLS)
        uu3 = uu.reshape(bsz, t, CONF_WIDTH)
        if bdt.ndim == 2:
            bdt = bdt.reshape(2 * GDN_HEADS, bsz, t).transpose(1, 0, 2)
        o, s_new = _gdn(qkv3, z.reshape(bsz, t, GDN_WIDTH), bd.reshape(bsz, t, 2 * GDN_HEADS),
                        bdt, hist_qkv, s0, wconv, wonorm, chunk, gdn_tile, gdn_rows)
        cc = _cconv(uu3, hist_glu, wdw, bdw, lng, lnb, conv_tile, gdn_rows)
        return o.reshape(bsz * t, GDN_WIDTH), cc.reshape(bsz * t, CONF_WIDTH), s_new, qkv3, uu3

    zero_qkv = jnp.zeros((bp, QKV_TAPS - 1, QKV_COLS), F32)
    zero_s = jnp.zeros((bp, GDN_HEADS, HEAD_DIM, HEAD_DIM), F32)
    zero_glu = jnp.zeros((bp, DW_TAPS - 1, CONF_WIDTH), F32)
    o_p, c_p, s_p, qkv_p, uu_p = mixers(xp, bp, tp, zero_qkv, zero_s, zero_glu,
                                        PROMPT_CHUNK, GDN_TILE, 1, CONV_TILE)
    o_s, c_s, s_s, qkv_s, uu_s = mixers(xs_tok, bs, ts, state_qkv_conv[0], state_gdn[0],
                                        state_dwconv[0], ts, ts, GDN_SAMPLE_ROWS, ts)

    n_tok = n_p + n_s
    x1, h2, metar, cnt_tile = _outproj((o_p, c_p, xp), (o_s, c_s, xs_tok), wo, wc, gffn, wr, br)

    bm = EXPERT_ROWS
    n_tiles = n_tok // TOKEN_TILE
    counts = cnt_tile[:, :, 0].astype(I32)
    seg_rows = (counts + GRANULE - 1) // GRANULE * GRANULE
    ids = jnp.arange(N_EXPERTS, dtype=I32)
    tile_ids = jnp.arange(n_tiles, dtype=I32)
    earlier_e = (ids[:, None] < ids[None, :]).astype(I32)
    earlier_t = (tile_ids[None, :] < tile_ids[:, None]).astype(I32)
    seg_local = jnp.sum(seg_rows[:, :, None] * earlier_e[None], axis=1)
    seg_before = jnp.sum(earlier_t[:, :, None] * seg_rows[None], axis=1)
    rows_e = jnp.sum(seg_rows, axis=0)
    padded = (rows_e + bm - 1) // bm * bm
    pstart = jnp.sum(padded[:, None] * earlier_e, axis=0)
    pend = pstart + padded
    seg_dst = pstart[None, :] + seg_before
    n_used = jnp.maximum(pend[-1] // bm, 1).astype(I32)
    max_rows = n_tok * TOP_K + n_tiles * N_EXPERTS * (GRANULE - 1) + N_EXPERTS * (bm - 1)
    n_blocks = -(-max_rows // bm)
    blk = jnp.minimum(jnp.arange(n_blocks, dtype=I32), n_used - 1)
    block_expert = jnp.minimum(
        jnp.sum((pend[None, :] <= (blk * bm)[:, None]).astype(I32), axis=1), N_EXPERTS - 1)
    seg_tables = (seg_local.reshape(-1).astype(I32), seg_dst.reshape(-1).astype(I32),
                  (seg_rows // GRANULE).reshape(-1).astype(I32))
    gap_tables = ((pstart + rows_e).astype(I32), ((padded - rows_e) // GRANULE).astype(I32))
    zero_rows = GRANULE << (GAP_BITS - 1)
    tail_table = jnp.stack([pend[-1], (n_blocks * bm - pend[-1]) // zero_rows]).astype(I32)

    xs, slots = _dispatch(seg_tables + gap_tables + (tail_table,), h2, metar, n_blocks * bm)
    active = padded > 0
    later = active[None, :] & (ids[None, :] > ids[:, None])
    next_active = jnp.min(jnp.where(later, ids[None, :], N_EXPERTS), axis=1)
    next_active = jnp.where(next_active == N_EXPERTS, -1, next_active)
    steps = jnp.arange(n_blocks, dtype=I32)
    of_block = (block_expert[:, None] == ids[None, :]).astype(I32)
    per_block = lambda table: jnp.sum(of_block * table[None, :], axis=1)
    first = ((steps * bm == per_block(pstart)) & (steps < n_used)).astype(I32)
    used = jnp.clip(per_block(pstart + rows_e) - steps * bm, 0, bm)
    used = jnp.where(steps < n_used, used, 0).astype(I32)
    expert_tables = (block_expert, blk, used, first, per_block(next_active).astype(I32))
    yb = _experts(expert_tables, xs, w_gate_up[0], bgu, w_down[0], bdn)
    y_p, y_s = _combine(seg_tables, x1, slots, yb, gfin, n_p)

    hist_glu_s = jnp.concatenate([state_dwconv[0], uu_s], axis=1)[:, -(DW_TAPS - 1):]
    return (y_p.reshape(bp, tp, D_MODEL),
            y_s.reshape(bs, ts, D_MODEL),
            s_p[None],
            qkv_p[:, -(QKV_TAPS - 1):][None],
            uu_p[:, -(DW_TAPS - 1):][None],
            s_s[None],
            qkv_s[:, -(QKV_TAPS - 1):][None],
            hist_glu_s[None])
```
